```python
import math
import jax, jax.numpy as jnp
from jax import lax
import numpy as np

D_MODEL = 2048
BATCH = 16
SEQ = 256
DEPTH = 2
DEC_BATCH = 2
DEC_SEQ = 4096
PAST_LEN = 256

GRID_W = 64
HEAD_DIM = 64
QBLK = 128
ROPE_BASE = 10000.0
A_HEADS = 8
A_KV_HEADS = 2
A_GROUP = A_HEADS // A_KV_HEADS
A_WINDOW = 128
B_HEADS = 4
B_QK_DIM = 64
B_V_DIM = 2 * B_QK_DIM
C_WIDTH = 512
C_BLOCKS = 8
C_BW = C_WIDTH // C_BLOCKS
C_CONV = 4
C_CONV_LEFT = 1
C_POW = 8.0
D_HEADS = 8
NA_ROWS = 8
NA_COLS = 16
MIX_SIZES = (A_HEADS * HEAD_DIM, B_HEADS * B_V_DIM, C_WIDTH, D_HEADS * HEAD_DIM)
MIX_WIDTH = sum(MIX_SIZES)
IN_SIZES = (A_HEADS * HEAD_DIM, A_KV_HEADS * HEAD_DIM, A_KV_HEADS * HEAD_DIM,
            2 * B_HEADS * B_QK_DIM, 2 * B_HEADS * B_QK_DIM, B_HEADS * B_V_DIM,
            C_WIDTH, C_WIDTH,
            D_HEADS * HEAD_DIM, D_HEADS * HEAD_DIM, D_HEADS * HEAD_DIM)
IN_WIDTH = sum(IN_SIZES)
N_EXPERTS = 64
TOP_K = 6
D_EXPERT = 512
D_SHARED = 512
ROUTED_SCALE = 2.5
MOE_BLK = 128
ALPHA = (2.0 * DEPTH) ** 0.25
BETA = (8.0 * DEPTH) ** -0.25
LN_EPS = 1e-5
NORM_EPS = 1e-6
NEG_INF = -1e30

kernel_name = 'hybrid_diffusion_prefix_trunk_step'

F32 = jnp.float32


def _split_last(x, sizes):
    out, o = [], 0
    for n in sizes:
        out.append(x[..., o:o + n])
        o += n
    return out


def _rms(x, g):
    xf = x.astype(F32)
    y = xf * lax.rsqrt(jnp.mean(xf * xf, axis=-1, keepdims=True) + NORM_EPS)
    return (y * g.astype(F32)).astype(x.dtype)


def _layernorm(x, g, b):
    xf = x.astype(F32)
    xc = xf - jnp.mean(xf, axis=-1, keepdims=True)
    var = jnp.mean(xc * xc, axis=-1, keepdims=True)
    return (xc * lax.rsqrt(var + LN_EPS) * g.astype(F32) + b.astype(F32)).astype(x.dtype)


def _lambda_init(l):
    return 0.8 - 0.6 * math.exp(-0.3 * l)


def _axial_rope_tables(L, dim):
    t = jnp.arange(L)
    row = (t // GRID_W).astype(F32)
    col = (t % GRID_W).astype(F32)
    nf = dim // 4
    inv = ROPE_BASE ** (-jnp.arange(nf, dtype=F32) / nf)
    ar = row[:, None] * inv[None, :]
    ac = col[:, None] * inv[None, :]
    ang = jnp.concatenate([ar, ar, ac, ac], axis=-1)
    return jnp.cos(ang), jnp.sin(ang)


def _apply_rope(x, cos, sin):
    d = x.shape[-1]
    q = d // 4
    x1, x2, x3, x4 = _split_last(x, (q, q, q, q))
    rot = jnp.concatenate([-x2, x1, -x4, x3], axis=-1)
    shape = (1, x.shape[1]) + (1,) * (x.ndim - 3) + (d,)
    return (x * cos.reshape(shape) + rot * sin.reshape(shape)).astype(x.dtype)


def _joint_softmax(parts, sink=None):
    sizes = [s.shape[-1] for s in parts]
    parts = list(parts)
    if sink is not None:
        parts.append(jnp.broadcast_to(sink.astype(F32)[..., None, None], parts[0].shape[:-1] + (1,)))
    p = jax.nn.softmax(jnp.concatenate(parts, axis=-1), axis=-1)
    return _split_last(p, sizes)


def _map_query_blocks(fn, q):
    B, L = q.shape[:2]
    nb = L // QBLK
    qb = jnp.moveaxis(q.reshape((B, nb, QBLK) + q.shape[2:]), 1, 0)
    out = jnp.moveaxis(lax.map(fn, qb), 0, 1)
    return out.reshape((B, L) + out.shape[3:])


def _modulation(cond, w_mod, b_mod):
    m = jax.nn.silu(cond) @ w_mod + b_mod
    return [u[:, None, :] for u in jnp.split(m, 6, axis=-1)]


def _project(h, w_in):
    B, L, _ = h.shape
    qa, ka, va, qb, kb, vb, xc, gc, qd, kd, vd = _split_last(h @ w_in, IN_SIZES)
    return (qa.reshape(B, L, A_HEADS, HEAD_DIM), ka.reshape(B, L, A_KV_HEADS, HEAD_DIM),
            va.reshape(B, L, A_KV_HEADS, HEAD_DIM),
            qb.reshape(B, L, B_HEADS, 2, B_QK_DIM), kb.reshape(B, L, B_HEADS, 2, B_QK_DIM),
            vb.reshape(B, L, B_HEADS, B_V_DIM),
            xc, gc,
            qd.reshape(B, L, D_HEADS, HEAD_DIM), kd.reshape(B, L, D_HEADS, HEAD_DIM),
            vd.reshape(B, L, D_HEADS, HEAD_DIM))


def _dense_attn(q, k, v, sink):
    B, L, Hq, d = q.shape
    Hkv = k.shape[2]
    G = Hq // Hkv
    scale = d ** -0.5
    sk = None if sink is None else sink.reshape(Hkv, G)

    def blk(qb):
        s = jnp.einsum('bqhgd,bkhd->bhgqk', qb, k).astype(F32) * scale
        p = _joint_softmax([s], sk)[0]
        return jnp.einsum('bhgqk,bkhd->bqhgd', p.astype(v.dtype), v)

    return _map_query_blocks(blk, q.reshape(B, L, Hkv, G, d)).reshape(B, L, Hq * d)


def _window_attn(q, k, v, kc, vc, sink):
    B, L, _, d = q.shape
    nb = L // QBLK
    scale = d ** -0.5
    qb = q.reshape(B, nb, QBLK, A_KV_HEADS, A_GROUP, d)
    pad = ((0, 0), (QBLK, QBLK), (0, 0), (0, 0))
    band = jnp.arange(nb)[:, None] * QBLK + jnp.arange(3 * QBLK)[None, :]
    kb = jnp.pad(k, pad)[:, band]
    vb = jnp.pad(v, pad)[:, band]
    qpos = jnp.arange(L).reshape(nb, QBLK)
    kpos = (band - QBLK)[:, None, :]
    valid = (jnp.abs(qpos[:, :, None] - kpos) <= A_WINDOW) & (kpos >= 0) & (kpos < L)
    s_loc = jnp.einsum('bnqhgd,bnkhd->bnhgqk', qb, kb).astype(F32) * scale
    s_loc = jnp.where(valid[None, :, None, None], s_loc, NEG_INF)
    s_ctx = jnp.einsum('bnqhgd,bkhd->bnhgqk', qb, kc).astype(F32) * scale
    p_loc, p_ctx = _joint_softmax([s_loc, s_ctx], sink.reshape(A_KV_HEADS, A_GROUP))
    o = (jnp.einsum('bnhgqk,bnkhd->bnqhgd', p_loc.astype(vb.dtype), vb)
         + jnp.einsum('bnhgqk,bkhd->bnqhgd', p_ctx.astype(vc.dtype), vc))
    return o.reshape(B, L, A_HEADS * d)


def _diff_lambda(lp, lam_init):
    lp = lp.astype(F32)
    return jnp.exp(jnp.sum(lp[0] * lp[1])) - jnp.exp(jnp.sum(lp[2] * lp[3])) + lam_init


def _diff_attn(q, k, v, lam):
    B, L, H = q.shape[:3]
    scale = B_QK_DIM ** -0.5

    def blk(qb):
        s = jnp.einsum('bqhmd,bkhmd->bhmqk', qb, k).astype(F32) * scale
        p = jax.nn.softmax(s, axis=-1)
        w = p[:, :, 0] - lam * p[:, :, 1]
        return jnp.einsum('bhqk,bkhe->bqhe', w.astype(v.dtype), v)

    return _map_query_blocks(blk, q).reshape(B, L, H * B_V_DIM)


def _short_conv(x, w, b):
    C = x.shape[-1]
    y = lax.conv_general_dilated(x, w[:, None, :].astype(x.dtype), window_strides=(1,),
                                 padding=[(C_CONV_LEFT, C_CONV - 1 - C_CONV_LEFT)],
                                 dimension_numbers=('NWC', 'WIO', 'NWC'), feature_group_count=C)
    return y + b


def _linear_scan(a, u, h0, reverse):
    first = a.shape[1] - 1 if reverse else 0
    u = u.at[:, first].add(a[:, first] * h0)

    def combine(e1, e2):
        a1, b1 = e1
        a2, b2 = e2
        return a1 * a2, a2 * b1 + b2

    _, h = lax.associative_scan(combine, (a, u), reverse=reverse, axis=1)
    return h


def _rglru_bidir(x, h0, w_r, b_r, w_i, b_i, lam):
    B, L, C = x.shape
    xf = x.astype(F32)
    xb = xf.reshape(B, L, C_BLOCKS, C_BW)

    def gate(w, b):
        z = jnp.einsum('blnc,kncm->kblnm', xb, w.astype(F32)).reshape(2, B, L, C)
        return jax.nn.sigmoid(z + b.astype(F32)[:, None, None, :])

    r = gate(w_r, b_r)
    i = gate(w_i, b_i)
    log_a = -C_POW * r * jax.nn.softplus(-lam.astype(F32))[:, None, None, :]
    a = jnp.exp(log_a)
    u = jnp.sqrt(-jnp.expm1(2.0 * log_a)) * i * xf[None]
    h_f = _linear_scan(a[0], u[0], h0[0].astype(F32), reverse=False)
    h_b = _linear_scan(a[1], u[1], h0[1].astype(F32), reverse=True)
    return h_f, h_b


def _neighbourhood_attn(q, k, v, kc, vc, rpb):
    B, L, H, d = q.shape
    rows = L // GRID_W
    kh = min(NA_ROWS, rows)
    kw = NA_COLS
    scale = d ** -0.5
    col = jnp.arange(GRID_W)
    kcol = jnp.clip(col - kw // 2, 0, GRID_W - kw)[:, None] + jnp.arange(kw)[None, :]
    dcol = kcol - col[:, None] + (NA_COLS - 1)
    qr = jnp.moveaxis(q.reshape(B, rows, GRID_W, H, d), 1, 0)

    def row_fn(args):
        q_row, r = args
        krow = jnp.clip(r - kh // 2, 0, rows - kh) + jnp.arange(kh)
        idx = (krow[None, :, None] * GRID_W + kcol[:, None, :]).reshape(GRID_W, kh * kw)
        kn = k[:, idx]
        vn = v[:, idx]
        drow = krow - r + (NA_ROWS - 1)
        bias = rpb[:, drow[None, :, None], dcol[:, None, :]].reshape(H, GRID_W, kh * kw)
        s_loc = jnp.einsum('bqhd,bqnhd->bhqn', q_row, kn).astype(F32) * scale + bias.astype(F32)[None]
        s_ctx = jnp.einsum('bqhd,bkhd->bhqk', q_row, kc).astype(F32) * scale
        p_loc, p_ctx = _joint_softmax([s_loc, s_ctx])
        return (jnp.einsum('bhqn,bqnhd->bqhd', p_loc.astype(vn.dtype), vn)
                + jnp.einsum('bhqk,bkhd->bqhd', p_ctx.astype(vc.dtype), vc))

    o = lax.map(row_fn, (qr, jnp.arange(rows)))
    return jnp.moveaxis(o, 0, 1).reshape(B, L, H * d)


def _mixer_out(oa, ob, oc, od, g_mix, lam_init, w_out):
    ga, gb, gc, gd = _split_last(g_mix, MIX_SIZES)
    B, L, _ = ob.shape
    ob = _rms(ob.reshape(B, L, B_HEADS, B_V_DIM), gb.reshape(B_HEADS, B_V_DIM)).reshape(B, L, -1) * (1.0 - lam_init)
    y = jnp.concatenate([_rms(oa, ga), ob, _rms(oc, gc), _rms(od, gd)], axis=-1)
    return y @ w_out


def _moe(x, p):
    B, L, D = x.shape
    xt = x.reshape(-1, D)
    T = xt.shape[0]
    scores = jax.nn.sigmoid((xt @ p['w_router']).astype(F32))
    _, idx = lax.top_k(scores + p['b_router'].astype(F32), TOP_K)
    w = jnp.take_along_axis(scores, idx, axis=-1)
    w = w / jnp.sum(w, axis=-1, keepdims=True) * ROUTED_SCALE
    A = T * TOP_K
    e_flat = idx.reshape(-1)
    order = jnp.argsort(e_flat)
    e_sorted = e_flat[order]
    tok_sorted = (order // TOP_K).astype(jnp.int32)
    w_sorted = w.reshape(-1)[order]
    counts = jnp.bincount(e_flat, length=N_EXPERTS)
    padded = (counts + MOE_BLK - 1) // MOE_BLK * MOE_BLK
    pad_end = jnp.cumsum(padded)
    pad_start = pad_end - padded
    start = jnp.cumsum(counts) - counts
    dest = pad_start[e_sorted] + jnp.arange(A) - start[e_sorted]
    nblk = -(-A // MOE_BLK) + N_EXPERTS
    P = nblk * MOE_BLK
    buf_tok = jnp.zeros((P,), jnp.int32).at[dest].set(tok_sorted)
    buf_w = jnp.zeros((P,), F32).at[dest].set(w_sorted)
    blk_e = jnp.minimum(jnp.searchsorted(pad_end, jnp.arange(nblk) * MOE_BLK, side='right'), N_EXPERTS - 1)
    w_g, w_u, w_d = p['w_e_gate'], p['w_e_up'], p['w_e_down']

    def expert_block(args):
        tok, wt, e = args
        xe = xt[tok]
        hid = jax.nn.silu(xe @ w_g[e]) * (xe @ w_u[e])
        return (hid @ w_d[e]).astype(F32) * wt[:, None]

    y_blk = lax.map(expert_block, (buf_tok.reshape(nblk, MOE_BLK), buf_w.reshape(nblk, MOE_BLK), blk_e))
    routed = jnp.zeros((T, D), F32).at[buf_tok].add(y_blk.reshape(P, D))
    shared = (jax.nn.silu(xt @ p['w_s_gate']) * (xt @ p['w_s_up'])) @ p['w_s_down']
    return (routed + shared.astype(F32)).astype(x.dtype).reshape(B, L, D)


def _context_layer(x, cond, p, lam_init):
    B = x.shape[0]
    sh1, sc1, g1, sh2, sc2, g2 = _modulation(cond, p['w_mod'], p['b_mod'])
    h = x * (1.0 + sc1) + sh1
    qa, ka, va, qb, kb, vb, xc, gc, qd, kd, vd = _project(h, p['w_in'])
    oa = _dense_attn(qa, ka, va, p['a_sink'])
    ob = _diff_attn(qb, kb, vb, _diff_lambda(p['b_lambda'], lam_init))
    xc = _short_conv(xc, p['c_conv_w'], p['c_conv_b'])
    h0 = jnp.zeros((2, B, C_WIDTH), F32)
    h_f, h_b = _rglru_bidir(xc, h0, p['c_w_rgate'], p['c_b_rgate'], p['c_w_igate'], p['c_b_igate'], p['c_lambda'])
    oc = jax.nn.gelu(gc) * (h_f + h_b)
    od = _dense_attn(qd, kd, vd, None)
    x = _layernorm(ALPHA * x + g1 * _mixer_out(oa, ob, oc, od, p['g_mix'], lam_init, p['w_out']), p['ln_g'][0], p['ln_b'][0])
    h = x * (1.0 + sc2) + sh2
    x = _layernorm(ALPHA * x + g2 * _moe(h, p), p['ln_g'][1], p['ln_b'][1])
    state_c = jnp.stack([h_f[:, -1], h_b[:, 0]], axis=1)
    return x, (ka, va, kb, vb, state_c, kd, vd)


def _latent_layer(x, cond, cache, p, lam_init, cos, sin):
    ka_c, va_c, kb_c, vb_c, st_c, kd_c, vd_c = cache
    sh1, sc1, g1, sh2, sc2, g2 = _modulation(cond, p['w_mod'], p['b_mod'])
    h = x * (1.0 + sc1) + sh1
    qa, ka, va, qb, kb, vb, xc, gc, qd, kd, vd = _project(h, p['w_in'])
    qa, ka = _apply_rope(qa, cos, sin), _apply_rope(ka, cos, sin)
    qb, kb = _apply_rope(qb, cos, sin), _apply_rope(kb, cos, sin)
    oa = _window_attn(qa, ka, va, ka_c, va_c, p['a_sink'])
    ob = _diff_attn(qb, jnp.concatenate([kb, kb_c.astype(kb.dtype)], axis=1),
                    jnp.concatenate([vb, vb_c.astype(vb.dtype)], axis=1), _diff_lambda(p['b_lambda'], lam_init))
    xc = _short_conv(xc, p['c_conv_w'], p['c_conv_b'])
    h_f, h_b = _rglru_bidir(xc, jnp.moveaxis(st_c, 1, 0), p['c_w_rgate'], p['c_b_rgate'], p['c_w_igate'], p['c_b_igate'], p['c_lambda'])
    oc = jax.nn.gelu(gc) * (h_f + h_b)
    od = _neighbourhood_attn(qd, kd, vd, kd_c, vd_c, p['d_rpb'])
    x = _layernorm(ALPHA * x + g1 * _mixer_out(oa, ob, oc, od, p['g_mix'], lam_init, p['w_out']), p['ln_g'][0], p['ln_b'][0])
    h = x * (1.0 + sc2) + sh2
    return _layernorm(ALPHA * x + g2 * _moe(h, p), p['ln_g'][1], p['ln_b'][1])


def setup_inputs(seed: int = 0) -> dict:
    key = jax.random.key(seed)
    ks = iter(jax.random.split(key, 48))

    def nrm(shape, s):
        return jax.random.normal(next(ks), shape, F32) * s

    f = D_MODEL ** -0.5
    u = jax.random.uniform(next(ks), (DEPTH, 2, C_WIDTH), F32, 0.9, 0.999)
    sl = u ** (1.0 / C_POW)
    c_lambda = jnp.log(sl) - jnp.log1p(-sl)
    return {
        'x_prompt': nrm((BATCH, SEQ, D_MODEL), 1.0),
        'x_sample': nrm((DEC_BATCH, DEC_SEQ, D_MODEL), 1.0),
        'c': nrm((DEC_BATCH, D_MODEL), 1.0),
        'cache_a_k': nrm((DEC_BATCH, DEPTH, PAST_LEN, A_KV_HEADS, HEAD_DIM), 1.0),
        'cache_a_v': nrm((DEC_BATCH, DEPTH, PAST_LEN, A_KV_HEADS, HEAD_DIM), 1.0),
        'cache_b_k': nrm((DEC_BATCH, DEPTH, PAST_LEN, B_HEADS, 2, B_QK_DIM), 1.0),
        'cache_b_v': nrm((DEC_BATCH, DEPTH, PAST_LEN, B_HEADS, B_V_DIM), 1.0),
        'state_c': nrm((DEC_BATCH, DEPTH, 2, C_WIDTH), 0.5),
        'cache_d_k': nrm((DEC_BATCH, DEPTH, PAST_LEN, D_HEADS, HEAD_DIM), 1.0),
        'cache_d_v': nrm((DEC_BATCH, DEPTH, PAST_LEN, D_HEADS, HEAD_DIM), 1.0),
        'c_ctx': nrm((D_MODEL,), 1.0),
        'w_mod': nrm((DEPTH, D_MODEL, 6 * D_MODEL), 0.5 * f),
        'b_mod': nrm((DEPTH, 6 * D_MODEL), 0.02),
        'w_in': nrm((DEPTH, D_MODEL, IN_WIDTH), f),
        'a_sink': nrm((DEPTH, A_HEADS), 0.5),
        'b_lambda': nrm((DEPTH, 4, B_QK_DIM), 0.1),
        'c_conv_w': nrm((DEPTH, C_CONV, C_WIDTH), C_CONV ** -0.5),
        'c_conv_b': nrm((DEPTH, C_WIDTH), 0.02),
        'c_w_rgate': nrm((DEPTH, 2, C_BLOCKS, C_BW, C_BW), C_BW ** -0.5),
        'c_b_rgate': nrm((DEPTH, 2, C_WIDTH), 0.02),
        'c_w_igate': nrm((DEPTH, 2, C_BLOCKS, C_BW, C_BW), C_BW ** -0.5),
        'c_b_igate': nrm((DEPTH, 2, C_WIDTH), 0.02),
        'c_lambda': c_lambda,
        'd_rpb': nrm((DEPTH, D_HEADS, 2 * NA_ROWS - 1, 2 * NA_COLS - 1), 0.1),
        'g_mix': 1.0 + nrm((DEPTH, MIX_WIDTH), 0.02),
        'w_out': nrm((DEPTH, MIX_WIDTH, D_MODEL), BETA * MIX_WIDTH ** -0.5),
        'ln_g': 1.0 + nrm((DEPTH, 2, D_MODEL), 0.02),
        'ln_b': nrm((DEPTH, 2, D_MODEL), 0.02),
        'w_router': nrm((DEPTH, D_MODEL, N_EXPERTS), f),
        'b_router': nrm((DEPTH, N_EXPERTS), 0.01),
        'w_e_gate': nrm((DEPTH, N_EXPERTS, D_MODEL, D_EXPERT), f),
        'w_e_up': nrm((DEPTH, N_EXPERTS, D_MODEL, D_EXPERT), f),
        'w_e_down': nrm((DEPTH, N_EXPERTS, D_EXPERT, D_MODEL), BETA * D_EXPERT ** -0.5),
        'w_s_gate': nrm((DEPTH, D_MODEL, D_SHARED), f),
        'w_s_up': nrm((DEPTH, D_MODEL, D_SHARED), f),
        'w_s_down': nrm((DEPTH, D_SHARED, D_MODEL), BETA * D_SHARED ** -0.5),
    }


def reference(x_prompt, x_sample, c, cache_a_k, cache_a_v, cache_b_k, cache_b_v, state_c, cache_d_k, cache_d_v,
              c_ctx, w_mod, b_mod, w_in, a_sink, b_lambda, c_conv_w, c_conv_b, c_w_rgate, c_b_rgate,
              c_w_igate, c_b_igate, c_lambda, d_rpb, g_mix, w_out, ln_g, ln_b, w_router, b_router,
              w_e_gate, w_e_up, w_e_down, w_s_gate, w_s_up, w_s_down):
    layers = [dict(w_mod=w_mod[l], b_mod=b_mod[l], w_in=w_in[l], a_sink=a_sink[l], b_lambda=b_lambda[l],
                   c_conv_w=c_conv_w[l], c_conv_b=c_conv_b[l], c_w_rgate=c_w_rgate[l], c_b_rgate=c_b_rgate[l],
                   c_w_igate=c_w_igate[l], c_b_igate=c_b_igate[l], c_lambda=c_lambda[l], d_rpb=d_rpb[l],
                   g_mix=g_mix[l], w_out=w_out[l], ln_g=ln_g[l], ln_b=ln_b[l], w_router=w_router[l],
                   b_router=b_router[l], w_e_gate=w_e_gate[l], w_e_up=w_e_up[l], w_e_down=w_e_down[l],
                   w_s_gate=w_s_gate[l], w_s_up=w_s_up[l], w_s_down=w_s_down[l]) for l in range(DEPTH)]

    xp = x_prompt
    per_layer = [[] for _ in range(7)]
    for l in range(DEPTH):
        xp, ctx = _context_layer(xp, c_ctx[None, :], layers[l], _lambda_init(l))
        for lst, t in zip(per_layer, ctx):
            lst.append(t)
    new_a_k, new_a_v, new_b_k, new_b_v, new_state_c, new_d_k, new_d_v = [jnp.stack(t, axis=1) for t in per_layer]

    cos, sin = _axial_rope_tables(x_sample.shape[1], HEAD_DIM)
    xs = x_sample
    for l in range(DEPTH):
        cache = (cache_a_k[:, l], cache_a_v[:, l], cache_b_k[:, l], cache_b_v[:, l], state_c[:, l],
                 cache_d_k[:, l], cache_d_v[:, l])
        xs = _latent_layer(xs, c, cache, layers[l], _lambda_init(l), cos, sin)

    return (xp, xs, new_a_k, new_a_v, new_b_k, new_b_v, new_state_c, new_d_k, new_d_v)
```

```python
import functools
import math

import numpy as np
import jax
import jax.numpy as jnp
from jax import lax
from jax.experimental import pallas as pl
from jax.experimental.pallas import tpu as pltpu

F32 = jnp.float32
BF16 = jnp.bfloat16

D_MODEL = 2048
BATCH = 16
SEQ = 256
DEPTH = 2
DEC_BATCH = 2
DEC_SEQ = 4096
PAST_LEN = 256
GRID_W = 64
HEAD_DIM = 64
ROPE_BASE = 10000.0
A_HEADS = 8
A_KV_HEADS = 2
A_WINDOW = 128
B_HEADS = 4
C_WIDTH = 512
C_BLOCKS = 8
C_BW = C_WIDTH // C_BLOCKS
C_POW = 8.0
D_HEADS = 8
NA_ROWS = 8
NA_COLS = 16
N_EXPERTS = 64
TOP_K = 6
D_EXPERT = 512
ROUTED_SCALE = 2.5
ALPHA = (2.0 * DEPTH) ** 0.25
LN_EPS = 1e-5
NORM_EPS = 1e-6
NEG_INF = -1e30
SCALE = HEAD_DIM ** -0.5

N_CTX = BATCH * SEQ
N_LAT = DEC_BATCH * DEC_SEQ
N_TOK = N_CTX + N_LAT
GROUP_ROWS = 4096
N_GROUPS = N_TOK // GROUP_ROWS
IN_WIDTH = 4864
LANES = 128
VMEM_LIMIT = 56 * 1024 * 1024

COL_QA, COL_QB, COL_KB, COL_VB, COL_XC, COL_GC, COL_QD, COL_KD, COL_VD = range(9)
COL_KA, COL_VA = 36, 37

TM_PROJ = 512
TN_PROJ = IN_WIDTH // 2
TM_PREP = 256
TQ_A = 128
TQ_B = 128
NA_QR = 4
NA_KR = 12
T_SCAN = 256
TM_OUT = 256
MOE_BM = 256
TM_FFN = 512


def _params(*sem):
    return pltpu.CompilerParams(dimension_semantics=sem, vmem_limit_bytes=VMEM_LIMIT)


def _nt(a, b):
    return lax.dot_general(a, b, (((1,), (1,)), ((), ())), preferred_element_type=F32)


def _mm(a, b):
    return jnp.dot(a, b, preferred_element_type=F32)


def _softmax_unnorm(parts, sink=None):
    m = parts[0].max(axis=-1, keepdims=True)
    for p in parts[1:]:
        m = jnp.maximum(m, p.max(axis=-1, keepdims=True))
    if sink is not None:
        m = jnp.maximum(m, sink)
    es = [jnp.exp(p - m) for p in parts]
    den = es[0].sum(axis=-1, keepdims=True)
    for e in es[1:]:
        den = den + e.sum(axis=-1, keepdims=True)
    if sink is not None:
        den = den + jnp.exp(sink - m)
    return es, den


def _rms_gain(x, g):
    return x * lax.rsqrt(jnp.mean(x * x, axis=-1, keepdims=True) + NORM_EPS) * g


def _layernorm(z, g, b):
    zc = z - jnp.mean(z, axis=-1, keepdims=True)
    var = jnp.mean(zc * zc, axis=-1, keepdims=True)
    return zc * lax.rsqrt(var + LN_EPS) * g + b


def _silu(x):
    return x * jax.nn.sigmoid(x)


def _low_half(shape):
    return lax.broadcasted_iota(jnp.int32, shape, 1) < HEAD_DIM


def _mod_kernel(c_ref, w_ref, b_ref, o_ref):
    c = c_ref[...]
    o_ref[0] = jnp.dot(_silu(c), w_ref[0], precision=lax.Precision.HIGHEST,
                       preferred_element_type=F32) + b_ref[0]


def _modulation(cond8, w_mod, b_mod):
    tn = 1024
    n = w_mod.shape[-1]
    return pl.pallas_call(
        _mod_kernel,
        grid=(DEPTH, n // tn),
        in_specs=[pl.BlockSpec((8, D_MODEL), lambda l, j: (0, 0)),
                  pl.BlockSpec((1, D_MODEL, tn), lambda l, j: (l, 0, j)),
                  pl.BlockSpec((1, 1, tn), lambda l, j: (l, 0, j))],
        out_specs=pl.BlockSpec((1, 8, tn), lambda l, j: (l, 0, j)),
        out_shape=jax.ShapeDtypeStruct((DEPTH, 8, n), F32),
        compiler_params=_params("parallel", "parallel"),
        name="modulation",
    )(cond8, w_mod, b_mod.reshape(DEPTH, 1, n))


def _mod_spec(which, tm):
    return pl.BlockSpec((None, None, 1, D_MODEL), lambda i, *_: (i * tm // GROUP_ROWS, which, 0, 0))


def _inproj_kernel(x_ref, sc_ref, sh_ref, w_ref, o_ref, xb_ref):
    @pl.when(pl.program_id(1) == 0)
    def _():
        xb_ref[...] = (x_ref[...] * (1.0 + sc_ref[...]) + sh_ref[...]).astype(BF16)

    o_ref[...] = _mm(xb_ref[...], w_ref[...])


def _in_projection(x, mod, w_in_b):
    tm, tn = TM_PROJ, TN_PROJ
    return pl.pallas_call(
        _inproj_kernel,
        grid=(N_TOK // tm, IN_WIDTH // tn),
        in_specs=[pl.BlockSpec((tm, D_MODEL), lambda i, j: (i, 0)),
                  _mod_spec(1, tm), _mod_spec(0, tm),
                  pl.BlockSpec((D_MODEL, tn), lambda i, j: (0, j))],
        out_specs=pl.BlockSpec((tm, tn), lambda i, j: (i, j)),
        out_shape=jax.ShapeDtypeStruct((N_TOK, IN_WIDTH), F32),
        scratch_shapes=[pltpu.VMEM((tm, D_MODEL), BF16)],
        compiler_params=_params("parallel", "arbitrary"),
        name="in_projection",
    )(x, mod, mod, w_in_b)


def _ctx_attn_kernel(scal_ref, qa_ref, ka_ref, va_ref, qb_ref, kb_ref, vb_ref, qd_ref, kd_ref, vd_ref,
                     ga_ref, gb_ref, gd_ref, oa_ref, ob_ref, od_ref, *, lam_init):
    L = SEQ
    lo = _low_half((L, LANES))

    ka = ka_ref[...].astype(BF16)
    va = va_ref[...].astype(BF16)
    qa = qa_ref[...]
    chunks = []
    for c in range(A_HEADS // 2):
        hk = c // 2
        chunk = qa[:, c * LANES:(c + 1) * LANES]
        keep = lo if hk == 0 else jnp.logical_not(lo)
        halves = []
        for half in range(2):
            x = chunk if half == hk else pltpu.roll(chunk, HEAD_DIM, 1)
            qh = jnp.where(keep, x, 0.0).astype(BF16)
            s = _nt(qh, ka) * SCALE
            (e,), den = _softmax_unnorm([s], scal_ref[2 * c + half])
            o = _mm(e.astype(BF16), va) / den
            halves.append(o if half == hk else pltpu.roll(o, HEAD_DIM, 1))
        chunks.append(jnp.where(lo, halves[0], halves[1]))
    oa = jnp.concatenate(chunks, axis=1)
    oa_ref[...] = _rms_gain(oa, ga_ref[...]).astype(BF16)

    lam = scal_ref[A_HEADS]
    qb = qb_ref[...]
    chunks = []
    for h in range(B_HEADS):
        sl = slice(h * LANES, (h + 1) * LANES)
        qc = qb[:, sl]
        kc = kb_ref[:, sl].astype(BF16)
        vh = vb_ref[:, sl].astype(BF16)
        (e1,), d1 = _softmax_unnorm([_nt(jnp.where(lo, qc, 0.0).astype(BF16), kc) * SCALE])
        (e2,), d2 = _softmax_unnorm([_nt(jnp.where(lo, 0.0, qc).astype(BF16), kc) * SCALE])
        w = e1 * (1.0 / d1) - e2 * (lam / d2)
        o = _mm(w.astype(BF16), vh)
        chunks.append(_rms_gain(o, gb_ref[:, sl]) * (1.0 - lam_init))
    ob_ref[...] = jnp.concatenate(chunks, axis=1).astype(BF16)

    qd = qd_ref[...]
    chunks = []
    for c in range(D_HEADS // 2):
        sl = slice(c * LANES, (c + 1) * LANES)
        qc = qd[:, sl]
        kc = kd_ref[:, sl].astype(BF16)
        vc = vd_ref[:, sl].astype(BF16)
        halves = []
        for half in range(2):
            qh = jnp.where(lo if half == 0 else jnp.logical_not(lo), qc, 0.0).astype(BF16)
            (e,), den = _softmax_unnorm([_nt(qh, kc) * SCALE])
            halves.append(_mm(e.astype(BF16), vc) / den)
        chunks.append(jnp.where(lo, halves[0], halves[1]))
    od = jnp.concatenate(chunks, axis=1)
    od_ref[...] = _rms_gain(od, gd_ref[...]).astype(BF16)


def _ctx_attention(slab, scal, g_mix2d, lam_init):
    L = SEQ

    def wide(col):
        return pl.BlockSpec((L, 512), lambda b: (b, col))

    def narrow(col):
        return pl.BlockSpec((L, LANES), lambda b: (b, col))

    def gain(col):
        return pl.BlockSpec((1, 512), lambda b: (0, col))

    out = jax.ShapeDtypeStruct((N_CTX, 512), BF16)
    ospec = pl.BlockSpec((L, 512), lambda b: (b, 0))
    return pl.pallas_call(
        functools.partial(_ctx_attn_kernel, lam_init=lam_init),
        grid=(BATCH,),
        in_specs=[pl.BlockSpec(memory_space=pltpu.SMEM),
                  wide(COL_QA), narrow(COL_KA), narrow(COL_VA),
                  wide(COL_QB), wide(COL_KB), wide(COL_VB),
                  wide(COL_QD), wide(COL_KD), wide(COL_VD),
                  gain(0), gain(1), gain(3)],
        out_specs=[ospec, ospec, ospec],
        out_shape=[out, out, out],
        compiler_params=_params("parallel"),
        name="ctx_attention",
    )(scal, slab, slab, slab, slab, slab, slab, slab, slab, slab, g_mix2d, g_mix2d, g_mix2d)


def _prep_kernel(qa_ref, ka_ref, va_ref, qb_ref, kb_ref, vb_ref, qd_ref, kd_ref, vd_ref,
                 cos_ref, sa_ref, sb_ref,
                 qa_o, ka_o, va_o, qb_o, kb_o, vb_o, qd_o, kd_o, vd_o):
    cos, sa, sb = cos_ref[...], sa_ref[...], sb_ref[...]
    lo = _low_half(cos.shape)
    hi = jnp.logical_not(lo)

    def rope(x):
        return (x * cos + pltpu.roll(x, LANES - HEAD_DIM // 4, 1) * sa
                + pltpu.roll(x, HEAD_DIM // 4, 1) * sb)

    ka_o[...] = rope(ka_ref[...]).astype(BF16)
    va_o[...] = va_ref[...].astype(BF16)
    vb_o[...] = vb_ref[...].astype(BF16)
    kd_o[...] = kd_ref[...].astype(BF16)
    vd_o[...] = vd_ref[...].astype(BF16)

    for c in range(A_HEADS // 2):
        hk = c // 2
        r = rope(qa_ref[:, c * LANES:(c + 1) * LANES])
        keep = lo if hk == 0 else hi
        for half in range(2):
            h = 2 * c + half
            x = r if half == hk else pltpu.roll(r, HEAD_DIM, 1)
            qa_o[:, h * LANES:(h + 1) * LANES] = jnp.where(keep, x, 0.0).astype(BF16)
    for h in range(B_HEADS):
        sl = slice(h * LANES, (h + 1) * LANES)
        r = rope(qb_ref[:, sl])
        kb_o[:, sl] = rope(kb_ref[:, sl]).astype(BF16)
        qb_o[:, (2 * h) * LANES:(2 * h + 1) * LANES] = jnp.where(lo, r, 0.0).astype(BF16)
        qb_o[:, (2 * h + 1) * LANES:(2 * h + 2) * LANES] = jnp.where(hi, r, 0.0).astype(BF16)
    for c in range(D_HEADS // 2):
        x = qd_ref[:, c * LANES:(c + 1) * LANES]
        qd_o[:, (2 * c) * LANES:(2 * c + 1) * LANES] = jnp.where(lo, x, 0.0).astype(BF16)
        qd_o[:, (2 * c + 1) * LANES:(2 * c + 2) * LANES] = jnp.where(hi, x, 0.0).astype(BF16)


def _latent_prep(slab, cos, sa, sb):
    tm = TM_PREP
    off = N_CTX // tm
    per_seq = DEC_SEQ // tm

    def wide(col):
        return pl.BlockSpec((tm, 512), lambda i: (i + off, col))

    def narrow(col):
        return pl.BlockSpec((tm, LANES), lambda i: (i + off, col))

    tab = pl.BlockSpec((tm, LANES), lambda i: (i % per_seq, 0))

    def out(width):
        return (pl.BlockSpec((tm, width), lambda i: (i, 0)), jax.ShapeDtypeStruct((N_LAT, width), BF16))

    outs = [out(1024), out(LANES), out(LANES), out(1024), out(512), out(512), out(1024), out(512), out(512)]
    return pl.pallas_call(
        _prep_kernel,
        grid=(N_LAT // tm,),
        in_specs=[wide(COL_QA), narrow(COL_KA), narrow(COL_VA), wide(COL_QB), wide(COL_KB), wide(COL_VB),
                  wide(COL_QD), wide(COL_KD), wide(COL_VD), tab, tab, tab],
        out_specs=[o[0] for o in outs],
        out_shape=[o[1] for o in outs],
        compiler_params=_params("parallel"),
        name="latent_prep",
    )(slab, slab, slab, slab, slab, slab, slab, slab, slab, cos, sa, sb)


def _win_attn_kernel(scal_ref, q_ref, kp_ref, kc_ref, kn_ref, vp_ref, vc_ref, vn_ref, kctx_ref, vctx_ref,
                     g_ref, o_ref):
    n = pl.program_id(1)
    nb = pl.num_programs(1)
    tq = TQ_A
    group = A_HEADS // A_KV_HEADS
    k = jnp.concatenate([kp_ref[...], kc_ref[...], kn_ref[...], kctx_ref[...].astype(BF16)], axis=0)
    v = jnp.concatenate([vp_ref[...], vc_ref[...], vn_ref[...], vctx_ref[...].astype(BF16)], axis=0)
    nk = 3 * tq + PAST_LEN
    row = lax.broadcasted_iota(jnp.int32, (group * tq, nk), 0) & (tq - 1)
    col = lax.broadcasted_iota(jnp.int32, (group * tq, nk), 1)
    prev_thr = row + jnp.where(n > 0, 0, tq)
    next_thr = row + 2 * tq - jnp.where(n < nb - 1, 0, tq)
    masked = ((col < tq) & (col < prev_thr)) | ((col >= 2 * tq) & (col < 3 * tq) & (col > next_thr))
    valid = jnp.logical_not(masked)
    lo = _low_half((tq, LANES))
    heads = []
    for hk in range(A_KV_HEADS):
        q4 = jnp.concatenate([q_ref[:, (hk * group + g) * LANES:(hk * group + g + 1) * LANES]
                              for g in range(group)], axis=0)
        s = jnp.where(valid, _nt(q4, k) * SCALE, NEG_INF)
        sink = jnp.concatenate([jnp.full((tq, 1), scal_ref[hk * group + g], F32) for g in range(group)], axis=0)
        (e,), den = _softmax_unnorm([s], sink)
        o = _mm(e.astype(BF16), v) / den
        for g in range(group):
            h = hk * group + g
            og = o[g * tq:(g + 1) * tq]
            heads.append(og if (h % 2) == hk else pltpu.roll(og, HEAD_DIM, 1))
    oa = jnp.concatenate([jnp.where(lo, heads[2 * c], heads[2 * c + 1]) for c in range(A_HEADS // 2)], axis=1)
    o_ref[...] = _rms_gain(oa, g_ref[...]).astype(BF16)


def _window_attention(qa_p, ka_r, va_b, cache_k, cache_v, scal, g_mix2d):
    tq = TQ_A
    nb = DEC_SEQ // tq

    def band(d):
        return pl.BlockSpec((tq, LANES), lambda b, n: (b * nb + jnp.clip(n + d, 0, nb - 1), 0))

    ctx = pl.BlockSpec((None, PAST_LEN, LANES), lambda b, n: (b, 0, 0))
    return pl.pallas_call(
        _win_attn_kernel,
        grid=(DEC_BATCH, nb),
        in_specs=[pl.BlockSpec(memory_space=pltpu.SMEM),
                  pl.BlockSpec((tq, A_HEADS * LANES), lambda b, n: (b * nb + n, 0)),
                  band(-1), band(0), band(1), band(-1), band(0), band(1), ctx, ctx,
                  pl.BlockSpec((1, 512), lambda b, n: (0, 0))],
        out_specs=pl.BlockSpec((tq, 512), lambda b, n: (b * nb + n, 0)),
        out_shape=jax.ShapeDtypeStruct((N_LAT, 512), BF16),
        compiler_params=_params("parallel", "parallel"),
        name="window_attention",
    )(scal, qa_p, ka_r, ka_r, ka_r, va_b, va_b, va_b, cache_k, cache_v, g_mix2d)


def _diff_attn_kernel(scal_ref, q_ref, k_ref, v_ref, kctx_ref, vctx_ref, g_ref, o_ref, *, lam_init):
    lam = scal_ref[A_HEADS]
    chunks = []
    for h in range(B_HEADS):
        sl = slice(h * LANES, (h + 1) * LANES)
        kh = k_ref[:, sl]
        kc = kctx_ref[:, sl].astype(BF16)
        es, dens = [], []
        for m in range(2):
            q = q_ref[:, (2 * h + m) * LANES:(2 * h + m + 1) * LANES]
            e, d = _softmax_unnorm([_nt(q, kh) * SCALE, _nt(q, kc) * SCALE])
            es.append(e)
            dens.append(d)
        c1 = 1.0 / dens[0]
        c2 = lam / dens[1]
        w_lat = (es[0][0] * c1 - es[1][0] * c2).astype(BF16)
        w_ctx = (es[0][1] * c1 - es[1][1] * c2).astype(BF16)
        o = _mm(w_lat, v_ref[:, sl]) + _mm(w_ctx, vctx_ref[:, sl].astype(BF16))
        chunks.append(_rms_gain(o, g_ref[:, sl]) * (1.0 - lam_init))
    o_ref[...] = jnp.concatenate(chunks, axis=1).astype(BF16)


def _diff_attention(qb_p, kb_r, vb_b, cache_k, cache_v, scal, g_mix2d, lam_init):
    tq = TQ_B
    nb = DEC_SEQ // tq
    full = pl.BlockSpec((DEC_SEQ, 512), lambda b, n: (b, 0))
    ctx = pl.BlockSpec((None, PAST_LEN, 512), lambda b, n: (b, 0, 0))
    return pl.pallas_call(
        functools.partial(_diff_attn_kernel, lam_init=lam_init),
        grid=(DEC_BATCH, nb),
        in_specs=[pl.BlockSpec(memory_space=pltpu.SMEM),
                  pl.BlockSpec((tq, 2 * B_HEADS * LANES), lambda b, n: (b * nb + n, 0)),
                  full, full, ctx, ctx,
                  pl.BlockSpec((1, 512), lambda b, n: (0, 1))],
        out_specs=pl.BlockSpec((tq, 512), lambda b, n: (b * nb + n, 0)),
        out_shape=jax.ShapeDtypeStruct((N_LAT, 512), BF16),
        compiler_params=_params("parallel", "parallel"),
        name="diff_attention",
    )(scal, qb_p, kb_r, vb_b, cache_k, cache_v, g_mix2d)


def _na_window_start(step):
    return np.clip(step * NA_QR - NA_ROWS // 2, 0, GRID_W - NA_KR)


def _na_bias_indices():
    rows = DEC_SEQ // GRID_W
    steps = rows // NA_QR
    pats = []
    for step in range(steps):
        w0 = _na_window_start(step)
        r = step * NA_QR + np.arange(NA_QR)[:, None, None, None]
        c = np.arange(GRID_W)[None, :, None, None]
        kr = w0 + np.arange(NA_KR)[None, None, :, None]
        kc = np.arange(GRID_W)[None, None, None, :]
        kr0 = np.clip(r - NA_ROWS // 2, 0, rows - NA_ROWS)
        kc0 = np.clip(c - NA_COLS // 2, 0, GRID_W - NA_COLS)
        valid = (kr >= kr0) & (kr < kr0 + NA_ROWS) & (kc >= kc0) & (kc < kc0 + NA_COLS)
        drow = np.clip(kr - r + NA_ROWS - 1, 0, 2 * NA_ROWS - 2)
        dcol = np.clip(kc - c + NA_COLS - 1, 0, 2 * NA_COLS - 2)
        shape = (NA_QR * GRID_W, NA_KR * GRID_W)
        full = np.broadcast_to
        pats.append((full(drow, valid.shape).reshape(shape), full(dcol, valid.shape).reshape(shape),
                     valid.reshape(shape)))
    for step in range(2, steps - 1):
        for a, b in zip(pats[1], pats[step]):
            assert np.array_equal(a, b)
    kinds = [pats[0], pats[1], pats[-1]]
    return tuple(np.stack([k[i] for k in kinds]) for i in range(3))


def _na_kernel(q_ref, k_ref, v_ref, kctx_ref, vctx_ref, bias_ref, g_ref, o_ref):
    step = pl.program_id(1)
    nq = NA_QR * GRID_W
    nk = NA_KR * GRID_W
    w0 = jnp.clip(step * NA_QR - NA_ROWS // 2, 0, GRID_W - NA_KR)
    start = pl.multiple_of(w0 * GRID_W, GRID_W)
    lo = _low_half((nq, LANES))
    chunks = []
    for c in range(D_HEADS // 2):
        sl = slice(c * LANES, (c + 1) * LANES)
        kw = k_ref[pl.ds(start, nk), sl]
        vw = v_ref[pl.ds(start, nk), sl]
        kc = kctx_ref[:, sl].astype(BF16)
        vc = vctx_ref[:, sl].astype(BF16)
        halves = []
        for half in range(2):
            h = 2 * c + half
            q = q_ref[:, h * LANES:(h + 1) * LANES]
            s_loc = _nt(q, kw) * SCALE + bias_ref[h]
            s_ctx = _nt(q, kc) * SCALE
            (e_loc, e_ctx), den = _softmax_unnorm([s_loc, s_ctx])
            halves.append((_mm(e_loc.astype(BF16), vw) + _mm(e_ctx.astype(BF16), vc)) / den)
        chunks.append(jnp.where(lo, halves[0], halves[1]))
    od = jnp.concatenate(chunks, axis=1)
    o_ref[...] = _rms_gain(od, g_ref[...]).astype(BF16)


def _neighbourhood_attention(qd_p, kd_b, vd_b, cache_k, cache_v, bias, g_mix2d):
    nq = NA_QR * GRID_W
    steps = DEC_SEQ // nq
    full = pl.BlockSpec((DEC_SEQ, 512), lambda b, j: (b, 0))
    ctx = pl.BlockSpec((None, PAST_LEN, 512), lambda b, j: (b, 0, 0))

    def kind(b, j):
        return (jnp.where(j == 0, 0, jnp.where(j == steps - 1, 2, 1)), 0, 0, 0)

    return pl.pallas_call(
        _na_kernel,
        grid=(DEC_BATCH, steps),
        in_specs=[pl.BlockSpec((nq, D_HEADS * LANES), lambda b, j: (b * steps + j, 0)),
                  full, full, ctx, ctx,
                  pl.BlockSpec((None, D_HEADS, nq, NA_KR * GRID_W), kind),
                  pl.BlockSpec((1, 512), lambda b, j: (0, 3))],
        out_specs=pl.BlockSpec((nq, 512), lambda b, j: (b * steps + j, 0)),
        out_shape=jax.ShapeDtypeStruct((N_LAT, 512), BF16),
        compiler_params=_params("parallel", "arbitrary"),
        name="neighbourhood_attention",
    )(qd_p, kd_b, vd_b, cache_k, cache_v, bias, g_mix2d)


N_SCAN_TILES = N_TOK // T_SCAN
CTX_TILES = N_CTX // T_SCAN
TILES_PER_LAT = DEC_SEQ // T_SCAN


def _scan_tile_flags(i):
    is_ctx = i < CTX_TILES
    pos = (i - CTX_TILES) % TILES_PER_LAT
    return is_ctx | (pos == 0), is_ctx | (pos == TILES_PER_LAT - 1)


def _scan_seq(i):
    return jnp.where(i < CTX_TILES, i, CTX_TILES + (i - CTX_TILES) // TILES_PER_LAT)


def _conv_gates(x_ref, prev_ref, next_ref, cw_ref, cb_ref, wg_ref, bg_ref, lam_ref, first, last):
    x = x_ref[...]
    t = T_SCAN
    row = lax.broadcasted_iota(jnp.int32, x.shape, 0)
    pm1 = jnp.where(first, 0.0, prev_ref[7:8, :])
    n0 = jnp.where(last, 0.0, next_ref[0:1, :])
    n1 = jnp.where(last, 0.0, next_ref[1:2, :])
    x_m1 = jnp.where(row == 0, pm1, pltpu.roll(x, 1, 0))
    x_p1 = jnp.where(row == t - 1, n0, pltpu.roll(x, t - 1, 0))
    x_p2 = jnp.where(row == t - 2, n0, jnp.where(row == t - 1, n1, pltpu.roll(x, t - 2, 0)))
    y = (cw_ref[0:1, :] * x_m1 + cw_ref[1:2, :] * x + cw_ref[2:3, :] * x_p1 + cw_ref[3:4, :] * x_p2
         + cb_ref[...])
    z = jnp.dot(y, wg_ref[0], precision=lax.Precision.HIGHEST, preferred_element_type=F32) + bg_ref[0]
    r = jax.nn.sigmoid(z[:, :C_WIDTH])
    gate_i = jax.nn.sigmoid(z[:, C_WIDTH:])
    nl = -lam_ref[0]
    softplus = jnp.maximum(nl, 0.0) + jnp.log1p(jnp.exp(-jnp.abs(nl)))
    log_a = -C_POW * r * softplus
    a = jnp.exp(log_a)
    u = jnp.sqrt(1.0 - jnp.exp(2.0 * log_a)) * gate_i * y
    return a, u


def _scan_rows(a_s, u_s, h_s, h, reverse):
    groups = T_SCAN // 8

    def body(i, h):
        g = (groups - 1 - i) if reverse else i
        base = pl.multiple_of(g * 8, 8)
        a8 = a_s[pl.ds(base, 8), :]
        u8 = u_s[pl.ds(base, 8), :]
        rows = [None] * 8
        for k in (range(7, -1, -1) if reverse else range(8)):
            h = a8[k:k + 1, :] * h + u8[k:k + 1, :]
            rows[k] = h
        h_s[pl.ds(base, 8), :] = jnp.concatenate(rows, axis=0)
        return h

    return lax.fori_loop(0, groups, body, h)


def _scan_fwd_kernel(x_ref, prev_ref, next_ref, cw_ref, cb_ref, wg_ref, bg_ref, lam_ref, h0_ref,
                     hf_ref, fin_ref, a_s, u_s, carry_s):
    i = pl.program_id(0)
    first, last = _scan_tile_flags(i)
    a, u = _conv_gates(x_ref, prev_ref, next_ref, cw_ref, cb_ref, wg_ref, bg_ref, lam_ref, first, last)
    a_s[...] = a
    u_s[...] = u

    @pl.when(first)
    def _():
        carry_s[...] = h0_ref[...]

    h = _scan_rows(a_s, u_s, hf_ref, carry_s[...], reverse=False)
    carry_s[...] = h
    fin_ref[...] = h


def _scan_bwd_kernel(x_ref, prev_ref, next_ref, cw_ref, cb_ref, wg_ref, bg_ref, lam_ref, h0_ref,
                     hf_ref, gc_ref, g_ref, oc_ref, fin_ref, a_s, u_s, hb_s, carry_s):
    i = N_SCAN_TILES - 1 - pl.program_id(0)
    first, last = _scan_tile_flags(i)
    a, u = _conv_gates(x_ref, prev_ref, next_ref, cw_ref, cb_ref, wg_ref, bg_ref, lam_ref, first, last)
    a_s[...] = a
    u_s[...] = u

    @pl.when(last)
    def _():
        carry_s[...] = h0_ref[...]

    h = _scan_rows(a_s, u_s, hb_s, carry_s[...], reverse=True)
    carry_s[...] = h
    fin_ref[...] = h
    g = gc_ref[...]
    gelu = 0.5 * g * (1.0 + jnp.tanh(math.sqrt(2.0 / math.pi) * (g + 0.044715 * (g * g * g))))
    oc_ref[...] = _rms_gain(gelu * (hf_ref[...] + hb_s[...]), g_ref[...]).astype(BF16)


def _scan_common_specs(tile_of, direction):
    halo = T_SCAN // 8
    last_block = N_TOK // 8 - 1
    return [pl.BlockSpec((T_SCAN, 512), lambda j: (tile_of(j), COL_XC)),
            pl.BlockSpec((8, 512), lambda j: (jnp.maximum(tile_of(j) * halo - 1, 0), COL_XC)),
            pl.BlockSpec((8, 512), lambda j: (jnp.minimum((tile_of(j) + 1) * halo, last_block), COL_XC)),
            pl.BlockSpec((4, C_WIDTH), lambda j: (0, 0)),
            pl.BlockSpec((1, C_WIDTH), lambda j: (0, 0)),
            pl.BlockSpec((1, C_WIDTH, 2 * C_WIDTH), lambda j: (direction, 0, 0)),
            pl.BlockSpec((1, 1, 2 * C_WIDTH), lambda j: (direction, 0, 0)),
            pl.BlockSpec((1, 1, C_WIDTH), lambda j: (direction, 0, 0)),
            pl.BlockSpec((None, 1, C_WIDTH), lambda j: (_scan_seq(tile_of(j)), 0, 0))]


def _rglru(slab, conv_w, conv_b, w_gates, b_gates, lam, h0_f, h0_b, g_mix2d):
    fin = jax.ShapeDtypeStruct((N_SCAN_TILES, 1, C_WIDTH), F32)
    fwd_tile = lambda j: j
    h_f, fin_f = pl.pallas_call(
        _scan_fwd_kernel,
        grid=(N_SCAN_TILES,),
        in_specs=_scan_common_specs(fwd_tile, 0),
        out_specs=[pl.BlockSpec((T_SCAN, C_WIDTH), lambda j: (j, 0)),
                   pl.BlockSpec((None, 1, C_WIDTH), lambda j: (j, 0, 0))],
        out_shape=[jax.ShapeDtypeStruct((N_TOK, C_WIDTH), F32), fin],
        scratch_shapes=[pltpu.VMEM((T_SCAN, C_WIDTH), F32), pltpu.VMEM((T_SCAN, C_WIDTH), F32),
                        pltpu.VMEM((1, C_WIDTH), F32)],
        compiler_params=_params("arbitrary"),
        name="rglru_forward",
    )(slab, slab, slab, conv_w, conv_b, w_gates, b_gates, lam, h0_f)
    bwd_tile = lambda j: N_SCAN_TILES - 1 - j
    oc, fin_b = pl.pallas_call(
        _scan_bwd_kernel,
        grid=(N_SCAN_TILES,),
        in_specs=_scan_common_specs(bwd_tile, 1) + [
            pl.BlockSpec((T_SCAN, C_WIDTH), lambda j: (bwd_tile(j), 0)),
            pl.BlockSpec((T_SCAN, 512), lambda j: (bwd_tile(j), COL_GC)),
            pl.BlockSpec((1, 512), lambda j: (0, 2))],
        out_specs=[pl.BlockSpec((T_SCAN, C_WIDTH), lambda j: (bwd_tile(j), 0)),
                   pl.BlockSpec((None, 1, C_WIDTH), lambda j: (bwd_tile(j), 0, 0))],
        out_shape=[jax.ShapeDtypeStruct((N_TOK, C_WIDTH), BF16), fin],
        scratch_shapes=[pltpu.VMEM((T_SCAN, C_WIDTH), F32), pltpu.VMEM((T_SCAN, C_WIDTH), F32),
                        pltpu.VMEM((T_SCAN, C_WIDTH), F32), pltpu.VMEM((1, C_WIDTH), F32)],
        compiler_params=_params("arbitrary"),
        name="rglru_backward",
    )(slab, slab, slab, conv_w, conv_b, w_gates, b_gates, lam, h0_b, h_f, slab, g_mix2d)
    return oc, fin_f, fin_b


def _mixout_kernel(oa_ref, ob_ref, oc_ref, od_ref, w_ref, x_ref, g1_ref, sc2_ref, sh2_ref, lng_ref, lnb_ref,
                   wr_ref, br_ref, x1_ref, h2_ref, idx_ref, wt_ref):
    y = _mm(oa_ref[...], w_ref[0:512, :])
    y += _mm(ob_ref[...], w_ref[512:1024, :])
    y += _mm(oc_ref[...], w_ref[1024:1536, :])
    y += _mm(od_ref[...], w_ref[1536:2048, :])
    x1 = _layernorm(ALPHA * x_ref[...] + g1_ref[...] * y, lng_ref[...], lnb_ref[...])
    x1_ref[...] = x1
    h2 = x1 * (1.0 + sc2_ref[...]) + sh2_ref[...]
    h2_ref[...] = h2.astype(BF16)

    scores = jax.nn.sigmoid(jnp.dot(h2, wr_ref[...], precision=lax.Precision.HIGHEST,
                                    preferred_element_type=F32))
    sel = scores + br_ref[...]
    lane = lax.broadcasted_iota(jnp.int32, sel.shape, 1).astype(F32)
    slot = lax.broadcasted_iota(jnp.int32, idx_ref.shape, 1)
    idx_out = jnp.zeros(idx_ref.shape, F32)
    wt_out = jnp.zeros(wt_ref.shape, F32)
    total = jnp.zeros((sel.shape[0], 1), F32)
    for k in range(TOP_K):
        m = sel.max(axis=-1, keepdims=True)
        idx = jnp.where(sel == m, lane, float(N_EXPERTS)).min(axis=-1, keepdims=True)
        hit = lane == idx
        w = jnp.where(hit, scores, 0.0).sum(axis=-1, keepdims=True)
        total = total + w
        idx_out = jnp.where(slot == k, idx, idx_out)
        wt_out = jnp.where(slot == k, w, wt_out)
        sel = jnp.where(hit, -jnp.inf, sel)
    idx_ref[...] = idx_out.astype(jnp.int32)
    wt_ref[...] = wt_out / total * ROUTED_SCALE


def _mixer_out(oa, ob, oc, od, w_out_b, x, mod, ln_g, ln_b, w_router, b_router):
    tm = TM_OUT

    def cat_spec():
        ctx_tiles = N_CTX // tm
        return pl.BlockSpec((tm, 512), lambda i: (i, 0))

    row = pl.BlockSpec((tm, D_MODEL), lambda i: (i, 0))
    vec = pl.BlockSpec((1, D_MODEL), lambda i: (0, 0))
    k8 = pl.BlockSpec((tm, 8), lambda i: (i, 0))
    return pl.pallas_call(
        _mixout_kernel,
        grid=(N_TOK // tm,),
        in_specs=[cat_spec(), cat_spec(), cat_spec(), cat_spec(),
                  pl.BlockSpec((D_MODEL, D_MODEL), lambda i: (0, 0)),
                  row, _mod_spec(2, tm), _mod_spec(4, tm), _mod_spec(3, tm), vec, vec,
                  pl.BlockSpec((D_MODEL, N_EXPERTS), lambda i: (0, 0)),
                  pl.BlockSpec((1, N_EXPERTS), lambda i: (0, 0))],
        out_specs=[row, row, k8, k8],
        out_shape=[jax.ShapeDtypeStruct((N_TOK, D_MODEL), F32), jax.ShapeDtypeStruct((N_TOK, D_MODEL), BF16),
                   jax.ShapeDtypeStruct((N_TOK, 8), jnp.int32), jax.ShapeDtypeStruct((N_TOK, 8), F32)],
        compiler_params=_params("parallel"),
        name="mixer_out_router",
    )(oa, ob, oc, od, w_out_b, x, mod, mod, mod, ln_g, ln_b, w_router, b_router)


def _expert_kernel(blk_e_ref, nused_ref, x_ref, wt_ref, wg_ref, wu_ref, wd_ref, o_ref, wg_s, wu_s, wd_s):
    i = pl.program_id(0)
    prev = blk_e_ref[jnp.maximum(i - 1, 0)]

    @pl.when((i == 0) | (blk_e_ref[i] != prev))
    def _():
        wg_s[...] = wg_ref[...].astype(BF16)
        wu_s[...] = wu_ref[...].astype(BF16)
        wd_s[...] = wd_ref[...].astype(BF16)

    @pl.when(i < nused_ref[0])
    def _():
        x = x_ref[...]
        hid = _silu(_mm(x, wg_s[...])) * _mm(x, wu_s[...])
        o_ref[...] = _mm(hid.astype(BF16), wd_s[...]) * wt_ref[...]

    @pl.when(i >= nused_ref[0])
    def _():
        o_ref[...] = jnp.zeros(o_ref.shape, F32)


def _routed_experts(xg, buf_w, blk_e, nused, w_g, w_u, w_d):
    bm = MOE_BM
    nblk = xg.shape[0] // bm
    grid_spec = pltpu.PrefetchScalarGridSpec(
        num_scalar_prefetch=2,
        grid=(nblk,),
        in_specs=[pl.BlockSpec((bm, D_MODEL), lambda i, e, n: (i, 0)),
                  pl.BlockSpec((bm, 1), lambda i, e, n: (i, 0)),
                  pl.BlockSpec((None, D_MODEL, D_EXPERT), lambda i, e, n: (e[i], 0, 0)),
                  pl.BlockSpec((None, D_MODEL, D_EXPERT), lambda i, e, n: (e[i], 0, 0)),
                  pl.BlockSpec((None, D_EXPERT, D_MODEL), lambda i, e, n: (e[i], 0, 0))],
        out_specs=pl.BlockSpec((bm, D_MODEL), lambda i, e, n: (i, 0)),
        scratch_shapes=[pltpu.VMEM((D_MODEL, D_EXPERT), BF16), pltpu.VMEM((D_MODEL, D_EXPERT), BF16),
                        pltpu.VMEM((D_EXPERT, D_MODEL), BF16)])
    return pl.pallas_call(
        _expert_kernel,
        grid_spec=grid_spec,
        out_shape=jax.ShapeDtypeStruct((xg.shape[0], D_MODEL), F32),
        compiler_params=_params("arbitrary"),
        name="routed_experts",
    )(blk_e, nused, xg, buf_w, w_g, w_u, w_d)


def _ffn_out_kernel(h_ref, routed_ref, x1_ref, wg_ref, wu_ref, wd_ref, g2_ref, lng_ref, lnb_ref, o_ref):
    h = h_ref[...]
    hid = _silu(_mm(h, wg_ref[...])) * _mm(h, wu_ref[...])
    y = routed_ref[...] + _mm(hid.astype(BF16), wd_ref[...])
    o_ref[...] = _layernorm(ALPHA * x1_ref[...] + g2_ref[...] * y, lng_ref[...], lnb_ref[...])


def _ffn_out(h2, routed, x1, ws_g, ws_u, ws_d, mod, ln_g, ln_b):
    tm = TM_FFN
    row = pl.BlockSpec((tm, D_MODEL), lambda i: (i, 0))
    vec = pl.BlockSpec((1, D_MODEL), lambda i: (0, 0))
    return pl.pallas_call(
        _ffn_out_kernel,
        grid=(N_TOK // tm,),
        in_specs=[row, row, row,
                  pl.BlockSpec((D_MODEL, D_EXPERT), lambda i: (0, 0)),
                  pl.BlockSpec((D_MODEL, D_EXPERT), lambda i: (0, 0)),
                  pl.BlockSpec((D_EXPERT, D_MODEL), lambda i: (0, 0)),
                  _mod_spec(5, tm), vec, vec],
        out_specs=row,
        out_shape=jax.ShapeDtypeStruct((N_TOK, D_MODEL), F32),
        compiler_params=_params("parallel"),
        name="shared_expert_out",
    )(h2, routed, x1, ws_g, ws_u, ws_d, mod, ln_g, ln_b)


def _dispatch(idx, wt):
    bm = MOE_BM
    n_assign = N_TOK * TOP_K
    nblk = n_assign // bm + N_EXPERTS
    e = idx[:, :TOP_K]
    onehot = (e[:, :, None] == jnp.arange(N_EXPERTS, dtype=jnp.int32)[None, None, :]).astype(jnp.int32)
    per_tok = onehot.sum(axis=1)
    counts = per_tok.sum(axis=0)
    padded = (counts + bm - 1) // bm * bm
    pad_end = jnp.cumsum(padded)
    pad_start = pad_end - padded
    rank = jnp.cumsum(per_tok, axis=0) - per_tok
    slot_te = pad_start[None, :] + rank
    slot = jnp.take_along_axis(slot_te, e, axis=1)
    flat = slot.reshape(-1)
    tok = jnp.repeat(jnp.arange(N_TOK, dtype=jnp.int32), TOP_K)
    buf_tok = jnp.zeros((nblk * bm,), jnp.int32).at[flat].set(tok)
    buf_w = jnp.zeros((nblk * bm,), F32).at[flat].set(wt[:, :TOP_K].reshape(-1))
    blk_e = jnp.minimum(jnp.searchsorted(pad_end, jnp.arange(nblk, dtype=jnp.int32) * bm, side='right'),
                        N_EXPERTS - 1).astype(jnp.int32)
    nused = (pad_end[-1:] // bm).astype(jnp.int32)
    return buf_tok, buf_w.reshape(-1, 1), blk_e, nused, slot


def _rope_tables():
    t = jnp.arange(DEC_SEQ)
    row = (t // GRID_W).astype(F32)
    col = (t % GRID_W).astype(F32)
    nf = HEAD_DIM // 4
    inv = ROPE_BASE ** (-jnp.arange(nf, dtype=F32) / nf)
    ar = row[:, None] * inv[None, :]
    ac = col[:, None] * inv[None, :]
    ang = jnp.concatenate([ar, ar, ac, ac], axis=-1)
    ang = jnp.concatenate([ang, ang], axis=-1)
    cos, sin = jnp.cos(ang), jnp.sin(ang)
    first = (jnp.arange(LANES) % (2 * nf)) < nf
    return cos, jnp.where(first, -sin, 0.0), jnp.where(first, 0.0, sin)


def _lambda_init(l):
    return 0.8 - 0.6 * math.exp(-0.3 * l)


def _block_diag(w):
    eye = jnp.eye(C_BLOCKS, dtype=w.dtype)
    return (eye[:, None, :, None] * w[:, :, None, :]).reshape(C_WIDTH, C_WIDTH)


def kernel(x_prompt, x_sample, c, cache_a_k, cache_a_v, cache_b_k, cache_b_v, state_c, cache_d_k, cache_d_v, c_ctx, w_mod, b_mod, w_in, a_sink, b_lambda, c_conv_w, c_conv_b, c_w_rgate, c_b_rgate, c_w_igate, c_b_igate, c_lambda, d_rpb, g_mix, w_out, ln_g, ln_b, w_router, b_router, w_e_gate, w_e_up, w_e_down, w_s_gate, w_s_up, w_s_down):
    x = jnp.concatenate([x_prompt.reshape(N_CTX, D_MODEL), x_sample.reshape(N_LAT, D_MODEL)], axis=0)
    cond8 = jnp.zeros((8, D_MODEL), F32).at[0].set(c_ctx).at[1:1 + DEC_BATCH].set(c)
    mod_all = _modulation(cond8, w_mod, b_mod)
    cos, sin_a, sin_b = _rope_tables()
    drow, dcol, na_valid = _na_bias_indices()

    order = np.concatenate([np.arange(0, 512), np.arange(768, 4864), np.arange(512, 768)])

    ctx_out = [[] for _ in range(7)]
    for l in range(DEPTH):
        lam_init = _lambda_init(l)
        mod = mod_all[l, :N_GROUPS].reshape(N_GROUPS, 6, 1, D_MODEL)
        w_in_b = w_in[l][:, order].astype(BF16)
        slab = _in_projection(x, mod, w_in_b)

        lp = b_lambda[l]
        lam = jnp.exp(jnp.sum(lp[0] * lp[1])) - jnp.exp(jnp.sum(lp[2] * lp[3])) + lam_init
        scal = jnp.zeros((16,), F32).at[:A_HEADS].set(a_sink[l]).at[A_HEADS].set(lam)
        g2d = g_mix[l].reshape(1, -1)

        oa_c, ob_c, od_c = _ctx_attention(slab, scal, g2d, lam_init)

        qa_p, ka_r, va_b, qb_p, kb_r, vb_b, qd_p, kd_b, vd_b = _latent_prep(slab, cos, sin_a, sin_b)
        oa_l = _window_attention(qa_p, ka_r, va_b, cache_a_k[:, l].reshape(DEC_BATCH, PAST_LEN, LANES),
                                 cache_a_v[:, l].reshape(DEC_BATCH, PAST_LEN, LANES), scal, g2d)
        ob_l = _diff_attention(qb_p, kb_r, vb_b, cache_b_k[:, l].reshape(DEC_BATCH, PAST_LEN, 512),
                               cache_b_v[:, l].reshape(DEC_BATCH, PAST_LEN, 512), scal, g2d, lam_init)
        bias = jnp.where(na_valid[:, None], d_rpb[l][:, drow, dcol].transpose(1, 0, 2, 3), NEG_INF)
        od_l = _neighbourhood_attention(qd_p, kd_b, vd_b, cache_d_k[:, l].reshape(DEC_BATCH, PAST_LEN, 512),
                                        cache_d_v[:, l].reshape(DEC_BATCH, PAST_LEN, 512), bias, g2d)

        w_gates = jnp.stack([jnp.concatenate([_block_diag(c_w_rgate[l, d]), _block_diag(c_w_igate[l, d])], axis=1)
                             for d in range(2)])
        b_gates = jnp.concatenate([c_b_rgate[l], c_b_igate[l]], axis=-1).reshape(2, 1, 2 * C_WIDTH)
        zeros = jnp.zeros((BATCH, 1, C_WIDTH), F32)
        h0_f = jnp.concatenate([zeros, state_c[:, l, 0][:, None, :]], axis=0)
        h0_b = jnp.concatenate([zeros, state_c[:, l, 1][:, None, :]], axis=0)
        oc, fin_f, fin_b = _rglru(slab, c_conv_w[l], c_conv_b[l].reshape(1, -1), w_gates, b_gates,
                                  c_lambda[l].reshape(2, 1, C_WIDTH), h0_f, h0_b, g2d)

        oa = jnp.concatenate([oa_c, oa_l], axis=0)
        ob = jnp.concatenate([ob_c, ob_l], axis=0)
        od = jnp.concatenate([od_c, od_l], axis=0)
        x1, h2, idx, wt = _mixer_out(oa, ob, oc, od, w_out[l].astype(BF16), x, mod,
                                     ln_g[l, 0].reshape(1, -1), ln_b[l, 0].reshape(1, -1),
                                     w_router[l], b_router[l].reshape(1, -1))

        buf_tok, buf_w, blk_e, nused, slot = _dispatch(idx, wt)
        y_blk = _routed_experts(h2[buf_tok], buf_w, blk_e, nused, w_e_gate[l], w_e_up[l], w_e_down[l])
        routed = y_blk[slot].sum(axis=1)
        x = _ffn_out(h2, routed, x1, w_s_gate[l].astype(BF16), w_s_up[l].astype(BF16), w_s_down[l].astype(BF16),
                     mod, ln_g[l, 1].reshape(1, -1), ln_b[l, 1].reshape(1, -1))

        ctx = slab[:N_CTX]
        ctx_out[0].append(ctx[:, COL_KA * LANES:(COL_KA + 1) * LANES].reshape(BATCH, SEQ, A_KV_HEADS, HEAD_DIM))
        ctx_out[1].append(ctx[:, COL_VA * LANES:(COL_VA + 1) * LANES].reshape(BATCH, SEQ, A_KV_HEADS, HEAD_DIM))
        ctx_out[2].append(ctx[:, COL_KB * 512:(COL_KB + 1) * 512].reshape(BATCH, SEQ, B_HEADS, 2, HEAD_DIM))
        ctx_out[3].append(ctx[:, COL_VB * 512:(COL_VB + 1) * 512].reshape(BATCH, SEQ, B_HEADS, 2 * HEAD_DIM))
        ctx_out[4].append(jnp.concatenate([fin_f[:BATCH], fin_b[:BATCH]], axis=1))
        ctx_out[5].append(ctx[:, COL_KD * 512:(COL_KD + 1) * 512].reshape(BATCH, SEQ, D_HEADS, HEAD_DIM))
        ctx_out[6].append(ctx[:, COL_VD * 512:(COL_VD + 1) * 512].reshape(BATCH, SEQ, D_HEADS, HEAD_DIM))

    new = [jnp.stack(t, axis=1) for t in ctx_out]
    return (x[:N_CTX].reshape(BATCH, SEQ, D_MODEL), x[N_CTX:].reshape(DEC_BATCH, DEC_SEQ, D_MODEL), *new)
```

```python
import functools
import math

import numpy as np
import jax
import jax.numpy as jnp
from jax import lax
from jax.experimental import pallas as pl
from jax.experimental.pallas import tpu as pltpu

F32 = jnp.float32
BF16 = jnp.bfloat16

D_MODEL = 2048
BATCH = 16
SEQ = 256
DEPTH = 2
DEC_BATCH = 2
DEC_SEQ = 4096
PAST_LEN = 256
GRID_W = 64
HEAD_DIM = 64
ROPE_BASE = 10000.0
A_HEADS = 8
A_KV_HEADS = 2
A_WINDOW = 128
B_HEADS = 4
C_WIDTH = 512
C_BLOCKS = 8
C_BW = C_WIDTH // C_BLOCKS
C_POW = 8.0
D_HEADS = 8
NA_ROWS = 8
NA_COLS = 16
N_EXPERTS = 64
TOP_K = 6
D_EXPERT = 512
ROUTED_SCALE = 2.5
ALPHA = (2.0 * DEPTH) ** 0.25
LN_EPS = 1e-5
NORM_EPS = 1e-6
NEG_INF = -1e30
SCALE = HEAD_DIM ** -0.5

N_CTX = BATCH * SEQ
N_LAT = DEC_BATCH * DEC_SEQ
N_TOK = N_CTX + N_LAT
GROUP_ROWS = 4096
N_GROUPS = N_TOK // GROUP_ROWS
IN_WIDTH = 4864
LANES = 128
VMEM_LIMIT = 56 * 1024 * 1024

COL_QA, COL_QB, COL_KB, COL_VB, COL_XC, COL_GC, COL_QD, COL_KD, COL_VD = range(9)
COL_KA, COL_VA = 36, 37

TM_PROJ = 512
TN_PROJ = IN_WIDTH // 2
TM_PREP = 256
TQ_A = 128
TQ_B = 128
NA_QR = 4
NA_KR = 12
T_SCAN = 256
TM_OUT = 256
MOE_BM = 256
TM_DISPATCH = 512
TM_FFN = 128


def _params(*sem):
    return pltpu.CompilerParams(dimension_semantics=sem, vmem_limit_bytes=VMEM_LIMIT)


def _nt(a, b):
    return lax.dot_general(a, b, (((1,), (1,)), ((), ())), preferred_element_type=F32)


def _mm(a, b):
    return jnp.dot(a, b, preferred_element_type=F32)


def _softmax_unnorm(parts, sink=None):
    m = parts[0].max(axis=-1, keepdims=True)
    for p in parts[1:]:
        m = jnp.maximum(m, p.max(axis=-1, keepdims=True))
    if sink is not None:
        m = jnp.maximum(m, sink)
    es = [jnp.exp(p - m) for p in parts]
    den = es[0].sum(axis=-1, keepdims=True)
    for e in es[1:]:
        den = den + e.sum(axis=-1, keepdims=True)
    if sink is not None:
        den = den + jnp.exp(sink - m)
    return es, den


def _rms_gain(x, g):
    return x * lax.rsqrt(jnp.mean(x * x, axis=-1, keepdims=True) + NORM_EPS) * g


def _layernorm(z, g, b):
    zc = z - jnp.mean(z, axis=-1, keepdims=True)
    var = jnp.mean(zc * zc, axis=-1, keepdims=True)
    return zc * lax.rsqrt(var + LN_EPS) * g + b


def _silu(x):
    return x * jax.nn.sigmoid(x)


def _low_half(shape):
    return lax.broadcasted_iota(jnp.int32, shape, 1) < HEAD_DIM


def _mod_kernel(c_ref, w_ref, b_ref, o_ref):
    c = c_ref[...]
    o_ref[0] = jnp.dot(_silu(c), w_ref[0], precision=lax.Precision.HIGHEST,
                       preferred_element_type=F32) + b_ref[0]


def _modulation(cond8, w_mod, b_mod):
    tn = 1024
    n = w_mod.shape[-1]
    return pl.pallas_call(
        _mod_kernel,
        grid=(DEPTH, n // tn),
        in_specs=[pl.BlockSpec((8, D_MODEL), lambda l, j: (0, 0)),
                  pl.BlockSpec((1, D_MODEL, tn), lambda l, j: (l, 0, j)),
                  pl.BlockSpec((1, 1, tn), lambda l, j: (l, 0, j))],
        out_specs=pl.BlockSpec((1, 8, tn), lambda l, j: (l, 0, j)),
        out_shape=jax.ShapeDtypeStruct((DEPTH, 8, n), F32),
        compiler_params=_params("parallel", "parallel"),
        name="modulation",
    )(cond8, w_mod, b_mod.reshape(DEPTH, 1, n))


def _mod_spec(which, tm):
    return pl.BlockSpec((None, None, 1, D_MODEL), lambda i, *_: (i * tm // GROUP_ROWS, which, 0, 0))


def _inproj_kernel(x_ref, sc_ref, sh_ref, w_ref, o_ref, xb_ref):
    @pl.when(pl.program_id(1) == 0)
    def _():
        xb_ref[...] = (x_ref[...] * (1.0 + sc_ref[...]) + sh_ref[...]).astype(BF16)

    o_ref[...] = _mm(xb_ref[...], w_ref[...])


def _in_projection(x, mod, w_in_b):
    tm, tn = TM_PROJ, TN_PROJ
    return pl.pallas_call(
        _inproj_kernel,
        grid=(N_TOK // tm, IN_WIDTH // tn),
        in_specs=[pl.BlockSpec((tm, D_MODEL), lambda i, j: (i, 0)),
                  _mod_spec(1, tm), _mod_spec(0, tm),
                  pl.BlockSpec((D_MODEL, tn), lambda i, j: (0, j))],
        out_specs=pl.BlockSpec((tm, tn), lambda i, j: (i, j)),
        out_shape=jax.ShapeDtypeStruct((N_TOK, IN_WIDTH), F32),
        scratch_shapes=[pltpu.VMEM((tm, D_MODEL), BF16)],
        compiler_params=_params("parallel", "arbitrary"),
        name="in_projection",
    )(x, mod, mod, w_in_b)


def _ctx_attn_kernel(scal_ref, qa_ref, ka_ref, va_ref, qb_ref, kb_ref, vb_ref, qd_ref, kd_ref, vd_ref,
                     ga_ref, gb_ref, gd_ref, oa_ref, ob_ref, od_ref, *, lam_init):
    L = SEQ
    lo = _low_half((L, LANES))

    ka = ka_ref[...].astype(BF16)
    va = va_ref[...].astype(BF16)
    qa = qa_ref[...]
    chunks = []
    for c in range(A_HEADS // 2):
        hk = c // 2
        chunk = qa[:, c * LANES:(c + 1) * LANES]
        keep = lo if hk == 0 else jnp.logical_not(lo)
        halves = []
        for half in range(2):
            x = chunk if half == hk else pltpu.roll(chunk, HEAD_DIM, 1)
            qh = jnp.where(keep, x, 0.0).astype(BF16)
            s = _nt(qh, ka) * SCALE
            (e,), den = _softmax_unnorm([s], scal_ref[2 * c + half])
            o = _mm(e.astype(BF16), va) / den
            halves.append(o if half == hk else pltpu.roll(o, HEAD_DIM, 1))
        chunks.append(jnp.where(lo, halves[0], halves[1]))
    oa = jnp.concatenate(chunks, axis=1)
    oa_ref[...] = _rms_gain(oa, ga_ref[...]).astype(BF16)

    lam = scal_ref[A_HEADS]
    qb = qb_ref[...]
    chunks = []
    for h in range(B_HEADS):
        sl = slice(h * LANES, (h + 1) * LANES)
        qc = qb[:, sl]
        kc = kb_ref[:, sl].astype(BF16)
        vh = vb_ref[:, sl].astype(BF16)
        (e1,), d1 = _softmax_unnorm([_nt(jnp.where(lo, qc, 0.0).astype(BF16), kc) * SCALE])
        (e2,), d2 = _softmax_unnorm([_nt(jnp.where(lo, 0.0, qc).astype(BF16), kc) * SCALE])
        w = e1 * (1.0 / d1) - e2 * (lam / d2)
        o = _mm(w.astype(BF16), vh)
        chunks.append(_rms_gain(o, gb_ref[:, sl]) * (1.0 - lam_init))
    ob_ref[...] = jnp.concatenate(chunks, axis=1).astype(BF16)

    qd = qd_ref[...]
    chunks = []
    for c in range(D_HEADS // 2):
        sl = slice(c * LANES, (c + 1) * LANES)
        qc = qd[:, sl]
        kc = kd_ref[:, sl].astype(BF16)
        vc = vd_ref[:, sl].astype(BF16)
        halves = []
        for half in range(2):
            qh = jnp.where(lo if half == 0 else jnp.logical_not(lo), qc, 0.0).astype(BF16)
            (e,), den = _softmax_unnorm([_nt(qh, kc) * SCALE])
            halves.append(_mm(e.astype(BF16), vc) / den)
        chunks.append(jnp.where(lo, halves[0], halves[1]))
    od = jnp.concatenate(chunks, axis=1)
    od_ref[...] = _rms_gain(od, gd_ref[...]).astype(BF16)


def _ctx_attention(slab, scal, g_mix2d, lam_init):
    L = SEQ

    def wide(col):
        return pl.BlockSpec((L, 512), lambda b: (b, col))

    def narrow(col):
        return pl.BlockSpec((L, LANES), lambda b: (b, col))

    def gain(col):
        return pl.BlockSpec((1, 512), lambda b: (0, col))

    out = jax.ShapeDtypeStruct((N_CTX, 512), BF16)
    ospec = pl.BlockSpec((L, 512), lambda b: (b, 0))
    return pl.pallas_call(
        functools.partial(_ctx_attn_kernel, lam_init=lam_init),
        grid=(BATCH,),
        in_specs=[pl.BlockSpec(memory_space=pltpu.SMEM),
                  wide(COL_QA), narrow(COL_KA), narrow(COL_VA),
                  wide(COL_QB), wide(COL_KB), wide(COL_VB),
                  wide(COL_QD), wide(COL_KD), wide(COL_VD),
                  gain(0), gain(1), gain(3)],
        out_specs=[ospec, ospec, ospec],
        out_shape=[out, out, out],
        compiler_params=_params("parallel"),
        name="ctx_attention",
    )(scal, slab, slab, slab, slab, slab, slab, slab, slab, slab, g_mix2d, g_mix2d, g_mix2d)


def _prep_kernel(qa_ref, ka_ref, va_ref, qb_ref, kb_ref, vb_ref, qd_ref, kd_ref, vd_ref,
                 cos_ref, sa_ref, sb_ref,
                 qa_o, ka_o, va_o, qb_o, kb_o, vb_o, qd_o, kd_o, vd_o):
    cos, sa, sb = cos_ref[...], sa_ref[...], sb_ref[...]
    lo = _low_half(cos.shape)
    hi = jnp.logical_not(lo)

    def rope(x):
        return (x * cos + pltpu.roll(x, LANES - HEAD_DIM // 4, 1) * sa
                + pltpu.roll(x, HEAD_DIM // 4, 1) * sb)

    ka_o[...] = rope(ka_ref[...]).astype(BF16)
    va_o[...] = va_ref[...].astype(BF16)
    vb_o[...] = vb_ref[...].astype(BF16)
    kd_o[...] = kd_ref[...].astype(BF16)
    vd_o[...] = vd_ref[...].astype(BF16)

    for c in range(A_HEADS // 2):
        hk = c // 2
        r = rope(qa_ref[:, c * LANES:(c + 1) * LANES])
        keep = lo if hk == 0 else hi
        for half in range(2):
            h = 2 * c + half
            x = r if half == hk else pltpu.roll(r, HEAD_DIM, 1)
            qa_o[:, h * LANES:(h + 1) * LANES] = jnp.where(keep, x, 0.0).astype(BF16)
    for h in range(B_HEADS):
        sl = slice(h * LANES, (h + 1) * LANES)
        r = rope(qb_ref[:, sl])
        kb_o[:, sl] = rope(kb_ref[:, sl]).astype(BF16)
        qb_o[:, (2 * h) * LANES:(2 * h + 1) * LANES] = jnp.where(lo, r, 0.0).astype(BF16)
        qb_o[:, (2 * h + 1) * LANES:(2 * h + 2) * LANES] = jnp.where(hi, r, 0.0).astype(BF16)
    for c in range(D_HEADS // 2):
        x = qd_ref[:, c * LANES:(c + 1) * LANES]
        qd_o[:, (2 * c) * LANES:(2 * c + 1) * LANES] = jnp.where(lo, x, 0.0).astype(BF16)
        qd_o[:, (2 * c + 1) * LANES:(2 * c + 2) * LANES] = jnp.where(hi, x, 0.0).astype(BF16)


def _latent_prep(slab, cos, sa, sb):
    tm = TM_PREP
    off = N_CTX // tm
    per_seq = DEC_SEQ // tm

    def wide(col):
        return pl.BlockSpec((tm, 512), lambda i: (i + off, col))

    def narrow(col):
        return pl.BlockSpec((tm, LANES), lambda i: (i + off, col))

    tab = pl.BlockSpec((tm, LANES), lambda i: (i % per_seq, 0))

    def out(width):
        return (pl.BlockSpec((tm, width), lambda i: (i, 0)), jax.ShapeDtypeStruct((N_LAT, width), BF16))

    outs = [out(1024), out(LANES), out(LANES), out(1024), out(512), out(512), out(1024), out(512), out(512)]
    return pl.pallas_call(
        _prep_kernel,
        grid=(N_LAT // tm,),
        in_specs=[wide(COL_QA), narrow(COL_KA), narrow(COL_VA), wide(COL_QB), wide(COL_KB), wide(COL_VB),
                  wide(COL_QD), wide(COL_KD), wide(COL_VD), tab, tab, tab],
        out_specs=[o[0] for o in outs],
        out_shape=[o[1] for o in outs],
        compiler_params=_params("parallel"),
        name="latent_prep",
    )(slab, slab, slab, slab, slab, slab, slab, slab, slab, cos, sa, sb)


def _win_attn_kernel(scal_ref, q_ref, kp_ref, kc_ref, kn_ref, vp_ref, vc_ref, vn_ref, kctx_ref, vctx_ref,
                     g_ref, o_ref):
    n = pl.program_id(1)
    nb = pl.num_programs(1)
    tq = TQ_A
    group = A_HEADS // A_KV_HEADS
    k = jnp.concatenate([kp_ref[...], kc_ref[...], kn_ref[...], kctx_ref[...].astype(BF16)], axis=0)
    v = jnp.concatenate([vp_ref[...], vc_ref[...], vn_ref[...], vctx_ref[...].astype(BF16)], axis=0)
    nk = 3 * tq + PAST_LEN
    row = lax.broadcasted_iota(jnp.int32, (group * tq, nk), 0) & (tq - 1)
    col = lax.broadcasted_iota(jnp.int32, (group * tq, nk), 1)
    prev_thr = row + jnp.where(n > 0, 0, tq)
    next_thr = row + 2 * tq - jnp.where(n < nb - 1, 0, tq)
    masked = ((col < tq) & (col < prev_thr)) | ((col >= 2 * tq) & (col < 3 * tq) & (col > next_thr))
    valid = jnp.logical_not(masked)
    lo = _low_half((tq, LANES))
    heads = []
    for hk in range(A_KV_HEADS):
        q4 = jnp.concatenate([q_ref[:, (hk * group + g) * LANES:(hk * group + g + 1) * LANES]
                              for g in range(group)], axis=0)
        s = jnp.where(valid, _nt(q4, k) * SCALE, NEG_INF)
        sink = jnp.concatenate([jnp.full((tq, 1), scal_ref[hk * group + g], F32) for g in range(group)], axis=0)
        (e,), den = _softmax_unnorm([s], sink)
        o = _mm(e.astype(BF16), v) / den
        for g in range(group):
            h = hk * group + g
            og = o[g * tq:(g + 1) * tq]
            heads.append(og if (h % 2) == hk else pltpu.roll(og, HEAD_DIM, 1))
    oa = jnp.concatenate([jnp.where(lo, heads[2 * c], heads[2 * c + 1]) for c in range(A_HEADS // 2)], axis=1)
    o_ref[...] = _rms_gain(oa, g_ref[...]).astype(BF16)


def _window_attention(qa_p, ka_r, va_b, cache_k, cache_v, scal, g_mix2d):
    tq = TQ_A
    nb = DEC_SEQ // tq

    def band(d):
        return pl.BlockSpec((tq, LANES), lambda b, n: (b * nb + jnp.clip(n + d, 0, nb - 1), 0))

    ctx = pl.BlockSpec((None, PAST_LEN, LANES), lambda b, n: (b, 0, 0))
    return pl.pallas_call(
        _win_attn_kernel,
        grid=(DEC_BATCH, nb),
        in_specs=[pl.BlockSpec(memory_space=pltpu.SMEM),
                  pl.BlockSpec((tq, A_HEADS * LANES), lambda b, n: (b * nb + n, 0)),
                  band(-1), band(0), band(1), band(-1), band(0), band(1), ctx, ctx,
                  pl.BlockSpec((1, 512), lambda b, n: (0, 0))],
        out_specs=pl.BlockSpec((tq, 512), lambda b, n: (b * nb + n, 0)),
        out_shape=jax.ShapeDtypeStruct((N_LAT, 512), BF16),
        compiler_params=_params("parallel", "parallel"),
        name="window_attention",
    )(scal, qa_p, ka_r, ka_r, ka_r, va_b, va_b, va_b, cache_k, cache_v, g_mix2d)


def _diff_attn_kernel(scal_ref, q_ref, k_ref, v_ref, kctx_ref, vctx_ref, g_ref, o_ref, *, lam_init):
    lam = scal_ref[A_HEADS]
    chunks = []
    for h in range(B_HEADS):
        sl = slice(h * LANES, (h + 1) * LANES)
        kh = k_ref[:, sl]
        kc = kctx_ref[:, sl].astype(BF16)
        es, dens = [], []
        for m in range(2):
            q = q_ref[:, (2 * h + m) * LANES:(2 * h + m + 1) * LANES]
            e, d = _softmax_unnorm([_nt(q, kh) * SCALE, _nt(q, kc) * SCALE])
            es.append(e)
            dens.append(d)
        c1 = 1.0 / dens[0]
        c2 = lam / dens[1]
        w_lat = (es[0][0] * c1 - es[1][0] * c2).astype(BF16)
        w_ctx = (es[0][1] * c1 - es[1][1] * c2).astype(BF16)
        o = _mm(w_lat, v_ref[:, sl]) + _mm(w_ctx, vctx_ref[:, sl].astype(BF16))
        chunks.append(_rms_gain(o, g_ref[:, sl]) * (1.0 - lam_init))
    o_ref[...] = jnp.concatenate(chunks, axis=1).astype(BF16)


def _diff_attention(qb_p, kb_r, vb_b, cache_k, cache_v, scal, g_mix2d, lam_init):
    tq = TQ_B
    nb = DEC_SEQ // tq
    full = pl.BlockSpec((DEC_SEQ, 512), lambda b, n: (b, 0))
    ctx = pl.BlockSpec((None, PAST_LEN, 512), lambda b, n: (b, 0, 0))
    return pl.pallas_call(
        functools.partial(_diff_attn_kernel, lam_init=lam_init),
        grid=(DEC_BATCH, nb),
        in_specs=[pl.BlockSpec(memory_space=pltpu.SMEM),
                  pl.BlockSpec((tq, 2 * B_HEADS * LANES), lambda b, n: (b * nb + n, 0)),
                  full, full, ctx, ctx,
                  pl.BlockSpec((1, 512), lambda b, n: (0, 1))],
        out_specs=pl.BlockSpec((tq, 512), lambda b, n: (b * nb + n, 0)),
        out_shape=jax.ShapeDtypeStruct((N_LAT, 512), BF16),
        compiler_params=_params("parallel", "parallel"),
        name="diff_attention",
    )(scal, qb_p, kb_r, vb_b, cache_k, cache_v, g_mix2d)


def _na_window_start(step):
    return np.clip(step * NA_QR - NA_ROWS // 2, 0, GRID_W - NA_KR)


def _na_bias_indices():
    rows = DEC_SEQ // GRID_W
    steps = rows // NA_QR
    pats = []
    for step in range(steps):
        w0 = _na_window_start(step)
        r = step * NA_QR + np.arange(NA_QR)[:, None, None, None]
        c = np.arange(GRID_W)[None, :, None, None]
        kr = w0 + np.arange(NA_KR)[None, None, :, None]
        kc = np.arange(GRID_W)[None, None, None, :]
        kr0 = np.clip(r - NA_ROWS // 2, 0, rows - NA_ROWS)
        kc0 = np.clip(c - NA_COLS // 2, 0, GRID_W - NA_COLS)
        valid = (kr >= kr0) & (kr < kr0 + NA_ROWS) & (kc >= kc0) & (kc < kc0 + NA_COLS)
        drow = np.clip(kr - r + NA_ROWS - 1, 0, 2 * NA_ROWS - 2)
        dcol = np.clip(kc - c + NA_COLS - 1, 0, 2 * NA_COLS - 2)
        shape = (NA_QR * GRID_W, NA_KR * GRID_W)
        full = np.broadcast_to
        pats.append((full(drow, valid.shape).reshape(shape), full(dcol, valid.shape).reshape(shape),
                     valid.reshape(shape)))
    for step in range(2, steps - 1):
        for a, b in zip(pats[1], pats[step]):
            assert np.array_equal(a, b)
    kinds = [pats[0], pats[1], pats[-1]]
    return tuple(np.stack([k[i] for k in kinds]) for i in range(3))


def _na_kernel(q_ref, k_ref, v_ref, kctx_ref, vctx_ref, bias_ref, g_ref, o_ref):
    step = pl.program_id(1)
    nq = NA_QR * GRID_W
    nk = NA_KR * GRID_W
    w0 = jnp.clip(step * NA_QR - NA_ROWS // 2, 0, GRID_W - NA_KR)
    start = pl.multiple_of(w0 * GRID_W, GRID_W)
    lo = _low_half((nq, LANES))
    chunks = []
    for c in range(D_HEADS // 2):
        sl = slice(c * LANES, (c + 1) * LANES)
        kw = k_ref[pl.ds(start, nk), sl]
        vw = v_ref[pl.ds(start, nk), sl]
        kc = kctx_ref[:, sl].astype(BF16)
        vc = vctx_ref[:, sl].astype(BF16)
        halves = []
        for half in range(2):
            h = 2 * c + half
            q = q_ref[:, h * LANES:(h + 1) * LANES]
            s_loc = _nt(q, kw) * SCALE + bias_ref[h]
            s_ctx = _nt(q, kc) * SCALE
            (e_loc, e_ctx), den = _softmax_unnorm([s_loc, s_ctx])
            halves.append((_mm(e_loc.astype(BF16), vw) + _mm(e_ctx.astype(BF16), vc)) / den)
        chunks.append(jnp.where(lo, halves[0], halves[1]))
    od = jnp.concatenate(chunks, axis=1)
    o_ref[...] = _rms_gain(od, g_ref[...]).astype(BF16)


def _neighbourhood_attention(qd_p, kd_b, vd_b, cache_k, cache_v, bias, g_mix2d):
    nq = NA_QR * GRID_W
    steps = DEC_SEQ // nq
    full = pl.BlockSpec((DEC_SEQ, 512), lambda b, j: (b, 0))
    ctx = pl.BlockSpec((None, PAST_LEN, 512), lambda b, j: (b, 0, 0))

    def kind(b, j):
        return (jnp.where(j == 0, 0, jnp.where(j == steps - 1, 2, 1)), 0, 0, 0)

    return pl.pallas_call(
        _na_kernel,
        grid=(DEC_BATCH, steps),
        in_specs=[pl.BlockSpec((nq, D_HEADS * LANES), lambda b, j: (b * steps + j, 0)),
                  full, full, ctx, ctx,
                  pl.BlockSpec((None, D_HEADS, nq, NA_KR * GRID_W), kind),
                  pl.BlockSpec((1, 512), lambda b, j: (0, 3))],
        out_specs=pl.BlockSpec((nq, 512), lambda b, j: (b * steps + j, 0)),
        out_shape=jax.ShapeDtypeStruct((N_LAT, 512), BF16),
        compiler_params=_params("parallel", "arbitrary"),
        name="neighbourhood_attention",
    )(qd_p, kd_b, vd_b, cache_k, cache_v, bias, g_mix2d)


N_SCAN_TILES = N_TOK // T_SCAN
CTX_TILES = N_CTX // T_SCAN
TILES_PER_LAT = DEC_SEQ // T_SCAN


def _scan_tile_flags(i):
    is_ctx = i < CTX_TILES
    pos = (i - CTX_TILES) % TILES_PER_LAT
    return is_ctx | (pos == 0), is_ctx | (pos == TILES_PER_LAT - 1)


def _scan_seq(i):
    return jnp.where(i < CTX_TILES, i, CTX_TILES + (i - CTX_TILES) // TILES_PER_LAT)


def _conv_gates(x_ref, prev_ref, next_ref, cw_ref, cb_ref, wg_ref, bg_ref, lam_ref, first, last):
    x = x_ref[...]
    t = T_SCAN
    row = lax.broadcasted_iota(jnp.int32, x.shape, 0)
    pm1 = jnp.where(first, 0.0, prev_ref[7:8, :])
    n0 = jnp.where(last, 0.0, next_ref[0:1, :])
    n1 = jnp.where(last, 0.0, next_ref[1:2, :])
    x_m1 = jnp.where(row == 0, pm1, pltpu.roll(x, 1, 0))
    x_p1 = jnp.where(row == t - 1, n0, pltpu.roll(x, t - 1, 0))
    x_p2 = jnp.where(row == t - 2, n0, jnp.where(row == t - 1, n1, pltpu.roll(x, t - 2, 0)))
    y = (cw_ref[0:1, :] * x_m1 + cw_ref[1:2, :] * x + cw_ref[2:3, :] * x_p1 + cw_ref[3:4, :] * x_p2
         + cb_ref[...])
    z = jnp.dot(y, wg_ref[0], precision=lax.Precision.HIGHEST, preferred_element_type=F32) + bg_ref[0]
    r = jax.nn.sigmoid(z[:, :C_WIDTH])
    gate_i = jax.nn.sigmoid(z[:, C_WIDTH:])
    nl = -lam_ref[0]
    softplus = jnp.maximum(nl, 0.0) + jnp.log1p(jnp.exp(-jnp.abs(nl)))
    log_a = -C_POW * r * softplus
    a = jnp.exp(log_a)
    u = jnp.sqrt(1.0 - jnp.exp(2.0 * log_a)) * gate_i * y
    return a, u


def _scan_rows(a_s, u_s, h_s, h, reverse):
    groups = T_SCAN // 8

    def body(i, h):
        g = (groups - 1 - i) if reverse else i
        base = pl.multiple_of(g * 8, 8)
        a8 = a_s[pl.ds(base, 8), :]
        u8 = u_s[pl.ds(base, 8), :]
        rows = [None] * 8
        for k in (range(7, -1, -1) if reverse else range(8)):
            h = a8[k:k + 1, :] * h + u8[k:k + 1, :]
            rows[k] = h
        h_s[pl.ds(base, 8), :] = jnp.concatenate(rows, axis=0)
        return h

    return lax.fori_loop(0, groups, body, h)


def _scan_fwd_kernel(x_ref, prev_ref, next_ref, cw_ref, cb_ref, wg_ref, bg_ref, lam_ref, h0_ref,
                     hf_ref, fin_ref, a_s, u_s, carry_s):
    i = pl.program_id(0)
    first, last = _scan_tile_flags(i)
    a, u = _conv_gates(x_ref, prev_ref, next_ref, cw_ref, cb_ref, wg_ref, bg_ref, lam_ref, first, last)
    a_s[...] = a
    u_s[...] = u

    @pl.when(first)
    def _():
        carry_s[...] = h0_ref[...]

    h = _scan_rows(a_s, u_s, hf_ref, carry_s[...], reverse=False)
    carry_s[...] = h
    fin_ref[...] = h


def _scan_bwd_kernel(x_ref, prev_ref, next_ref, cw_ref, cb_ref, wg_ref, bg_ref, lam_ref, h0_ref,
                     hf_ref, gc_ref, g_ref, oc_ref, fin_ref, a_s, u_s, hb_s, carry_s):
    i = N_SCAN_TILES - 1 - pl.program_id(0)
    first, last = _scan_tile_flags(i)
    a, u = _conv_gates(x_ref, prev_ref, next_ref, cw_ref, cb_ref, wg_ref, bg_ref, lam_ref, first, last)
    a_s[...] = a
    u_s[...] = u

    @pl.when(last)
    def _():
        carry_s[...] = h0_ref[...]

    h = _scan_rows(a_s, u_s, hb_s, carry_s[...], reverse=True)
    carry_s[...] = h
    fin_ref[...] = h
    g = gc_ref[...]
    gelu = 0.5 * g * (1.0 + jnp.tanh(math.sqrt(2.0 / math.pi) * (g + 0.044715 * (g * g * g))))
    oc_ref[...] = _rms_gain(gelu * (hf_ref[...] + hb_s[...]), g_ref[...]).astype(BF16)


def _scan_common_specs(tile_of, direction):
    halo = T_SCAN // 8
    last_block = N_TOK // 8 - 1
    return [pl.BlockSpec((T_SCAN, 512), lambda j: (tile_of(j), COL_XC)),
            pl.BlockSpec((8, 512), lambda j: (jnp.maximum(tile_of(j) * halo - 1, 0), COL_XC)),
            pl.BlockSpec((8, 512), lambda j: (jnp.minimum((tile_of(j) + 1) * halo, last_block), COL_XC)),
            pl.BlockSpec((4, C_WIDTH), lambda j: (0, 0)),
            pl.BlockSpec((1, C_WIDTH), lambda j: (0, 0)),
            pl.BlockSpec((1, C_WIDTH, 2 * C_WIDTH), lambda j: (direction, 0, 0)),
            pl.BlockSpec((1, 1, 2 * C_WIDTH), lambda j: (direction, 0, 0)),
            pl.BlockSpec((1, 1, C_WIDTH), lambda j: (direction, 0, 0)),
            pl.BlockSpec((None, 1, C_WIDTH), lambda j: (_scan_seq(tile_of(j)), 0, 0))]


def _rglru(slab, conv_w, conv_b, w_gates, b_gates, lam, h0_f, h0_b, g_mix2d):
    fin = jax.ShapeDtypeStruct((N_SCAN_TILES, 1, C_WIDTH), F32)
    fwd_tile = lambda j: j
    h_f, fin_f = pl.pallas_call(
        _scan_fwd_kernel,
        grid=(N_SCAN_TILES,),
        in_specs=_scan_common_specs(fwd_tile, 0),
        out_specs=[pl.BlockSpec((T_SCAN, C_WIDTH), lambda j: (j, 0)),
                   pl.BlockSpec((None, 1, C_WIDTH), lambda j: (j, 0, 0))],
        out_shape=[jax.ShapeDtypeStruct((N_TOK, C_WIDTH), F32), fin],
        scratch_shapes=[pltpu.VMEM((T_SCAN, C_WIDTH), F32), pltpu.VMEM((T_SCAN, C_WIDTH), F32),
                        pltpu.VMEM((1, C_WIDTH), F32)],
        compiler_params=_params("arbitrary"),
        name="rglru_forward",
    )(slab, slab, slab, conv_w, conv_b, w_gates, b_gates, lam, h0_f)
    bwd_tile = lambda j: N_SCAN_TILES - 1 - j
    oc, fin_b = pl.pallas_call(
        _scan_bwd_kernel,
        grid=(N_SCAN_TILES,),
        in_specs=_scan_common_specs(bwd_tile, 1) + [
            pl.BlockSpec((T_SCAN, C_WIDTH), lambda j: (bwd_tile(j), 0)),
            pl.BlockSpec((T_SCAN, 512), lambda j: (bwd_tile(j), COL_GC)),
            pl.BlockSpec((1, 512), lambda j: (0, 2))],
        out_specs=[pl.BlockSpec((T_SCAN, C_WIDTH), lambda j: (bwd_tile(j), 0)),
                   pl.BlockSpec((None, 1, C_WIDTH), lambda j: (bwd_tile(j), 0, 0))],
        out_shape=[jax.ShapeDtypeStruct((N_TOK, C_WIDTH), BF16), fin],
        scratch_shapes=[pltpu.VMEM((T_SCAN, C_WIDTH), F32), pltpu.VMEM((T_SCAN, C_WIDTH), F32),
                        pltpu.VMEM((T_SCAN, C_WIDTH), F32), pltpu.VMEM((1, C_WIDTH), F32)],
        compiler_params=_params("arbitrary"),
        name="rglru_backward",
    )(slab, slab, slab, conv_w, conv_b, w_gates, b_gates, lam, h0_b, h_f, slab, g_mix2d)
    return oc, fin_f, fin_b


def _mixout_kernel(oa_ref, ob_ref, oc_ref, od_ref, w_ref, x_ref, g1_ref, sc2_ref, sh2_ref, lng_ref, lnb_ref,
                   wr_ref, br_ref, x1_ref, h2_ref, idx_ref, wt_ref, rank_ref, cnt_ref, cnt_s):
    @pl.when(pl.program_id(0) == 0)
    def _():
        cnt_s[...] = jnp.zeros(cnt_s.shape, F32)

    y = _mm(oa_ref[...], w_ref[0:512, :])
    y += _mm(ob_ref[...], w_ref[512:1024, :])
    y += _mm(oc_ref[...], w_ref[1024:1536, :])
    y += _mm(od_ref[...], w_ref[1536:2048, :])
    x1 = _layernorm(ALPHA * x_ref[...] + g1_ref[...] * y, lng_ref[...], lnb_ref[...])
    x1_ref[...] = x1
    h2 = x1 * (1.0 + sc2_ref[...]) + sh2_ref[...]
    h2_ref[...] = h2

    scores = jax.nn.sigmoid(jnp.dot(h2, wr_ref[...], precision=lax.Precision.HIGHEST,
                                    preferred_element_type=F32))
    sel = scores + br_ref[...]
    lane = lax.broadcasted_iota(jnp.int32, sel.shape, 1).astype(F32)
    slot = lax.broadcasted_iota(jnp.int32, idx_ref.shape, 1)
    idx_out = jnp.zeros(idx_ref.shape, F32)
    wt_out = jnp.zeros(wt_ref.shape, F32)
    total = jnp.zeros((sel.shape[0], 1), F32)
    hits = []
    for k in range(TOP_K):
        m = sel.max(axis=-1, keepdims=True)
        idx = jnp.where(sel == m, lane, float(N_EXPERTS)).min(axis=-1, keepdims=True)
        hit = lane == idx
        hits.append(hit)
        w = jnp.where(hit, scores, 0.0).sum(axis=-1, keepdims=True)
        total = total + w
        idx_out = jnp.where(slot == k, idx, idx_out)
        wt_out = jnp.where(slot == k, w, wt_out)
        sel = jnp.where(hit, -jnp.inf, sel)
    idx_ref[...] = idx_out.astype(jnp.int32)
    wt_ref[...] = wt_out / total * ROUTED_SCALE

    tm = sel.shape[0]
    chosen = jnp.zeros(sel.shape, F32)
    for hit in hits:
        chosen = chosen + hit.astype(F32)
    earlier = (lax.broadcasted_iota(jnp.int32, (tm, tm), 1) < lax.broadcasted_iota(jnp.int32, (tm, tm), 0))
    before = _mm(earlier.astype(F32).astype(BF16), chosen.astype(BF16)) + cnt_s[...]
    rank_out = jnp.zeros(idx_ref.shape, F32)
    for k, hit in enumerate(hits):
        rank_out = jnp.where(slot == k, jnp.where(hit, before, 0.0).sum(axis=-1, keepdims=True), rank_out)
    rank_ref[...] = rank_out.astype(jnp.int32)
    cnt_s[...] = cnt_s[...] + chosen.sum(axis=0, keepdims=True)
    cnt_ref[...] = cnt_s[...]


def _mixer_out(oa, ob, oc, od, w_out_b, x, mod, ln_g, ln_b, w_router, b_router):
    tm = TM_OUT
    cat = pl.BlockSpec((tm, 512), lambda i: (i, 0))
    row = pl.BlockSpec((tm, D_MODEL), lambda i: (i, 0))
    vec = pl.BlockSpec((1, D_MODEL), lambda i: (0, 0))
    k8 = pl.BlockSpec((tm, 8), lambda i: (i, 0))
    per_expert = pl.BlockSpec((1, N_EXPERTS), lambda i: (0, 0))
    return pl.pallas_call(
        _mixout_kernel,
        grid=(N_TOK // tm,),
        in_specs=[cat, cat, cat, cat,
                  pl.BlockSpec((D_MODEL, D_MODEL), lambda i: (0, 0)),
                  row, _mod_spec(2, tm), _mod_spec(4, tm), _mod_spec(3, tm), vec, vec,
                  pl.BlockSpec((D_MODEL, N_EXPERTS), lambda i: (0, 0)), per_expert],
        out_specs=[row, row, k8, k8, k8, per_expert],
        out_shape=[jax.ShapeDtypeStruct((N_TOK, D_MODEL), F32), jax.ShapeDtypeStruct((N_TOK, D_MODEL), F32),
                   jax.ShapeDtypeStruct((N_TOK, 8), jnp.int32), jax.ShapeDtypeStruct((N_TOK, 8), F32),
                   jax.ShapeDtypeStruct((N_TOK, 8), jnp.int32), jax.ShapeDtypeStruct((1, N_EXPERTS), F32)],
        scratch_shapes=[pltpu.VMEM((1, N_EXPERTS), F32)],
        compiler_params=_params("arbitrary"),
        name="mixer_out_router",
    )(oa, ob, oc, od, w_out_b, x, mod, mod, mod, ln_g, ln_b, w_router, b_router)


MOE_BLOCKS = N_TOK * TOP_K // MOE_BM + N_EXPERTS
MOE_ROWS = MOE_BLOCKS * MOE_BM


def _row_copy(src, src_row, dst, dst_row, sem):
    return pltpu.make_async_copy(src.at[pl.ds(src_row, 1)], dst.at[pl.ds(dst_row, 1)], sem)


def _dispatch_kernel(slot_ref, h_ref, xg_ref, sem):
    tm = h_ref.shape[0]
    base = pl.program_id(0) * (tm * TOP_K)

    def body(r, carry):
        for k in range(TOP_K):
            _row_copy(h_ref, r, xg_ref, slot_ref[base + r * TOP_K + k], sem).start()
        return carry

    lax.fori_loop(0, tm, body, 0, unroll=8)
    for k in range(TOP_K):
        pltpu.make_async_copy(h_ref, xg_ref.at[pl.ds(0, tm)], sem).wait()


def _dispatch_rows(slot_flat, h2):
    tm = TM_DISPATCH
    grid_spec = pltpu.PrefetchScalarGridSpec(
        num_scalar_prefetch=1,
        grid=(N_TOK // tm,),
        in_specs=[pl.BlockSpec((tm, D_MODEL), lambda i, s: (i, 0))],
        out_specs=pl.BlockSpec(memory_space=pl.ANY),
        scratch_shapes=[pltpu.SemaphoreType.DMA])
    return pl.pallas_call(
        _dispatch_kernel,
        grid_spec=grid_spec,
        out_shape=jax.ShapeDtypeStruct((MOE_ROWS, D_MODEL), F32),
        compiler_params=_params("arbitrary"),
        name="moe_dispatch",
    )(slot_flat, h2)


def _expert_kernel(blk_e_ref, blk_n_ref, x_ref, wg_ref, wu_ref, wd_ref, o_ref, wg_s, wu_s, wd_s):
    i = pl.program_id(0)
    prev = blk_e_ref[jnp.maximum(i - 1, 0)]

    @pl.when((i == 0) | (blk_e_ref[i] != prev))
    def _():
        wg_s[...] = wg_ref[...].astype(BF16)
        wu_s[...] = wu_ref[...].astype(BF16)
        wd_s[...] = wd_ref[...].astype(BF16)

    n_rows = blk_n_ref[i]

    @pl.when(n_rows > 0)
    def _():
        row = lax.broadcasted_iota(jnp.int32, x_ref.shape, 0)
        x = jnp.where(row < n_rows, x_ref[...], 0.0).astype(BF16)
        hid = _silu(_mm(x, wg_s[...])) * _mm(x, wu_s[...])
        o_ref[...] = _mm(hid.astype(BF16), wd_s[...])

    @pl.when(n_rows == 0)
    def _():
        o_ref[...] = jnp.zeros(o_ref.shape, F32)


def _routed_experts(xg, blk_e, blk_n, layer, w_g, w_u, w_d):
    bm = MOE_BM

    def weight(shape):
        return pl.BlockSpec((None, None) + shape, lambda i, e, n: (layer, e[i], 0, 0))

    grid_spec = pltpu.PrefetchScalarGridSpec(
        num_scalar_prefetch=2,
        grid=(MOE_BLOCKS,),
        in_specs=[pl.BlockSpec((bm, D_MODEL), lambda i, e, n: (i, 0)),
                  weight((D_MODEL, D_EXPERT)), weight((D_MODEL, D_EXPERT)), weight((D_EXPERT, D_MODEL))],
        out_specs=pl.BlockSpec((bm, D_MODEL), lambda i, e, n: (i, 0)),
        scratch_shapes=[pltpu.VMEM((D_MODEL, D_EXPERT), BF16), pltpu.VMEM((D_MODEL, D_EXPERT), BF16),
                        pltpu.VMEM((D_EXPERT, D_MODEL), BF16)])
    return pl.pallas_call(
        _expert_kernel,
        grid_spec=grid_spec,
        out_shape=jax.ShapeDtypeStruct((MOE_ROWS, D_MODEL), F32),
        compiler_params=_params("arbitrary"),
        name="routed_experts",
    )(blk_e, blk_n, xg, w_g, w_u, w_d)


def _ffn_out_kernel(slot_ref, h_ref, wt_ref, x1_ref, wg_ref, wu_ref, wd_ref, g2_ref, lng_ref, lnb_ref, y_ref,
                    o_ref, ybuf, sems):
    tm = h_ref.shape[0]
    i = pl.program_id(0)
    n = pl.num_programs(0)

    def gather(tile, buf):
        base = tile * (tm * TOP_K)

        def body(r, carry):
            for k in range(TOP_K):
                _row_copy(y_ref, slot_ref[base + r * TOP_K + k], ybuf.at[buf, k], r, sems.at[buf]).start()
            return carry

        lax.fori_loop(0, tm, body, 0, unroll=8)

    @pl.when(i == 0)
    def _():
        gather(0, 0)

    @pl.when(i + 1 < n)
    def _():
        gather(i + 1, (i + 1) % 2)

    h = h_ref[...].astype(BF16)
    hid = _silu(_mm(h, wg_ref[...])) * _mm(h, wu_ref[...])
    y = _mm(hid.astype(BF16), wd_ref[...])
    buf = i % 2
    for k in range(TOP_K):
        pltpu.make_async_copy(y_ref.at[pl.ds(0, tm)], ybuf.at[buf, k], sems.at[buf]).wait()
    wt = wt_ref[...]
    for k in range(TOP_K):
        y = y + ybuf[buf, k] * wt[:, k:k + 1]
    o_ref[...] = _layernorm(ALPHA * x1_ref[...] + g2_ref[...] * y, lng_ref[...], lnb_ref[...])


def _ffn_out(slot_flat, h2, wt, x1, ws_g, ws_u, ws_d, mod, ln_g, ln_b, y_blk):
    tm = TM_FFN
    row = pl.BlockSpec((tm, D_MODEL), lambda i, s: (i, 0))
    vec = pl.BlockSpec((1, D_MODEL), lambda i, s: (0, 0))
    grid_spec = pltpu.PrefetchScalarGridSpec(
        num_scalar_prefetch=1,
        grid=(N_TOK // tm,),
        in_specs=[row, pl.BlockSpec((tm, 8), lambda i, s: (i, 0)), row,
                  pl.BlockSpec((D_MODEL, D_EXPERT), lambda i, s: (0, 0)),
                  pl.BlockSpec((D_MODEL, D_EXPERT), lambda i, s: (0, 0)),
                  pl.BlockSpec((D_EXPERT, D_MODEL), lambda i, s: (0, 0)),
                  _mod_spec(5, tm), vec, vec,
                  pl.BlockSpec(memory_space=pl.ANY)],
        out_specs=row,
        scratch_shapes=[pltpu.VMEM((2, TOP_K, tm, D_MODEL), F32), pltpu.SemaphoreType.DMA((2,))])
    return pl.pallas_call(
        _ffn_out_kernel,
        grid_spec=grid_spec,
        out_shape=jax.ShapeDtypeStruct((N_TOK, D_MODEL), F32),
        compiler_params=_params("arbitrary"),
        name="shared_expert_out",
    )(slot_flat, h2, wt, x1, ws_g, ws_u, ws_d, mod, ln_g, ln_b, y_blk)


def _dispatch_plan(idx, rank, counts):
    bm = MOE_BM
    counts = counts.reshape(N_EXPERTS).astype(jnp.int32)
    padded = (counts + bm - 1) // bm * bm
    pad_end = jnp.cumsum(padded)
    pad_start = pad_end - padded
    slot = pad_start[idx[:, :TOP_K]] + rank[:, :TOP_K]
    blk_row = jnp.arange(MOE_BLOCKS, dtype=jnp.int32) * bm
    blk_e = jnp.minimum(jnp.sum(pad_end[None, :] <= blk_row[:, None], axis=1), N_EXPERTS - 1).astype(jnp.int32)
    blk_n = jnp.clip(counts[blk_e] - (blk_row - pad_start[blk_e]), 0, bm).astype(jnp.int32)
    return slot.reshape(-1).astype(jnp.int32), blk_e, blk_n


def _rope_tables():
    t = jnp.arange(DEC_SEQ)
    row = (t // GRID_W).astype(F32)
    col = (t % GRID_W).astype(F32)
    nf = HEAD_DIM // 4
    inv = ROPE_BASE ** (-jnp.arange(nf, dtype=F32) / nf)
    ar = row[:, None] * inv[None, :]
    ac = col[:, None] * inv[None, :]
    ang = jnp.concatenate([ar, ar, ac, ac], axis=-1)
    ang = jnp.concatenate([ang, ang], axis=-1)
    cos, sin = jnp.cos(ang), jnp.sin(ang)
    first = (jnp.arange(LANES) % (2 * nf)) < nf
    return cos, jnp.where(first, -sin, 0.0), jnp.where(first, 0.0, sin)


def _lambda_init(l):
    return 0.8 - 0.6 * math.exp(-0.3 * l)


def _block_diag(w):
    eye = jnp.eye(C_BLOCKS, dtype=w.dtype)
    return (eye[:, None, :, None] * w[:, :, None, :]).reshape(C_WIDTH, C_WIDTH)


def kernel(x_prompt, x_sample, c, cache_a_k, cache_a_v, cache_b_k, cache_b_v, state_c, cache_d_k, cache_d_v, c_ctx, w_mod, b_mod, w_in, a_sink, b_lambda, c_conv_w, c_conv_b, c_w_rgate, c_b_rgate, c_w_igate, c_b_igate, c_lambda, d_rpb, g_mix, w_out, ln_g, ln_b, w_router, b_router, w_e_gate, w_e_up, w_e_down, w_s_gate, w_s_up, w_s_down):
    x = jnp.concatenate([x_prompt.reshape(N_CTX, D_MODEL), x_sample.reshape(N_LAT, D_MODEL)], axis=0)
    cond8 = jnp.zeros((8, D_MODEL), F32).at[0].set(c_ctx).at[1:1 + DEC_BATCH].set(c)
    mod_all = _modulation(cond8, w_mod, b_mod)
    cos, sin_a, sin_b = _rope_tables()
    drow, dcol, na_valid = _na_bias_indices()

    order = np.concatenate([np.arange(0, 512), np.arange(768, 4864), np.arange(512, 768)])

    ctx_out = [[] for _ in range(7)]
    for l in range(DEPTH):
        lam_init = _lambda_init(l)
        mod = mod_all[l, :N_GROUPS].reshape(N_GROUPS, 6, 1, D_MODEL)
        w_in_b = w_in[l][:, order].astype(BF16)
        slab = _in_projection(x, mod, w_in_b)

        lp = b_lambda[l]
        lam = jnp.exp(jnp.sum(lp[0] * lp[1])) - jnp.exp(jnp.sum(lp[2] * lp[3])) + lam_init
        scal = jnp.zeros((16,), F32).at[:A_HEADS].set(a_sink[l]).at[A_HEADS].set(lam)
        g2d = g_mix[l].reshape(1, -1)

        oa_c, ob_c, od_c = _ctx_attention(slab, scal, g2d, lam_init)

        qa_p, ka_r, va_b, qb_p, kb_r, vb_b, qd_p, kd_b, vd_b = _latent_prep(slab, cos, sin_a, sin_b)
        oa_l = _window_attention(qa_p, ka_r, va_b, cache_a_k[:, l].reshape(DEC_BATCH, PAST_LEN, LANES),
                                 cache_a_v[:, l].reshape(DEC_BATCH, PAST_LEN, LANES), scal, g2d)
        ob_l = _diff_attention(qb_p, kb_r, vb_b, cache_b_k[:, l].reshape(DEC_BATCH, PAST_LEN, 512),
                               cache_b_v[:, l].reshape(DEC_BATCH, PAST_LEN, 512), scal, g2d, lam_init)
        bias = jnp.where(na_valid[:, None], d_rpb[l][:, drow, dcol].transpose(1, 0, 2, 3), NEG_INF)
        od_l = _neighbourhood_attention(qd_p, kd_b, vd_b, cache_d_k[:, l].reshape(DEC_BATCH, PAST_LEN, 512),
                                        cache_d_v[:, l].reshape(DEC_BATCH, PAST_LEN, 512), bias, g2d)

        w_gates = jnp.stack([jnp.concatenate([_block_diag(c_w_rgate[l, d]), _block_diag(c_w_igate[l, d])], axis=1)
                             for d in range(2)])
        b_gates = jnp.concatenate([c_b_rgate[l], c_b_igate[l]], axis=-1).reshape(2, 1, 2 * C_WIDTH)
        zeros = jnp.zeros((BATCH, 1, C_WIDTH), F32)
        h0_f = jnp.concatenate([zeros, state_c[:, l, 0][:, None, :]], axis=0)
        h0_b = jnp.concatenate([zeros, state_c[:, l, 1][:, None, :]], axis=0)
        oc, fin_f, fin_b = _rglru(slab, c_conv_w[l], c_conv_b[l].reshape(1, -1), w_gates, b_gates,
                                  c_lambda[l].reshape(2, 1, C_WIDTH), h0_f, h0_b, g2d)

        oa = jnp.concatenate([oa_c, oa_l], axis=0)
        ob = jnp.concatenate([ob_c, ob_l], axis=0)
        od = jnp.concatenate([od_c, od_l], axis=0)
        x1, h2, idx, wt, rank, counts = _mixer_out(oa, ob, oc, od, w_out[l].astype(BF16), x, mod,
                                                   ln_g[l, 0].reshape(1, -1), ln_b[l, 0].reshape(1, -1),
                                                   w_router[l], b_router[l].reshape(1, -1))

        slot, blk_e, blk_n = _dispatch_plan(idx, rank, counts)
        xg = _dispatch_rows(slot, h2)
        y_blk = _routed_experts(xg, blk_e, blk_n, l, w_e_gate, w_e_up, w_e_down)
        x = _ffn_out(slot, h2, wt, x1, w_s_gate[l].astype(BF16), w_s_up[l].astype(BF16),
                     w_s_down[l].astype(BF16), mod, ln_g[l, 1].reshape(1, -1), ln_b[l, 1].reshape(1, -1), y_blk)

        ctx = slab[:N_CTX]
        ctx_out[0].append(ctx[:, COL_KA * LANES:(COL_KA + 1) * LANES].reshape(BATCH, SEQ, A_KV_HEADS, HEAD_DIM))
        ctx_out[1].append(ctx[:, COL_VA * LANES:(COL_VA + 1) * LANES].reshape(BATCH, SEQ, A_KV_HEADS, HEAD_DIM))
        ctx_out[2].append(ctx[:, COL_KB * 512:(COL_KB + 1) * 512].reshape(BATCH, SEQ, B_HEADS, 2, HEAD_DIM))
        ctx_out[3].append(ctx[:, COL_VB * 512:(COL_VB + 1) * 512].reshape(BATCH, SEQ, B_HEADS, 2 * HEAD_DIM))
        ctx_out[4].append(jnp.concatenate([fin_f[:BATCH], fin_b[:BATCH]], axis=1))
        ctx_out[5].append(ctx[:, COL_KD * 512:(COL_KD + 1) * 512].reshape(BATCH, SEQ, D_HEADS, HEAD_DIM))
        ctx_out[6].append(ctx[:, COL_VD * 512:(COL_VD + 1) * 512].reshape(BATCH, SEQ, D_HEADS, HEAD_DIM))

    new = [jnp.stack(t, axis=1) for t in ctx_out]
    return (x[:N_CTX].reshape(BATCH, SEQ, D_MODEL), x[N_CTX:].reshape(DEC_BATCH, DEC_SEQ, D_MODEL), *new)
```

```python
import functools
import math

import numpy as np
import jax
import jax.numpy as jnp
from jax import lax
from jax.experimental import pallas as pl
from jax.experimental.pallas import tpu as pltpu

F32 = jnp.float32
BF16 = jnp.bfloat16

D_MODEL = 2048
BATCH = 16
SEQ = 256
DEPTH = 2
DEC_BATCH = 2
DEC_SEQ = 4096
PAST_LEN = 256
GRID_W = 64
HEAD_DIM = 64
ROPE_BASE = 10000.0
A_HEADS = 8
A_KV_HEADS = 2
A_WINDOW = 128
B_HEADS = 4
C_WIDTH = 512
C_BLOCKS = 8
C_BW = C_WIDTH // C_BLOCKS
C_POW = 8.0
D_HEADS = 8
NA_ROWS = 8
NA_COLS = 16
N_EXPERTS = 64
TOP_K = 6
D_EXPERT = 512
ROUTED_SCALE = 2.5
ALPHA = (2.0 * DEPTH) ** 0.25
LN_EPS = 1e-5
NORM_EPS = 1e-6
NEG_INF = -1e30
SCALE = HEAD_DIM ** -0.5

N_CTX = BATCH * SEQ
N_LAT = DEC_BATCH * DEC_SEQ
N_TOK = N_CTX + N_LAT
GROUP_ROWS = 4096
N_GROUPS = N_TOK // GROUP_ROWS
IN_WIDTH = 4864
LANES = 128
VMEM_LIMIT = 56 * 1024 * 1024

COL_QA, COL_QB, COL_KB, COL_VB, COL_XC, COL_GC, COL_QD, COL_KD, COL_VD = range(9)
COL_KA, COL_VA = 36, 37

TM_PROJ = 512
TN_PROJ = IN_WIDTH // 2
TM_PREP = 256
TQ_A = 128
TQ_B = 128
NA_QR = 4
NA_KR = 12
T_SCAN = 256
TM_OUT = 256
MOE_BM = 256
TM_DISPATCH = 512
TM_FFN = 256


def _params(*sem):
    return pltpu.CompilerParams(dimension_semantics=sem, vmem_limit_bytes=VMEM_LIMIT)


def _nt(a, b):
    return lax.dot_general(a, b, (((1,), (1,)), ((), ())), preferred_element_type=F32)


def _mm(a, b):
    return jnp.dot(a, b, preferred_element_type=F32)


def _softmax_unnorm(parts, sink=None):
    m = parts[0].max(axis=-1, keepdims=True)
    for p in parts[1:]:
        m = jnp.maximum(m, p.max(axis=-1, keepdims=True))
    if sink is not None:
        m = jnp.maximum(m, sink)
    es = [jnp.exp(p - m) for p in parts]
    den = es[0].sum(axis=-1, keepdims=True)
    for e in es[1:]:
        den = den + e.sum(axis=-1, keepdims=True)
    if sink is not None:
        den = den + jnp.exp(sink - m)
    return es, den


def _rms_gain(x, g):
    return x * lax.rsqrt(jnp.mean(x * x, axis=-1, keepdims=True) + NORM_EPS) * g


def _layernorm(z, g, b):
    zc = z - jnp.mean(z, axis=-1, keepdims=True)
    var = jnp.mean(zc * zc, axis=-1, keepdims=True)
    return zc * lax.rsqrt(var + LN_EPS) * g + b


def _silu(x):
    return x * jax.nn.sigmoid(x)


def _low_half(shape):
    return lax.broadcasted_iota(jnp.int32, shape, 1) < HEAD_DIM


HALF_D = D_MODEL // 2
HIGH16 = 0xFFFF0000


def _pack_rows(x):
    bits = pltpu.bitcast(x.astype(BF16).astype(F32), jnp.uint32)
    return (bits[:, :HALF_D] >> 16) | (bits[:, HALF_D:] & jnp.uint32(HIGH16))


def _unpack_rows(p):
    return (pltpu.bitcast(p << 16, F32), pltpu.bitcast(p & jnp.uint32(HIGH16), F32))


def _mod_kernel(c_ref, w_ref, b_ref, o_ref):
    c = c_ref[...]
    o_ref[0] = jnp.dot(_silu(c), w_ref[0], precision=lax.Precision.HIGHEST,
                       preferred_element_type=F32) + b_ref[0]


def _modulation(cond8, w_mod, b_mod):
    tn = 1024
    n = w_mod.shape[-1]
    return pl.pallas_call(
        _mod_kernel,
        grid=(DEPTH, n // tn),
        in_specs=[pl.BlockSpec((8, D_MODEL), lambda l, j: (0, 0)),
                  pl.BlockSpec((1, D_MODEL, tn), lambda l, j: (l, 0, j)),
                  pl.BlockSpec((1, 1, tn), lambda l, j: (l, 0, j))],
        out_specs=pl.BlockSpec((1, 8, tn), lambda l, j: (l, 0, j)),
        out_shape=jax.ShapeDtypeStruct((DEPTH, 8, n), F32),
        compiler_params=_params("parallel", "parallel"),
        name="modulation",
    )(cond8, w_mod, b_mod.reshape(DEPTH, 1, n))


def _mod_spec(which, tm):
    return pl.BlockSpec((None, None, 1, D_MODEL), lambda i, *_: (i * tm // GROUP_ROWS, which, 0, 0))


def _inproj_kernel(x_ref, sc_ref, sh_ref, w_ref, o_ref, xb_ref):
    @pl.when(pl.program_id(1) == 0)
    def _():
        xb_ref[...] = (x_ref[...] * (1.0 + sc_ref[...]) + sh_ref[...]).astype(BF16)

    o_ref[...] = _mm(xb_ref[...], w_ref[...])


def _in_projection(x, mod, w_in_b):
    tm, tn = TM_PROJ, TN_PROJ
    return pl.pallas_call(
        _inproj_kernel,
        grid=(N_TOK // tm, IN_WIDTH // tn),
        in_specs=[pl.BlockSpec((tm, D_MODEL), lambda i, j: (i, 0)),
                  _mod_spec(1, tm), _mod_spec(0, tm),
                  pl.BlockSpec((D_MODEL, tn), lambda i, j: (0, j))],
        out_specs=pl.BlockSpec((tm, tn), lambda i, j: (i, j)),
        out_shape=jax.ShapeDtypeStruct((N_TOK, IN_WIDTH), F32),
        scratch_shapes=[pltpu.VMEM((tm, D_MODEL), BF16)],
        compiler_params=_params("parallel", "arbitrary"),
        name="in_projection",
    )(x, mod, mod, w_in_b)


def _ctx_attn_kernel(scal_ref, qa_ref, ka_ref, va_ref, qb_ref, kb_ref, vb_ref, qd_ref, kd_ref, vd_ref,
                     ga_ref, gb_ref, gd_ref, oa_ref, ob_ref, od_ref, *, lam_init):
    L = SEQ
    lo = _low_half((L, LANES))

    ka = ka_ref[...].astype(BF16)
    va = va_ref[...].astype(BF16)
    qa = qa_ref[...]
    chunks = []
    for c in range(A_HEADS // 2):
        hk = c // 2
        chunk = qa[:, c * LANES:(c + 1) * LANES]
        keep = lo if hk == 0 else jnp.logical_not(lo)
        halves = []
        for half in range(2):
            x = chunk if half == hk else pltpu.roll(chunk, HEAD_DIM, 1)
            qh = jnp.where(keep, x, 0.0).astype(BF16)
            s = _nt(qh, ka) * SCALE
            (e,), den = _softmax_unnorm([s], scal_ref[2 * c + half])
            o = _mm(e.astype(BF16), va) / den
            halves.append(o if half == hk else pltpu.roll(o, HEAD_DIM, 1))
        chunks.append(jnp.where(lo, halves[0], halves[1]))
    oa = jnp.concatenate(chunks, axis=1)
    oa_ref[...] = _rms_gain(oa, ga_ref[...]).astype(BF16)

    lam = scal_ref[A_HEADS]
    qb = qb_ref[...]
    chunks = []
    for h in range(B_HEADS):
        sl = slice(h * LANES, (h + 1) * LANES)
        qc = qb[:, sl]
        kc = kb_ref[:, sl].astype(BF16)
        vh = vb_ref[:, sl].astype(BF16)
        (e1,), d1 = _softmax_unnorm([_nt(jnp.where(lo, qc, 0.0).astype(BF16), kc) * SCALE])
        (e2,), d2 = _softmax_unnorm([_nt(jnp.where(lo, 0.0, qc).astype(BF16), kc) * SCALE])
        w = e1 * (1.0 / d1) - e2 * (lam / d2)
        o = _mm(w.astype(BF16), vh)
        chunks.append(_rms_gain(o, gb_ref[:, sl]) * (1.0 - lam_init))
    ob_ref[...] = jnp.concatenate(chunks, axis=1).astype(BF16)

    qd = qd_ref[...]
    chunks = []
    for c in range(D_HEADS // 2):
        sl = slice(c * LANES, (c + 1) * LANES)
        qc = qd[:, sl]
        kc = kd_ref[:, sl].astype(BF16)
        vc = vd_ref[:, sl].astype(BF16)
        halves = []
        for half in range(2):
            qh = jnp.where(lo if half == 0 else jnp.logical_not(lo), qc, 0.0).astype(BF16)
            (e,), den = _softmax_unnorm([_nt(qh, kc) * SCALE])
            halves.append(_mm(e.astype(BF16), vc) / den)
        chunks.append(jnp.where(lo, halves[0], halves[1]))
    od = jnp.concatenate(chunks, axis=1)
    od_ref[...] = _rms_gain(od, gd_ref[...]).astype(BF16)


def _ctx_attention(slab, scal, g_mix2d, lam_init):
    L = SEQ

    def wide(col):
        return pl.BlockSpec((L, 512), lambda b: (b, col))

    def narrow(col):
        return pl.BlockSpec((L, LANES), lambda b: (b, col))

    def gain(col):
        return pl.BlockSpec((1, 512), lambda b: (0, col))

    out = jax.ShapeDtypeStruct((N_CTX, 512), BF16)
    ospec = pl.BlockSpec((L, 512), lambda b: (b, 0))
    return pl.pallas_call(
        functools.partial(_ctx_attn_kernel, lam_init=lam_init),
        grid=(BATCH,),
        in_specs=[pl.BlockSpec(memory_space=pltpu.SMEM),
                  wide(COL_QA), narrow(COL_KA), narrow(COL_VA),
                  wide(COL_QB), wide(COL_KB), wide(COL_VB),
                  wide(COL_QD), wide(COL_KD), wide(COL_VD),
                  gain(0), gain(1), gain(3)],
        out_specs=[ospec, ospec, ospec],
        out_shape=[out, out, out],
        compiler_params=_params("parallel"),
        name="ctx_attention",
    )(scal, slab, slab, slab, slab, slab, slab, slab, slab, slab, g_mix2d, g_mix2d, g_mix2d)


def _prep_kernel(qa_ref, ka_ref, va_ref, qb_ref, kb_ref, vb_ref, qd_ref, kd_ref, vd_ref,
                 cos_ref, sa_ref, sb_ref,
                 qa_o, ka_o, va_o, qb_o, kb_o, vb_o, qd_o, kd_o, vd_o):
    cos, sa, sb = cos_ref[...], sa_ref[...], sb_ref[...]
    lo = _low_half(cos.shape)
    hi = jnp.logical_not(lo)

    def rope(x):
        return (x * cos + pltpu.roll(x, LANES - HEAD_DIM // 4, 1) * sa
                + pltpu.roll(x, HEAD_DIM // 4, 1) * sb)

    ka_o[...] = rope(ka_ref[...]).astype(BF16)
    va_o[...] = va_ref[...].astype(BF16)
    vb_o[...] = vb_ref[...].astype(BF16)
    kd_o[...] = kd_ref[...].astype(BF16)
    vd_o[...] = vd_ref[...].astype(BF16)

    for c in range(A_HEADS // 2):
        hk = c // 2
        r = rope(qa_ref[:, c * LANES:(c + 1) * LANES])
        keep = lo if hk == 0 else hi
        for half in range(2):
            h = 2 * c + half
            x = r if half == hk else pltpu.roll(r, HEAD_DIM, 1)
            qa_o[:, h * LANES:(h + 1) * LANES] = jnp.where(keep, x, 0.0).astype(BF16)
    for h in range(B_HEADS):
        sl = slice(h * LANES, (h + 1) * LANES)
        r = rope(qb_ref[:, sl])
        kb_o[:, sl] = rope(kb_ref[:, sl]).astype(BF16)
        qb_o[:, (2 * h) * LANES:(2 * h + 1) * LANES] = jnp.where(lo, r, 0.0).astype(BF16)
        qb_o[:, (2 * h + 1) * LANES:(2 * h + 2) * LANES] = jnp.where(hi, r, 0.0).astype(BF16)
    for c in range(D_HEADS // 2):
        x = qd_ref[:, c * LANES:(c + 1) * LANES]
        qd_o[:, (2 * c) * LANES:(2 * c + 1) * LANES] = jnp.where(lo, x, 0.0).astype(BF16)
        qd_o[:, (2 * c + 1) * LANES:(2 * c + 2) * LANES] = jnp.where(hi, x, 0.0).astype(BF16)


def _latent_prep(slab, cos, sa, sb):
    tm = TM_PREP
    off = N_CTX // tm
    per_seq = DEC_SEQ // tm

    def wide(col):
        return pl.BlockSpec((tm, 512), lambda i: (i + off, col))

    def narrow(col):
        return pl.BlockSpec((tm, LANES), lambda i: (i + off, col))

    tab = pl.BlockSpec((tm, LANES), lambda i: (i % per_seq, 0))

    def out(width):
        return (pl.BlockSpec((tm, width), lambda i: (i, 0)), jax.ShapeDtypeStruct((N_LAT, width), BF16))

    outs = [out(1024), out(LANES), out(LANES), out(1024), out(512), out(512), out(1024), out(512), out(512)]
    return pl.pallas_call(
        _prep_kernel,
        grid=(N_LAT // tm,),
        in_specs=[wide(COL_QA), narrow(COL_KA), narrow(COL_VA), wide(COL_QB), wide(COL_KB), wide(COL_VB),
                  wide(COL_QD), wide(COL_KD), wide(COL_VD), tab, tab, tab],
        out_specs=[o[0] for o in outs],
        out_shape=[o[1] for o in outs],
        compiler_params=_params("parallel"),
        name="latent_prep",
    )(slab, slab, slab, slab, slab, slab, slab, slab, slab, cos, sa, sb)


def _win_attn_kernel(scal_ref, q_ref, kp_ref, kc_ref, kn_ref, vp_ref, vc_ref, vn_ref, kctx_ref, vctx_ref,
                     g_ref, o_ref):
    n = pl.program_id(1)
    nb = pl.num_programs(1)
    tq = TQ_A
    group = A_HEADS // A_KV_HEADS
    k = jnp.concatenate([kp_ref[...], kc_ref[...], kn_ref[...], kctx_ref[...].astype(BF16)], axis=0)
    v = jnp.concatenate([vp_ref[...], vc_ref[...], vn_ref[...], vctx_ref[...].astype(BF16)], axis=0)
    nk = 3 * tq + PAST_LEN
    row = lax.broadcasted_iota(jnp.int32, (group * tq, nk), 0) & (tq - 1)
    col = lax.broadcasted_iota(jnp.int32, (group * tq, nk), 1)
    prev_thr = row + jnp.where(n > 0, 0, tq)
    next_thr = row + 2 * tq - jnp.where(n < nb - 1, 0, tq)
    masked = ((col < tq) & (col < prev_thr)) | ((col >= 2 * tq) & (col < 3 * tq) & (col > next_thr))
    valid = jnp.logical_not(masked)
    lo = _low_half((tq, LANES))
    heads = []
    for hk in range(A_KV_HEADS):
        q4 = jnp.concatenate([q_ref[:, (hk * group + g) * LANES:(hk * group + g + 1) * LANES]
                              for g in range(group)], axis=0)
        s = jnp.where(valid, _nt(q4, k) * SCALE, NEG_INF)
        sink = jnp.concatenate([jnp.full((tq, 1), scal_ref[hk * group + g], F32) for g in range(group)], axis=0)
        (e,), den = _softmax_unnorm([s], sink)
        o = _mm(e.astype(BF16), v) / den
        for g in range(group):
            h = hk * group + g
            og = o[g * tq:(g + 1) * tq]
            heads.append(og if (h % 2) == hk else pltpu.roll(og, HEAD_DIM, 1))
    oa = jnp.concatenate([jnp.where(lo, heads[2 * c], heads[2 * c + 1]) for c in range(A_HEADS // 2)], axis=1)
    o_ref[...] = _rms_gain(oa, g_ref[...]).astype(BF16)


def _window_attention(qa_p, ka_r, va_b, cache_k, cache_v, scal, g_mix2d):
    tq = TQ_A
    nb = DEC_SEQ // tq

    def band(d):
        return pl.BlockSpec((tq, LANES), lambda b, n: (b * nb + jnp.clip(n + d, 0, nb - 1), 0))

    ctx = pl.BlockSpec((None, PAST_LEN, LANES), lambda b, n: (b, 0, 0))
    return pl.pallas_call(
        _win_attn_kernel,
        grid=(DEC_BATCH, nb),
        in_specs=[pl.BlockSpec(memory_space=pltpu.SMEM),
                  pl.BlockSpec((tq, A_HEADS * LANES), lambda b, n: (b * nb + n, 0)),
                  band(-1), band(0), band(1), band(-1), band(0), band(1), ctx, ctx,
                  pl.BlockSpec((1, 512), lambda b, n: (0, 0))],
        out_specs=pl.BlockSpec((tq, 512), lambda b, n: (b * nb + n, 0)),
        out_shape=jax.ShapeDtypeStruct((N_LAT, 512), BF16),
        compiler_params=_params("parallel", "parallel"),
        name="window_attention",
    )(scal, qa_p, ka_r, ka_r, ka_r, va_b, va_b, va_b, cache_k, cache_v, g_mix2d)


def _diff_attn_kernel(scal_ref, q_ref, k_ref, v_ref, kctx_ref, vctx_ref, g_ref, o_ref, *, lam_init):
    lam = scal_ref[A_HEADS]
    chunks = []
    for h in range(B_HEADS):
        sl = slice(h * LANES, (h + 1) * LANES)
        kh = k_ref[:, sl]
        kc = kctx_ref[:, sl].astype(BF16)
        es, dens = [], []
        for m in range(2):
            q = q_ref[:, (2 * h + m) * LANES:(2 * h + m + 1) * LANES]
            e, d = _softmax_unnorm([_nt(q, kh) * SCALE, _nt(q, kc) * SCALE])
            es.append(e)
            dens.append(d)
        c1 = 1.0 / dens[0]
        c2 = lam / dens[1]
        w_lat = (es[0][0] * c1 - es[1][0] * c2).astype(BF16)
        w_ctx = (es[0][1] * c1 - es[1][1] * c2).astype(BF16)
        o = _mm(w_lat, v_ref[:, sl]) + _mm(w_ctx, vctx_ref[:, sl].astype(BF16))
        chunks.append(_rms_gain(o, g_ref[:, sl]) * (1.0 - lam_init))
    o_ref[...] = jnp.concatenate(chunks, axis=1).astype(BF16)


def _diff_attention(qb_p, kb_r, vb_b, cache_k, cache_v, scal, g_mix2d, lam_init):
    tq = TQ_B
    nb = DEC_SEQ // tq
    full = pl.BlockSpec((DEC_SEQ, 512), lambda b, n: (b, 0))
    ctx = pl.BlockSpec((None, PAST_LEN, 512), lambda b, n: (b, 0, 0))
    return pl.pallas_call(
        functools.partial(_diff_attn_kernel, lam_init=lam_init),
        grid=(DEC_BATCH, nb),
        in_specs=[pl.BlockSpec(memory_space=pltpu.SMEM),
                  pl.BlockSpec((tq, 2 * B_HEADS * LANES), lambda b, n: (b * nb + n, 0)),
                  full, full, ctx, ctx,
                  pl.BlockSpec((1, 512), lambda b, n: (0, 1))],
        out_specs=pl.BlockSpec((tq, 512), lambda b, n: (b * nb + n, 0)),
        out_shape=jax.ShapeDtypeStruct((N_LAT, 512), BF16),
        compiler_params=_params("parallel", "parallel"),
        name="diff_attention",
    )(scal, qb_p, kb_r, vb_b, cache_k, cache_v, g_mix2d)


def _na_window_start(step):
    return np.clip(step * NA_QR - NA_ROWS // 2, 0, GRID_W - NA_KR)


def _na_bias_indices():
    rows = DEC_SEQ // GRID_W
    steps = rows // NA_QR
    pats = []
    for step in range(steps):
        w0 = _na_window_start(step)
        r = step * NA_QR + np.arange(NA_QR)[:, None, None, None]
        c = np.arange(GRID_W)[None, :, None, None]
        kr = w0 + np.arange(NA_KR)[None, None, :, None]
        kc = np.arange(GRID_W)[None, None, None, :]
        kr0 = np.clip(r - NA_ROWS // 2, 0, rows - NA_ROWS)
        kc0 = np.clip(c - NA_COLS // 2, 0, GRID_W - NA_COLS)
        valid = (kr >= kr0) & (kr < kr0 + NA_ROWS) & (kc >= kc0) & (kc < kc0 + NA_COLS)
        drow = np.clip(kr - r + NA_ROWS - 1, 0, 2 * NA_ROWS - 2)
        dcol = np.clip(kc - c + NA_COLS - 1, 0, 2 * NA_COLS - 2)
        shape = (NA_QR * GRID_W, NA_KR * GRID_W)
        full = np.broadcast_to
        pats.append((full(drow, valid.shape).reshape(shape), full(dcol, valid.shape).reshape(shape),
                     valid.reshape(shape)))
    for step in range(2, steps - 1):
        for a, b in zip(pats[1], pats[step]):
            assert np.array_equal(a, b)
    kinds = [pats[0], pats[1], pats[-1]]
    drow, dcol, valid = (np.stack([k[i] for k in kinds]) for i in range(3))
    shape6 = (len(kinds), NA_QR, GRID_W, NA_KR, GRID_W)
    row_sel = np.eye(2 * NA_ROWS - 1, dtype=np.float32)[drow.reshape(shape6)[:, :, 0, :, 0]]
    col_sel = np.eye(2 * NA_COLS - 1, dtype=np.float32)[dcol.reshape(shape6)[0, 0, :, 0, :]]
    return row_sel, col_sel, valid


def _na_bias(rpb, row_sel, col_sel, valid):
    hp = lax.Precision.HIGHEST
    cols = jnp.einsum('hab,cdb->hacd', rpb, col_sel, precision=hp)
    bias = jnp.einsum('kria,hacd->khrcid', row_sel, cols, precision=hp)
    bias = bias.reshape(valid.shape[0], D_HEADS, NA_QR * GRID_W, NA_KR * GRID_W)
    return jnp.where(valid[:, None], bias, NEG_INF)


def _na_kernel(q_ref, k_ref, v_ref, kctx_ref, vctx_ref, bias_ref, g_ref, o_ref):
    step = pl.program_id(1)
    nq = NA_QR * GRID_W
    nk = NA_KR * GRID_W
    w0 = jnp.clip(step * NA_QR - NA_ROWS // 2, 0, GRID_W - NA_KR)
    start = pl.multiple_of(w0 * GRID_W, GRID_W)
    lo = _low_half((nq, LANES))
    chunks = []
    for c in range(D_HEADS // 2):
        sl = slice(c * LANES, (c + 1) * LANES)
        kw = k_ref[pl.ds(start, nk), sl]
        vw = v_ref[pl.ds(start, nk), sl]
        kc = kctx_ref[:, sl].astype(BF16)
        vc = vctx_ref[:, sl].astype(BF16)
        halves = []
        for half in range(2):
            h = 2 * c + half
            q = q_ref[:, h * LANES:(h + 1) * LANES]
            s_loc = _nt(q, kw) * SCALE + bias_ref[h]
            s_ctx = _nt(q, kc) * SCALE
            (e_loc, e_ctx), den = _softmax_unnorm([s_loc, s_ctx])
            halves.append((_mm(e_loc.astype(BF16), vw) + _mm(e_ctx.astype(BF16), vc)) / den)
        chunks.append(jnp.where(lo, halves[0], halves[1]))
    od = jnp.concatenate(chunks, axis=1)
    o_ref[...] = _rms_gain(od, g_ref[...]).astype(BF16)


def _neighbourhood_attention(qd_p, kd_b, vd_b, cache_k, cache_v, bias, g_mix2d):
    nq = NA_QR * GRID_W
    steps = DEC_SEQ // nq
    full = pl.BlockSpec((DEC_SEQ, 512), lambda b, j: (b, 0))
    ctx = pl.BlockSpec((None, PAST_LEN, 512), lambda b, j: (b, 0, 0))

    def kind(b, j):
        return (jnp.where(j == 0, 0, jnp.where(j == steps - 1, 2, 1)), 0, 0, 0)

    return pl.pallas_call(
        _na_kernel,
        grid=(DEC_BATCH, steps),
        in_specs=[pl.BlockSpec((nq, D_HEADS * LANES), lambda b, j: (b * steps + j, 0)),
                  full, full, ctx, ctx,
                  pl.BlockSpec((None, D_HEADS, nq, NA_KR * GRID_W), kind),
                  pl.BlockSpec((1, 512), lambda b, j: (0, 3))],
        out_specs=pl.BlockSpec((nq, 512), lambda b, j: (b * steps + j, 0)),
        out_shape=jax.ShapeDtypeStruct((N_LAT, 512), BF16),
        compiler_params=_params("parallel", "arbitrary"),
        name="neighbourhood_attention",
    )(qd_p, kd_b, vd_b, cache_k, cache_v, bias, g_mix2d)


N_SCAN_TILES = N_TOK // T_SCAN
CTX_TILES = N_CTX // T_SCAN
TILES_PER_LAT = DEC_SEQ // T_SCAN


def _scan_tile_flags(i):
    is_ctx = i < CTX_TILES
    pos = (i - CTX_TILES) % TILES_PER_LAT
    return is_ctx | (pos == 0), is_ctx | (pos == TILES_PER_LAT - 1)


def _scan_seq(i):
    return jnp.where(i < CTX_TILES, i, CTX_TILES + (i - CTX_TILES) // TILES_PER_LAT)


def _conv_gates(x_ref, prev_ref, next_ref, cw_ref, cb_ref, wg_ref, bg_ref, lam_ref, first, last):
    x = x_ref[...]
    t = T_SCAN
    row = lax.broadcasted_iota(jnp.int32, x.shape, 0)
    pm1 = jnp.where(first, 0.0, prev_ref[7:8, :])
    n0 = jnp.where(last, 0.0, next_ref[0:1, :])
    n1 = jnp.where(last, 0.0, next_ref[1:2, :])
    x_m1 = jnp.where(row == 0, pm1, pltpu.roll(x, 1, 0))
    x_p1 = jnp.where(row == t - 1, n0, pltpu.roll(x, t - 1, 0))
    x_p2 = jnp.where(row == t - 2, n0, jnp.where(row == t - 1, n1, pltpu.roll(x, t - 2, 0)))
    y = (cw_ref[0:1, :] * x_m1 + cw_ref[1:2, :] * x + cw_ref[2:3, :] * x_p1 + cw_ref[3:4, :] * x_p2
         + cb_ref[...])
    z = jnp.dot(y, wg_ref[0], precision=lax.Precision.HIGHEST, preferred_element_type=F32) + bg_ref[0]
    r = jax.nn.sigmoid(z[:, :C_WIDTH])
    gate_i = jax.nn.sigmoid(z[:, C_WIDTH:])
    nl = -lam_ref[0]
    softplus = jnp.maximum(nl, 0.0) + jnp.log1p(jnp.exp(-jnp.abs(nl)))
    log_a = -C_POW * r * softplus
    a = jnp.exp(log_a)
    u = jnp.sqrt(1.0 - jnp.exp(2.0 * log_a)) * gate_i * y
    return a, u


def _scan_rows(a_s, u_s, h_s, h, reverse):
    groups = T_SCAN // 8

    def body(i, h):
        g = (groups - 1 - i) if reverse else i
        base = pl.multiple_of(g * 8, 8)
        a8 = a_s[pl.ds(base, 8), :]
        u8 = u_s[pl.ds(base, 8), :]
        rows = [None] * 8
        for k in (range(7, -1, -1) if reverse else range(8)):
            h = a8[k:k + 1, :] * h + u8[k:k + 1, :]
            rows[k] = h
        h_s[pl.ds(base, 8), :] = jnp.concatenate(rows, axis=0)
        return h

    return lax.fori_loop(0, groups, body, h)


def _scan_fwd_kernel(x_ref, prev_ref, next_ref, cw_ref, cb_ref, wg_ref, bg_ref, lam_ref, h0_ref,
                     hf_ref, fin_ref, a_s, u_s, carry_s):
    i = pl.program_id(0)
    first, last = _scan_tile_flags(i)
    a, u = _conv_gates(x_ref, prev_ref, next_ref, cw_ref, cb_ref, wg_ref, bg_ref, lam_ref, first, last)
    a_s[...] = a
    u_s[...] = u

    @pl.when(first)
    def _():
        carry_s[...] = h0_ref[...]

    h = _scan_rows(a_s, u_s, hf_ref, carry_s[...], reverse=False)
    carry_s[...] = h
    fin_ref[...] = h


def _scan_bwd_kernel(x_ref, prev_ref, next_ref, cw_ref, cb_ref, wg_ref, bg_ref, lam_ref, h0_ref,
                     hf_ref, gc_ref, g_ref, oc_ref, fin_ref, a_s, u_s, hb_s, carry_s):
    i = N_SCAN_TILES - 1 - pl.program_id(0)
    first, last = _scan_tile_flags(i)
    a, u = _conv_gates(x_ref, prev_ref, next_ref, cw_ref, cb_ref, wg_ref, bg_ref, lam_ref, first, last)
    a_s[...] = a
    u_s[...] = u

    @pl.when(last)
    def _():
        carry_s[...] = h0_ref[...]

    h = _scan_rows(a_s, u_s, hb_s, carry_s[...], reverse=True)
    carry_s[...] = h
    fin_ref[...] = h
    g = gc_ref[...]
    gelu = 0.5 * g * (1.0 + jnp.tanh(math.sqrt(2.0 / math.pi) * (g + 0.044715 * (g * g * g))))
    oc_ref[...] = _rms_gain(gelu * (hf_ref[...] + hb_s[...]), g_ref[...]).astype(BF16)


def _scan_common_specs(tile_of, direction):
    halo = T_SCAN // 8
    last_block = N_TOK // 8 - 1
    return [pl.BlockSpec((T_SCAN, 512), lambda j: (tile_of(j), COL_XC)),
            pl.BlockSpec((8, 512), lambda j: (jnp.maximum(tile_of(j) * halo - 1, 0), COL_XC)),
            pl.BlockSpec((8, 512), lambda j: (jnp.minimum((tile_of(j) + 1) * halo, last_block), COL_XC)),
            pl.BlockSpec((4, C_WIDTH), lambda j: (0, 0)),
            pl.BlockSpec((1, C_WIDTH), lambda j: (0, 0)),
            pl.BlockSpec((1, C_WIDTH, 2 * C_WIDTH), lambda j: (direction, 0, 0)),
            pl.BlockSpec((1, 1, 2 * C_WIDTH), lambda j: (direction, 0, 0)),
            pl.BlockSpec((1, 1, C_WIDTH), lambda j: (direction, 0, 0)),
            pl.BlockSpec((None, 1, C_WIDTH), lambda j: (_scan_seq(tile_of(j)), 0, 0))]


def _rglru(slab, conv_w, conv_b, w_gates, b_gates, lam, h0_f, h0_b, g_mix2d):
    fin = jax.ShapeDtypeStruct((N_SCAN_TILES, 1, C_WIDTH), F32)
    fwd_tile = lambda j: j
    h_f, fin_f = pl.pallas_call(
        _scan_fwd_kernel,
        grid=(N_SCAN_TILES,),
        in_specs=_scan_common_specs(fwd_tile, 0),
        out_specs=[pl.BlockSpec((T_SCAN, C_WIDTH), lambda j: (j, 0)),
                   pl.BlockSpec((None, 1, C_WIDTH), lambda j: (j, 0, 0))],
        out_shape=[jax.ShapeDtypeStruct((N_TOK, C_WIDTH), F32), fin],
        scratch_shapes=[pltpu.VMEM((T_SCAN, C_WIDTH), F32), pltpu.VMEM((T_SCAN, C_WIDTH), F32),
                        pltpu.VMEM((1, C_WIDTH), F32)],
        compiler_params=_params("arbitrary"),
        name="rglru_forward",
    )(slab, slab, slab, conv_w, conv_b, w_gates, b_gates, lam, h0_f)
    bwd_tile = lambda j: N_SCAN_TILES - 1 - j
    oc, fin_b = pl.pallas_call(
        _scan_bwd_kernel,
        grid=(N_SCAN_TILES,),
        in_specs=_scan_common_specs(bwd_tile, 1) + [
            pl.BlockSpec((T_SCAN, C_WIDTH), lambda j: (bwd_tile(j), 0)),
            pl.BlockSpec((T_SCAN, 512), lambda j: (bwd_tile(j), COL_GC)),
            pl.BlockSpec((1, 512), lambda j: (0, 2))],
        out_specs=[pl.BlockSpec((T_SCAN, C_WIDTH), lambda j: (bwd_tile(j), 0)),
                   pl.BlockSpec((None, 1, C_WIDTH), lambda j: (bwd_tile(j), 0, 0))],
        out_shape=[jax.ShapeDtypeStruct((N_TOK, C_WIDTH), BF16), fin],
        scratch_shapes=[pltpu.VMEM((T_SCAN, C_WIDTH), F32), pltpu.VMEM((T_SCAN, C_WIDTH), F32),
                        pltpu.VMEM((T_SCAN, C_WIDTH), F32), pltpu.VMEM((1, C_WIDTH), F32)],
        compiler_params=_params("arbitrary"),
        name="rglru_backward",
    )(slab, slab, slab, conv_w, conv_b, w_gates, b_gates, lam, h0_b, h_f, slab, g_mix2d)
    return oc, fin_f, fin_b


def _mixout_kernel(oa_ref, ob_ref, oc_ref, od_ref, w_ref, x_ref, g1_ref, sc2_ref, sh2_ref, lng_ref, lnb_ref,
                   wr_ref, br_ref, x1_ref, h2_ref, idx_ref, wt_ref, rank_ref, cnt_ref, cnt_s):
    @pl.when(pl.program_id(0) == 0)
    def _():
        cnt_s[...] = jnp.zeros(cnt_s.shape, F32)

    y = _mm(oa_ref[...], w_ref[0:512, :])
    y += _mm(ob_ref[...], w_ref[512:1024, :])
    y += _mm(oc_ref[...], w_ref[1024:1536, :])
    y += _mm(od_ref[...], w_ref[1536:2048, :])
    x1 = _layernorm(ALPHA * x_ref[...] + g1_ref[...] * y, lng_ref[...], lnb_ref[...])
    x1_ref[...] = x1
    h2 = x1 * (1.0 + sc2_ref[...]) + sh2_ref[...]
    h2_ref[...] = _pack_rows(h2)

    scores = jax.nn.sigmoid(jnp.dot(h2, wr_ref[...], precision=lax.Precision.HIGHEST,
                                    preferred_element_type=F32))
    sel = scores + br_ref[...]
    lane = lax.broadcasted_iota(jnp.int32, sel.shape, 1).astype(F32)
    slot = lax.broadcasted_iota(jnp.int32, idx_ref.shape, 1)
    idx_out = jnp.zeros(idx_ref.shape, F32)
    wt_out = jnp.zeros(wt_ref.shape, F32)
    total = jnp.zeros((sel.shape[0], 1), F32)
    hits = []
    for k in range(TOP_K):
        m = sel.max(axis=-1, keepdims=True)
        idx = jnp.where(sel == m, lane, float(N_EXPERTS)).min(axis=-1, keepdims=True)
        hit = lane == idx
        hits.append(hit)
        w = jnp.where(hit, scores, 0.0).sum(axis=-1, keepdims=True)
        total = total + w
        idx_out = jnp.where(slot == k, idx, idx_out)
        wt_out = jnp.where(slot == k, w, wt_out)
        sel = jnp.where(hit, -jnp.inf, sel)
    idx_ref[...] = idx_out.astype(jnp.int32)
    wt_ref[...] = wt_out / total * ROUTED_SCALE

    tm = sel.shape[0]
    chosen = jnp.zeros(sel.shape, F32)
    for hit in hits:
        chosen = chosen + hit.astype(F32)
    earlier = (lax.broadcasted_iota(jnp.int32, (tm, tm), 1) < lax.broadcasted_iota(jnp.int32, (tm, tm), 0))
    before = _mm(earlier.astype(F32).astype(BF16), chosen.astype(BF16)) + cnt_s[...]
    rank_out = jnp.zeros(idx_ref.shape, F32)
    for k, hit in enumerate(hits):
        rank_out = jnp.where(slot == k, jnp.where(hit, before, 0.0).sum(axis=-1, keepdims=True), rank_out)
    rank_ref[...] = rank_out.astype(jnp.int32)
    cnt_s[...] = cnt_s[...] + chosen.sum(axis=0, keepdims=True)
    cnt_ref[...] = cnt_s[...]


def _mixer_out(oa, ob, oc, od, w_out_b, x, mod, ln_g, ln_b, w_router, b_router):
    tm = TM_OUT
    cat = pl.BlockSpec((tm, 512), lambda i: (i, 0))
    row = pl.BlockSpec((tm, D_MODEL), lambda i: (i, 0))
    vec = pl.BlockSpec((1, D_MODEL), lambda i: (0, 0))
    k8 = pl.BlockSpec((tm, 8), lambda i: (i, 0))
    per_expert = pl.BlockSpec((1, N_EXPERTS), lambda i: (0, 0))
    return pl.pallas_call(
        _mixout_kernel,
        grid=(N_TOK // tm,),
        in_specs=[cat, cat, cat, cat,
                  pl.BlockSpec((D_MODEL, D_MODEL), lambda i: (0, 0)),
                  row, _mod_spec(2, tm), _mod_spec(4, tm), _mod_spec(3, tm), vec, vec,
                  pl.BlockSpec((D_MODEL, N_EXPERTS), lambda i: (0, 0)), per_expert],
        out_specs=[row, pl.BlockSpec((tm, HALF_D), lambda i: (i, 0)), k8, k8, k8, per_expert],
        out_shape=[jax.ShapeDtypeStruct((N_TOK, D_MODEL), F32), jax.ShapeDtypeStruct((N_TOK, HALF_D), jnp.uint32),
                   jax.ShapeDtypeStruct((N_TOK, 8), jnp.int32), jax.ShapeDtypeStruct((N_TOK, 8), F32),
                   jax.ShapeDtypeStruct((N_TOK, 8), jnp.int32), jax.ShapeDtypeStruct((1, N_EXPERTS), F32)],
        scratch_shapes=[pltpu.VMEM((1, N_EXPERTS), F32)],
        compiler_params=_params("arbitrary"),
        name="mixer_out_router",
    )(oa, ob, oc, od, w_out_b, x, mod, mod, mod, ln_g, ln_b, w_router, b_router)


MOE_BLOCKS = N_TOK * TOP_K // MOE_BM + N_EXPERTS
MOE_ROWS = MOE_BLOCKS * MOE_BM


def _row_copy(src, src_row, dst, dst_row, sem):
    return pltpu.make_async_copy(src.at[pl.ds(src_row, 1)], dst.at[pl.ds(dst_row, 1)], sem)


def _dispatch_kernel(slot_ref, h_ref, xg_ref, sem):
    tm = h_ref.shape[0]
    base = pl.program_id(0) * (tm * TOP_K)

    def body(r, carry):
        for k in range(TOP_K):
            _row_copy(h_ref, r, xg_ref, slot_ref[base + r * TOP_K + k], sem).start()
        return carry

    lax.fori_loop(0, tm, body, 0, unroll=8)
    for k in range(TOP_K):
        pltpu.make_async_copy(h_ref, xg_ref.at[pl.ds(0, tm)], sem).wait()


def _dispatch_rows(slot_flat, h2):
    tm = TM_DISPATCH
    grid_spec = pltpu.PrefetchScalarGridSpec(
        num_scalar_prefetch=1,
        grid=(N_TOK // tm,),
        in_specs=[pl.BlockSpec((tm, HALF_D), lambda i, s: (i, 0))],
        out_specs=pl.BlockSpec(memory_space=pl.ANY),
        scratch_shapes=[pltpu.SemaphoreType.DMA])
    return pl.pallas_call(
        _dispatch_kernel,
        grid_spec=grid_spec,
        out_shape=jax.ShapeDtypeStruct((MOE_ROWS, HALF_D), jnp.uint32),
        compiler_params=_params("arbitrary"),
        name="moe_dispatch",
    )(slot_flat, h2)


def _expert_kernel(blk_e_ref, blk_n_ref, x_ref, wg_ref, wu_ref, wd_ref, o_ref, wg_s, wu_s, wd_s):
    i = pl.program_id(0)
    prev = blk_e_ref[jnp.maximum(i - 1, 0)]

    @pl.when((i == 0) | (blk_e_ref[i] != prev))
    def _():
        wg_s[...] = wg_ref[...].astype(BF16)
        wu_s[...] = wu_ref[...].astype(BF16)
        wd_s[...] = wd_ref[...].astype(BF16)

    n_rows = blk_n_ref[i]

    @pl.when(n_rows > 0)
    def _():
        row = lax.broadcasted_iota(jnp.int32, x_ref.shape, 0)
        lo, hi = _unpack_rows(jnp.where(row < n_rows, x_ref[...], jnp.uint32(0)))
        lo, hi = lo.astype(BF16), hi.astype(BF16)
        gate = _mm(lo, wg_s[:HALF_D, :]) + _mm(hi, wg_s[HALF_D:, :])
        up = _mm(lo, wu_s[:HALF_D, :]) + _mm(hi, wu_s[HALF_D:, :])
        o_ref[...] = _pack_rows(_mm((_silu(gate) * up).astype(BF16), wd_s[...]))

    @pl.when(n_rows == 0)
    def _():
        o_ref[...] = jnp.zeros(o_ref.shape, jnp.uint32)


def _routed_experts(xg, blk_e, blk_n, layer, w_g, w_u, w_d):
    bm = MOE_BM

    def weight(shape):
        return pl.BlockSpec((None, None) + shape, lambda i, e, n: (layer, e[i], 0, 0))

    grid_spec = pltpu.PrefetchScalarGridSpec(
        num_scalar_prefetch=2,
        grid=(MOE_BLOCKS,),
        in_specs=[pl.BlockSpec((bm, HALF_D), lambda i, e, n: (i, 0)),
                  weight((D_MODEL, D_EXPERT)), weight((D_MODEL, D_EXPERT)), weight((D_EXPERT, D_MODEL))],
        out_specs=pl.BlockSpec((bm, HALF_D), lambda i, e, n: (i, 0)),
        scratch_shapes=[pltpu.VMEM((D_MODEL, D_EXPERT), BF16), pltpu.VMEM((D_MODEL, D_EXPERT), BF16),
                        pltpu.VMEM((D_EXPERT, D_MODEL), BF16)])
    return pl.pallas_call(
        _expert_kernel,
        grid_spec=grid_spec,
        out_shape=jax.ShapeDtypeStruct((MOE_ROWS, HALF_D), jnp.uint32),
        compiler_params=_params("arbitrary"),
        name="routed_experts",
    )(blk_e, blk_n, xg, w_g, w_u, w_d)


def _ffn_out_kernel(slot_ref, h_ref, wt_ref, x1_ref, wg_ref, wu_ref, wd_ref, g2_ref, lng_ref, lnb_ref, y_ref,
                    o_ref, ybuf, sems):
    tm = h_ref.shape[0]
    i = pl.program_id(0)
    n = pl.num_programs(0)

    def gather(tile, buf):
        base = tile * (tm * TOP_K)

        def body(r, carry):
            for k in range(TOP_K):
                _row_copy(y_ref, slot_ref[base + r * TOP_K + k], ybuf.at[buf, k], r, sems.at[buf]).start()
            return carry

        lax.fori_loop(0, tm, body, 0, unroll=8)

    @pl.when(i == 0)
    def _():
        gather(0, 0)

    @pl.when(i + 1 < n)
    def _():
        gather(i + 1, (i + 1) % 2)

    lo, hi = _unpack_rows(h_ref[...])
    lo, hi = lo.astype(BF16), hi.astype(BF16)
    gate = _mm(lo, wg_ref[:HALF_D, :]) + _mm(hi, wg_ref[HALF_D:, :])
    up = _mm(lo, wu_ref[:HALF_D, :]) + _mm(hi, wu_ref[HALF_D:, :])
    y = _mm((_silu(gate) * up).astype(BF16), wd_ref[...])
    buf = i % 2
    for k in range(TOP_K):
        pltpu.make_async_copy(y_ref.at[pl.ds(0, tm)], ybuf.at[buf, k], sems.at[buf]).wait()
    wt = wt_ref[...]
    y_lo, y_hi = y[:, :HALF_D], y[:, HALF_D:]
    for k in range(TOP_K):
        r_lo, r_hi = _unpack_rows(ybuf[buf, k])
        y_lo = y_lo + r_lo * wt[:, k:k + 1]
        y_hi = y_hi + r_hi * wt[:, k:k + 1]
    y = jnp.concatenate([y_lo, y_hi], axis=1)
    o_ref[...] = _layernorm(ALPHA * x1_ref[...] + g2_ref[...] * y, lng_ref[...], lnb_ref[...])


def _ffn_out(slot_flat, h2, wt, x1, ws_g, ws_u, ws_d, mod, ln_g, ln_b, y_blk):
    tm = TM_FFN
    row = pl.BlockSpec((tm, D_MODEL), lambda i, s: (i, 0))
    vec = pl.BlockSpec((1, D_MODEL), lambda i, s: (0, 0))
    grid_spec = pltpu.PrefetchScalarGridSpec(
        num_scalar_prefetch=1,
        grid=(N_TOK // tm,),
        in_specs=[pl.BlockSpec((tm, HALF_D), lambda i, s: (i, 0)), pl.BlockSpec((tm, 8), lambda i, s: (i, 0)), row,
                  pl.BlockSpec((D_MODEL, D_EXPERT), lambda i, s: (0, 0)),
                  pl.BlockSpec((D_MODEL, D_EXPERT), lambda i, s: (0, 0)),
                  pl.BlockSpec((D_EXPERT, D_MODEL), lambda i, s: (0, 0)),
                  _mod_spec(5, tm), vec, vec,
                  pl.BlockSpec(memory_space=pl.ANY)],
        out_specs=row,
        scratch_shapes=[pltpu.VMEM((2, TOP_K, tm, HALF_D), jnp.uint32), pltpu.SemaphoreType.DMA((2,))])
    return pl.pallas_call(
        _ffn_out_kernel,
        grid_spec=grid_spec,
        out_shape=jax.ShapeDtypeStruct((N_TOK, D_MODEL), F32),
        compiler_params=_params("arbitrary"),
        name="shared_expert_out",
    )(slot_flat, h2, wt, x1, ws_g, ws_u, ws_d, mod, ln_g, ln_b, y_blk)


def _dispatch_plan(idx, rank, counts):
    bm = MOE_BM
    counts = counts.reshape(N_EXPERTS).astype(jnp.int32)
    padded = (counts + bm - 1) // bm * bm
    pad_end = jnp.cumsum(padded)
    pad_start = pad_end - padded
    slot = pad_start[idx[:, :TOP_K]] + rank[:, :TOP_K]
    blk_row = jnp.arange(MOE_BLOCKS, dtype=jnp.int32) * bm
    blk_e = jnp.minimum(jnp.sum(pad_end[None, :] <= blk_row[:, None], axis=1), N_EXPERTS - 1).astype(jnp.int32)
    blk_n = jnp.clip(counts[blk_e] - (blk_row - pad_start[blk_e]), 0, bm).astype(jnp.int32)
    return slot.reshape(-1).astype(jnp.int32), blk_e, blk_n


def _rope_tables():
    t = jnp.arange(DEC_SEQ)
    row = (t // GRID_W).astype(F32)
    col = (t % GRID_W).astype(F32)
    nf = HEAD_DIM // 4
    inv = ROPE_BASE ** (-jnp.arange(nf, dtype=F32) / nf)
    ar = row[:, None] * inv[None, :]
    ac = col[:, None] * inv[None, :]
    ang = jnp.concatenate([ar, ar, ac, ac], axis=-1)
    ang = jnp.concatenate([ang, ang], axis=-1)
    cos, sin = jnp.cos(ang), jnp.sin(ang)
    first = (jnp.arange(LANES) % (2 * nf)) < nf
    return cos, jnp.where(first, -sin, 0.0), jnp.where(first, 0.0, sin)


def _lambda_init(l):
    return 0.8 - 0.6 * math.exp(-0.3 * l)


def _block_diag(w):
    eye = jnp.eye(C_BLOCKS, dtype=w.dtype)
    return (eye[:, None, :, None] * w[:, :, None, :]).reshape(C_WIDTH, C_WIDTH)


def kernel(x_prompt, x_sample, c, cache_a_k, cache_a_v, cache_b_k, cache_b_v, state_c, cache_d_k, cache_d_v, c_ctx, w_mod, b_mod, w_in, a_sink, b_lambda, c_conv_w, c_conv_b, c_w_rgate, c_b_rgate, c_w_igate, c_b_igate, c_lambda, d_rpb, g_mix, w_out, ln_g, ln_b, w_router, b_router, w_e_gate, w_e_up, w_e_down, w_s_gate, w_s_up, w_s_down):
    x = jnp.concatenate([x_prompt.reshape(N_CTX, D_MODEL), x_sample.reshape(N_LAT, D_MODEL)], axis=0)
    cond8 = jnp.zeros((8, D_MODEL), F32).at[0].set(c_ctx).at[1:1 + DEC_BATCH].set(c)
    mod_all = _modulation(cond8, w_mod, b_mod)
    cos, sin_a, sin_b = _rope_tables()
    row_sel, col_sel, na_valid = _na_bias_indices()

    order = np.concatenate([np.arange(0, 512), np.arange(768, 4864), np.arange(512, 768)])

    ctx_out = [[] for _ in range(7)]
    for l in range(DEPTH):
        lam_init = _lambda_init(l)
        mod = mod_all[l, :N_GROUPS].reshape(N_GROUPS, 6, 1, D_MODEL)
        w_in_b = w_in[l][:, order].astype(BF16)
        slab = _in_projection(x, mod, w_in_b)

        lp = b_lambda[l]
        lam = jnp.exp(jnp.sum(lp[0] * lp[1])) - jnp.exp(jnp.sum(lp[2] * lp[3])) + lam_init
        scal = jnp.zeros((16,), F32).at[:A_HEADS].set(a_sink[l]).at[A_HEADS].set(lam)
        g2d = g_mix[l].reshape(1, -1)

        oa_c, ob_c, od_c = _ctx_attention(slab, scal, g2d, lam_init)

        qa_p, ka_r, va_b, qb_p, kb_r, vb_b, qd_p, kd_b, vd_b = _latent_prep(slab, cos, sin_a, sin_b)
        oa_l = _window_attention(qa_p, ka_r, va_b, cache_a_k[:, l].reshape(DEC_BATCH, PAST_LEN, LANES),
                                 cache_a_v[:, l].reshape(DEC_BATCH, PAST_LEN, LANES), scal, g2d)
        ob_l = _diff_attention(qb_p, kb_r, vb_b, cache_b_k[:, l].reshape(DEC_BATCH, PAST_LEN, 512),
                               cache_b_v[:, l].reshape(DEC_BATCH, PAST_LEN, 512), scal, g2d, lam_init)
        bias = _na_bias(d_rpb[l], row_sel, col_sel, na_valid)
        od_l = _neighbourhood_attention(qd_p, kd_b, vd_b, cache_d_k[:, l].reshape(DEC_BATCH, PAST_LEN, 512),
                                        cache_d_v[:, l].reshape(DEC_BATCH, PAST_LEN, 512), bias, g2d)

        w_gates = jnp.stack([jnp.concatenate([_block_diag(c_w_rgate[l, d]), _block_diag(c_w_igate[l, d])], axis=1)
                             for d in range(2)])
        b_gates = jnp.concatenate([c_b_rgate[l], c_b_igate[l]], axis=-1).reshape(2, 1, 2 * C_WIDTH)
        zeros = jnp.zeros((BATCH, 1, C_WIDTH), F32)
        h0_f = jnp.concatenate([zeros, state_c[:, l, 0][:, None, :]], axis=0)
        h0_b = jnp.concatenate([zeros, state_c[:, l, 1][:, None, :]], axis=0)
        oc, fin_f, fin_b = _rglru(slab, c_conv_w[l], c_conv_b[l].reshape(1, -1), w_gates, b_gates,
                                  c_lambda[l].reshape(2, 1, C_WIDTH), h0_f, h0_b, g2d)

        oa = jnp.concatenate([oa_c, oa_l], axis=0)
        ob = jnp.concatenate([ob_c, ob_l], axis=0)
        od = jnp.concatenate([od_c, od_l], axis=0)
        x1, h2, idx, wt, rank, counts = _mixer_out(oa, ob, oc, od, w_out[l].astype(BF16), x, mod,
                                                   ln_g[l, 0].reshape(1, -1), ln_b[l, 0].reshape(1, -1),
                                                   w_router[l], b_router[l].reshape(1, -1))

        slot, blk_e, blk_n = _dispatch_plan(idx, rank, counts)
        xg = _dispatch_rows(slot, h2)
        y_blk = _routed_experts(xg, blk_e, blk_n, l, w_e_gate, w_e_up, w_e_down)
        x = _ffn_out(slot, h2, wt, x1, w_s_gate[l].astype(BF16), w_s_up[l].astype(BF16),
                     w_s_down[l].astype(BF16), mod, ln_g[l, 1].reshape(1, -1), ln_b[l, 1].reshape(1, -1), y_blk)

        ctx = slab[:N_CTX]
        ctx_out[0].append(ctx[:, COL_KA * LANES:(COL_KA + 1) * LANES].reshape(BATCH, SEQ, A_KV_HEADS, HEAD_DIM))
        ctx_out[1].append(ctx[:, COL_VA * LANES:(COL_VA + 1) * LANES].reshape(BATCH, SEQ, A_KV_HEADS, HEAD_DIM))
        ctx_out[2].append(ctx[:, COL_KB * 512:(COL_KB + 1) * 512].reshape(BATCH, SEQ, B_HEADS, 2, HEAD_DIM))
        ctx_out[3].append(ctx[:, COL_VB * 512:(COL_VB + 1) * 512].reshape(BATCH, SEQ, B_HEADS, 2 * HEAD_DIM))
        ctx_out[4].append(jnp.concatenate([fin_f[:BATCH], fin_b[:BATCH]], axis=1))
        ctx_out[5].append(ctx[:, COL_KD * 512:(COL_KD + 1) * 512].reshape(BATCH, SEQ, D_HEADS, HEAD_DIM))
        ctx_out[6].append(ctx[:, COL_VD * 512:(COL_VD + 1) * 512].reshape(BATCH, SEQ, D_HEADS, HEAD_DIM))

    new = [jnp.stack(t, axis=1) for t in ctx_out]
    return (x[:N_CTX].reshape(BATCH, SEQ, D_MODEL), x[N_CTX:].reshape(DEC_BATCH, DEC_SEQ, D_MODEL), *new)
```

```python
import functools
import math

import numpy as np
import jax
import jax.numpy as jnp
from jax import lax
from jax.experimental import pallas as pl
from jax.experimental.pallas import tpu as pltpu

F32 = jnp.float32
BF16 = jnp.bfloat16

D_MODEL = 2048
BATCH = 16
SEQ = 256
DEPTH = 2
DEC_BATCH = 2
DEC_SEQ = 4096
PAST_LEN = 256
GRID_W = 64
HEAD_DIM = 64
ROPE_BASE = 10000.0
A_HEADS = 8
A_KV_HEADS = 2
A_WINDOW = 128
B_HEADS = 4
C_WIDTH = 512
C_BLOCKS = 8
C_BW = C_WIDTH // C_BLOCKS
C_POW = 8.0
D_HEADS = 8
NA_ROWS = 8
NA_COLS = 16
N_EXPERTS = 64
TOP_K = 6
D_EXPERT = 512
ROUTED_SCALE = 2.5
ALPHA = (2.0 * DEPTH) ** 0.25
LN_EPS = 1e-5
NORM_EPS = 1e-6
NEG_INF = -1e30
SCALE = HEAD_DIM ** -0.5

N_CTX = BATCH * SEQ
N_LAT = DEC_BATCH * DEC_SEQ
N_TOK = N_CTX + N_LAT
GROUP_ROWS = 4096
N_GROUPS = N_TOK // GROUP_ROWS
IN_WIDTH = 4864
LANES = 128
VMEM_LIMIT = 56 * 1024 * 1024

COL_QA, COL_QB, COL_KB, COL_VB, COL_XC, COL_GC, COL_QD, COL_KD, COL_VD = range(9)
COL_KA, COL_VA = 36, 37

TM_PROJ = 512
TN_PROJ = IN_WIDTH // 2
TM_PREP = 256
TQ_A = 128
TQ_B = 128
NA_QR = 4
NA_KR = 12
T_SCAN = 256
TM_OUT = 256
MOE_BM = 256
TM_DISPATCH = 128
TM_FFN = 128


def _params(*sem):
    return pltpu.CompilerParams(dimension_semantics=sem, vmem_limit_bytes=VMEM_LIMIT)


def _nt(a, b):
    return lax.dot_general(a, b, (((1,), (1,)), ((), ())), preferred_element_type=F32)


def _mm(a, b):
    return jnp.dot(a, b, preferred_element_type=F32)


def _softmax_unnorm(parts, sink=None):
    m = parts[0].max(axis=-1, keepdims=True)
    for p in parts[1:]:
        m = jnp.maximum(m, p.max(axis=-1, keepdims=True))
    if sink is not None:
        m = jnp.maximum(m, sink)
    es = [jnp.exp(p - m) for p in parts]
    den = es[0].sum(axis=-1, keepdims=True)
    for e in es[1:]:
        den = den + e.sum(axis=-1, keepdims=True)
    if sink is not None:
        den = den + jnp.exp(sink - m)
    return es, den


def _rms_gain(x, g):
    return x * lax.rsqrt(jnp.mean(x * x, axis=-1, keepdims=True) + NORM_EPS) * g


def _layernorm(z, g, b):
    zc = z - jnp.mean(z, axis=-1, keepdims=True)
    var = jnp.mean(zc * zc, axis=-1, keepdims=True)
    return zc * lax.rsqrt(var + LN_EPS) * g + b


def _silu(x):
    return x * jax.nn.sigmoid(x)


def _low_half(shape):
    return lax.broadcasted_iota(jnp.int32, shape, 1) < HEAD_DIM


HALF_D = D_MODEL // 2
HIGH16 = 0xFFFF0000


def _pack_rows(x):
    bits = pltpu.bitcast(x.astype(BF16).astype(F32), jnp.uint32)
    return (bits[:, :HALF_D] >> 16) | (bits[:, HALF_D:] & jnp.uint32(HIGH16))


def _unpack_rows(p):
    return (pltpu.bitcast(p << 16, F32), pltpu.bitcast(p & jnp.uint32(HIGH16), F32))


SUBLANES = 8
assert HALF_D == SUBLANES * LANES


def _store_token_tiles(ref, index, packed, first=0):
    m = packed.shape[0]
    for s in range(SUBLANES):
        rows = pl.ds(first * SUBLANES + s, m, stride=SUBLANES)
        ref[index + (rows, slice(None))] = packed[:, s * LANES:(s + 1) * LANES]


def _load_token_tiles(ref, index, m):
    return jnp.concatenate([ref[index + (pl.ds(s, m, stride=SUBLANES), slice(None))] for s in range(SUBLANES)],
                           axis=1)


def _mod_kernel(c_ref, w_ref, b_ref, o_ref):
    c = c_ref[...]
    o_ref[0] = jnp.dot(_silu(c), w_ref[0], precision=lax.Precision.HIGHEST,
                       preferred_element_type=F32) + b_ref[0]


def _modulation(cond8, w_mod, b_mod):
    tn = 1024
    n = w_mod.shape[-1]
    return pl.pallas_call(
        _mod_kernel,
        grid=(DEPTH, n // tn),
        in_specs=[pl.BlockSpec((8, D_MODEL), lambda l, j: (0, 0)),
                  pl.BlockSpec((1, D_MODEL, tn), lambda l, j: (l, 0, j)),
                  pl.BlockSpec((1, 1, tn), lambda l, j: (l, 0, j))],
        out_specs=pl.BlockSpec((1, 8, tn), lambda l, j: (l, 0, j)),
        out_shape=jax.ShapeDtypeStruct((DEPTH, 8, n), F32),
        compiler_params=_params("parallel", "parallel"),
        name="modulation",
    )(cond8, w_mod, b_mod.reshape(DEPTH, 1, n))


def _mod_spec(which, tm):
    return pl.BlockSpec((None, None, 1, D_MODEL), lambda i, *_: (i * tm // GROUP_ROWS, which, 0, 0))


def _inproj_kernel(x_ref, sc_ref, sh_ref, w_ref, o_ref, xb_ref):
    @pl.when(pl.program_id(1) == 0)
    def _():
        xb_ref[...] = (x_ref[...] * (1.0 + sc_ref[...]) + sh_ref[...]).astype(BF16)

    o_ref[...] = _mm(xb_ref[...], w_ref[...])


def _in_projection(x, mod, w_in_b):
    tm, tn = TM_PROJ, TN_PROJ
    return pl.pallas_call(
        _inproj_kernel,
        grid=(N_TOK // tm, IN_WIDTH // tn),
        in_specs=[pl.BlockSpec((tm, D_MODEL), lambda i, j: (i, 0)),
                  _mod_spec(1, tm), _mod_spec(0, tm),
                  pl.BlockSpec((D_MODEL, tn), lambda i, j: (0, j))],
        out_specs=pl.BlockSpec((tm, tn), lambda i, j: (i, j)),
        out_shape=jax.ShapeDtypeStruct((N_TOK, IN_WIDTH), F32),
        scratch_shapes=[pltpu.VMEM((tm, D_MODEL), BF16)],
        compiler_params=_params("parallel", "arbitrary"),
        name="in_projection",
    )(x, mod, mod, w_in_b)


def _ctx_attn_kernel(scal_ref, qa_ref, ka_ref, va_ref, qb_ref, kb_ref, vb_ref, qd_ref, kd_ref, vd_ref,
                     ga_ref, gb_ref, gd_ref, oa_ref, ob_ref, od_ref, *, lam_init):
    L = SEQ
    lo = _low_half((L, LANES))

    ka = ka_ref[...].astype(BF16)
    va = va_ref[...].astype(BF16)
    qa = qa_ref[...]
    chunks = []
    for c in range(A_HEADS // 2):
        hk = c // 2
        chunk = qa[:, c * LANES:(c + 1) * LANES]
        keep = lo if hk == 0 else jnp.logical_not(lo)
        halves = []
        for half in range(2):
            x = chunk if half == hk else pltpu.roll(chunk, HEAD_DIM, 1)
            qh = jnp.where(keep, x, 0.0).astype(BF16)
            s = _nt(qh, ka) * SCALE
            (e,), den = _softmax_unnorm([s], scal_ref[2 * c + half])
            o = _mm(e.astype(BF16), va) / den
            halves.append(o if half == hk else pltpu.roll(o, HEAD_DIM, 1))
        chunks.append(jnp.where(lo, halves[0], halves[1]))
    oa = jnp.concatenate(chunks, axis=1)
    oa_ref[...] = _rms_gain(oa, ga_ref[...]).astype(BF16)

    lam = scal_ref[A_HEADS]
    qb = qb_ref[...]
    chunks = []
    for h in range(B_HEADS):
        sl = slice(h * LANES, (h + 1) * LANES)
        qc = qb[:, sl]
        kc = kb_ref[:, sl].astype(BF16)
        vh = vb_ref[:, sl].astype(BF16)
        (e1,), d1 = _softmax_unnorm([_nt(jnp.where(lo, qc, 0.0).astype(BF16), kc) * SCALE])
        (e2,), d2 = _softmax_unnorm([_nt(jnp.where(lo, 0.0, qc).astype(BF16), kc) * SCALE])
        w = e1 * (1.0 / d1) - e2 * (lam / d2)
        o = _mm(w.astype(BF16), vh)
        chunks.append(_rms_gain(o, gb_ref[:, sl]) * (1.0 - lam_init))
    ob_ref[...] = jnp.concatenate(chunks, axis=1).astype(BF16)

    qd = qd_ref[...]
    chunks = []
    for c in range(D_HEADS // 2):
        sl = slice(c * LANES, (c + 1) * LANES)
        qc = qd[:, sl]
        kc = kd_ref[:, sl].astype(BF16)
        vc = vd_ref[:, sl].astype(BF16)
        halves = []
        for half in range(2):
            qh = jnp.where(lo if half == 0 else jnp.logical_not(lo), qc, 0.0).astype(BF16)
            (e,), den = _softmax_unnorm([_nt(qh, kc) * SCALE])
            halves.append(_mm(e.astype(BF16), vc) / den)
        chunks.append(jnp.where(lo, halves[0], halves[1]))
    od = jnp.concatenate(chunks, axis=1)
    od_ref[...] = _rms_gain(od, gd_ref[...]).astype(BF16)


def _ctx_attention(slab, scal, g_mix2d, lam_init):
    L = SEQ

    def wide(col):
        return pl.BlockSpec((L, 512), lambda b: (b, col))

    def narrow(col):
        return pl.BlockSpec((L, LANES), lambda b: (b, col))

    def gain(col):
        return pl.BlockSpec((1, 512), lambda b: (0, col))

    out = jax.ShapeDtypeStruct((N_CTX, 512), BF16)
    ospec = pl.BlockSpec((L, 512), lambda b: (b, 0))
    return pl.pallas_call(
        functools.partial(_ctx_attn_kernel, lam_init=lam_init),
        grid=(BATCH,),
        in_specs=[pl.BlockSpec(memory_space=pltpu.SMEM),
                  wide(COL_QA), narrow(COL_KA), narrow(COL_VA),
                  wide(COL_QB), wide(COL_KB), wide(COL_VB),
                  wide(COL_QD), wide(COL_KD), wide(COL_VD),
                  gain(0), gain(1), gain(3)],
        out_specs=[ospec, ospec, ospec],
        out_shape=[out, out, out],
        compiler_params=_params("parallel"),
        name="ctx_attention",
    )(scal, slab, slab, slab, slab, slab, slab, slab, slab, slab, g_mix2d, g_mix2d, g_mix2d)


def _prep_kernel(qa_ref, ka_ref, va_ref, qb_ref, kb_ref, vb_ref, qd_ref, kd_ref, vd_ref,
                 cos_ref, sa_ref, sb_ref,
                 qa_o, ka_o, va_o, qb_o, kb_o, vb_o, qd_o, kd_o, vd_o):
    cos, sa, sb = cos_ref[...], sa_ref[...], sb_ref[...]
    lo = _low_half(cos.shape)
    hi = jnp.logical_not(lo)

    def rope(x):
        return (x * cos + pltpu.roll(x, LANES - HEAD_DIM // 4, 1) * sa
                + pltpu.roll(x, HEAD_DIM // 4, 1) * sb)

    ka_o[...] = rope(ka_ref[...]).astype(BF16)
    va_o[...] = va_ref[...].astype(BF16)
    vb_o[...] = vb_ref[...].astype(BF16)
    kd_o[...] = kd_ref[...].astype(BF16)
    vd_o[...] = vd_ref[...].astype(BF16)

    for c in range(A_HEADS // 2):
        hk = c // 2
        r = rope(qa_ref[:, c * LANES:(c + 1) * LANES])
        keep = lo if hk == 0 else hi
        for half in range(2):
            h = 2 * c + half
            x = r if half == hk else pltpu.roll(r, HEAD_DIM, 1)
            qa_o[:, h * LANES:(h + 1) * LANES] = jnp.where(keep, x, 0.0).astype(BF16)
    for h in range(B_HEADS):
        sl = slice(h * LANES, (h + 1) * LANES)
        r = rope(qb_ref[:, sl])
        kb_o[:, sl] = rope(kb_ref[:, sl]).astype(BF16)
        qb_o[:, (2 * h) * LANES:(2 * h + 1) * LANES] = jnp.where(lo, r, 0.0).astype(BF16)
        qb_o[:, (2 * h + 1) * LANES:(2 * h + 2) * LANES] = jnp.where(hi, r, 0.0).astype(BF16)
    for c in range(D_HEADS // 2):
        x = qd_ref[:, c * LANES:(c + 1) * LANES]
        qd_o[:, (2 * c) * LANES:(2 * c + 1) * LANES] = jnp.where(lo, x, 0.0).astype(BF16)
        qd_o[:, (2 * c + 1) * LANES:(2 * c + 2) * LANES] = jnp.where(hi, x, 0.0).astype(BF16)


def _latent_prep(slab, cos, sa, sb):
    tm = TM_PREP
    off = N_CTX // tm
    per_seq = DEC_SEQ // tm

    def wide(col):
        return pl.BlockSpec((tm, 512), lambda i: (i + off, col))

    def narrow(col):
        return pl.BlockSpec((tm, LANES), lambda i: (i + off, col))

    tab = pl.BlockSpec((tm, LANES), lambda i: (i % per_seq, 0))

    def out(width):
        return (pl.BlockSpec((tm, width), lambda i: (i, 0)), jax.ShapeDtypeStruct((N_LAT, width), BF16))

    outs = [out(1024), out(LANES), out(LANES), out(1024), out(512), out(512), out(1024), out(512), out(512)]
    return pl.pallas_call(
        _prep_kernel,
        grid=(N_LAT // tm,),
        in_specs=[wide(COL_QA), narrow(COL_KA), narrow(COL_VA), wide(COL_QB), wide(COL_KB), wide(COL_VB),
                  wide(COL_QD), wide(COL_KD), wide(COL_VD), tab, tab, tab],
        out_specs=[o[0] for o in outs],
        out_shape=[o[1] for o in outs],
        compiler_params=_params("parallel"),
        name="latent_prep",
    )(slab, slab, slab, slab, slab, slab, slab, slab, slab, cos, sa, sb)


def _win_attn_kernel(scal_ref, q_ref, kp_ref, kc_ref, kn_ref, vp_ref, vc_ref, vn_ref, kctx_ref, vctx_ref,
                     g_ref, o_ref):
    n = pl.program_id(1)
    nb = pl.num_programs(1)
    tq = TQ_A
    group = A_HEADS // A_KV_HEADS
    k = jnp.concatenate([kp_ref[...], kc_ref[...], kn_ref[...], kctx_ref[...].astype(BF16)], axis=0)
    v = jnp.concatenate([vp_ref[...], vc_ref[...], vn_ref[...], vctx_ref[...].astype(BF16)], axis=0)
    nk = 3 * tq + PAST_LEN
    row = lax.broadcasted_iota(jnp.int32, (group * tq, nk), 0) & (tq - 1)
    col = lax.broadcasted_iota(jnp.int32, (group * tq, nk), 1)
    prev_thr = row + jnp.where(n > 0, 0, tq)
    next_thr = row + 2 * tq - jnp.where(n < nb - 1, 0, tq)
    masked = ((col < tq) & (col < prev_thr)) | ((col >= 2 * tq) & (col < 3 * tq) & (col > next_thr))
    valid = jnp.logical_not(masked)
    lo = _low_half((tq, LANES))
    heads = []
    for hk in range(A_KV_HEADS):
        q4 = jnp.concatenate([q_ref[:, (hk * group + g) * LANES:(hk * group + g + 1) * LANES]
                              for g in range(group)], axis=0)
        s = jnp.where(valid, _nt(q4, k) * SCALE, NEG_INF)
        sink = jnp.concatenate([jnp.full((tq, 1), scal_ref[hk * group + g], F32) for g in range(group)], axis=0)
        (e,), den = _softmax_unnorm([s], sink)
        o = _mm(e.astype(BF16), v) / den
        for g in range(group):
            h = hk * group + g
            og = o[g * tq:(g + 1) * tq]
            heads.append(og if (h % 2) == hk else pltpu.roll(og, HEAD_DIM, 1))
    oa = jnp.concatenate([jnp.where(lo, heads[2 * c], heads[2 * c + 1]) for c in range(A_HEADS // 2)], axis=1)
    o_ref[...] = _rms_gain(oa, g_ref[...]).astype(BF16)


def _window_attention(qa_p, ka_r, va_b, cache_k, cache_v, scal, g_mix2d):
    tq = TQ_A
    nb = DEC_SEQ // tq

    def band(d):
        return pl.BlockSpec((tq, LANES), lambda b, n: (b * nb + jnp.clip(n + d, 0, nb - 1), 0))

    ctx = pl.BlockSpec((None, PAST_LEN, LANES), lambda b, n: (b, 0, 0))
    return pl.pallas_call(
        _win_attn_kernel,
        grid=(DEC_BATCH, nb),
        in_specs=[pl.BlockSpec(memory_space=pltpu.SMEM),
                  pl.BlockSpec((tq, A_HEADS * LANES), lambda b, n: (b * nb + n, 0)),
                  band(-1), band(0), band(1), band(-1), band(0), band(1), ctx, ctx,
                  pl.BlockSpec((1, 512), lambda b, n: (0, 0))],
        out_specs=pl.BlockSpec((tq, 512), lambda b, n: (b * nb + n, 0)),
        out_shape=jax.ShapeDtypeStruct((N_LAT, 512), BF16),
        compiler_params=_params("parallel", "parallel"),
        name="window_attention",
    )(scal, qa_p, ka_r, ka_r, ka_r, va_b, va_b, va_b, cache_k, cache_v, g_mix2d)


def _diff_attn_kernel(scal_ref, q_ref, k_ref, v_ref, kctx_ref, vctx_ref, g_ref, o_ref, *, lam_init):
    lam = scal_ref[A_HEADS]
    chunks = []
    for h in range(B_HEADS):
        sl = slice(h * LANES, (h + 1) * LANES)
        kh = k_ref[:, sl]
        kc = kctx_ref[:, sl].astype(BF16)
        es, dens = [], []
        for m in range(2):
            q = q_ref[:, (2 * h + m) * LANES:(2 * h + m + 1) * LANES]
            e, d = _softmax_unnorm([_nt(q, kh) * SCALE, _nt(q, kc) * SCALE])
            es.append(e)
            dens.append(d)
        c1 = 1.0 / dens[0]
        c2 = lam / dens[1]
        w_lat = (es[0][0] * c1 - es[1][0] * c2).astype(BF16)
        w_ctx = (es[0][1] * c1 - es[1][1] * c2).astype(BF16)
        o = _mm(w_lat, v_ref[:, sl]) + _mm(w_ctx, vctx_ref[:, sl].astype(BF16))
        chunks.append(_rms_gain(o, g_ref[:, sl]) * (1.0 - lam_init))
    o_ref[...] = jnp.concatenate(chunks, axis=1).astype(BF16)


def _diff_attention(qb_p, kb_r, vb_b, cache_k, cache_v, scal, g_mix2d, lam_init):
    tq = TQ_B
    nb = DEC_SEQ // tq
    full = pl.BlockSpec((DEC_SEQ, 512), lambda b, n: (b, 0))
    ctx = pl.BlockSpec((None, PAST_LEN, 512), lambda b, n: (b, 0, 0))
    return pl.pallas_call(
        functools.partial(_diff_attn_kernel, lam_init=lam_init),
        grid=(DEC_BATCH, nb),
        in_specs=[pl.BlockSpec(memory_space=pltpu.SMEM),
                  pl.BlockSpec((tq, 2 * B_HEADS * LANES), lambda b, n: (b * nb + n, 0)),
                  full, full, ctx, ctx,
                  pl.BlockSpec((1, 512), lambda b, n: (0, 1))],
        out_specs=pl.BlockSpec((tq, 512), lambda b, n: (b * nb + n, 0)),
        out_shape=jax.ShapeDtypeStruct((N_LAT, 512), BF16),
        compiler_params=_params("parallel", "parallel"),
        name="diff_attention",
    )(scal, qb_p, kb_r, vb_b, cache_k, cache_v, g_mix2d)


def _na_window_start(step):
    return np.clip(step * NA_QR - NA_ROWS // 2, 0, GRID_W - NA_KR)


def _na_bias_indices():
    rows = DEC_SEQ // GRID_W
    steps = rows // NA_QR
    pats = []
    for step in range(steps):
        w0 = _na_window_start(step)
        r = step * NA_QR + np.arange(NA_QR)[:, None, None, None]
        c = np.arange(GRID_W)[None, :, None, None]
        kr = w0 + np.arange(NA_KR)[None, None, :, None]
        kc = np.arange(GRID_W)[None, None, None, :]
        kr0 = np.clip(r - NA_ROWS // 2, 0, rows - NA_ROWS)
        kc0 = np.clip(c - NA_COLS // 2, 0, GRID_W - NA_COLS)
        valid = (kr >= kr0) & (kr < kr0 + NA_ROWS) & (kc >= kc0) & (kc < kc0 + NA_COLS)
        drow = np.clip(kr - r + NA_ROWS - 1, 0, 2 * NA_ROWS - 2)
        dcol = np.clip(kc - c + NA_COLS - 1, 0, 2 * NA_COLS - 2)
        shape = (NA_QR * GRID_W, NA_KR * GRID_W)
        full = np.broadcast_to
        pats.append((full(drow, valid.shape).reshape(shape), full(dcol, valid.shape).reshape(shape),
                     valid.reshape(shape)))
    for step in range(2, steps - 1):
        for a, b in zip(pats[1], pats[step]):
            assert np.array_equal(a, b)
    kinds = [pats[0], pats[1], pats[-1]]
    drow, dcol, valid = (np.stack([k[i] for k in kinds]) for i in range(3))
    shape6 = (len(kinds), NA_QR, GRID_W, NA_KR, GRID_W)
    row_sel = np.eye(2 * NA_ROWS - 1, dtype=np.float32)[drow.reshape(shape6)[:, :, 0, :, 0]]
    col_sel = np.eye(2 * NA_COLS - 1, dtype=np.float32)[dcol.reshape(shape6)[0, 0, :, 0, :]]
    return row_sel, col_sel, valid


def _na_bias(rpb, row_sel, col_sel, valid):
    hp = lax.Precision.HIGHEST
    cols = jnp.einsum('hab,cdb->hacd', rpb, col_sel, precision=hp)
    bias = jnp.einsum('kria,hacd->khrcid', row_sel, cols, precision=hp)
    bias = bias.reshape(valid.shape[0], D_HEADS, NA_QR * GRID_W, NA_KR * GRID_W)
    return jnp.where(valid[:, None], bias, NEG_INF)


def _na_kernel(q_ref, k_ref, v_ref, kctx_ref, vctx_ref, bias_ref, g_ref, o_ref):
    step = pl.program_id(1)
    nq = NA_QR * GRID_W
    nk = NA_KR * GRID_W
    w0 = jnp.clip(step * NA_QR - NA_ROWS // 2, 0, GRID_W - NA_KR)
    start = pl.multiple_of(w0 * GRID_W, GRID_W)
    lo = _low_half((nq, LANES))
    chunks = []
    for c in range(D_HEADS // 2):
        sl = slice(c * LANES, (c + 1) * LANES)
        kw = k_ref[pl.ds(start, nk), sl]
        vw = v_ref[pl.ds(start, nk), sl]
        kc = kctx_ref[:, sl].astype(BF16)
        vc = vctx_ref[:, sl].astype(BF16)
        halves = []
        for half in range(2):
            h = 2 * c + half
            q = q_ref[:, h * LANES:(h + 1) * LANES]
            s_loc = _nt(q, kw) * SCALE + bias_ref[h]
            s_ctx = _nt(q, kc) * SCALE
            (e_loc, e_ctx), den = _softmax_unnorm([s_loc, s_ctx])
            halves.append((_mm(e_loc.astype(BF16), vw) + _mm(e_ctx.astype(BF16), vc)) / den)
        chunks.append(jnp.where(lo, halves[0], halves[1]))
    od = jnp.concatenate(chunks, axis=1)
    o_ref[...] = _rms_gain(od, g_ref[...]).astype(BF16)


def _neighbourhood_attention(qd_p, kd_b, vd_b, cache_k, cache_v, bias, g_mix2d):
    nq = NA_QR * GRID_W
    steps = DEC_SEQ // nq
    full = pl.BlockSpec((DEC_SEQ, 512), lambda b, j: (b, 0))
    ctx = pl.BlockSpec((None, PAST_LEN, 512), lambda b, j: (b, 0, 0))

    def kind(b, j):
        return (jnp.where(j == 0, 0, jnp.where(j == steps - 1, 2, 1)), 0, 0, 0)

    return pl.pallas_call(
        _na_kernel,
        grid=(DEC_BATCH, steps),
        in_specs=[pl.BlockSpec((nq, D_HEADS * LANES), lambda b, j: (b * steps + j, 0)),
                  full, full, ctx, ctx,
                  pl.BlockSpec((None, D_HEADS, nq, NA_KR * GRID_W), kind),
                  pl.BlockSpec((1, 512), lambda b, j: (0, 3))],
        out_specs=pl.BlockSpec((nq, 512), lambda b, j: (b * steps + j, 0)),
        out_shape=jax.ShapeDtypeStruct((N_LAT, 512), BF16),
        compiler_params=_params("parallel", "arbitrary"),
        name="neighbourhood_attention",
    )(qd_p, kd_b, vd_b, cache_k, cache_v, bias, g_mix2d)


N_SCAN_TILES = N_TOK // T_SCAN
CTX_TILES = N_CTX // T_SCAN
TILES_PER_LAT = DEC_SEQ // T_SCAN


def _scan_tile_flags(i):
    is_ctx = i < CTX_TILES
    pos = (i - CTX_TILES) % TILES_PER_LAT
    return is_ctx | (pos == 0), is_ctx | (pos == TILES_PER_LAT - 1)


def _scan_seq(i):
    return jnp.where(i < CTX_TILES, i, CTX_TILES + (i - CTX_TILES) // TILES_PER_LAT)


def _conv_gates(x_ref, prev_ref, next_ref, cw_ref, cb_ref, wgh_ref, wgl_ref, bg_ref, lam_ref, first, last):
    x = x_ref[...]
    t = T_SCAN
    row = lax.broadcasted_iota(jnp.int32, x.shape, 0)
    pm1 = jnp.where(first, 0.0, prev_ref[7:8, :])
    n0 = jnp.where(last, 0.0, next_ref[0:1, :])
    n1 = jnp.where(last, 0.0, next_ref[1:2, :])
    x_m1 = jnp.where(row == 0, pm1, pltpu.roll(x, 1, 0))
    x_p1 = jnp.where(row == t - 1, n0, pltpu.roll(x, t - 1, 0))
    x_p2 = jnp.where(row == t - 2, n0, jnp.where(row == t - 1, n1, pltpu.roll(x, t - 2, 0)))
    y = (cw_ref[0:1, :] * x_m1 + cw_ref[1:2, :] * x + cw_ref[2:3, :] * x_p1 + cw_ref[3:4, :] * x_p2
         + cb_ref[...])
    y_hi = y.astype(BF16)
    y_lo = (y - y_hi.astype(F32)).astype(BF16)
    zr, zi = [], []
    for c in range(C_WIDTH // LANES):
        sl = slice(c * LANES, (c + 1) * LANES)
        w_hi = wgh_ref[0, sl, :]
        z = _mm(y_hi[:, sl], w_hi) + _mm(y_lo[:, sl], w_hi) + _mm(y_hi[:, sl], wgl_ref[0, sl, :])
        zr.append(z[:, :LANES])
        zi.append(z[:, LANES:])
    bias = bg_ref[0]
    r = jax.nn.sigmoid(jnp.concatenate(zr, axis=1) + bias[:, :C_WIDTH])
    gate_i = jax.nn.sigmoid(jnp.concatenate(zi, axis=1) + bias[:, C_WIDTH:])
    nl = -lam_ref[0]
    softplus = jnp.maximum(nl, 0.0) + jnp.log1p(jnp.exp(-jnp.abs(nl)))
    log_a = -C_POW * r * softplus
    a = jnp.exp(log_a)
    u = jnp.sqrt(1.0 - jnp.exp(2.0 * log_a)) * gate_i * y
    return a, u


def _scan_rows(a_s, u_s, h_s, h, reverse):
    groups = T_SCAN // 8

    def body(i, h):
        g = (groups - 1 - i) if reverse else i
        base = pl.multiple_of(g * 8, 8)
        a8 = a_s[pl.ds(base, 8), :]
        u8 = u_s[pl.ds(base, 8), :]
        rows = [None] * 8
        for k in (range(7, -1, -1) if reverse else range(8)):
            h = a8[k:k + 1, :] * h + u8[k:k + 1, :]
            rows[k] = h
        h_s[pl.ds(base, 8), :] = jnp.concatenate(rows, axis=0)
        return h

    return lax.fori_loop(0, groups, body, h)


def _scan_fwd_kernel(x_ref, prev_ref, next_ref, cw_ref, cb_ref, wgh_ref, wgl_ref, bg_ref, lam_ref, h0_ref,
                     hf_ref, fin_ref, a_s, u_s, carry_s):
    i = pl.program_id(0)
    first, last = _scan_tile_flags(i)
    a, u = _conv_gates(x_ref, prev_ref, next_ref, cw_ref, cb_ref, wgh_ref, wgl_ref, bg_ref, lam_ref, first, last)
    a_s[...] = a
    u_s[...] = u

    @pl.when(first)
    def _():
        carry_s[...] = h0_ref[...]

    h = _scan_rows(a_s, u_s, hf_ref, carry_s[...], reverse=False)
    carry_s[...] = h
    fin_ref[...] = h


def _scan_bwd_kernel(x_ref, prev_ref, next_ref, cw_ref, cb_ref, wgh_ref, wgl_ref, bg_ref, lam_ref, h0_ref,
                     hf_ref, gc_ref, g_ref, oc_ref, fin_ref, a_s, u_s, hb_s, carry_s):
    i = N_SCAN_TILES - 1 - pl.program_id(0)
    first, last = _scan_tile_flags(i)
    a, u = _conv_gates(x_ref, prev_ref, next_ref, cw_ref, cb_ref, wgh_ref, wgl_ref, bg_ref, lam_ref, first, last)
    a_s[...] = a
    u_s[...] = u

    @pl.when(last)
    def _():
        carry_s[...] = h0_ref[...]

    h = _scan_rows(a_s, u_s, hb_s, carry_s[...], reverse=True)
    carry_s[...] = h
    fin_ref[...] = h
    g = gc_ref[...]
    gelu = 0.5 * g * (1.0 + jnp.tanh(math.sqrt(2.0 / math.pi) * (g + 0.044715 * (g * g * g))))
    oc_ref[...] = _rms_gain(gelu * (hf_ref[...] + hb_s[...]), g_ref[...]).astype(BF16)


def _scan_common_specs(tile_of, direction):
    halo = T_SCAN // 8
    last_block = N_TOK // 8 - 1
    return [pl.BlockSpec((T_SCAN, 512), lambda j: (tile_of(j), COL_XC)),
            pl.BlockSpec((8, 512), lambda j: (jnp.maximum(tile_of(j) * halo - 1, 0), COL_XC)),
            pl.BlockSpec((8, 512), lambda j: (jnp.minimum((tile_of(j) + 1) * halo, last_block), COL_XC)),
            pl.BlockSpec((4, C_WIDTH), lambda j: (0, 0)),
            pl.BlockSpec((1, C_WIDTH), lambda j: (0, 0)),
            pl.BlockSpec((1, C_WIDTH, 2 * LANES), lambda j: (direction, 0, 0)),
            pl.BlockSpec((1, C_WIDTH, 2 * LANES), lambda j: (direction, 0, 0)),
            pl.BlockSpec((1, 1, 2 * C_WIDTH), lambda j: (direction, 0, 0)),
            pl.BlockSpec((1, 1, C_WIDTH), lambda j: (direction, 0, 0)),
            pl.BlockSpec((None, 1, C_WIDTH), lambda j: (_scan_seq(tile_of(j)), 0, 0))]


def _rglru(slab, conv_w, conv_b, w_gates, b_gates, lam, h0_f, h0_b, g_mix2d):
    wg_hi = w_gates.astype(BF16)
    wg_lo = (w_gates - wg_hi.astype(F32)).astype(BF16)
    fin = jax.ShapeDtypeStruct((N_SCAN_TILES, 1, C_WIDTH), F32)
    fwd_tile = lambda j: j
    h_f, fin_f = pl.pallas_call(
        _scan_fwd_kernel,
        grid=(N_SCAN_TILES,),
        in_specs=_scan_common_specs(fwd_tile, 0),
        out_specs=[pl.BlockSpec((T_SCAN, C_WIDTH), lambda j: (j, 0)),
                   pl.BlockSpec((None, 1, C_WIDTH), lambda j: (j, 0, 0))],
        out_shape=[jax.ShapeDtypeStruct((N_TOK, C_WIDTH), F32), fin],
        scratch_shapes=[pltpu.VMEM((T_SCAN, C_WIDTH), F32), pltpu.VMEM((T_SCAN, C_WIDTH), F32),
                        pltpu.VMEM((1, C_WIDTH), F32)],
        compiler_params=_params("arbitrary"),
        name="rglru_forward",
    )(slab, slab, slab, conv_w, conv_b, wg_hi, wg_lo, b_gates, lam, h0_f)
    bwd_tile = lambda j: N_SCAN_TILES - 1 - j
    oc, fin_b = pl.pallas_call(
        _scan_bwd_kernel,
        grid=(N_SCAN_TILES,),
        in_specs=_scan_common_specs(bwd_tile, 1) + [
            pl.BlockSpec((T_SCAN, C_WIDTH), lambda j: (bwd_tile(j), 0)),
            pl.BlockSpec((T_SCAN, 512), lambda j: (bwd_tile(j), COL_GC)),
            pl.BlockSpec((1, 512), lambda j: (0, 2))],
        out_specs=[pl.BlockSpec((T_SCAN, C_WIDTH), lambda j: (bwd_tile(j), 0)),
                   pl.BlockSpec((None, 1, C_WIDTH), lambda j: (bwd_tile(j), 0, 0))],
        out_shape=[jax.ShapeDtypeStruct((N_TOK, C_WIDTH), BF16), fin],
        scratch_shapes=[pltpu.VMEM((T_SCAN, C_WIDTH), F32), pltpu.VMEM((T_SCAN, C_WIDTH), F32),
                        pltpu.VMEM((T_SCAN, C_WIDTH), F32), pltpu.VMEM((1, C_WIDTH), F32)],
        compiler_params=_params("arbitrary"),
        name="rglru_backward",
    )(slab, slab, slab, conv_w, conv_b, wg_hi, wg_lo, b_gates, lam, h0_b, h_f, slab, g_mix2d)
    return oc, fin_f, fin_b


def _mixout_kernel(oa_ref, ob_ref, oc_ref, od_ref, w_ref, x_ref, g1_ref, sc2_ref, sh2_ref, lng_ref, lnb_ref,
                   wrh_ref, wrl_ref, br_ref, x1_ref, h2_ref, idx_ref, wt_ref, rank_ref, cnt_ref, cnt_s):
    @pl.when(pl.program_id(0) == 0)
    def _():
        cnt_s[...] = jnp.zeros(cnt_s.shape, F32)

    parts = 2
    pm = x_ref.shape[0] // parts
    for part in range(parts):
        _mixout_rows(pl.ds(part * pm, pm), part * pm, pm, oa_ref, ob_ref, oc_ref, od_ref, w_ref, x_ref, g1_ref,
                     sc2_ref, sh2_ref, lng_ref, lnb_ref, wrh_ref, wrl_ref, br_ref, x1_ref, h2_ref, idx_ref,
                     wt_ref, rank_ref, cnt_s)
    cnt_ref[...] = cnt_s[...]


def _mixout_rows(rows, first, tm, oa_ref, ob_ref, oc_ref, od_ref, w_ref, x_ref, g1_ref, sc2_ref, sh2_ref,
                 lng_ref, lnb_ref, wrh_ref, wrl_ref, br_ref, x1_ref, h2_ref, idx_ref, wt_ref, rank_ref, cnt_s):
    y = _mm(oa_ref[rows, :], w_ref[0:512, :])
    y += _mm(ob_ref[rows, :], w_ref[512:1024, :])
    y += _mm(oc_ref[rows, :], w_ref[1024:1536, :])
    y += _mm(od_ref[rows, :], w_ref[1536:2048, :])
    x1 = _layernorm(ALPHA * x_ref[rows, :] + g1_ref[...] * y, lng_ref[...], lnb_ref[...])
    x1_ref[rows, :] = x1
    h2 = x1 * (1.0 + sc2_ref[...]) + sh2_ref[...]
    _store_token_tiles(h2_ref, (), _pack_rows(h2), first)

    h_hi = h2.astype(BF16)
    h_lo = (h2 - h_hi.astype(F32)).astype(BF16)
    scores = jax.nn.sigmoid(_mm(h_hi, wrh_ref[...]) + _mm(h_lo, wrh_ref[...]) + _mm(h_hi, wrl_ref[...]))
    sel = scores + br_ref[...]
    lane = lax.broadcasted_iota(jnp.int32, sel.shape, 1).astype(F32)
    slot = lax.broadcasted_iota(jnp.int32, (tm, 8), 1)
    idx_out = jnp.zeros((tm, 8), F32)
    wt_out = jnp.zeros((tm, 8), F32)
    total = jnp.zeros((sel.shape[0], 1), F32)
    hits = []
    for k in range(TOP_K):
        m = sel.max(axis=-1, keepdims=True)
        idx = jnp.where(sel == m, lane, float(N_EXPERTS)).min(axis=-1, keepdims=True)
        hit = lane == idx
        hits.append(hit)
        w = jnp.where(hit, scores, 0.0).sum(axis=-1, keepdims=True)
        total = total + w
        idx_out = jnp.where(slot == k, idx, idx_out)
        wt_out = jnp.where(slot == k, w, wt_out)
        sel = jnp.where(hit, -jnp.inf, sel)
    idx_ref[rows, :] = idx_out.astype(jnp.int32)
    wt_ref[rows, :] = wt_out / total * ROUTED_SCALE

    chosen = jnp.zeros(sel.shape, F32)
    for hit in hits:
        chosen = chosen + hit.astype(F32)
    earlier = (lax.broadcasted_iota(jnp.int32, (tm, tm), 1) < lax.broadcasted_iota(jnp.int32, (tm, tm), 0))
    before = _mm(earlier.astype(F32).astype(BF16), chosen.astype(BF16)) + cnt_s[...]
    rank_out = jnp.zeros((tm, 8), F32)
    for k, hit in enumerate(hits):
        rank_out = jnp.where(slot == k, jnp.where(hit, before, 0.0).sum(axis=-1, keepdims=True), rank_out)
    rank_ref[rows, :] = rank_out.astype(jnp.int32)
    cnt_s[...] = cnt_s[...] + chosen.sum(axis=0, keepdims=True)


def _mixer_out(oa, ob, oc, od, w_out_b, x, mod, ln_g, ln_b, w_router, b_router):
    tm = TM_OUT
    cat = pl.BlockSpec((tm, 512), lambda i: (i, 0))
    row = pl.BlockSpec((tm, D_MODEL), lambda i: (i, 0))
    vec = pl.BlockSpec((1, D_MODEL), lambda i: (0, 0))
    k8 = pl.BlockSpec((tm, 8), lambda i: (i, 0))
    per_expert = pl.BlockSpec((1, N_EXPERTS), lambda i: (0, 0))
    router = pl.BlockSpec((D_MODEL, N_EXPERTS), lambda i: (0, 0))
    wr_hi = w_router.astype(BF16)
    wr_lo = (w_router - wr_hi.astype(F32)).astype(BF16)
    return pl.pallas_call(
        _mixout_kernel,
        grid=(N_TOK // tm,),
        in_specs=[cat, cat, cat, cat,
                  pl.BlockSpec((D_MODEL, D_MODEL), lambda i: (0, 0)),
                  row, _mod_spec(2, tm), _mod_spec(4, tm), _mod_spec(3, tm), vec, vec,
                  router, router, per_expert],
        out_specs=[row, pl.BlockSpec((tm * SUBLANES, LANES), lambda i: (i, 0)), k8, k8, k8, per_expert],
        out_shape=[jax.ShapeDtypeStruct((N_TOK, D_MODEL), F32),
                   jax.ShapeDtypeStruct((N_TOK * SUBLANES, LANES), jnp.uint32),
                   jax.ShapeDtypeStruct((N_TOK, 8), jnp.int32), jax.ShapeDtypeStruct((N_TOK, 8), F32),
                   jax.ShapeDtypeStruct((N_TOK, 8), jnp.int32), jax.ShapeDtypeStruct((1, N_EXPERTS), F32)],
        scratch_shapes=[pltpu.VMEM((1, N_EXPERTS), F32)],
        compiler_params=_params("arbitrary"),
        name="mixer_out_router",
    )(oa, ob, oc, od, w_out_b, x, mod, mod, mod, ln_g, ln_b, wr_hi, wr_lo, b_router)


MOE_BLOCKS = N_TOK * TOP_K // MOE_BM + N_EXPERTS
MOE_ROWS = MOE_BLOCKS * MOE_BM


def _row_copy(src, src_row, dst, dst_row, sem):
    def tile(row):
        return pl.ds(row * SUBLANES if isinstance(row, int) else pl.multiple_of(row * SUBLANES, SUBLANES), SUBLANES)

    return pltpu.make_async_copy(src.at[tile(src_row)], dst.at[tile(dst_row)], sem)


def _dispatch_kernel(slot_ref, h_ref, xg_ref, sem):
    tm = h_ref.shape[0] // SUBLANES
    base = pl.program_id(0) * (tm * TOP_K)

    for r in range(tm):
        for k in range(TOP_K):
            _row_copy(h_ref, r, xg_ref, slot_ref[base + (r * TOP_K + k)], sem).start()
    for k in range(TOP_K):
        pltpu.make_async_copy(h_ref, xg_ref.at[pl.ds(0, tm * SUBLANES)], sem).wait()


def _dispatch_rows(slot_flat, h2):
    tm = TM_DISPATCH
    grid_spec = pltpu.PrefetchScalarGridSpec(
        num_scalar_prefetch=1,
        grid=(N_TOK // tm,),
        in_specs=[pl.BlockSpec((tm * SUBLANES, LANES), lambda i, s: (i, 0))],
        out_specs=pl.BlockSpec(memory_space=pl.ANY),
        scratch_shapes=[pltpu.SemaphoreType.DMA])
    return pl.pallas_call(
        _dispatch_kernel,
        grid_spec=grid_spec,
        out_shape=jax.ShapeDtypeStruct((MOE_ROWS * SUBLANES, LANES), jnp.uint32),
        compiler_params=_params("arbitrary"),
        name="moe_dispatch",
    )(slot_flat, h2)


def _expert_kernel(blk_e_ref, blk_n_ref, x_ref, wg_ref, wu_ref, wd_ref, o_ref, wg_s, wu_s, wd_s):
    i = pl.program_id(0)
    prev = blk_e_ref[jnp.maximum(i - 1, 0)]

    @pl.when((i == 0) | (blk_e_ref[i] != prev))
    def _():
        wg_s[...] = wg_ref[...].astype(BF16)
        wu_s[...] = wu_ref[...].astype(BF16)
        wd_s[...] = wd_ref[...].astype(BF16)

    n_rows = blk_n_ref[i]

    @pl.when(n_rows > 0)
    def _():
        x = _load_token_tiles(x_ref, (), MOE_BM)
        row = lax.broadcasted_iota(jnp.int32, x.shape, 0)
        lo, hi = _unpack_rows(jnp.where(row < n_rows, x, jnp.uint32(0)))
        lo, hi = lo.astype(BF16), hi.astype(BF16)
        gate = _mm(lo, wg_s[:HALF_D, :]) + _mm(hi, wg_s[HALF_D:, :])
        up = _mm(lo, wu_s[:HALF_D, :]) + _mm(hi, wu_s[HALF_D:, :])
        _store_token_tiles(o_ref, (), _pack_rows(_mm((_silu(gate) * up).astype(BF16), wd_s[...])))

    @pl.when(n_rows == 0)
    def _():
        o_ref[...] = jnp.zeros(o_ref.shape, jnp.uint32)


def _routed_experts(xg, blk_e, blk_n, layer, w_g, w_u, w_d):
    bm = MOE_BM

    def weight(shape):
        return pl.BlockSpec((None, None) + shape, lambda i, e, n: (layer, e[i], 0, 0))

    grid_spec = pltpu.PrefetchScalarGridSpec(
        num_scalar_prefetch=2,
        grid=(MOE_BLOCKS,),
        in_specs=[pl.BlockSpec((bm * SUBLANES, LANES), lambda i, e, n: (i, 0)),
                  weight((D_MODEL, D_EXPERT)), weight((D_MODEL, D_EXPERT)), weight((D_EXPERT, D_MODEL))],
        out_specs=pl.BlockSpec((bm * SUBLANES, LANES), lambda i, e, n: (i, 0)),
        scratch_shapes=[pltpu.VMEM((D_MODEL, D_EXPERT), BF16), pltpu.VMEM((D_MODEL, D_EXPERT), BF16),
                        pltpu.VMEM((D_EXPERT, D_MODEL), BF16)])
    return pl.pallas_call(
        _expert_kernel,
        grid_spec=grid_spec,
        out_shape=jax.ShapeDtypeStruct((MOE_ROWS * SUBLANES, LANES), jnp.uint32),
        compiler_params=_params("arbitrary"),
        name="routed_experts",
    )(blk_e, blk_n, xg, w_g, w_u, w_d)


def _ffn_out_kernel(slot_ref, h_ref, wt_ref, x1_ref, wg_ref, wu_ref, wd_ref, g2_ref, lng_ref, lnb_ref, y_ref,
                    o_ref, ybuf, sems):
    tm = h_ref.shape[0] // SUBLANES
    i = pl.program_id(0)
    n = pl.num_programs(0)

    def gather(tile, buf):
        base = tile * (tm * TOP_K)

        for r in range(tm):
            for k in range(TOP_K):
                _row_copy(y_ref, slot_ref[base + (r * TOP_K + k)], ybuf.at[buf, k], r, sems.at[buf]).start()

    @pl.when(i == 0)
    def _():
        gather(0, 0)

    @pl.when(i + 1 < n)
    def _():
        gather(i + 1, (i + 1) % 2)

    lo, hi = _unpack_rows(_load_token_tiles(h_ref, (), tm))
    lo, hi = lo.astype(BF16), hi.astype(BF16)
    gate = _mm(lo, wg_ref[:HALF_D, :]) + _mm(hi, wg_ref[HALF_D:, :])
    up = _mm(lo, wu_ref[:HALF_D, :]) + _mm(hi, wu_ref[HALF_D:, :])
    y = _mm((_silu(gate) * up).astype(BF16), wd_ref[...])
    buf = i % 2
    for k in range(TOP_K):
        pltpu.make_async_copy(y_ref.at[pl.ds(0, tm * SUBLANES)], ybuf.at[buf, k], sems.at[buf]).wait()
    wt = wt_ref[...]
    y_lo, y_hi = y[:, :HALF_D], y[:, HALF_D:]
    for k in range(TOP_K):
        r_lo, r_hi = _unpack_rows(_load_token_tiles(ybuf, (buf, k), tm))
        y_lo = y_lo + r_lo * wt[:, k:k + 1]
        y_hi = y_hi + r_hi * wt[:, k:k + 1]
    y = jnp.concatenate([y_lo, y_hi], axis=1)
    o_ref[...] = _layernorm(ALPHA * x1_ref[...] + g2_ref[...] * y, lng_ref[...], lnb_ref[...])


def _ffn_out(slot_flat, h2, wt, x1, ws_g, ws_u, ws_d, mod, ln_g, ln_b, y_blk):
    tm = TM_FFN
    row = pl.BlockSpec((tm, D_MODEL), lambda i, s: (i, 0))
    vec = pl.BlockSpec((1, D_MODEL), lambda i, s: (0, 0))
    grid_spec = pltpu.PrefetchScalarGridSpec(
        num_scalar_prefetch=1,
        grid=(N_TOK // tm,),
        in_specs=[pl.BlockSpec((tm * SUBLANES, LANES), lambda i, s: (i, 0)),
                  pl.BlockSpec((tm, 8), lambda i, s: (i, 0)), row,
                  pl.BlockSpec((D_MODEL, D_EXPERT), lambda i, s: (0, 0)),
                  pl.BlockSpec((D_MODEL, D_EXPERT), lambda i, s: (0, 0)),
                  pl.BlockSpec((D_EXPERT, D_MODEL), lambda i, s: (0, 0)),
                  _mod_spec(5, tm), vec, vec,
                  pl.BlockSpec(memory_space=pl.ANY)],
        out_specs=row,
        scratch_shapes=[pltpu.VMEM((2, TOP_K, tm * SUBLANES, LANES), jnp.uint32),
                        pltpu.SemaphoreType.DMA((2,))])
    return pl.pallas_call(
        _ffn_out_kernel,
        grid_spec=grid_spec,
        out_shape=jax.ShapeDtypeStruct((N_TOK, D_MODEL), F32),
        compiler_params=_params("arbitrary"),
        name="shared_expert_out",
    )(slot_flat, h2, wt, x1, ws_g, ws_u, ws_d, mod, ln_g, ln_b, y_blk)


def _dispatch_plan(idx, rank, counts):
    bm = MOE_BM
    counts = counts.reshape(N_EXPERTS).astype(jnp.int32)
    padded = (counts + bm - 1) // bm * bm
    pad_end = jnp.cumsum(padded)
    pad_start = pad_end - padded
    slot = pad_start[idx[:, :TOP_K]] + rank[:, :TOP_K]
    blk_row = jnp.arange(MOE_BLOCKS, dtype=jnp.int32) * bm
    blk_e = jnp.minimum(jnp.sum(pad_end[None, :] <= blk_row[:, None], axis=1), N_EXPERTS - 1).astype(jnp.int32)
    blk_n = jnp.clip(counts[blk_e] - (blk_row - pad_start[blk_e]), 0, bm).astype(jnp.int32)
    return slot.reshape(-1).astype(jnp.int32), blk_e, blk_n


def _rope_tables():
    t = jnp.arange(DEC_SEQ)
    row = (t // GRID_W).astype(F32)
    col = (t % GRID_W).astype(F32)
    nf = HEAD_DIM // 4
    inv = ROPE_BASE ** (-jnp.arange(nf, dtype=F32) / nf)
    ar = row[:, None] * inv[None, :]
    ac = col[:, None] * inv[None, :]
    ang = jnp.concatenate([ar, ar, ac, ac], axis=-1)
    ang = jnp.concatenate([ang, ang], axis=-1)
    cos, sin = jnp.cos(ang), jnp.sin(ang)
    first = (jnp.arange(LANES) % (2 * nf)) < nf
    return cos, jnp.where(first, -sin, 0.0), jnp.where(first, 0.0, sin)


def _lambda_init(l):
    return 0.8 - 0.6 * math.exp(-0.3 * l)


def _gate_pairs(w_r, w_i):
    per = LANES // C_BW
    eye = jnp.eye(per, dtype=w_r.dtype)

    def chunk_diag(w):
        w = w.reshape(C_BLOCKS // per, per, C_BW, C_BW)
        return (eye[None, :, None, :, None] * w[:, :, :, None, :]).reshape(C_BLOCKS // per, LANES, LANES)

    return jnp.concatenate([chunk_diag(w_r), chunk_diag(w_i)], axis=2).reshape(C_WIDTH, 2 * LANES)


def kernel(x_prompt, x_sample, c, cache_a_k, cache_a_v, cache_b_k, cache_b_v, state_c, cache_d_k, cache_d_v, c_ctx, w_mod, b_mod, w_in, a_sink, b_lambda, c_conv_w, c_conv_b, c_w_rgate, c_b_rgate, c_w_igate, c_b_igate, c_lambda, d_rpb, g_mix, w_out, ln_g, ln_b, w_router, b_router, w_e_gate, w_e_up, w_e_down, w_s_gate, w_s_up, w_s_down):
    x = jnp.concatenate([x_prompt.reshape(N_CTX, D_MODEL), x_sample.reshape(N_LAT, D_MODEL)], axis=0)
    cond8 = jnp.zeros((8, D_MODEL), F32).at[0].set(c_ctx).at[1:1 + DEC_BATCH].set(c)
    mod_all = _modulation(cond8, w_mod, b_mod)
    cos, sin_a, sin_b = _rope_tables()
    row_sel, col_sel, na_valid = _na_bias_indices()

    order = np.concatenate([np.arange(0, 512), np.arange(768, 4864), np.arange(512, 768)])

    ctx_out = [[] for _ in range(7)]
    for l in range(DEPTH):
        lam_init = _lambda_init(l)
        mod = mod_all[l, :N_GROUPS].reshape(N_GROUPS, 6, 1, D_MODEL)
        w_in_b = w_in[l][:, order].astype(BF16)
        slab = _in_projection(x, mod, w_in_b)

        lp = b_lambda[l]
        lam = jnp.exp(jnp.sum(lp[0] * lp[1])) - jnp.exp(jnp.sum(lp[2] * lp[3])) + lam_init
        scal = jnp.zeros((16,), F32).at[:A_HEADS].set(a_sink[l]).at[A_HEADS].set(lam)
        g2d = g_mix[l].reshape(1, -1)

        oa_c, ob_c, od_c = _ctx_attention(slab, scal, g2d, lam_init)

        qa_p, ka_r, va_b, qb_p, kb_r, vb_b, qd_p, kd_b, vd_b = _latent_prep(slab, cos, sin_a, sin_b)
        oa_l = _window_attention(qa_p, ka_r, va_b, cache_a_k[:, l].reshape(DEC_BATCH, PAST_LEN, LANES),
                                 cache_a_v[:, l].reshape(DEC_BATCH, PAST_LEN, LANES), scal, g2d)
        ob_l = _diff_attention(qb_p, kb_r, vb_b, cache_b_k[:, l].reshape(DEC_BATCH, PAST_LEN, 512),
                               cache_b_v[:, l].reshape(DEC_BATCH, PAST_LEN, 512), scal, g2d, lam_init)
        bias = _na_bias(d_rpb[l], row_sel, col_sel, na_valid)
        od_l = _neighbourhood_attention(qd_p, kd_b, vd_b, cache_d_k[:, l].reshape(DEC_BATCH, PAST_LEN, 512),
                                        cache_d_v[:, l].reshape(DEC_BATCH, PAST_LEN, 512), bias, g2d)

        w_gates = jnp.stack([_gate_pairs(c_w_rgate[l, d], c_w_igate[l, d]) for d in range(2)])
        b_gates = jnp.concatenate([c_b_rgate[l], c_b_igate[l]], axis=-1).reshape(2, 1, 2 * C_WIDTH)
        zeros = jnp.zeros((BATCH, 1, C_WIDTH), F32)
        h0_f = jnp.concatenate([zeros, state_c[:, l, 0][:, None, :]], axis=0)
        h0_b = jnp.concatenate([zeros, state_c[:, l, 1][:, None, :]], axis=0)
        oc, fin_f, fin_b = _rglru(slab, c_conv_w[l], c_conv_b[l].reshape(1, -1), w_gates, b_gates,
                                  c_lambda[l].reshape(2, 1, C_WIDTH), h0_f, h0_b, g2d)

        oa = jnp.concatenate([oa_c, oa_l], axis=0)
        ob = jnp.concatenate([ob_c, ob_l], axis=0)
        od = jnp.concatenate([od_c, od_l], axis=0)
        x1, h2, idx, wt, rank, counts = _mixer_out(oa, ob, oc, od, w_out[l].astype(BF16), x, mod,
                                                   ln_g[l, 0].reshape(1, -1), ln_b[l, 0].reshape(1, -1),
                                                   w_router[l], b_router[l].reshape(1, -1))

        slot, blk_e, blk_n = _dispatch_plan(idx, rank, counts)
        xg = _dispatch_rows(slot, h2)
        y_blk = _routed_experts(xg, blk_e, blk_n, l, w_e_gate, w_e_up, w_e_down)
        x = _ffn_out(slot, h2, wt, x1, w_s_gate[l].astype(BF16), w_s_up[l].astype(BF16),
                     w_s_down[l].astype(BF16), mod, ln_g[l, 1].reshape(1, -1), ln_b[l, 1].reshape(1, -1), y_blk)

        ctx = slab[:N_CTX]
        ctx_out[0].append(ctx[:, COL_KA * LANES:(COL_KA + 1) * LANES].reshape(BATCH, SEQ, A_KV_HEADS, HEAD_DIM))
        ctx_out[1].append(ctx[:, COL_VA * LANES:(COL_VA + 1) * LANES].reshape(BATCH, SEQ, A_KV_HEADS, HEAD_DIM))
        ctx_out[2].append(ctx[:, COL_KB * 512:(COL_KB + 1) * 512].reshape(BATCH, SEQ, B_HEADS, 2, HEAD_DIM))
        ctx_out[3].append(ctx[:, COL_VB * 512:(COL_VB + 1) * 512].reshape(BATCH, SEQ, B_HEADS, 2 * HEAD_DIM))
        ctx_out[4].append(jnp.concatenate([fin_f[:BATCH], fin_b[:BATCH]], axis=1))
        ctx_out[5].append(ctx[:, COL_KD * 512:(COL_KD + 1) * 512].reshape(BATCH, SEQ, D_HEADS, HEAD_DIM))
        ctx_out[6].append(ctx[:, COL_VD * 512:(COL_VD + 1) * 512].reshape(BATCH, SEQ, D_HEADS, HEAD_DIM))

    new = [jnp.stack(t, axis=1) for t in ctx_out]
    return (x[:N_CTX].reshape(BATCH, SEQ, D_MODEL), x[N_CTX:].reshape(DEC_BATCH, DEC_SEQ, D_MODEL), *new)
```

```python
import functools
import math

import numpy as np
import jax
import jax.numpy as jnp
from jax import lax
from jax.experimental import pallas as pl
from jax.experimental.pallas import tpu as pltpu

F32 = jnp.float32
BF16 = jnp.bfloat16

D_MODEL = 2048
BATCH = 16
SEQ = 256
DEPTH = 2
DEC_BATCH = 2
DEC_SEQ = 4096
PAST_LEN = 256
GRID_W = 64
HEAD_DIM = 64
ROPE_BASE = 10000.0
A_HEADS = 8
A_KV_HEADS = 2
A_WINDOW = 128
B_HEADS = 4
C_WIDTH = 512
C_BLOCKS = 8
C_BW = C_WIDTH // C_BLOCKS
C_POW = 8.0
D_HEADS = 8
NA_ROWS = 8
NA_COLS = 16
N_EXPERTS = 64
TOP_K = 6
D_EXPERT = 512
ROUTED_SCALE = 2.5
ALPHA = (2.0 * DEPTH) ** 0.25
LN_EPS = 1e-5
NORM_EPS = 1e-6
NEG_INF = -1e30
SCALE = HEAD_DIM ** -0.5

N_CTX = BATCH * SEQ
N_LAT = DEC_BATCH * DEC_SEQ
N_TOK = N_CTX + N_LAT
GROUP_ROWS = 4096
N_GROUPS = N_TOK // GROUP_ROWS
IN_WIDTH = 4864
LANES = 128
VMEM_LIMIT = 56 * 1024 * 1024

COL_QA, COL_QB, COL_KB, COL_VB, COL_XC, COL_GC, COL_QD, COL_KD, COL_VD = range(9)
COL_KA, COL_VA = 36, 37

TM_PROJ = 512
TN_PROJ = IN_WIDTH // 2
TM_PREP = 256
TQ_A = 128
TQ_B = 128
NA_QR = 4
NA_KR = 12
T_SCAN = 256
TM_OUT = 256
MOE_BM = 256
TM_DISPATCH = 128
TM_FFN = 128


def _params(*sem):
    return pltpu.CompilerParams(dimension_semantics=sem, vmem_limit_bytes=VMEM_LIMIT)


def _nt(a, b):
    return lax.dot_general(a, b, (((1,), (1,)), ((), ())), preferred_element_type=F32)


def _mm(a, b):
    return jnp.dot(a, b, preferred_element_type=F32)


def _softmax_unnorm(parts, sink=None):
    m = parts[0].max(axis=-1, keepdims=True)
    for p in parts[1:]:
        m = jnp.maximum(m, p.max(axis=-1, keepdims=True))
    if sink is not None:
        m = jnp.maximum(m, sink)
    es = [jnp.exp(p - m) for p in parts]
    den = es[0].sum(axis=-1, keepdims=True)
    for e in es[1:]:
        den = den + e.sum(axis=-1, keepdims=True)
    if sink is not None:
        den = den + jnp.exp(sink - m)
    return es, den


def _rms_gain(x, g):
    return x * lax.rsqrt(jnp.mean(x * x, axis=-1, keepdims=True) + NORM_EPS) * g


def _layernorm(z, g, b):
    zc = z - jnp.mean(z, axis=-1, keepdims=True)
    var = jnp.mean(zc * zc, axis=-1, keepdims=True)
    return zc * lax.rsqrt(var + LN_EPS) * g + b


def _silu(x):
    return x * jax.nn.sigmoid(x)


def _low_half(shape):
    return lax.broadcasted_iota(jnp.int32, shape, 1) < HEAD_DIM


HALF_D = D_MODEL // 2
HIGH16 = 0xFFFF0000


def _pack_rows(x):
    bits = pltpu.bitcast(x.astype(BF16).astype(F32), jnp.uint32)
    return (bits[:, :HALF_D] >> 16) | (bits[:, HALF_D:] & jnp.uint32(HIGH16))


def _unpack_rows(p):
    return (pltpu.bitcast(p << 16, F32), pltpu.bitcast(p & jnp.uint32(HIGH16), F32))


SUBLANES = 8
assert HALF_D == SUBLANES * LANES


def _store_token_tiles(ref, index, packed, first=0):
    m = packed.shape[0]
    for s in range(SUBLANES):
        rows = pl.ds(first * SUBLANES + s, m, stride=SUBLANES)
        ref[index + (rows, slice(None))] = packed[:, s * LANES:(s + 1) * LANES]


def _load_token_tiles(ref, index, m):
    return jnp.concatenate([ref[index + (pl.ds(s, m, stride=SUBLANES), slice(None))] for s in range(SUBLANES)],
                           axis=1)


def _mod_kernel(c_ref, w_ref, b_ref, o_ref):
    c = c_ref[...]
    o_ref[0] = jnp.dot(_silu(c), w_ref[0], precision=lax.Precision.HIGHEST,
                       preferred_element_type=F32) + b_ref[0]


def _modulation(cond8, w_mod, b_mod):
    tn = 1024
    n = w_mod.shape[-1]
    return pl.pallas_call(
        _mod_kernel,
        grid=(DEPTH, n // tn),
        in_specs=[pl.BlockSpec((8, D_MODEL), lambda l, j: (0, 0)),
                  pl.BlockSpec((1, D_MODEL, tn), lambda l, j: (l, 0, j)),
                  pl.BlockSpec((1, 1, tn), lambda l, j: (l, 0, j))],
        out_specs=pl.BlockSpec((1, 8, tn), lambda l, j: (l, 0, j)),
        out_shape=jax.ShapeDtypeStruct((DEPTH, 8, n), F32),
        compiler_params=_params("parallel", "parallel"),
        name="modulation",
    )(cond8, w_mod, b_mod.reshape(DEPTH, 1, n))


def _mod_spec(which, tm):
    return pl.BlockSpec((None, None, 1, D_MODEL), lambda i, *_: (i * tm // GROUP_ROWS, which, 0, 0))


def _inproj_kernel(x_ref, sc_ref, sh_ref, w_ref, o_ref, xb_ref):
    @pl.when(pl.program_id(1) == 0)
    def _():
        xb_ref[...] = (x_ref[...] * (1.0 + sc_ref[...]) + sh_ref[...]).astype(BF16)

    o_ref[...] = _mm(xb_ref[...], w_ref[...])


def _in_projection(x, mod, w_in_b):
    tm, tn = TM_PROJ, TN_PROJ
    return pl.pallas_call(
        _inproj_kernel,
        grid=(N_TOK // tm, IN_WIDTH // tn),
        in_specs=[pl.BlockSpec((tm, D_MODEL), lambda i, j: (i, 0)),
                  _mod_spec(1, tm), _mod_spec(0, tm),
                  pl.BlockSpec((D_MODEL, tn), lambda i, j: (0, j))],
        out_specs=pl.BlockSpec((tm, tn), lambda i, j: (i, j)),
        out_shape=jax.ShapeDtypeStruct((N_TOK, IN_WIDTH), F32),
        scratch_shapes=[pltpu.VMEM((tm, D_MODEL), BF16)],
        compiler_params=_params("parallel", "arbitrary"),
        name="in_projection",
    )(x, mod, mod, w_in_b)


def _ctx_attn_kernel(scal_ref, qa_ref, ka_ref, va_ref, qb_ref, kb_ref, vb_ref, qd_ref, kd_ref, vd_ref,
                     ga_ref, gb_ref, gd_ref, oa_ref, ob_ref, od_ref, *, lam_init):
    L = SEQ
    lo = _low_half((L, LANES))

    ka = ka_ref[...].astype(BF16)
    va = va_ref[...].astype(BF16)
    qa = qa_ref[...]
    chunks = []
    for c in range(A_HEADS // 2):
        hk = c // 2
        chunk = qa[:, c * LANES:(c + 1) * LANES]
        keep = lo if hk == 0 else jnp.logical_not(lo)
        halves = []
        for half in range(2):
            x = chunk if half == hk else pltpu.roll(chunk, HEAD_DIM, 1)
            qh = jnp.where(keep, x, 0.0).astype(BF16)
            s = _nt(qh, ka) * SCALE
            (e,), den = _softmax_unnorm([s], scal_ref[2 * c + half])
            o = _mm(e.astype(BF16), va) / den
            halves.append(o if half == hk else pltpu.roll(o, HEAD_DIM, 1))
        chunks.append(jnp.where(lo, halves[0], halves[1]))
    oa = jnp.concatenate(chunks, axis=1)
    oa_ref[...] = _rms_gain(oa, ga_ref[...]).astype(BF16)

    lam = scal_ref[A_HEADS]
    qb = qb_ref[...]
    chunks = []
    for h in range(B_HEADS):
        sl = slice(h * LANES, (h + 1) * LANES)
        qc = qb[:, sl]
        kc = kb_ref[:, sl].astype(BF16)
        vh = vb_ref[:, sl].astype(BF16)
        (e1,), d1 = _softmax_unnorm([_nt(jnp.where(lo, qc, 0.0).astype(BF16), kc) * SCALE])
        (e2,), d2 = _softmax_unnorm([_nt(jnp.where(lo, 0.0, qc).astype(BF16), kc) * SCALE])
        w = e1 * (1.0 / d1) - e2 * (lam / d2)
        o = _mm(w.astype(BF16), vh)
        chunks.append(_rms_gain(o, gb_ref[:, sl]) * (1.0 - lam_init))
    ob_ref[...] = jnp.concatenate(chunks, axis=1).astype(BF16)

    qd = qd_ref[...]
    chunks = []
    for c in range(D_HEADS // 2):
        sl = slice(c * LANES, (c + 1) * LANES)
        qc = qd[:, sl]
        kc = kd_ref[:, sl].astype(BF16)
        vc = vd_ref[:, sl].astype(BF16)
        halves = []
        for half in range(2):
            qh = jnp.where(lo if half == 0 else jnp.logical_not(lo), qc, 0.0).astype(BF16)
            (e,), den = _softmax_unnorm([_nt(qh, kc) * SCALE])
            halves.append(_mm(e.astype(BF16), vc) / den)
        chunks.append(jnp.where(lo, halves[0], halves[1]))
    od = jnp.concatenate(chunks, axis=1)
    od_ref[...] = _rms_gain(od, gd_ref[...]).astype(BF16)


def _ctx_attention(slab, scal, g_mix2d, lam_init):
    L = SEQ

    def wide(col):
        return pl.BlockSpec((L, 512), lambda b: (b, col))

    def narrow(col):
        return pl.BlockSpec((L, LANES), lambda b: (b, col))

    def gain(col):
        return pl.BlockSpec((1, 512), lambda b: (0, col))

    out = jax.ShapeDtypeStruct((N_CTX, 512), BF16)
    ospec = pl.BlockSpec((L, 512), lambda b: (b, 0))
    return pl.pallas_call(
        functools.partial(_ctx_attn_kernel, lam_init=lam_init),
        grid=(BATCH,),
        in_specs=[pl.BlockSpec(memory_space=pltpu.SMEM),
                  wide(COL_QA), narrow(COL_KA), narrow(COL_VA),
                  wide(COL_QB), wide(COL_KB), wide(COL_VB),
                  wide(COL_QD), wide(COL_KD), wide(COL_VD),
                  gain(0), gain(1), gain(3)],
        out_specs=[ospec, ospec, ospec],
        out_shape=[out, out, out],
        compiler_params=_params("parallel"),
        name="ctx_attention",
    )(scal, slab, slab, slab, slab, slab, slab, slab, slab, slab, g_mix2d, g_mix2d, g_mix2d)


def _prep_kernel(qa_ref, ka_ref, va_ref, qb_ref, kb_ref, vb_ref, qd_ref, kd_ref, vd_ref,
                 cos_ref, sa_ref, sb_ref,
                 qa_o, ka_o, va_o, qb_o, kb_o, vb_o, qd_o, kd_o, vd_o):
    cos, sa, sb = cos_ref[...], sa_ref[...], sb_ref[...]
    lo = _low_half(cos.shape)
    hi = jnp.logical_not(lo)

    def rope(x):
        return (x * cos + pltpu.roll(x, LANES - HEAD_DIM // 4, 1) * sa
                + pltpu.roll(x, HEAD_DIM // 4, 1) * sb)

    ka_o[...] = rope(ka_ref[...]).astype(BF16)
    va_o[...] = va_ref[...].astype(BF16)
    vb_o[...] = vb_ref[...].astype(BF16)
    kd_o[...] = kd_ref[...].astype(BF16)
    vd_o[...] = vd_ref[...].astype(BF16)

    for c in range(A_HEADS // 2):
        hk = c // 2
        r = rope(qa_ref[:, c * LANES:(c + 1) * LANES])
        keep = lo if hk == 0 else hi
        for half in range(2):
            h = 2 * c + half
            x = r if half == hk else pltpu.roll(r, HEAD_DIM, 1)
            qa_o[:, h * LANES:(h + 1) * LANES] = jnp.where(keep, x, 0.0).astype(BF16)
    for h in range(B_HEADS):
        sl = slice(h * LANES, (h + 1) * LANES)
        r = rope(qb_ref[:, sl])
        kb_o[:, sl] = rope(kb_ref[:, sl]).astype(BF16)
        qb_o[:, (2 * h) * LANES:(2 * h + 1) * LANES] = jnp.where(lo, r, 0.0).astype(BF16)
        qb_o[:, (2 * h + 1) * LANES:(2 * h + 2) * LANES] = jnp.where(hi, r, 0.0).astype(BF16)
    for c in range(D_HEADS // 2):
        x = qd_ref[:, c * LANES:(c + 1) * LANES]
        qd_o[:, (2 * c) * LANES:(2 * c + 1) * LANES] = jnp.where(lo, x, 0.0).astype(BF16)
        qd_o[:, (2 * c + 1) * LANES:(2 * c + 2) * LANES] = jnp.where(hi, x, 0.0).astype(BF16)


def _latent_prep(slab, cos, sa, sb):
    tm = TM_PREP
    off = N_CTX // tm
    per_seq = DEC_SEQ // tm

    def wide(col):
        return pl.BlockSpec((tm, 512), lambda i: (i + off, col))

    def narrow(col):
        return pl.BlockSpec((tm, LANES), lambda i: (i + off, col))

    tab = pl.BlockSpec((tm, LANES), lambda i: (i % per_seq, 0))

    def out(width):
        return (pl.BlockSpec((tm, width), lambda i: (i, 0)), jax.ShapeDtypeStruct((N_LAT, width), BF16))

    outs = [out(1024), out(LANES), out(LANES), out(1024), out(512), out(512), out(1024), out(512), out(512)]
    return pl.pallas_call(
        _prep_kernel,
        grid=(N_LAT // tm,),
        in_specs=[wide(COL_QA), narrow(COL_KA), narrow(COL_VA), wide(COL_QB), wide(COL_KB), wide(COL_VB),
                  wide(COL_QD), wide(COL_KD), wide(COL_VD), tab, tab, tab],
        out_specs=[o[0] for o in outs],
        out_shape=[o[1] for o in outs],
        compiler_params=_params("parallel"),
        name="latent_prep",
    )(slab, slab, slab, slab, slab, slab, slab, slab, slab, cos, sa, sb)


def _win_attn_kernel(scal_ref, q_ref, kp_ref, kc_ref, kn_ref, vp_ref, vc_ref, vn_ref, kctx_ref, vctx_ref,
                     g_ref, o_ref):
    n = pl.program_id(1)
    nb = pl.num_programs(1)
    tq = TQ_A
    group = A_HEADS // A_KV_HEADS
    k = jnp.concatenate([kp_ref[...], kc_ref[...], kn_ref[...], kctx_ref[...].astype(BF16)], axis=0)
    v = jnp.concatenate([vp_ref[...], vc_ref[...], vn_ref[...], vctx_ref[...].astype(BF16)], axis=0)
    nk = 3 * tq + PAST_LEN
    row = lax.broadcasted_iota(jnp.int32, (group * tq, nk), 0) & (tq - 1)
    col = lax.broadcasted_iota(jnp.int32, (group * tq, nk), 1)
    prev_thr = row + jnp.where(n > 0, 0, tq)
    next_thr = row + 2 * tq - jnp.where(n < nb - 1, 0, tq)
    masked = ((col < tq) & (col < prev_thr)) | ((col >= 2 * tq) & (col < 3 * tq) & (col > next_thr))
    valid = jnp.logical_not(masked)
    lo = _low_half((tq, LANES))
    heads = []
    for hk in range(A_KV_HEADS):
        q4 = jnp.concatenate([q_ref[:, (hk * group + g) * LANES:(hk * group + g + 1) * LANES]
                              for g in range(group)], axis=0)
        s = jnp.where(valid, _nt(q4, k) * SCALE, NEG_INF)
        sink = jnp.concatenate([jnp.full((tq, 1), scal_ref[hk * group + g], F32) for g in range(group)], axis=0)
        (e,), den = _softmax_unnorm([s], sink)
        o = _mm(e.astype(BF16), v) / den
        for g in range(group):
            h = hk * group + g
            og = o[g * tq:(g + 1) * tq]
            heads.append(og if (h % 2) == hk else pltpu.roll(og, HEAD_DIM, 1))
    oa = jnp.concatenate([jnp.where(lo, heads[2 * c], heads[2 * c + 1]) for c in range(A_HEADS // 2)], axis=1)
    o_ref[...] = _rms_gain(oa, g_ref[...]).astype(BF16)


def _window_attention(qa_p, ka_r, va_b, cache_k, cache_v, scal, g_mix2d):
    tq = TQ_A
    nb = DEC_SEQ // tq

    def band(d):
        return pl.BlockSpec((tq, LANES), lambda b, n: (b * nb + jnp.clip(n + d, 0, nb - 1), 0))

    ctx = pl.BlockSpec((None, PAST_LEN, LANES), lambda b, n: (b, 0, 0))
    return pl.pallas_call(
        _win_attn_kernel,
        grid=(DEC_BATCH, nb),
        in_specs=[pl.BlockSpec(memory_space=pltpu.SMEM),
                  pl.BlockSpec((tq, A_HEADS * LANES), lambda b, n: (b * nb + n, 0)),
                  band(-1), band(0), band(1), band(-1), band(0), band(1), ctx, ctx,
                  pl.BlockSpec((1, 512), lambda b, n: (0, 0))],
        out_specs=pl.BlockSpec((tq, 512), lambda b, n: (b * nb + n, 0)),
        out_shape=jax.ShapeDtypeStruct((N_LAT, 512), BF16),
        compiler_params=_params("parallel", "parallel"),
        name="window_attention",
    )(scal, qa_p, ka_r, ka_r, ka_r, va_b, va_b, va_b, cache_k, cache_v, g_mix2d)


def _diff_attn_kernel(scal_ref, q_ref, k_ref, v_ref, kctx_ref, vctx_ref, g_ref, o_ref, *, lam_init):
    lam = scal_ref[A_HEADS]
    chunks = []
    for h in range(B_HEADS):
        sl = slice(h * LANES, (h + 1) * LANES)
        kh = k_ref[:, sl]
        kc = kctx_ref[:, sl].astype(BF16)
        es, dens = [], []
        for m in range(2):
            q = q_ref[:, (2 * h + m) * LANES:(2 * h + m + 1) * LANES]
            e, d = _softmax_unnorm([_nt(q, kh) * SCALE, _nt(q, kc) * SCALE])
            es.append(e)
            dens.append(d)
        c1 = 1.0 / dens[0]
        c2 = lam / dens[1]
        w_lat = (es[0][0] * c1 - es[1][0] * c2).astype(BF16)
        w_ctx = (es[0][1] * c1 - es[1][1] * c2).astype(BF16)
        o = _mm(w_lat, v_ref[:, sl]) + _mm(w_ctx, vctx_ref[:, sl].astype(BF16))
        chunks.append(_rms_gain(o, g_ref[:, sl]) * (1.0 - lam_init))
    o_ref[...] = jnp.concatenate(chunks, axis=1).astype(BF16)


def _diff_attention(qb_p, kb_r, vb_b, cache_k, cache_v, scal, g_mix2d, lam_init):
    tq = TQ_B
    nb = DEC_SEQ // tq
    full = pl.BlockSpec((DEC_SEQ, 512), lambda b, n: (b, 0))
    ctx = pl.BlockSpec((None, PAST_LEN, 512), lambda b, n: (b, 0, 0))
    return pl.pallas_call(
        functools.partial(_diff_attn_kernel, lam_init=lam_init),
        grid=(DEC_BATCH, nb),
        in_specs=[pl.BlockSpec(memory_space=pltpu.SMEM),
                  pl.BlockSpec((tq, 2 * B_HEADS * LANES), lambda b, n: (b * nb + n, 0)),
                  full, full, ctx, ctx,
                  pl.BlockSpec((1, 512), lambda b, n: (0, 1))],
        out_specs=pl.BlockSpec((tq, 512), lambda b, n: (b * nb + n, 0)),
        out_shape=jax.ShapeDtypeStruct((N_LAT, 512), BF16),
        compiler_params=_params("parallel", "parallel"),
        name="diff_attention",
    )(scal, qb_p, kb_r, vb_b, cache_k, cache_v, g_mix2d)


def _na_window_start(step):
    return np.clip(step * NA_QR - NA_ROWS // 2, 0, GRID_W - NA_KR)


def _na_bias_indices():
    rows = DEC_SEQ // GRID_W
    steps = rows // NA_QR
    pats = []
    for step in range(steps):
        w0 = _na_window_start(step)
        r = step * NA_QR + np.arange(NA_QR)[:, None, None, None]
        c = np.arange(GRID_W)[None, :, None, None]
        kr = w0 + np.arange(NA_KR)[None, None, :, None]
        kc = np.arange(GRID_W)[None, None, None, :]
        kr0 = np.clip(r - NA_ROWS // 2, 0, rows - NA_ROWS)
        kc0 = np.clip(c - NA_COLS // 2, 0, GRID_W - NA_COLS)
        valid = (kr >= kr0) & (kr < kr0 + NA_ROWS) & (kc >= kc0) & (kc < kc0 + NA_COLS)
        drow = np.clip(kr - r + NA_ROWS - 1, 0, 2 * NA_ROWS - 2)
        dcol = np.clip(kc - c + NA_COLS - 1, 0, 2 * NA_COLS - 2)
        shape = (NA_QR * GRID_W, NA_KR * GRID_W)
        full = np.broadcast_to
        pats.append((full(drow, valid.shape).reshape(shape), full(dcol, valid.shape).reshape(shape),
                     valid.reshape(shape)))
    for step in range(2, steps - 1):
        for a, b in zip(pats[1], pats[step]):
            assert np.array_equal(a, b)
    kinds = [pats[0], pats[1], pats[-1]]
    drow, dcol, valid = (np.stack([k[i] for k in kinds]) for i in range(3))
    shape6 = (len(kinds), NA_QR, GRID_W, NA_KR, GRID_W)
    row_sel = np.eye(2 * NA_ROWS - 1, dtype=np.float32)[drow.reshape(shape6)[:, :, 0, :, 0]]
    col_sel = np.eye(2 * NA_COLS - 1, dtype=np.float32)[dcol.reshape(shape6)[0, 0, :, 0, :]]
    return row_sel, col_sel, valid


def _na_bias(rpb, row_sel, col_sel, valid):
    hp = lax.Precision.HIGHEST
    cols = jnp.einsum('hab,cdb->hacd', rpb, col_sel, precision=hp)
    bias = jnp.einsum('kria,hacd->khrcid', row_sel, cols, precision=hp)
    bias = bias.reshape(valid.shape[0], D_HEADS, NA_QR * GRID_W, NA_KR * GRID_W)
    return jnp.where(valid[:, None], bias, NEG_INF)


def _na_kernel(q_ref, k_ref, v_ref, kctx_ref, vctx_ref, bias_ref, g_ref, o_ref):
    step = pl.program_id(1)
    nq = NA_QR * GRID_W
    nk = NA_KR * GRID_W
    w0 = jnp.clip(step * NA_QR - NA_ROWS // 2, 0, GRID_W - NA_KR)
    start = pl.multiple_of(w0 * GRID_W, GRID_W)
    lo = _low_half((nq, LANES))
    chunks = []
    for c in range(D_HEADS // 2):
        sl = slice(c * LANES, (c + 1) * LANES)
        kw = k_ref[pl.ds(start, nk), sl]
        vw = v_ref[pl.ds(start, nk), sl]
        kc = kctx_ref[:, sl].astype(BF16)
        vc = vctx_ref[:, sl].astype(BF16)
        halves = []
        for half in range(2):
            h = 2 * c + half
            q = q_ref[:, h * LANES:(h + 1) * LANES]
            s_loc = _nt(q, kw) * SCALE + bias_ref[h]
            s_ctx = _nt(q, kc) * SCALE
            (e_loc, e_ctx), den = _softmax_unnorm([s_loc, s_ctx])
            halves.append((_mm(e_loc.astype(BF16), vw) + _mm(e_ctx.astype(BF16), vc)) / den)
        chunks.append(jnp.where(lo, halves[0], halves[1]))
    od = jnp.concatenate(chunks, axis=1)
    o_ref[...] = _rms_gain(od, g_ref[...]).astype(BF16)


def _neighbourhood_attention(qd_p, kd_b, vd_b, cache_k, cache_v, bias, g_mix2d):
    nq = NA_QR * GRID_W
    steps = DEC_SEQ // nq
    full = pl.BlockSpec((DEC_SEQ, 512), lambda b, j: (b, 0))
    ctx = pl.BlockSpec((None, PAST_LEN, 512), lambda b, j: (b, 0, 0))

    def kind(b, j):
        return (jnp.where(j == 0, 0, jnp.where(j == steps - 1, 2, 1)), 0, 0, 0)

    return pl.pallas_call(
        _na_kernel,
        grid=(DEC_BATCH, steps),
        in_specs=[pl.BlockSpec((nq, D_HEADS * LANES), lambda b, j: (b * steps + j, 0)),
                  full, full, ctx, ctx,
                  pl.BlockSpec((None, D_HEADS, nq, NA_KR * GRID_W), kind),
                  pl.BlockSpec((1, 512), lambda b, j: (0, 3))],
        out_specs=pl.BlockSpec((nq, 512), lambda b, j: (b * steps + j, 0)),
        out_shape=jax.ShapeDtypeStruct((N_LAT, 512), BF16),
        compiler_params=_params("parallel", "arbitrary"),
        name="neighbourhood_attention",
    )(qd_p, kd_b, vd_b, cache_k, cache_v, bias, g_mix2d)


N_SCAN_TILES = N_TOK // T_SCAN
CTX_TILES = N_CTX // T_SCAN
TILES_PER_LAT = DEC_SEQ // T_SCAN


def _scan_tile_flags(i):
    is_ctx = i < CTX_TILES
    pos = (i - CTX_TILES) % TILES_PER_LAT
    return is_ctx | (pos == 0), is_ctx | (pos == TILES_PER_LAT - 1)


def _scan_seq(i):
    return jnp.where(i < CTX_TILES, i, CTX_TILES + (i - CTX_TILES) // TILES_PER_LAT)


def _conv_gates(x_ref, prev_ref, next_ref, cw_ref, cb_ref, wgh_ref, wgl_ref, bg_ref, lam_ref, first, last):
    x = x_ref[...]
    t = T_SCAN
    row = lax.broadcasted_iota(jnp.int32, x.shape, 0)
    pm1 = jnp.where(first, 0.0, prev_ref[7:8, :])
    n0 = jnp.where(last, 0.0, next_ref[0:1, :])
    n1 = jnp.where(last, 0.0, next_ref[1:2, :])
    x_m1 = jnp.where(row == 0, pm1, pltpu.roll(x, 1, 0))
    x_p1 = jnp.where(row == t - 1, n0, pltpu.roll(x, t - 1, 0))
    x_p2 = jnp.where(row == t - 2, n0, jnp.where(row == t - 1, n1, pltpu.roll(x, t - 2, 0)))
    y = (cw_ref[0:1, :] * x_m1 + cw_ref[1:2, :] * x + cw_ref[2:3, :] * x_p1 + cw_ref[3:4, :] * x_p2
         + cb_ref[...])
    y_hi = y.astype(BF16)
    y_lo = (y - y_hi.astype(F32)).astype(BF16)
    zr, zi = [], []
    for c in range(C_WIDTH // LANES):
        sl = slice(c * LANES, (c + 1) * LANES)
        w_hi = wgh_ref[0, sl, :]
        z = _mm(y_hi[:, sl], w_hi) + _mm(y_lo[:, sl], w_hi) + _mm(y_hi[:, sl], wgl_ref[0, sl, :])
        zr.append(z[:, :LANES])
        zi.append(z[:, LANES:])
    bias = bg_ref[0]
    r = jax.nn.sigmoid(jnp.concatenate(zr, axis=1) + bias[:, :C_WIDTH])
    gate_i = jax.nn.sigmoid(jnp.concatenate(zi, axis=1) + bias[:, C_WIDTH:])
    nl = -lam_ref[0]
    softplus = jnp.maximum(nl, 0.0) + jnp.log1p(jnp.exp(-jnp.abs(nl)))
    log_a = -C_POW * r * softplus
    a = jnp.exp(log_a)
    u = jnp.sqrt(1.0 - jnp.exp(2.0 * log_a)) * gate_i * y
    return a, u


def _scan_rows(a_s, u_s, h_s, h, reverse):
    groups = T_SCAN // 8

    def body(i, h):
        g = (groups - 1 - i) if reverse else i
        base = pl.multiple_of(g * 8, 8)
        a8 = a_s[pl.ds(base, 8), :]
        u8 = u_s[pl.ds(base, 8), :]
        rows = [None] * 8
        for k in (range(7, -1, -1) if reverse else range(8)):
            h = a8[k:k + 1, :] * h + u8[k:k + 1, :]
            rows[k] = h
        h_s[pl.ds(base, 8), :] = jnp.concatenate(rows, axis=0)
        return h

    return lax.fori_loop(0, groups, body, h)


def _scan_fwd_kernel(x_ref, prev_ref, next_ref, cw_ref, cb_ref, wgh_ref, wgl_ref, bg_ref, lam_ref, h0_ref,
                     hf_ref, fin_ref, a_s, u_s, carry_s):
    i = pl.program_id(0)
    first, last = _scan_tile_flags(i)
    a, u = _conv_gates(x_ref, prev_ref, next_ref, cw_ref, cb_ref, wgh_ref, wgl_ref, bg_ref, lam_ref, first, last)
    a_s[...] = a
    u_s[...] = u

    @pl.when(first)
    def _():
        carry_s[...] = h0_ref[...]

    h = _scan_rows(a_s, u_s, hf_ref, carry_s[...], reverse=False)
    carry_s[...] = h
    fin_ref[...] = h


def _scan_bwd_kernel(x_ref, prev_ref, next_ref, cw_ref, cb_ref, wgh_ref, wgl_ref, bg_ref, lam_ref, h0_ref,
                     hf_ref, gc_ref, g_ref, oc_ref, fin_ref, a_s, u_s, hb_s, carry_s):
    i = N_SCAN_TILES - 1 - pl.program_id(0)
    first, last = _scan_tile_flags(i)
    a, u = _conv_gates(x_ref, prev_ref, next_ref, cw_ref, cb_ref, wgh_ref, wgl_ref, bg_ref, lam_ref, first, last)
    a_s[...] = a
    u_s[...] = u

    @pl.when(last)
    def _():
        carry_s[...] = h0_ref[...]

    h = _scan_rows(a_s, u_s, hb_s, carry_s[...], reverse=True)
    carry_s[...] = h
    fin_ref[...] = h
    g = gc_ref[...]
    gelu = 0.5 * g * (1.0 + jnp.tanh(math.sqrt(2.0 / math.pi) * (g + 0.044715 * (g * g * g))))
    oc_ref[...] = _rms_gain(gelu * (hf_ref[...] + hb_s[...]), g_ref[...]).astype(BF16)


def _scan_common_specs(tile_of, direction):
    halo = T_SCAN // 8
    last_block = N_TOK // 8 - 1
    return [pl.BlockSpec((T_SCAN, 512), lambda j: (tile_of(j), COL_XC)),
            pl.BlockSpec((8, 512), lambda j: (jnp.maximum(tile_of(j) * halo - 1, 0), COL_XC)),
            pl.BlockSpec((8, 512), lambda j: (jnp.minimum((tile_of(j) + 1) * halo, last_block), COL_XC)),
            pl.BlockSpec((4, C_WIDTH), lambda j: (0, 0)),
            pl.BlockSpec((1, C_WIDTH), lambda j: (0, 0)),
            pl.BlockSpec((1, C_WIDTH, 2 * LANES), lambda j: (direction, 0, 0)),
            pl.BlockSpec((1, C_WIDTH, 2 * LANES), lambda j: (direction, 0, 0)),
            pl.BlockSpec((1, 1, 2 * C_WIDTH), lambda j: (direction, 0, 0)),
            pl.BlockSpec((1, 1, C_WIDTH), lambda j: (direction, 0, 0)),
            pl.BlockSpec((None, 1, C_WIDTH), lambda j: (_scan_seq(tile_of(j)), 0, 0))]


def _rglru(slab, conv_w, conv_b, w_gates, b_gates, lam, h0_f, h0_b, g_mix2d):
    wg_hi = w_gates.astype(BF16)
    wg_lo = (w_gates - wg_hi.astype(F32)).astype(BF16)
    fin = jax.ShapeDtypeStruct((N_SCAN_TILES, 1, C_WIDTH), F32)
    fwd_tile = lambda j: j
    h_f, fin_f = pl.pallas_call(
        _scan_fwd_kernel,
        grid=(N_SCAN_TILES,),
        in_specs=_scan_common_specs(fwd_tile, 0),
        out_specs=[pl.BlockSpec((T_SCAN, C_WIDTH), lambda j: (j, 0)),
                   pl.BlockSpec((None, 1, C_WIDTH), lambda j: (j, 0, 0))],
        out_shape=[jax.ShapeDtypeStruct((N_TOK, C_WIDTH), F32), fin],
        scratch_shapes=[pltpu.VMEM((T_SCAN, C_WIDTH), F32), pltpu.VMEM((T_SCAN, C_WIDTH), F32),
                        pltpu.VMEM((1, C_WIDTH), F32)],
        compiler_params=_params("arbitrary"),
        name="rglru_forward",
    )(slab, slab, slab, conv_w, conv_b, wg_hi, wg_lo, b_gates, lam, h0_f)
    bwd_tile = lambda j: N_SCAN_TILES - 1 - j
    oc, fin_b = pl.pallas_call(
        _scan_bwd_kernel,
        grid=(N_SCAN_TILES,),
        in_specs=_scan_common_specs(bwd_tile, 1) + [
            pl.BlockSpec((T_SCAN, C_WIDTH), lambda j: (bwd_tile(j), 0)),
            pl.BlockSpec((T_SCAN, 512), lambda j: (bwd_tile(j), COL_GC)),
            pl.BlockSpec((1, 512), lambda j: (0, 2))],
        out_specs=[pl.BlockSpec((T_SCAN, C_WIDTH), lambda j: (bwd_tile(j), 0)),
                   pl.BlockSpec((None, 1, C_WIDTH), lambda j: (bwd_tile(j), 0, 0))],
        out_shape=[jax.ShapeDtypeStruct((N_TOK, C_WIDTH), BF16), fin],
        scratch_shapes=[pltpu.VMEM((T_SCAN, C_WIDTH), F32), pltpu.VMEM((T_SCAN, C_WIDTH), F32),
                        pltpu.VMEM((T_SCAN, C_WIDTH), F32), pltpu.VMEM((1, C_WIDTH), F32)],
        compiler_params=_params("arbitrary"),
        name="rglru_backward",
    )(slab, slab, slab, conv_w, conv_b, wg_hi, wg_lo, b_gates, lam, h0_b, h_f, slab, g_mix2d)
    return oc, fin_f, fin_b


def _mixout_kernel(oa_ref, ob_ref, oc_ref, od_ref, w_ref, x_ref, g1_ref, sc2_ref, sh2_ref, lng_ref, lnb_ref,
                   wrh_ref, wrl_ref, br_ref, x1_ref, h2_ref, idx_ref, wt_ref, rank_ref, cnt_ref, cnt_s):
    @pl.when(pl.program_id(0) == 0)
    def _():
        cnt_s[...] = jnp.zeros(cnt_s.shape, F32)

    parts = 2
    pm = x_ref.shape[0] // parts
    for part in range(parts):
        _mixout_rows(pl.ds(part * pm, pm), part * pm, pm, oa_ref, ob_ref, oc_ref, od_ref, w_ref, x_ref, g1_ref,
                     sc2_ref, sh2_ref, lng_ref, lnb_ref, wrh_ref, wrl_ref, br_ref, x1_ref, h2_ref, idx_ref,
                     wt_ref, rank_ref, cnt_s)
    cnt_ref[...] = cnt_s[...]


def _mixout_rows(rows, first, tm, oa_ref, ob_ref, oc_ref, od_ref, w_ref, x_ref, g1_ref, sc2_ref, sh2_ref,
                 lng_ref, lnb_ref, wrh_ref, wrl_ref, br_ref, x1_ref, h2_ref, idx_ref, wt_ref, rank_ref, cnt_s):
    y = _mm(oa_ref[rows, :], w_ref[0:512, :])
    y += _mm(ob_ref[rows, :], w_ref[512:1024, :])
    y += _mm(oc_ref[rows, :], w_ref[1024:1536, :])
    y += _mm(od_ref[rows, :], w_ref[1536:2048, :])
    x1 = _layernorm(ALPHA * x_ref[rows, :] + g1_ref[...] * y, lng_ref[...], lnb_ref[...])
    x1_ref[rows, :] = x1
    h2 = x1 * (1.0 + sc2_ref[...]) + sh2_ref[...]
    _store_token_tiles(h2_ref, (), _pack_rows(h2), first)

    h_hi = h2.astype(BF16)
    h_lo = (h2 - h_hi.astype(F32)).astype(BF16)
    scores = jax.nn.sigmoid(_mm(h_hi, wrh_ref[...]) + _mm(h_lo, wrh_ref[...]) + _mm(h_hi, wrl_ref[...]))
    sel = scores + br_ref[...]
    lane = lax.broadcasted_iota(jnp.int32, sel.shape, 1).astype(F32)
    slot = lax.broadcasted_iota(jnp.int32, (tm, 8), 1)
    idx_out = jnp.zeros((tm, 8), F32)
    wt_out = jnp.zeros((tm, 8), F32)
    total = jnp.zeros((sel.shape[0], 1), F32)
    hits = []
    for k in range(TOP_K):
        m = sel.max(axis=-1, keepdims=True)
        idx = jnp.where(sel == m, lane, float(N_EXPERTS)).min(axis=-1, keepdims=True)
        hit = lane == idx
        hits.append(hit)
        w = jnp.where(hit, scores, 0.0).sum(axis=-1, keepdims=True)
        total = total + w
        idx_out = jnp.where(slot == k, idx, idx_out)
        wt_out = jnp.where(slot == k, w, wt_out)
        sel = jnp.where(hit, -jnp.inf, sel)
    idx_ref[rows, :] = idx_out.astype(jnp.int32)
    wt_ref[rows, :] = wt_out / total * ROUTED_SCALE

    chosen = jnp.zeros(sel.shape, F32)
    for hit in hits:
        chosen = chosen + hit.astype(F32)
    earlier = (lax.broadcasted_iota(jnp.int32, (tm, tm), 1) < lax.broadcasted_iota(jnp.int32, (tm, tm), 0))
    before = _mm(earlier.astype(F32).astype(BF16), chosen.astype(BF16)) + cnt_s[...]
    rank_out = jnp.zeros((tm, 8), F32)
    for k, hit in enumerate(hits):
        rank_out = jnp.where(slot == k, jnp.where(hit, before, 0.0).sum(axis=-1, keepdims=True), rank_out)
    rank_ref[rows, :] = rank_out.astype(jnp.int32)
    cnt_s[...] = cnt_s[...] + chosen.sum(axis=0, keepdims=True)


def _mixer_out(oa, ob, oc, od, w_out_b, x, mod, ln_g, ln_b, w_router, b_router):
    tm = TM_OUT
    cat = pl.BlockSpec((tm, 512), lambda i: (i, 0))
    row = pl.BlockSpec((tm, D_MODEL), lambda i: (i, 0))
    vec = pl.BlockSpec((1, D_MODEL), lambda i: (0, 0))
    k8 = pl.BlockSpec((tm, 8), lambda i: (i, 0))
    per_expert = pl.BlockSpec((1, N_EXPERTS), lambda i: (0, 0))
    router = pl.BlockSpec((D_MODEL, N_EXPERTS), lambda i: (0, 0))
    wr_hi = w_router.astype(BF16)
    wr_lo = (w_router - wr_hi.astype(F32)).astype(BF16)
    return pl.pallas_call(
        _mixout_kernel,
        grid=(N_TOK // tm,),
        in_specs=[cat, cat, cat, cat,
                  pl.BlockSpec((D_MODEL, D_MODEL), lambda i: (0, 0)),
                  row, _mod_spec(2, tm), _mod_spec(4, tm), _mod_spec(3, tm), vec, vec,
                  router, router, per_expert],
        out_specs=[row, pl.BlockSpec((tm * SUBLANES, LANES), lambda i: (i, 0)), k8, k8, k8, per_expert],
        out_shape=[jax.ShapeDtypeStruct((N_TOK, D_MODEL), F32),
                   jax.ShapeDtypeStruct((N_TOK * SUBLANES, LANES), jnp.uint32),
                   jax.ShapeDtypeStruct((N_TOK, 8), jnp.int32), jax.ShapeDtypeStruct((N_TOK, 8), F32),
                   jax.ShapeDtypeStruct((N_TOK, 8), jnp.int32), jax.ShapeDtypeStruct((1, N_EXPERTS), F32)],
        scratch_shapes=[pltpu.VMEM((1, N_EXPERTS), F32)],
        compiler_params=_params("arbitrary"),
        name="mixer_out_router",
    )(oa, ob, oc, od, w_out_b, x, mod, mod, mod, ln_g, ln_b, wr_hi, wr_lo, b_router)


MOE_BLOCKS = N_TOK * TOP_K // MOE_BM + N_EXPERTS
MOE_ROWS = MOE_BLOCKS * MOE_BM


def _row_copy(src, src_row, dst, dst_row, sem):
    def tile(row):
        return pl.ds(row * SUBLANES if isinstance(row, int) else pl.multiple_of(row * SUBLANES, SUBLANES), SUBLANES)

    return pltpu.make_async_copy(src.at[tile(src_row)], dst.at[tile(dst_row)], sem)


def _dispatch_kernel(slot_ref, h_ref, xg_ref, sem):
    tm = h_ref.shape[0] // SUBLANES
    base = pl.program_id(0) * (tm * TOP_K)

    for r in range(tm):
        for k in range(TOP_K):
            _row_copy(h_ref, r, xg_ref, slot_ref[base + (r * TOP_K + k)], sem).start()
    for k in range(TOP_K):
        pltpu.make_async_copy(h_ref, xg_ref.at[pl.ds(0, tm * SUBLANES)], sem).wait()


def _dispatch_rows(slot_flat, h2):
    tm = TM_DISPATCH
    grid_spec = pltpu.PrefetchScalarGridSpec(
        num_scalar_prefetch=1,
        grid=(N_TOK // tm,),
        in_specs=[pl.BlockSpec((tm * SUBLANES, LANES), lambda i, s: (i, 0))],
        out_specs=pl.BlockSpec(memory_space=pl.ANY),
        scratch_shapes=[pltpu.SemaphoreType.DMA])
    return pl.pallas_call(
        _dispatch_kernel,
        grid_spec=grid_spec,
        out_shape=jax.ShapeDtypeStruct((MOE_ROWS * SUBLANES, LANES), jnp.uint32),
        compiler_params=_params("arbitrary"),
        name="moe_dispatch",
    )(slot_flat, h2)


BLOCK_TILE_ROWS = MOE_BM * SUBLANES


def _expert_kernel(blk0_ref, nblk_ref, cnt_ref, wg_ref, wu_ref, wd_ref, x_hbm, y_hbm,
                   wg_s, wu_s, wd_s, xbuf, ybuf, xsem, ysem):
    e = pl.program_id(0)
    first_blk = blk0_ref[e]
    n_blk = nblk_ref[e]
    count = cnt_ref[e]

    def rows_of(j):
        return pl.ds(pl.multiple_of((first_blk + j) * BLOCK_TILE_ROWS, BLOCK_TILE_ROWS), BLOCK_TILE_ROWS)

    def x_copy(j, buf):
        return pltpu.make_async_copy(x_hbm.at[rows_of(j)], xbuf.at[buf], xsem.at[buf])

    def y_copy(j, buf):
        return pltpu.make_async_copy(ybuf.at[buf], y_hbm.at[rows_of(j)], ysem.at[buf])

    @pl.when(n_blk > 0)
    def _():
        x_copy(0, 0).start()

    wg_s[...] = wg_ref[...].astype(BF16)
    wu_s[...] = wu_ref[...].astype(BF16)
    wd_s[...] = wd_ref[...].astype(BF16)

    def block(j, carry):
        buf = j % 2

        @pl.when(j + 1 < n_blk)
        def _():
            x_copy(j + 1, 1 - buf).start()

        x_copy(j, buf).wait()

        @pl.when(j >= 2)
        def _():
            y_copy(j - 2, buf).wait()

        x = _load_token_tiles(xbuf, (buf,), MOE_BM)
        row = lax.broadcasted_iota(jnp.int32, x.shape, 0)
        lo, hi = _unpack_rows(jnp.where(row < count - j * MOE_BM, x, jnp.uint32(0)))
        lo, hi = lo.astype(BF16), hi.astype(BF16)
        gate = _mm(lo, wg_s[:HALF_D, :]) + _mm(hi, wg_s[HALF_D:, :])
        up = _mm(lo, wu_s[:HALF_D, :]) + _mm(hi, wu_s[HALF_D:, :])
        _store_token_tiles(ybuf, (buf,), _pack_rows(_mm((_silu(gate) * up).astype(BF16), wd_s[...])))
        y_copy(j, buf).start()
        return carry

    lax.fori_loop(0, n_blk, block, 0)

    @pl.when(n_blk >= 2)
    def _():
        y_copy(n_blk - 2, n_blk % 2).wait()

    @pl.when(n_blk >= 1)
    def _():
        y_copy(n_blk - 1, (n_blk - 1) % 2).wait()


def _routed_experts(xg, first_blk, n_blk, counts, layer, w_g, w_u, w_d):
    def weight(shape):
        return pl.BlockSpec((None, None) + shape, lambda e, *_: (layer, e, 0, 0))

    any_space = pl.BlockSpec(memory_space=pl.ANY)
    block_buf = pltpu.VMEM((2, BLOCK_TILE_ROWS, LANES), jnp.uint32)
    grid_spec = pltpu.PrefetchScalarGridSpec(
        num_scalar_prefetch=3,
        grid=(N_EXPERTS,),
        in_specs=[weight((D_MODEL, D_EXPERT)), weight((D_MODEL, D_EXPERT)), weight((D_EXPERT, D_MODEL)), any_space],
        out_specs=any_space,
        scratch_shapes=[pltpu.VMEM((D_MODEL, D_EXPERT), BF16), pltpu.VMEM((D_MODEL, D_EXPERT), BF16),
                        pltpu.VMEM((D_EXPERT, D_MODEL), BF16), block_buf, block_buf,
                        pltpu.SemaphoreType.DMA((2,)), pltpu.SemaphoreType.DMA((2,))])
    return pl.pallas_call(
        _expert_kernel,
        grid_spec=grid_spec,
        out_shape=jax.ShapeDtypeStruct((MOE_ROWS * SUBLANES, LANES), jnp.uint32),
        compiler_params=_params("arbitrary"),
        name="routed_experts",
    )(first_blk, n_blk, counts, w_g, w_u, w_d, xg)


def _ffn_out_kernel(slot_ref, h_ref, wt_ref, x1_ref, wg_ref, wu_ref, wd_ref, g2_ref, lng_ref, lnb_ref, y_ref,
                    o_ref, ybuf, sems):
    tm = h_ref.shape[0] // SUBLANES
    i = pl.program_id(0)
    n = pl.num_programs(0)

    def gather(tile, buf):
        base = tile * (tm * TOP_K)

        for r in range(tm):
            for k in range(TOP_K):
                _row_copy(y_ref, slot_ref[base + (r * TOP_K + k)], ybuf.at[buf, k], r, sems.at[buf]).start()

    @pl.when(i == 0)
    def _():
        gather(0, 0)

    @pl.when(i + 1 < n)
    def _():
        gather(i + 1, (i + 1) % 2)

    lo, hi = _unpack_rows(_load_token_tiles(h_ref, (), tm))
    lo, hi = lo.astype(BF16), hi.astype(BF16)
    gate = _mm(lo, wg_ref[:HALF_D, :]) + _mm(hi, wg_ref[HALF_D:, :])
    up = _mm(lo, wu_ref[:HALF_D, :]) + _mm(hi, wu_ref[HALF_D:, :])
    y = _mm((_silu(gate) * up).astype(BF16), wd_ref[...])
    buf = i % 2
    for k in range(TOP_K):
        pltpu.make_async_copy(y_ref.at[pl.ds(0, tm * SUBLANES)], ybuf.at[buf, k], sems.at[buf]).wait()
    wt = wt_ref[...]
    y_lo, y_hi = y[:, :HALF_D], y[:, HALF_D:]
    for k in range(TOP_K):
        r_lo, r_hi = _unpack_rows(_load_token_tiles(ybuf, (buf, k), tm))
        y_lo = y_lo + r_lo * wt[:, k:k + 1]
        y_hi = y_hi + r_hi * wt[:, k:k + 1]
    y = jnp.concatenate([y_lo, y_hi], axis=1)
    o_ref[...] = _layernorm(ALPHA * x1_ref[...] + g2_ref[...] * y, lng_ref[...], lnb_ref[...])


def _ffn_out(slot_flat, h2, wt, x1, ws_g, ws_u, ws_d, mod, ln_g, ln_b, y_blk):
    tm = TM_FFN
    row = pl.BlockSpec((tm, D_MODEL), lambda i, s: (i, 0))
    vec = pl.BlockSpec((1, D_MODEL), lambda i, s: (0, 0))
    grid_spec = pltpu.PrefetchScalarGridSpec(
        num_scalar_prefetch=1,
        grid=(N_TOK // tm,),
        in_specs=[pl.BlockSpec((tm * SUBLANES, LANES), lambda i, s: (i, 0)),
                  pl.BlockSpec((tm, 8), lambda i, s: (i, 0)), row,
                  pl.BlockSpec((D_MODEL, D_EXPERT), lambda i, s: (0, 0)),
                  pl.BlockSpec((D_MODEL, D_EXPERT), lambda i, s: (0, 0)),
                  pl.BlockSpec((D_EXPERT, D_MODEL), lambda i, s: (0, 0)),
                  _mod_spec(5, tm), vec, vec,
                  pl.BlockSpec(memory_space=pl.ANY)],
        out_specs=row,
        scratch_shapes=[pltpu.VMEM((2, TOP_K, tm * SUBLANES, LANES), jnp.uint32),
                        pltpu.SemaphoreType.DMA((2,))])
    return pl.pallas_call(
        _ffn_out_kernel,
        grid_spec=grid_spec,
        out_shape=jax.ShapeDtypeStruct((N_TOK, D_MODEL), F32),
        compiler_params=_params("arbitrary"),
        name="shared_expert_out",
    )(slot_flat, h2, wt, x1, ws_g, ws_u, ws_d, mod, ln_g, ln_b, y_blk)


def _dispatch_plan(idx, rank, counts):
    bm = MOE_BM
    counts = counts.reshape(N_EXPERTS).astype(jnp.int32)
    padded = (counts + bm - 1) // bm * bm
    pad_end = jnp.cumsum(padded)
    pad_start = pad_end - padded
    slot = pad_start[idx[:, :TOP_K]] + rank[:, :TOP_K]
    return slot.reshape(-1).astype(jnp.int32), pad_start // bm, padded // bm, counts


def _rope_tables():
    t = jnp.arange(DEC_SEQ)
    row = (t // GRID_W).astype(F32)
    col = (t % GRID_W).astype(F32)
    nf = HEAD_DIM // 4
    inv = ROPE_BASE ** (-jnp.arange(nf, dtype=F32) / nf)
    ar = row[:, None] * inv[None, :]
    ac = col[:, None] * inv[None, :]
    ang = jnp.concatenate([ar, ar, ac, ac], axis=-1)
    ang = jnp.concatenate([ang, ang], axis=-1)
    cos, sin = jnp.cos(ang), jnp.sin(ang)
    first = (jnp.arange(LANES) % (2 * nf)) < nf
    return cos, jnp.where(first, -sin, 0.0), jnp.where(first, 0.0, sin)


def _lambda_init(l):
    return 0.8 - 0.6 * math.exp(-0.3 * l)


def _gate_pairs(w_r, w_i):
    per = LANES // C_BW
    eye = jnp.eye(per, dtype=w_r.dtype)

    def chunk_diag(w):
        w = w.reshape(C_BLOCKS // per, per, C_BW, C_BW)
        return (eye[None, :, None, :, None] * w[:, :, :, None, :]).reshape(C_BLOCKS // per, LANES, LANES)

    return jnp.concatenate([chunk_diag(w_r), chunk_diag(w_i)], axis=2).reshape(C_WIDTH, 2 * LANES)


def kernel(x_prompt, x_sample, c, cache_a_k, cache_a_v, cache_b_k, cache_b_v, state_c, cache_d_k, cache_d_v, c_ctx, w_mod, b_mod, w_in, a_sink, b_lambda, c_conv_w, c_conv_b, c_w_rgate, c_b_rgate, c_w_igate, c_b_igate, c_lambda, d_rpb, g_mix, w_out, ln_g, ln_b, w_router, b_router, w_e_gate, w_e_up, w_e_down, w_s_gate, w_s_up, w_s_down):
    x = jnp.concatenate([x_prompt.reshape(N_CTX, D_MODEL), x_sample.reshape(N_LAT, D_MODEL)], axis=0)
    cond8 = jnp.zeros((8, D_MODEL), F32).at[0].set(c_ctx).at[1:1 + DEC_BATCH].set(c)
    mod_all = _modulation(cond8, w_mod, b_mod)
    cos, sin_a, sin_b = _rope_tables()
    row_sel, col_sel, na_valid = _na_bias_indices()

    order = np.concatenate([np.arange(0, 512), np.arange(768, 4864), np.arange(512, 768)])

    ctx_out = [[] for _ in range(7)]
    for l in range(DEPTH):
        lam_init = _lambda_init(l)
        mod = mod_all[l, :N_GROUPS].reshape(N_GROUPS, 6, 1, D_MODEL)
        w_in_b = w_in[l][:, order].astype(BF16)
        slab = _in_projection(x, mod, w_in_b)

        lp = b_lambda[l]
        lam = jnp.exp(jnp.sum(lp[0] * lp[1])) - jnp.exp(jnp.sum(lp[2] * lp[3])) + lam_init
        scal = jnp.zeros((16,), F32).at[:A_HEADS].set(a_sink[l]).at[A_HEADS].set(lam)
        g2d = g_mix[l].reshape(1, -1)

        oa_c, ob_c, od_c = _ctx_attention(slab, scal, g2d, lam_init)

        qa_p, ka_r, va_b, qb_p, kb_r, vb_b, qd_p, kd_b, vd_b = _latent_prep(slab, cos, sin_a, sin_b)
        oa_l = _window_attention(qa_p, ka_r, va_b, cache_a_k[:, l].reshape(DEC_BATCH, PAST_LEN, LANES),
                                 cache_a_v[:, l].reshape(DEC_BATCH, PAST_LEN, LANES), scal, g2d)
        ob_l = _diff_attention(qb_p, kb_r, vb_b, cache_b_k[:, l].reshape(DEC_BATCH, PAST_LEN, 512),
                               cache_b_v[:, l].reshape(DEC_BATCH, PAST_LEN, 512), scal, g2d, lam_init)
        bias = _na_bias(d_rpb[l], row_sel, col_sel, na_valid)
        od_l = _neighbourhood_attention(qd_p, kd_b, vd_b, cache_d_k[:, l].reshape(DEC_BATCH, PAST_LEN, 512),
                                        cache_d_v[:, l].reshape(DEC_BATCH, PAST_LEN, 512), bias, g2d)

        w_gates = jnp.stack([_gate_pairs(c_w_rgate[l, d], c_w_igate[l, d]) for d in range(2)])
        b_gates = jnp.concatenate([c_b_rgate[l], c_b_igate[l]], axis=-1).reshape(2, 1, 2 * C_WIDTH)
        zeros = jnp.zeros((BATCH, 1, C_WIDTH), F32)
        h0_f = jnp.concatenate([zeros, state_c[:, l, 0][:, None, :]], axis=0)
        h0_b = jnp.concatenate([zeros, state_c[:, l, 1][:, None, :]], axis=0)
        oc, fin_f, fin_b = _rglru(slab, c_conv_w[l], c_conv_b[l].reshape(1, -1), w_gates, b_gates,
                                  c_lambda[l].reshape(2, 1, C_WIDTH), h0_f, h0_b, g2d)

        oa = jnp.concatenate([oa_c, oa_l], axis=0)
        ob = jnp.concatenate([ob_c, ob_l], axis=0)
        od = jnp.concatenate([od_c, od_l], axis=0)
        x1, h2, idx, wt, rank, counts = _mixer_out(oa, ob, oc, od, w_out[l].astype(BF16), x, mod,
                                                   ln_g[l, 0].reshape(1, -1), ln_b[l, 0].reshape(1, -1),
                                                   w_router[l], b_router[l].reshape(1, -1))

        slot, first_blk, n_blk, counts = _dispatch_plan(idx, rank, counts)
        xg = _dispatch_rows(slot, h2)
        y_blk = _routed_experts(xg, first_blk, n_blk, counts, l, w_e_gate, w_e_up, w_e_down)
        x = _ffn_out(slot, h2, wt, x1, w_s_gate[l].astype(BF16), w_s_up[l].astype(BF16),
                     w_s_down[l].astype(BF16), mod, ln_g[l, 1].reshape(1, -1), ln_b[l, 1].reshape(1, -1), y_blk)

        ctx = slab[:N_CTX]
        ctx_out[0].append(ctx[:, COL_KA * LANES:(COL_KA + 1) * LANES].reshape(BATCH, SEQ, A_KV_HEADS, HEAD_DIM))
        ctx_out[1].append(ctx[:, COL_VA * LANES:(COL_VA + 1) * LANES].reshape(BATCH, SEQ, A_KV_HEADS, HEAD_DIM))
        ctx_out[2].append(ctx[:, COL_KB * 512:(COL_KB + 1) * 512].reshape(BATCH, SEQ, B_HEADS, 2, HEAD_DIM))
        ctx_out[3].append(ctx[:, COL_VB * 512:(COL_VB + 1) * 512].reshape(BATCH, SEQ, B_HEADS, 2 * HEAD_DIM))
        ctx_out[4].append(jnp.concatenate([fin_f[:BATCH], fin_b[:BATCH]], axis=1))
        ctx_out[5].append(ctx[:, COL_KD * 512:(COL_KD + 1) * 512].reshape(BATCH, SEQ, D_HEADS, HEAD_DIM))
        ctx_out[6].append(ctx[:, COL_VD * 512:(COL_VD + 1) * 512].reshape(BATCH, SEQ, D_HEADS, HEAD_DIM))

    new = [jnp.stack(t, axis=1) for t in ctx_out]
    return (x[:N_CTX].reshape(BATCH, SEQ, D_MODEL), x[N_CTX:].reshape(DEC_BATCH, DEC_SEQ, D_MODEL), *new)
```

```python
import functools
import math

import numpy as np
import jax
import jax.numpy as jnp
from jax import lax
from jax.experimental import pallas as pl
from jax.experimental.pallas import tpu as pltpu

F32 = jnp.float32
BF16 = jnp.bfloat16

D_MODEL = 2048
BATCH = 16
SEQ = 256
DEPTH = 2
DEC_BATCH = 2
DEC_SEQ = 4096
PAST_LEN = 256
GRID_W = 64
HEAD_DIM = 64
ROPE_BASE = 10000.0
A_HEADS = 8
A_KV_HEADS = 2
A_WINDOW = 128
B_HEADS = 4
C_WIDTH = 512
C_BLOCKS = 8
C_BW = C_WIDTH // C_BLOCKS
C_POW = 8.0
D_HEADS = 8
NA_ROWS = 8
NA_COLS = 16
N_EXPERTS = 64
TOP_K = 6
D_EXPERT = 512
ROUTED_SCALE = 2.5
ALPHA = (2.0 * DEPTH) ** 0.25
LN_EPS = 1e-5
NORM_EPS = 1e-6
NEG_INF = -1e30
SCALE = HEAD_DIM ** -0.5

N_CTX = BATCH * SEQ
N_LAT = DEC_BATCH * DEC_SEQ
N_TOK = N_CTX + N_LAT
GROUP_ROWS = 4096
N_GROUPS = N_TOK // GROUP_ROWS
IN_WIDTH = 4864
LANES = 128
VMEM_LIMIT = 56 * 1024 * 1024

COL_QA, COL_QB, COL_KB, COL_VB, COL_XC, COL_GC, COL_QD, COL_KD, COL_VD = range(9)
COL_KA, COL_VA = 36, 37

TM_PROJ = 512
TN_PROJ = IN_WIDTH // 2
TM_PREP = 256
TQ_A = 128
TQ_B = 128
NA_QR = 4
NA_KR = 12
T_SCAN = 256
TM_OUT = 256
MOE_BM = 256
TM_DISPATCH = 128
TM_FFN = 128


def _params(*sem):
    return pltpu.CompilerParams(dimension_semantics=sem, vmem_limit_bytes=VMEM_LIMIT)


def _nt(a, b):
    return lax.dot_general(a, b, (((1,), (1,)), ((), ())), preferred_element_type=F32)


def _mm(a, b):
    return jnp.dot(a, b, preferred_element_type=F32)


def _softmax_unnorm(parts, sink=None):
    m = parts[0].max(axis=-1, keepdims=True)
    for p in parts[1:]:
        m = jnp.maximum(m, p.max(axis=-1, keepdims=True))
    if sink is not None:
        m = jnp.maximum(m, sink)
    es = [jnp.exp(p - m) for p in parts]
    den = es[0].sum(axis=-1, keepdims=True)
    for e in es[1:]:
        den = den + e.sum(axis=-1, keepdims=True)
    if sink is not None:
        den = den + jnp.exp(sink - m)
    return es, den


def _rms_gain(x, g):
    return x * lax.rsqrt(jnp.mean(x * x, axis=-1, keepdims=True) + NORM_EPS) * g


def _layernorm(z, g, b):
    zc = z - jnp.mean(z, axis=-1, keepdims=True)
    var = jnp.mean(zc * zc, axis=-1, keepdims=True)
    return zc * lax.rsqrt(var + LN_EPS) * g + b


def _silu(x):
    return x * jax.nn.sigmoid(x)


def _low_half(shape):
    return lax.broadcasted_iota(jnp.int32, shape, 1) < HEAD_DIM


HALF_D = D_MODEL // 2
HIGH16 = 0xFFFF0000


def _pack_rows(x):
    bits = pltpu.bitcast(x.astype(BF16).astype(F32), jnp.uint32)
    return (bits[:, :HALF_D] >> 16) | (bits[:, HALF_D:] & jnp.uint32(HIGH16))


def _unpack_rows(p):
    return (pltpu.bitcast(p << 16, F32), pltpu.bitcast(p & jnp.uint32(HIGH16), F32))


SUBLANES = 8
assert HALF_D == SUBLANES * LANES


def _store_token_tiles(ref, index, packed, first=0):
    m = packed.shape[0]
    for s in range(SUBLANES):
        rows = pl.ds(first * SUBLANES + s, m, stride=SUBLANES)
        ref[index + (rows, slice(None))] = packed[:, s * LANES:(s + 1) * LANES]


def _load_token_tiles(ref, index, m):
    return jnp.concatenate([ref[index + (pl.ds(s, m, stride=SUBLANES), slice(None))] for s in range(SUBLANES)],
                           axis=1)


def _mod_kernel(c_ref, w_ref, b_ref, o_ref):
    c = c_ref[...]
    o_ref[0] = jnp.dot(_silu(c), w_ref[0], precision=lax.Precision.HIGHEST,
                       preferred_element_type=F32) + b_ref[0]


def _modulation(cond8, w_mod, b_mod):
    tn = 1024
    n = w_mod.shape[-1]
    return pl.pallas_call(
        _mod_kernel,
        grid=(DEPTH, n // tn),
        in_specs=[pl.BlockSpec((8, D_MODEL), lambda l, j: (0, 0)),
                  pl.BlockSpec((1, D_MODEL, tn), lambda l, j: (l, 0, j)),
                  pl.BlockSpec((1, 1, tn), lambda l, j: (l, 0, j))],
        out_specs=pl.BlockSpec((1, 8, tn), lambda l, j: (l, 0, j)),
        out_shape=jax.ShapeDtypeStruct((DEPTH, 8, n), F32),
        compiler_params=_params("parallel", "parallel"),
        name="modulation",
    )(cond8, w_mod, b_mod.reshape(DEPTH, 1, n))


def _mod_spec(which, tm):
    return pl.BlockSpec((None, None, 1, D_MODEL), lambda i, *_: (i * tm // GROUP_ROWS, which, 0, 0))


def _inproj_kernel(x_ref, sc_ref, sh_ref, w_ref, o_ref, xb_ref):
    @pl.when(pl.program_id(1) == 0)
    def _():
        xb_ref[...] = (x_ref[...] * (1.0 + sc_ref[...]) + sh_ref[...]).astype(BF16)

    o_ref[...] = _mm(xb_ref[...], w_ref[...])


def _in_projection(x, mod, w_in_b):
    tm, tn = TM_PROJ, TN_PROJ
    return pl.pallas_call(
        _inproj_kernel,
        grid=(N_TOK // tm, IN_WIDTH // tn),
        in_specs=[pl.BlockSpec((tm, D_MODEL), lambda i, j: (i, 0)),
                  _mod_spec(1, tm), _mod_spec(0, tm),
                  pl.BlockSpec((D_MODEL, tn), lambda i, j: (0, j))],
        out_specs=pl.BlockSpec((tm, tn), lambda i, j: (i, j)),
        out_shape=jax.ShapeDtypeStruct((N_TOK, IN_WIDTH), F32),
        scratch_shapes=[pltpu.VMEM((tm, D_MODEL), BF16)],
        compiler_params=_params("parallel", "arbitrary"),
        name="in_projection",
    )(x, mod, mod, w_in_b)


def _ctx_attn_kernel(scal_ref, qa_ref, ka_ref, va_ref, qb_ref, kb_ref, vb_ref, qd_ref, kd_ref, vd_ref,
                     ga_ref, gb_ref, gd_ref, oa_ref, ob_ref, od_ref, *, lam_init):
    L = SEQ
    lo = _low_half((L, LANES))

    ka = ka_ref[...].astype(BF16)
    va = va_ref[...].astype(BF16)
    qa = qa_ref[...]
    chunks = []
    for c in range(A_HEADS // 2):
        hk = c // 2
        chunk = qa[:, c * LANES:(c + 1) * LANES]
        keep = lo if hk == 0 else jnp.logical_not(lo)
        halves = []
        for half in range(2):
            x = chunk if half == hk else pltpu.roll(chunk, HEAD_DIM, 1)
            qh = jnp.where(keep, x, 0.0).astype(BF16)
            s = _nt(qh, ka) * SCALE
            (e,), den = _softmax_unnorm([s], scal_ref[2 * c + half])
            o = _mm(e.astype(BF16), va) / den
            halves.append(o if half == hk else pltpu.roll(o, HEAD_DIM, 1))
        chunks.append(jnp.where(lo, halves[0], halves[1]))
    oa = jnp.concatenate(chunks, axis=1)
    oa_ref[...] = _rms_gain(oa, ga_ref[...]).astype(BF16)

    lam = scal_ref[A_HEADS]
    qb = qb_ref[...]
    chunks = []
    for h in range(B_HEADS):
        sl = slice(h * LANES, (h + 1) * LANES)
        qc = qb[:, sl]
        kc = kb_ref[:, sl].astype(BF16)
        vh = vb_ref[:, sl].astype(BF16)
        (e1,), d1 = _softmax_unnorm([_nt(jnp.where(lo, qc, 0.0).astype(BF16), kc) * SCALE])
        (e2,), d2 = _softmax_unnorm([_nt(jnp.where(lo, 0.0, qc).astype(BF16), kc) * SCALE])
        w = e1 * (1.0 / d1) - e2 * (lam / d2)
        o = _mm(w.astype(BF16), vh)
        chunks.append(_rms_gain(o, gb_ref[:, sl]) * (1.0 - lam_init))
    ob_ref[...] = jnp.concatenate(chunks, axis=1).astype(BF16)

    qd = qd_ref[...]
    chunks = []
    for c in range(D_HEADS // 2):
        sl = slice(c * LANES, (c + 1) * LANES)
        qc = qd[:, sl]
        kc = kd_ref[:, sl].astype(BF16)
        vc = vd_ref[:, sl].astype(BF16)
        halves = []
        for half in range(2):
            qh = jnp.where(lo if half == 0 else jnp.logical_not(lo), qc, 0.0).astype(BF16)
            (e,), den = _softmax_unnorm([_nt(qh, kc) * SCALE])
            halves.append(_mm(e.astype(BF16), vc) / den)
        chunks.append(jnp.where(lo, halves[0], halves[1]))
    od = jnp.concatenate(chunks, axis=1)
    od_ref[...] = _rms_gain(od, gd_ref[...]).astype(BF16)


def _ctx_attention(slab, scal, g_mix2d, lam_init):
    L = SEQ

    def wide(col):
        return pl.BlockSpec((L, 512), lambda b: (b, col))

    def narrow(col):
        return pl.BlockSpec((L, LANES), lambda b: (b, col))

    def gain(col):
        return pl.BlockSpec((1, 512), lambda b: (0, col))

    out = jax.ShapeDtypeStruct((N_TOK, 512), BF16)
    ospec = pl.BlockSpec((L, 512), lambda b: (b, 0))
    return pl.pallas_call(
        functools.partial(_ctx_attn_kernel, lam_init=lam_init),
        grid=(BATCH,),
        in_specs=[pl.BlockSpec(memory_space=pltpu.SMEM),
                  wide(COL_QA), narrow(COL_KA), narrow(COL_VA),
                  wide(COL_QB), wide(COL_KB), wide(COL_VB),
                  wide(COL_QD), wide(COL_KD), wide(COL_VD),
                  gain(0), gain(1), gain(3)],
        out_specs=[ospec, ospec, ospec],
        out_shape=[out, out, out],
        compiler_params=_params("parallel"),
        name="ctx_attention",
    )(scal, slab, slab, slab, slab, slab, slab, slab, slab, slab, g_mix2d, g_mix2d, g_mix2d)


def _prep_kernel(qa_ref, ka_ref, va_ref, qb_ref, kb_ref, vb_ref, qd_ref, kd_ref, vd_ref,
                 cos_ref, sa_ref, sb_ref,
                 qa_o, ka_o, va_o, qb_o, kb_o, vb_o, qd_o, kd_o, vd_o):
    cos, sa, sb = cos_ref[...], sa_ref[...], sb_ref[...]
    lo = _low_half(cos.shape)
    hi = jnp.logical_not(lo)

    def rope(x):
        return (x * cos + pltpu.roll(x, LANES - HEAD_DIM // 4, 1) * sa
                + pltpu.roll(x, HEAD_DIM // 4, 1) * sb)

    ka_o[...] = rope(ka_ref[...]).astype(BF16)
    va_o[...] = va_ref[...].astype(BF16)
    vb_o[...] = vb_ref[...].astype(BF16)
    kd_o[...] = kd_ref[...].astype(BF16)
    vd_o[...] = vd_ref[...].astype(BF16)

    for c in range(A_HEADS // 2):
        hk = c // 2
        r = rope(qa_ref[:, c * LANES:(c + 1) * LANES])
        keep = lo if hk == 0 else hi
        for half in range(2):
            h = 2 * c + half
            x = r if half == hk else pltpu.roll(r, HEAD_DIM, 1)
            qa_o[:, h * LANES:(h + 1) * LANES] = jnp.where(keep, x, 0.0).astype(BF16)
    for h in range(B_HEADS):
        sl = slice(h * LANES, (h + 1) * LANES)
        r = rope(qb_ref[:, sl])
        kb_o[:, sl] = rope(kb_ref[:, sl]).astype(BF16)
        qb_o[:, (2 * h) * LANES:(2 * h + 1) * LANES] = jnp.where(lo, r, 0.0).astype(BF16)
        qb_o[:, (2 * h + 1) * LANES:(2 * h + 2) * LANES] = jnp.where(hi, r, 0.0).astype(BF16)
    for c in range(D_HEADS // 2):
        x = qd_ref[:, c * LANES:(c + 1) * LANES]
        qd_o[:, (2 * c) * LANES:(2 * c + 1) * LANES] = jnp.where(lo, x, 0.0).astype(BF16)
        qd_o[:, (2 * c + 1) * LANES:(2 * c + 2) * LANES] = jnp.where(hi, x, 0.0).astype(BF16)


def _latent_prep(slab, cos, sa, sb):
    tm = TM_PREP
    off = N_CTX // tm
    per_seq = DEC_SEQ // tm

    def wide(col):
        return pl.BlockSpec((tm, 512), lambda i: (i + off, col))

    def narrow(col):
        return pl.BlockSpec((tm, LANES), lambda i: (i + off, col))

    tab = pl.BlockSpec((tm, LANES), lambda i: (i % per_seq, 0))

    def out(width):
        return (pl.BlockSpec((tm, width), lambda i: (i, 0)), jax.ShapeDtypeStruct((N_LAT, width), BF16))

    outs = [out(1024), out(LANES), out(LANES), out(1024), out(512), out(512), out(1024), out(512), out(512)]
    return pl.pallas_call(
        _prep_kernel,
        grid=(N_LAT // tm,),
        in_specs=[wide(COL_QA), narrow(COL_KA), narrow(COL_VA), wide(COL_QB), wide(COL_KB), wide(COL_VB),
                  wide(COL_QD), wide(COL_KD), wide(COL_VD), tab, tab, tab],
        out_specs=[o[0] for o in outs],
        out_shape=[o[1] for o in outs],
        compiler_params=_params("parallel"),
        name="latent_prep",
    )(slab, slab, slab, slab, slab, slab, slab, slab, slab, cos, sa, sb)


def _win_attn_kernel(scal_ref, q_ref, kp_ref, kc_ref, kn_ref, vp_ref, vc_ref, vn_ref, kctx_ref, vctx_ref,
                     g_ref, _all_rows_ref, o_ref):
    n = pl.program_id(1)
    nb = pl.num_programs(1)
    tq = TQ_A
    group = A_HEADS // A_KV_HEADS
    k = jnp.concatenate([kp_ref[...], kc_ref[...], kn_ref[...], kctx_ref[...].astype(BF16)], axis=0)
    v = jnp.concatenate([vp_ref[...], vc_ref[...], vn_ref[...], vctx_ref[...].astype(BF16)], axis=0)
    nk = 3 * tq + PAST_LEN
    row = lax.broadcasted_iota(jnp.int32, (group * tq, nk), 0) & (tq - 1)
    col = lax.broadcasted_iota(jnp.int32, (group * tq, nk), 1)
    prev_thr = row + jnp.where(n > 0, 0, tq)
    next_thr = row + 2 * tq - jnp.where(n < nb - 1, 0, tq)
    masked = ((col < tq) & (col < prev_thr)) | ((col >= 2 * tq) & (col < 3 * tq) & (col > next_thr))
    valid = jnp.logical_not(masked)
    lo = _low_half((tq, LANES))
    heads = []
    for hk in range(A_KV_HEADS):
        q4 = jnp.concatenate([q_ref[:, (hk * group + g) * LANES:(hk * group + g + 1) * LANES]
                              for g in range(group)], axis=0)
        s = jnp.where(valid, _nt(q4, k) * SCALE, NEG_INF)
        sink = jnp.concatenate([jnp.full((tq, 1), scal_ref[hk * group + g], F32) for g in range(group)], axis=0)
        (e,), den = _softmax_unnorm([s], sink)
        o = _mm(e.astype(BF16), v) / den
        for g in range(group):
            h = hk * group + g
            og = o[g * tq:(g + 1) * tq]
            heads.append(og if (h % 2) == hk else pltpu.roll(og, HEAD_DIM, 1))
    oa = jnp.concatenate([jnp.where(lo, heads[2 * c], heads[2 * c + 1]) for c in range(A_HEADS // 2)], axis=1)
    o_ref[...] = _rms_gain(oa, g_ref[...]).astype(BF16)


def _window_attention(qa_p, ka_r, va_b, cache_k, cache_v, scal, g_mix2d, o_all):
    tq = TQ_A
    nb = DEC_SEQ // tq

    def band(d):
        return pl.BlockSpec((tq, LANES), lambda b, n: (b * nb + jnp.clip(n + d, 0, nb - 1), 0))

    ctx = pl.BlockSpec((None, PAST_LEN, LANES), lambda b, n: (b, 0, 0))
    return pl.pallas_call(
        _win_attn_kernel,
        grid=(DEC_BATCH, nb),
        in_specs=[pl.BlockSpec(memory_space=pltpu.SMEM),
                  pl.BlockSpec((tq, A_HEADS * LANES), lambda b, n: (b * nb + n, 0)),
                  band(-1), band(0), band(1), band(-1), band(0), band(1), ctx, ctx,
                  pl.BlockSpec((1, 512), lambda b, n: (0, 0)),
                  pl.BlockSpec(memory_space=pl.ANY)],
        out_specs=pl.BlockSpec((tq, 512), lambda b, n: (N_CTX // tq + b * nb + n, 0)),
        out_shape=jax.ShapeDtypeStruct((N_TOK, 512), BF16),
        input_output_aliases={11: 0},
        compiler_params=_params("parallel", "parallel"),
        name="window_attention",
    )(scal, qa_p, ka_r, ka_r, ka_r, va_b, va_b, va_b, cache_k, cache_v, g_mix2d, o_all)


def _diff_attn_kernel(scal_ref, q_ref, k_ref, v_ref, kctx_ref, vctx_ref, g_ref, _all_rows_ref, o_ref, *,
                      lam_init):
    lam = scal_ref[A_HEADS]
    chunks = []
    for h in range(B_HEADS):
        sl = slice(h * LANES, (h + 1) * LANES)
        kh = k_ref[:, sl]
        kc = kctx_ref[:, sl].astype(BF16)
        es, dens = [], []
        for m in range(2):
            q = q_ref[:, (2 * h + m) * LANES:(2 * h + m + 1) * LANES]
            e, d = _softmax_unnorm([_nt(q, kh) * SCALE, _nt(q, kc) * SCALE])
            es.append(e)
            dens.append(d)
        c1 = 1.0 / dens[0]
        c2 = lam / dens[1]
        w_lat = (es[0][0] * c1 - es[1][0] * c2).astype(BF16)
        w_ctx = (es[0][1] * c1 - es[1][1] * c2).astype(BF16)
        o = _mm(w_lat, v_ref[:, sl]) + _mm(w_ctx, vctx_ref[:, sl].astype(BF16))
        chunks.append(_rms_gain(o, g_ref[:, sl]) * (1.0 - lam_init))
    o_ref[...] = jnp.concatenate(chunks, axis=1).astype(BF16)


def _diff_attention(qb_p, kb_r, vb_b, cache_k, cache_v, scal, g_mix2d, lam_init, o_all):
    tq = TQ_B
    nb = DEC_SEQ // tq
    full = pl.BlockSpec((DEC_SEQ, 512), lambda b, n: (b, 0))
    ctx = pl.BlockSpec((None, PAST_LEN, 512), lambda b, n: (b, 0, 0))
    return pl.pallas_call(
        functools.partial(_diff_attn_kernel, lam_init=lam_init),
        grid=(DEC_BATCH, nb),
        in_specs=[pl.BlockSpec(memory_space=pltpu.SMEM),
                  pl.BlockSpec((tq, 2 * B_HEADS * LANES), lambda b, n: (b * nb + n, 0)),
                  full, full, ctx, ctx,
                  pl.BlockSpec((1, 512), lambda b, n: (0, 1)),
                  pl.BlockSpec(memory_space=pl.ANY)],
        out_specs=pl.BlockSpec((tq, 512), lambda b, n: (N_CTX // tq + b * nb + n, 0)),
        out_shape=jax.ShapeDtypeStruct((N_TOK, 512), BF16),
        input_output_aliases={7: 0},
        compiler_params=_params("parallel", "parallel"),
        name="diff_attention",
    )(scal, qb_p, kb_r, vb_b, cache_k, cache_v, g_mix2d, o_all)


def _na_window_start(step):
    return np.clip(step * NA_QR - NA_ROWS // 2, 0, GRID_W - NA_KR)


def _na_bias_indices():
    rows = DEC_SEQ // GRID_W
    steps = rows // NA_QR
    pats = []
    for step in range(steps):
        w0 = _na_window_start(step)
        r = step * NA_QR + np.arange(NA_QR)[:, None, None, None]
        c = np.arange(GRID_W)[None, :, None, None]
        kr = w0 + np.arange(NA_KR)[None, None, :, None]
        kc = np.arange(GRID_W)[None, None, None, :]
        kr0 = np.clip(r - NA_ROWS // 2, 0, rows - NA_ROWS)
        kc0 = np.clip(c - NA_COLS // 2, 0, GRID_W - NA_COLS)
        valid = (kr >= kr0) & (kr < kr0 + NA_ROWS) & (kc >= kc0) & (kc < kc0 + NA_COLS)
        drow = np.clip(kr - r + NA_ROWS - 1, 0, 2 * NA_ROWS - 2)
        dcol = np.clip(kc - c + NA_COLS - 1, 0, 2 * NA_COLS - 2)
        shape = (NA_QR * GRID_W, NA_KR * GRID_W)
        full = np.broadcast_to
        pats.append((full(drow, valid.shape).reshape(shape), full(dcol, valid.shape).reshape(shape),
                     valid.reshape(shape)))
    for step in range(2, steps - 1):
        for a, b in zip(pats[1], pats[step]):
            assert np.array_equal(a, b)
    kinds = [pats[0], pats[1], pats[-1]]
    drow, dcol, valid = (np.stack([k[i] for k in kinds]) for i in range(3))
    shape6 = (len(kinds), NA_QR, GRID_W, NA_KR, GRID_W)
    row_sel = np.eye(2 * NA_ROWS - 1, dtype=np.float32)[drow.reshape(shape6)[:, :, 0, :, 0]]
    col_sel = np.eye(2 * NA_COLS - 1, dtype=np.float32)[dcol.reshape(shape6)[0, 0, :, 0, :]]
    return row_sel, col_sel, valid


def _na_bias(rpb, row_sel, col_sel, valid):
    hp = lax.Precision.HIGHEST
    cols = jnp.einsum('hab,cdb->hacd', rpb, col_sel, precision=hp)
    bias = jnp.einsum('kria,hacd->khrcid', row_sel, cols, precision=hp)
    bias = bias.reshape(valid.shape[0], D_HEADS, NA_QR * GRID_W, NA_KR * GRID_W)
    return jnp.where(valid[:, None], bias, NEG_INF)


def _na_kernel(q_ref, k_ref, v_ref, kctx_ref, vctx_ref, bias_ref, g_ref, _all_rows_ref, o_ref):
    step = pl.program_id(1)
    nq = NA_QR * GRID_W
    nk = NA_KR * GRID_W
    w0 = jnp.clip(step * NA_QR - NA_ROWS // 2, 0, GRID_W - NA_KR)
    start = pl.multiple_of(w0 * GRID_W, GRID_W)
    lo = _low_half((nq, LANES))
    chunks = []
    for c in range(D_HEADS // 2):
        sl = slice(c * LANES, (c + 1) * LANES)
        kw = k_ref[pl.ds(start, nk), sl]
        vw = v_ref[pl.ds(start, nk), sl]
        kc = kctx_ref[:, sl].astype(BF16)
        vc = vctx_ref[:, sl].astype(BF16)
        halves = []
        for half in range(2):
            h = 2 * c + half
            q = q_ref[:, h * LANES:(h + 1) * LANES]
            s_loc = _nt(q, kw) * SCALE + bias_ref[h]
            s_ctx = _nt(q, kc) * SCALE
            (e_loc, e_ctx), den = _softmax_unnorm([s_loc, s_ctx])
            halves.append((_mm(e_loc.astype(BF16), vw) + _mm(e_ctx.astype(BF16), vc)) / den)
        chunks.append(jnp.where(lo, halves[0], halves[1]))
    od = jnp.concatenate(chunks, axis=1)
    o_ref[...] = _rms_gain(od, g_ref[...]).astype(BF16)


def _neighbourhood_attention(qd_p, kd_b, vd_b, cache_k, cache_v, bias, g_mix2d, o_all):
    nq = NA_QR * GRID_W
    steps = DEC_SEQ // nq
    full = pl.BlockSpec((DEC_SEQ, 512), lambda b, j: (b, 0))
    ctx = pl.BlockSpec((None, PAST_LEN, 512), lambda b, j: (b, 0, 0))

    def kind(b, j):
        return (jnp.where(j == 0, 0, jnp.where(j == steps - 1, 2, 1)), 0, 0, 0)

    return pl.pallas_call(
        _na_kernel,
        grid=(DEC_BATCH, steps),
        in_specs=[pl.BlockSpec((nq, D_HEADS * LANES), lambda b, j: (b * steps + j, 0)),
                  full, full, ctx, ctx,
                  pl.BlockSpec((None, D_HEADS, nq, NA_KR * GRID_W), kind),
                  pl.BlockSpec((1, 512), lambda b, j: (0, 3)),
                  pl.BlockSpec(memory_space=pl.ANY)],
        out_specs=pl.BlockSpec((nq, 512), lambda b, j: (N_CTX // nq + b * steps + j, 0)),
        out_shape=jax.ShapeDtypeStruct((N_TOK, 512), BF16),
        input_output_aliases={7: 0},
        compiler_params=_params("parallel", "arbitrary"),
        name="neighbourhood_attention",
    )(qd_p, kd_b, vd_b, cache_k, cache_v, bias, g_mix2d, o_all)


N_SCAN_TILES = N_TOK // T_SCAN
CTX_TILES = N_CTX // T_SCAN
TILES_PER_LAT = DEC_SEQ // T_SCAN


def _scan_tile_flags(i):
    is_ctx = i < CTX_TILES
    pos = (i - CTX_TILES) % TILES_PER_LAT
    return is_ctx | (pos == 0), is_ctx | (pos == TILES_PER_LAT - 1)


def _scan_seq(i):
    return jnp.where(i < CTX_TILES, i, CTX_TILES + (i - CTX_TILES) // TILES_PER_LAT)


def _conv_gates(x_ref, prev_ref, next_ref, cw_ref, cb_ref, wgh_ref, wgl_ref, bg_ref, lam_ref, first, last):
    x = x_ref[...]
    t = T_SCAN
    row = lax.broadcasted_iota(jnp.int32, x.shape, 0)
    pm1 = jnp.where(first, 0.0, prev_ref[7:8, :])
    n0 = jnp.where(last, 0.0, next_ref[0:1, :])
    n1 = jnp.where(last, 0.0, next_ref[1:2, :])
    x_m1 = jnp.where(row == 0, pm1, pltpu.roll(x, 1, 0))
    x_p1 = jnp.where(row == t - 1, n0, pltpu.roll(x, t - 1, 0))
    x_p2 = jnp.where(row == t - 2, n0, jnp.where(row == t - 1, n1, pltpu.roll(x, t - 2, 0)))
    y = (cw_ref[0:1, :] * x_m1 + cw_ref[1:2, :] * x + cw_ref[2:3, :] * x_p1 + cw_ref[3:4, :] * x_p2
         + cb_ref[...])
    y_hi = y.astype(BF16)
    y_lo = (y - y_hi.astype(F32)).astype(BF16)
    zr, zi = [], []
    for c in range(C_WIDTH // LANES):
        sl = slice(c * LANES, (c + 1) * LANES)
        w_hi = wgh_ref[0, sl, :]
        z = _mm(y_hi[:, sl], w_hi) + _mm(y_lo[:, sl], w_hi) + _mm(y_hi[:, sl], wgl_ref[0, sl, :])
        zr.append(z[:, :LANES])
        zi.append(z[:, LANES:])
    bias = bg_ref[0]
    r = jax.nn.sigmoid(jnp.concatenate(zr, axis=1) + bias[:, :C_WIDTH])
    gate_i = jax.nn.sigmoid(jnp.concatenate(zi, axis=1) + bias[:, C_WIDTH:])
    nl = -lam_ref[0]
    softplus = jnp.maximum(nl, 0.0) + jnp.log1p(jnp.exp(-jnp.abs(nl)))
    log_a = -C_POW * r * softplus
    a = jnp.exp(log_a)
    u = jnp.sqrt(1.0 - jnp.exp(2.0 * log_a)) * gate_i * y
    return a, u


def _scan_rows(a_s, u_s, h_s, h, reverse):
    groups = T_SCAN // 8

    def body(i, h):
        g = (groups - 1 - i) if reverse else i
        base = pl.multiple_of(g * 8, 8)
        a8 = a_s[pl.ds(base, 8), :]
        u8 = u_s[pl.ds(base, 8), :]
        rows = [None] * 8
        for k in (range(7, -1, -1) if reverse else range(8)):
            h = a8[k:k + 1, :] * h + u8[k:k + 1, :]
            rows[k] = h
        h_s[pl.ds(base, 8), :] = jnp.concatenate(rows, axis=0)
        return h

    return lax.fori_loop(0, groups, body, h)


def _scan_fwd_kernel(x_ref, prev_ref, next_ref, cw_ref, cb_ref, wgh_ref, wgl_ref, bg_ref, lam_ref, h0_ref,
                     hf_ref, fin_ref, a_s, u_s, carry_s):
    i = pl.program_id(0)
    first, last = _scan_tile_flags(i)
    a, u = _conv_gates(x_ref, prev_ref, next_ref, cw_ref, cb_ref, wgh_ref, wgl_ref, bg_ref, lam_ref, first, last)
    a_s[...] = a
    u_s[...] = u

    @pl.when(first)
    def _():
        carry_s[...] = h0_ref[...]

    h = _scan_rows(a_s, u_s, hf_ref, carry_s[...], reverse=False)
    carry_s[...] = h
    fin_ref[...] = h


def _scan_bwd_kernel(x_ref, prev_ref, next_ref, cw_ref, cb_ref, wgh_ref, wgl_ref, bg_ref, lam_ref, h0_ref,
                     hf_ref, gc_ref, g_ref, oc_ref, fin_ref, a_s, u_s, hb_s, carry_s):
    i = N_SCAN_TILES - 1 - pl.program_id(0)
    first, last = _scan_tile_flags(i)
    a, u = _conv_gates(x_ref, prev_ref, next_ref, cw_ref, cb_ref, wgh_ref, wgl_ref, bg_ref, lam_ref, first, last)
    a_s[...] = a
    u_s[...] = u

    @pl.when(last)
    def _():
        carry_s[...] = h0_ref[...]

    h = _scan_rows(a_s, u_s, hb_s, carry_s[...], reverse=True)
    carry_s[...] = h
    fin_ref[...] = h
    g = gc_ref[...]
    gelu = 0.5 * g * (1.0 + jnp.tanh(math.sqrt(2.0 / math.pi) * (g + 0.044715 * (g * g * g))))
    oc_ref[...] = _rms_gain(gelu * (hf_ref[...] + hb_s[...]), g_ref[...]).astype(BF16)


def _scan_common_specs(tile_of, direction):
    halo = T_SCAN // 8
    last_block = N_TOK // 8 - 1
    return [pl.BlockSpec((T_SCAN, 512), lambda j: (tile_of(j), COL_XC)),
            pl.BlockSpec((8, 512), lambda j: (jnp.maximum(tile_of(j) * halo - 1, 0), COL_XC)),
            pl.BlockSpec((8, 512), lambda j: (jnp.minimum((tile_of(j) + 1) * halo, last_block), COL_XC)),
            pl.BlockSpec((4, C_WIDTH), lambda j: (0, 0)),
            pl.BlockSpec((1, C_WIDTH), lambda j: (0, 0)),
            pl.BlockSpec((1, C_WIDTH, 2 * LANES), lambda j: (direction, 0, 0)),
            pl.BlockSpec((1, C_WIDTH, 2 * LANES), lambda j: (direction, 0, 0)),
            pl.BlockSpec((1, 1, 2 * C_WIDTH), lambda j: (direction, 0, 0)),
            pl.BlockSpec((1, 1, C_WIDTH), lambda j: (direction, 0, 0)),
            pl.BlockSpec((None, 1, C_WIDTH), lambda j: (_scan_seq(tile_of(j)), 0, 0))]


def _rglru(slab, conv_w, conv_b, w_gates, b_gates, lam, h0_f, h0_b, g_mix2d):
    wg_hi = w_gates.astype(BF16)
    wg_lo = (w_gates - wg_hi.astype(F32)).astype(BF16)
    fin = jax.ShapeDtypeStruct((N_SCAN_TILES, 1, C_WIDTH), F32)
    fwd_tile = lambda j: j
    h_f, fin_f = pl.pallas_call(
        _scan_fwd_kernel,
        grid=(N_SCAN_TILES,),
        in_specs=_scan_common_specs(fwd_tile, 0),
        out_specs=[pl.BlockSpec((T_SCAN, C_WIDTH), lambda j: (j, 0)),
                   pl.BlockSpec((None, 1, C_WIDTH), lambda j: (j, 0, 0))],
        out_shape=[jax.ShapeDtypeStruct((N_TOK, C_WIDTH), F32), fin],
        scratch_shapes=[pltpu.VMEM((T_SCAN, C_WIDTH), F32), pltpu.VMEM((T_SCAN, C_WIDTH), F32),
                        pltpu.VMEM((1, C_WIDTH), F32)],
        compiler_params=_params("arbitrary"),
        name="rglru_forward",
    )(slab, slab, slab, conv_w, conv_b, wg_hi, wg_lo, b_gates, lam, h0_f)
    bwd_tile = lambda j: N_SCAN_TILES - 1 - j
    oc, fin_b = pl.pallas_call(
        _scan_bwd_kernel,
        grid=(N_SCAN_TILES,),
        in_specs=_scan_common_specs(bwd_tile, 1) + [
            pl.BlockSpec((T_SCAN, C_WIDTH), lambda j: (bwd_tile(j), 0)),
            pl.BlockSpec((T_SCAN, 512), lambda j: (bwd_tile(j), COL_GC)),
            pl.BlockSpec((1, 512), lambda j: (0, 2))],
        out_specs=[pl.BlockSpec((T_SCAN, C_WIDTH), lambda j: (bwd_tile(j), 0)),
                   pl.BlockSpec((None, 1, C_WIDTH), lambda j: (bwd_tile(j), 0, 0))],
        out_shape=[jax.ShapeDtypeStruct((N_TOK, C_WIDTH), BF16), fin],
        scratch_shapes=[pltpu.VMEM((T_SCAN, C_WIDTH), F32), pltpu.VMEM((T_SCAN, C_WIDTH), F32),
                        pltpu.VMEM((T_SCAN, C_WIDTH), F32), pltpu.VMEM((1, C_WIDTH), F32)],
        compiler_params=_params("arbitrary"),
        name="rglru_backward",
    )(slab, slab, slab, conv_w, conv_b, wg_hi, wg_lo, b_gates, lam, h0_b, h_f, slab, g_mix2d)
    return oc, fin_f, fin_b


def _mixout_kernel(oa_ref, ob_ref, oc_ref, od_ref, w_ref, x_ref, g1_ref, sc2_ref, sh2_ref, lng_ref, lnb_ref,
                   wrh_ref, wrl_ref, br_ref, x1_ref, h2_ref, idx_ref, wt_ref, rank_ref, cnt_ref, cnt_s):
    @pl.when(pl.program_id(0) == 0)
    def _():
        cnt_s[...] = jnp.zeros(cnt_s.shape, F32)

    parts = 2
    pm = x_ref.shape[0] // parts
    for part in range(parts):
        _mixout_rows(pl.ds(part * pm, pm), part * pm, pm, oa_ref, ob_ref, oc_ref, od_ref, w_ref, x_ref, g1_ref,
                     sc2_ref, sh2_ref, lng_ref, lnb_ref, wrh_ref, wrl_ref, br_ref, x1_ref, h2_ref, idx_ref,
                     wt_ref, rank_ref, cnt_s)
    cnt_ref[...] = cnt_s[...]


def _mixout_rows(rows, first, tm, oa_ref, ob_ref, oc_ref, od_ref, w_ref, x_ref, g1_ref, sc2_ref, sh2_ref,
                 lng_ref, lnb_ref, wrh_ref, wrl_ref, br_ref, x1_ref, h2_ref, idx_ref, wt_ref, rank_ref, cnt_s):
    y = _mm(oa_ref[rows, :], w_ref[0:512, :])
    y += _mm(ob_ref[rows, :], w_ref[512:1024, :])
    y += _mm(oc_ref[rows, :], w_ref[1024:1536, :])
    y += _mm(od_ref[rows, :], w_ref[1536:2048, :])
    x1 = _layernorm(ALPHA * x_ref[rows, :] + g1_ref[...] * y, lng_ref[...], lnb_ref[...])
    x1_ref[rows, :] = x1
    h2 = x1 * (1.0 + sc2_ref[...]) + sh2_ref[...]
    _store_token_tiles(h2_ref, (), _pack_rows(h2), first)

    h_hi = h2.astype(BF16)
    h_lo = (h2 - h_hi.astype(F32)).astype(BF16)
    scores = jax.nn.sigmoid(_mm(h_hi, wrh_ref[...]) + _mm(h_lo, wrh_ref[...]) + _mm(h_hi, wrl_ref[...]))
    sel = scores + br_ref[...]
    lane = lax.broadcasted_iota(jnp.int32, sel.shape, 1).astype(F32)
    slot = lax.broadcasted_iota(jnp.int32, (tm, 8), 1)
    idx_out = jnp.zeros((tm, 8), F32)
    wt_out = jnp.zeros((tm, 8), F32)
    total = jnp.zeros((sel.shape[0], 1), F32)
    hits = []
    for k in range(TOP_K):
        m = sel.max(axis=-1, keepdims=True)
        idx = jnp.where(sel == m, lane, float(N_EXPERTS)).min(axis=-1, keepdims=True)
        hit = lane == idx
        hits.append(hit)
        w = jnp.where(hit, scores, 0.0).sum(axis=-1, keepdims=True)
        total = total + w
        idx_out = jnp.where(slot == k, idx, idx_out)
        wt_out = jnp.where(slot == k, w, wt_out)
        sel = jnp.where(hit, -jnp.inf, sel)
    idx_ref[rows, :] = idx_out.astype(jnp.int32)
    wt_ref[rows, :] = wt_out / total * ROUTED_SCALE

    chosen = jnp.zeros(sel.shape, F32)
    for hit in hits:
        chosen = chosen + hit.astype(F32)
    earlier = (lax.broadcasted_iota(jnp.int32, (tm, tm), 1) < lax.broadcasted_iota(jnp.int32, (tm, tm), 0))
    before = _mm(earlier.astype(F32).astype(BF16), chosen.astype(BF16)) + cnt_s[...]
    rank_out = jnp.zeros((tm, 8), F32)
    for k, hit in enumerate(hits):
        rank_out = jnp.where(slot == k, jnp.where(hit, before, 0.0).sum(axis=-1, keepdims=True), rank_out)
    rank_ref[rows, :] = rank_out.astype(jnp.int32)
    cnt_s[...] = cnt_s[...] + chosen.sum(axis=0, keepdims=True)


def _mixer_out(oa, ob, oc, od, w_out_b, x, mod, ln_g, ln_b, w_router, b_router):
    tm = TM_OUT
    cat = pl.BlockSpec((tm, 512), lambda i: (i, 0))
    row = pl.BlockSpec((tm, D_MODEL), lambda i: (i, 0))
    vec = pl.BlockSpec((1, D_MODEL), lambda i: (0, 0))
    k8 = pl.BlockSpec((tm, 8), lambda i: (i, 0))
    per_expert = pl.BlockSpec((1, N_EXPERTS), lambda i: (0, 0))
    router = pl.BlockSpec((D_MODEL, N_EXPERTS), lambda i: (0, 0))
    wr_hi = w_router.astype(BF16)
    wr_lo = (w_router - wr_hi.astype(F32)).astype(BF16)
    return pl.pallas_call(
        _mixout_kernel,
        grid=(N_TOK // tm,),
        in_specs=[cat, cat, cat, cat,
                  pl.BlockSpec((D_MODEL, D_MODEL), lambda i: (0, 0)),
                  row, _mod_spec(2, tm), _mod_spec(4, tm), _mod_spec(3, tm), vec, vec,
                  router, router, per_expert],
        out_specs=[row, pl.BlockSpec((tm * SUBLANES, LANES), lambda i: (i, 0)), k8, k8, k8, per_expert],
        out_shape=[jax.ShapeDtypeStruct((N_TOK, D_MODEL), F32),
                   jax.ShapeDtypeStruct((N_TOK * SUBLANES, LANES), jnp.uint32),
                   jax.ShapeDtypeStruct((N_TOK, 8), jnp.int32), jax.ShapeDtypeStruct((N_TOK, 8), F32),
                   jax.ShapeDtypeStruct((N_TOK, 8), jnp.int32), jax.ShapeDtypeStruct((1, N_EXPERTS), F32)],
        scratch_shapes=[pltpu.VMEM((1, N_EXPERTS), F32)],
        compiler_params=_params("arbitrary"),
        name="mixer_out_router",
    )(oa, ob, oc, od, w_out_b, x, mod, mod, mod, ln_g, ln_b, wr_hi, wr_lo, b_router)


MOE_BLOCKS = N_TOK * TOP_K // MOE_BM + N_EXPERTS
MOE_ROWS = MOE_BLOCKS * MOE_BM


def _row_copy(src, src_row, dst, dst_row, sem):
    def tile(row):
        return pl.ds(row * SUBLANES if isinstance(row, int) else pl.multiple_of(row * SUBLANES, SUBLANES), SUBLANES)

    return pltpu.make_async_copy(src.at[tile(src_row)], dst.at[tile(dst_row)], sem)


def _dispatch_kernel(slot_ref, h_ref, xg_ref, sem):
    tm = h_ref.shape[0] // SUBLANES
    base = pl.program_id(0) * (tm * TOP_K)

    for r in range(tm):
        for k in range(TOP_K):
            _row_copy(h_ref, r, xg_ref, slot_ref[base + (r * TOP_K + k)], sem).start()
    for k in range(TOP_K):
        pltpu.make_async_copy(h_ref, xg_ref.at[pl.ds(0, tm * SUBLANES)], sem).wait()


def _dispatch_rows(slot_flat, h2):
    tm = TM_DISPATCH
    grid_spec = pltpu.PrefetchScalarGridSpec(
        num_scalar_prefetch=1,
        grid=(N_TOK // tm,),
        in_specs=[pl.BlockSpec((tm * SUBLANES, LANES), lambda i, s: (i, 0))],
        out_specs=pl.BlockSpec(memory_space=pl.ANY),
        scratch_shapes=[pltpu.SemaphoreType.DMA])
    return pl.pallas_call(
        _dispatch_kernel,
        grid_spec=grid_spec,
        out_shape=jax.ShapeDtypeStruct((MOE_ROWS * SUBLANES, LANES), jnp.uint32),
        compiler_params=_params("arbitrary"),
        name="moe_dispatch",
    )(slot_flat, h2)


BLOCK_TILE_ROWS = MOE_BM * SUBLANES


def _expert_kernel(blk0_ref, nblk_ref, cnt_ref, wg_ref, wu_ref, wd_ref, x_hbm, y_hbm,
                   wg_s, wu_s, wd_s, xbuf, ybuf, xsem, ysem):
    e = pl.program_id(0)
    first_blk = blk0_ref[e]
    n_blk = nblk_ref[e]
    count = cnt_ref[e]
    total = blk0_ref[N_EXPERTS - 1] + nblk_ref[N_EXPERTS - 1]

    def rows_of(g):
        return pl.ds(pl.multiple_of(g * BLOCK_TILE_ROWS, BLOCK_TILE_ROWS), BLOCK_TILE_ROWS)

    def x_copy(g, buf):
        return pltpu.make_async_copy(x_hbm.at[rows_of(g)], xbuf.at[buf], xsem.at[buf])

    def y_copy(g, buf):
        return pltpu.make_async_copy(ybuf.at[buf], y_hbm.at[rows_of(g)], ysem.at[buf])

    @pl.when((e == 0) & (total > 0))
    def _():
        x_copy(0, 0).start()

    wg_s[...] = wg_ref[...].astype(BF16)
    wu_s[...] = wu_ref[...].astype(BF16)
    wd_s[...] = wd_ref[...].astype(BF16)

    def block(j, carry):
        g = first_blk + j
        buf = g % 2

        @pl.when(g + 1 < total)
        def _():
            x_copy(g + 1, 1 - buf).start()

        x_copy(g, buf).wait()

        @pl.when(g >= 2)
        def _():
            y_copy(g - 2, buf).wait()

        x = _load_token_tiles(xbuf, (buf,), MOE_BM)
        row = lax.broadcasted_iota(jnp.int32, x.shape, 0)
        lo, hi = _unpack_rows(jnp.where(row < count - j * MOE_BM, x, jnp.uint32(0)))
        lo, hi = lo.astype(BF16), hi.astype(BF16)
        gate = _mm(lo, wg_s[:HALF_D, :]) + _mm(hi, wg_s[HALF_D:, :])
        up = _mm(lo, wu_s[:HALF_D, :]) + _mm(hi, wu_s[HALF_D:, :])
        _store_token_tiles(ybuf, (buf,), _pack_rows(_mm((_silu(gate) * up).astype(BF16), wd_s[...])))
        y_copy(g, buf).start()
        return carry

    lax.fori_loop(0, n_blk, block, 0)

    @pl.when((e == N_EXPERTS - 1) & (total >= 2))
    def _():
        y_copy(total - 2, total % 2).wait()

    @pl.when((e == N_EXPERTS - 1) & (total >= 1))
    def _():
        y_copy(total - 1, (total - 1) % 2).wait()


def _routed_experts(xg, first_blk, n_blk, counts, layer, w_g, w_u, w_d):
    def weight(shape):
        return pl.BlockSpec((None, None) + shape, lambda e, *_: (layer, e, 0, 0))

    any_space = pl.BlockSpec(memory_space=pl.ANY)
    block_buf = pltpu.VMEM((2, BLOCK_TILE_ROWS, LANES), jnp.uint32)
    grid_spec = pltpu.PrefetchScalarGridSpec(
        num_scalar_prefetch=3,
        grid=(N_EXPERTS,),
        in_specs=[weight((D_MODEL, D_EXPERT)), weight((D_MODEL, D_EXPERT)), weight((D_EXPERT, D_MODEL)), any_space],
        out_specs=any_space,
        scratch_shapes=[pltpu.VMEM((D_MODEL, D_EXPERT), BF16), pltpu.VMEM((D_MODEL, D_EXPERT), BF16),
                        pltpu.VMEM((D_EXPERT, D_MODEL), BF16), block_buf, block_buf,
                        pltpu.SemaphoreType.DMA((2,)), pltpu.SemaphoreType.DMA((2,))])
    return pl.pallas_call(
        _expert_kernel,
        grid_spec=grid_spec,
        out_shape=jax.ShapeDtypeStruct((MOE_ROWS * SUBLANES, LANES), jnp.uint32),
        compiler_params=_params("arbitrary"),
        name="routed_experts",
    )(first_blk, n_blk, counts, w_g, w_u, w_d, xg)


def _ffn_out_kernel(slot_ref, h_ref, wt_ref, x1_ref, wg_ref, wu_ref, wd_ref, g2_ref, lng_ref, lnb_ref, y_ref,
                    o_ref, ybuf, sems):
    tm = h_ref.shape[0] // SUBLANES
    i = pl.program_id(0)
    n = pl.num_programs(0)

    def gather(tile, buf):
        base = tile * (tm * TOP_K)

        for r in range(tm):
            for k in range(TOP_K):
                _row_copy(y_ref, slot_ref[base + (r * TOP_K + k)], ybuf.at[buf, k], r, sems.at[buf]).start()

    @pl.when(i == 0)
    def _():
        gather(0, 0)

    @pl.when(i + 1 < n)
    def _():
        gather(i + 1, (i + 1) % 2)

    lo, hi = _unpack_rows(_load_token_tiles(h_ref, (), tm))
    lo, hi = lo.astype(BF16), hi.astype(BF16)
    gate = _mm(lo, wg_ref[:HALF_D, :]) + _mm(hi, wg_ref[HALF_D:, :])
    up = _mm(lo, wu_ref[:HALF_D, :]) + _mm(hi, wu_ref[HALF_D:, :])
    y = _mm((_silu(gate) * up).astype(BF16), wd_ref[...])
    buf = i % 2
    for k in range(TOP_K):
        pltpu.make_async_copy(y_ref.at[pl.ds(0, tm * SUBLANES)], ybuf.at[buf, k], sems.at[buf]).wait()
    wt = wt_ref[...]
    y_lo, y_hi = y[:, :HALF_D], y[:, HALF_D:]
    for k in range(TOP_K):
        r_lo, r_hi = _unpack_rows(_load_token_tiles(ybuf, (buf, k), tm))
        y_lo = y_lo + r_lo * wt[:, k:k + 1]
        y_hi = y_hi + r_hi * wt[:, k:k + 1]
    y = jnp.concatenate([y_lo, y_hi], axis=1)
    o_ref[...] = _layernorm(ALPHA * x1_ref[...] + g2_ref[...] * y, lng_ref[...], lnb_ref[...])


def _ffn_out(slot_flat, h2, wt, x1, ws_g, ws_u, ws_d, mod, ln_g, ln_b, y_blk):
    tm = TM_FFN
    row = pl.BlockSpec((tm, D_MODEL), lambda i, s: (i, 0))
    vec = pl.BlockSpec((1, D_MODEL), lambda i, s: (0, 0))
    grid_spec = pltpu.PrefetchScalarGridSpec(
        num_scalar_prefetch=1,
        grid=(N_TOK // tm,),
        in_specs=[pl.BlockSpec((tm * SUBLANES, LANES), lambda i, s: (i, 0)),
                  pl.BlockSpec((tm, 8), lambda i, s: (i, 0)), row,
                  pl.BlockSpec((D_MODEL, D_EXPERT), lambda i, s: (0, 0)),
                  pl.BlockSpec((D_MODEL, D_EXPERT), lambda i, s: (0, 0)),
                  pl.BlockSpec((D_EXPERT, D_MODEL), lambda i, s: (0, 0)),
                  _mod_spec(5, tm), vec, vec,
                  pl.BlockSpec(memory_space=pl.ANY)],
        out_specs=row,
        scratch_shapes=[pltpu.VMEM((2, TOP_K, tm * SUBLANES, LANES), jnp.uint32),
                        pltpu.SemaphoreType.DMA((2,))])
    return pl.pallas_call(
        _ffn_out_kernel,
        grid_spec=grid_spec,
        out_shape=jax.ShapeDtypeStruct((N_TOK, D_MODEL), F32),
        compiler_params=_params("arbitrary"),
        name="shared_expert_out",
    )(slot_flat, h2, wt, x1, ws_g, ws_u, ws_d, mod, ln_g, ln_b, y_blk)


def _dispatch_plan(idx, rank, counts):
    bm = MOE_BM
    counts = counts.reshape(N_EXPERTS).astype(jnp.int32)
    padded = (counts + bm - 1) // bm * bm
    pad_end = jnp.cumsum(padded)
    pad_start = pad_end - padded
    slot = pad_start[idx[:, :TOP_K]] + rank[:, :TOP_K]
    return slot.reshape(-1).astype(jnp.int32), pad_start // bm, padded // bm, counts


def _rope_tables():
    t = jnp.arange(DEC_SEQ)
    row = (t // GRID_W).astype(F32)
    col = (t % GRID_W).astype(F32)
    nf = HEAD_DIM // 4
    inv = ROPE_BASE ** (-jnp.arange(nf, dtype=F32) / nf)
    ar = row[:, None] * inv[None, :]
    ac = col[:, None] * inv[None, :]
    ang = jnp.concatenate([ar, ar, ac, ac], axis=-1)
    ang = jnp.concatenate([ang, ang], axis=-1)
    cos, sin = jnp.cos(ang), jnp.sin(ang)
    first = (jnp.arange(LANES) % (2 * nf)) < nf
    return cos, jnp.where(first, -sin, 0.0), jnp.where(first, 0.0, sin)


def _lambda_init(l):
    return 0.8 - 0.6 * math.exp(-0.3 * l)


def _gate_pairs(w_r, w_i):
    per = LANES // C_BW
    eye = jnp.eye(per, dtype=w_r.dtype)

    def chunk_diag(w):
        w = w.reshape(C_BLOCKS // per, per, C_BW, C_BW)
        return (eye[None, :, None, :, None] * w[:, :, :, None, :]).reshape(C_BLOCKS // per, LANES, LANES)

    return jnp.concatenate([chunk_diag(w_r), chunk_diag(w_i)], axis=2).reshape(C_WIDTH, 2 * LANES)


def kernel(x_prompt, x_sample, c, cache_a_k, cache_a_v, cache_b_k, cache_b_v, state_c, cache_d_k, cache_d_v, c_ctx, w_mod, b_mod, w_in, a_sink, b_lambda, c_conv_w, c_conv_b, c_w_rgate, c_b_rgate, c_w_igate, c_b_igate, c_lambda, d_rpb, g_mix, w_out, ln_g, ln_b, w_router, b_router, w_e_gate, w_e_up, w_e_down, w_s_gate, w_s_up, w_s_down):
    x = jnp.concatenate([x_prompt.reshape(N_CTX, D_MODEL), x_sample.reshape(N_LAT, D_MODEL)], axis=0)
    cond8 = jnp.zeros((8, D_MODEL), F32).at[0].set(c_ctx).at[1:1 + DEC_BATCH].set(c)
    mod_all = _modulation(cond8, w_mod, b_mod)
    cos, sin_a, sin_b = _rope_tables()
    row_sel, col_sel, na_valid = _na_bias_indices()

    order = np.concatenate([np.arange(0, 512), np.arange(768, 4864), np.arange(512, 768)])

    ctx_out = [[] for _ in range(7)]
    for l in range(DEPTH):
        lam_init = _lambda_init(l)
        mod = mod_all[l, :N_GROUPS].reshape(N_GROUPS, 6, 1, D_MODEL)
        w_in_b = w_in[l][:, order].astype(BF16)
        slab = _in_projection(x, mod, w_in_b)

        lp = b_lambda[l]
        lam = jnp.exp(jnp.sum(lp[0] * lp[1])) - jnp.exp(jnp.sum(lp[2] * lp[3])) + lam_init
        scal = jnp.zeros((16,), F32).at[:A_HEADS].set(a_sink[l]).at[A_HEADS].set(lam)
        g2d = g_mix[l].reshape(1, -1)

        oa, ob, od = _ctx_attention(slab, scal, g2d, lam_init)

        qa_p, ka_r, va_b, qb_p, kb_r, vb_b, qd_p, kd_b, vd_b = _latent_prep(slab, cos, sin_a, sin_b)
        oa = _window_attention(qa_p, ka_r, va_b, cache_a_k[:, l].reshape(DEC_BATCH, PAST_LEN, LANES),
                               cache_a_v[:, l].reshape(DEC_BATCH, PAST_LEN, LANES), scal, g2d, oa)
        ob = _diff_attention(qb_p, kb_r, vb_b, cache_b_k[:, l].reshape(DEC_BATCH, PAST_LEN, 512),
                             cache_b_v[:, l].reshape(DEC_BATCH, PAST_LEN, 512), scal, g2d, lam_init, ob)
        bias = _na_bias(d_rpb[l], row_sel, col_sel, na_valid)
        od = _neighbourhood_attention(qd_p, kd_b, vd_b, cache_d_k[:, l].reshape(DEC_BATCH, PAST_LEN, 512),
                                      cache_d_v[:, l].reshape(DEC_BATCH, PAST_LEN, 512), bias, g2d, od)

        w_gates = jnp.stack([_gate_pairs(c_w_rgate[l, d], c_w_igate[l, d]) for d in range(2)])
        b_gates = jnp.concatenate([c_b_rgate[l], c_b_igate[l]], axis=-1).reshape(2, 1, 2 * C_WIDTH)
        zeros = jnp.zeros((BATCH, 1, C_WIDTH), F32)
        h0_f = jnp.concatenate([zeros, state_c[:, l, 0][:, None, :]], axis=0)
        h0_b = jnp.concatenate([zeros, state_c[:, l, 1][:, None, :]], axis=0)
        oc, fin_f, fin_b = _rglru(slab, c_conv_w[l], c_conv_b[l].reshape(1, -1), w_gates, b_gates,
                                  c_lambda[l].reshape(2, 1, C_WIDTH), h0_f, h0_b, g2d)

        x1, h2, idx, wt, rank, counts = _mixer_out(oa, ob, oc, od, w_out[l].astype(BF16), x, mod,
                                                   ln_g[l, 0].reshape(1, -1), ln_b[l, 0].reshape(1, -1),
                                                   w_router[l], b_router[l].reshape(1, -1))

        slot, first_blk, n_blk, counts = _dispatch_plan(idx, rank, counts)
        xg = _dispatch_rows(slot, h2)
        y_blk = _routed_experts(xg, first_blk, n_blk, counts, l, w_e_gate, w_e_up, w_e_down)
        x = _ffn_out(slot, h2, wt, x1, w_s_gate[l].astype(BF16), w_s_up[l].astype(BF16),
                     w_s_down[l].astype(BF16), mod, ln_g[l, 1].reshape(1, -1), ln_b[l, 1].reshape(1, -1), y_blk)

        ctx = slab[:N_CTX]
        ctx_out[0].append(ctx[:, COL_KA * LANES:(COL_KA + 1) * LANES].reshape(BATCH, SEQ, A_KV_HEADS, HEAD_DIM))
        ctx_out[1].append(ctx[:, COL_VA * LANES:(COL_VA + 1) * LANES].reshape(BATCH, SEQ, A_KV_HEADS, HEAD_DIM))
        ctx_out[2].append(ctx[:, COL_KB * 512:(COL_KB + 1) * 512].reshape(BATCH, SEQ, B_HEADS, 2, HEAD_DIM))
        ctx_out[3].append(ctx[:, COL_VB * 512:(COL_VB + 1) * 512].reshape(BATCH, SEQ, B_HEADS, 2 * HEAD_DIM))
        ctx_out[4].append(jnp.concatenate([fin_f[:BATCH], fin_b[:BATCH]], axis=1))
        ctx_out[5].append(ctx[:, COL_KD * 512:(COL_KD + 1) * 512].reshape(BATCH, SEQ, D_HEADS, HEAD_DIM))
        ctx_out[6].append(ctx[:, COL_VD * 512:(COL_VD + 1) * 512].reshape(BATCH, SEQ, D_HEADS, HEAD_DIM))

    new = [jnp.stack(t, axis=1) for t in ctx_out]
    return (x[:N_CTX].reshape(BATCH, SEQ, D_MODEL), x[N_CTX:].reshape(DEC_BATCH, DEC_SEQ, D_MODEL), *new)
```

```python
import functools
import math

import numpy as np
import jax
import jax.numpy as jnp
from jax import lax
from jax.experimental import pallas as pl
from jax.experimental.pallas import tpu as pltpu

F32 = jnp.float32
BF16 = jnp.bfloat16

D_MODEL = 2048
BATCH = 16
SEQ = 256
DEPTH = 2
DEC_BATCH = 2
DEC_SEQ = 4096
PAST_LEN = 256
GRID_W = 64
HEAD_DIM = 64
ROPE_BASE = 10000.0
A_HEADS = 8
A_KV_HEADS = 2
A_WINDOW = 128
B_HEADS = 4
C_WIDTH = 512
C_BLOCKS = 8
C_BW = C_WIDTH // C_BLOCKS
C_POW = 8.0
D_HEADS = 8
NA_ROWS = 8
NA_COLS = 16
N_EXPERTS = 64
TOP_K = 6
D_EXPERT = 512
ROUTED_SCALE = 2.5
ALPHA = (2.0 * DEPTH) ** 0.25
LN_EPS = 1e-5
NORM_EPS = 1e-6
NEG_INF = -1e30
SCALE = HEAD_DIM ** -0.5
LOG2E = math.log2(math.e)

N_CTX = BATCH * SEQ
N_LAT = DEC_BATCH * DEC_SEQ
N_TOK = N_CTX + N_LAT
GROUP_ROWS = 4096
N_GROUPS = N_TOK // GROUP_ROWS
IN_WIDTH = 4864
LANES = 128
VMEM_LIMIT = 56 * 1024 * 1024

COL_QA, COL_QB, COL_KB, COL_VB, COL_XC, COL_GC, COL_QD, COL_KD, COL_VD = range(9)
COL_KA, COL_VA = 36, 37

TM_PROJ = 512
TN_PROJ = IN_WIDTH // 2
TM_PREP = 256
TQ_A = 128
TQ_B = 256
NA_QR = 4
NA_KR = 12
T_SCAN = 256
TM_OUT = 256
MOE_BM = 256
TM_DISPATCH = 128
TM_FFN = 128


def _params(*sem):
    return pltpu.CompilerParams(dimension_semantics=sem, vmem_limit_bytes=VMEM_LIMIT)


def _nt(a, b):
    return lax.dot_general(a, b, (((1,), (1,)), ((), ())), preferred_element_type=F32)


def _mm(a, b):
    return jnp.dot(a, b, preferred_element_type=F32)


def _softmax_unnorm(parts, sink=None):
    m = parts[0].max(axis=-1, keepdims=True)
    for p in parts[1:]:
        m = jnp.maximum(m, p.max(axis=-1, keepdims=True))
    if sink is not None:
        m = jnp.maximum(m, sink)
    es = [jnp.exp(p - m) for p in parts]
    den = es[0].sum(axis=-1, keepdims=True)
    for e in es[1:]:
        den = den + e.sum(axis=-1, keepdims=True)
    if sink is not None:
        den = den + jnp.exp(sink - m)
    return es, den


def _rms_gain(x, g):
    return x * lax.rsqrt(jnp.mean(x * x, axis=-1, keepdims=True) + NORM_EPS) * g


def _layernorm(z, g, b):
    zc = z - jnp.mean(z, axis=-1, keepdims=True)
    var = jnp.mean(zc * zc, axis=-1, keepdims=True)
    return zc * lax.rsqrt(var + LN_EPS) * g + b


def _silu(x):
    return x * jax.nn.sigmoid(x)


def _low_half(shape):
    return lax.broadcasted_iota(jnp.int32, shape, 1) < HEAD_DIM


HALF_D = D_MODEL // 2
HIGH16 = 0xFFFF0000


def _pack_rows(x):
    bits = pltpu.bitcast(x.astype(BF16).astype(F32), jnp.uint32)
    return (bits[:, :HALF_D] >> 16) | (bits[:, HALF_D:] & jnp.uint32(HIGH16))


def _unpack_rows(p):
    return (pltpu.bitcast(p << 16, F32), pltpu.bitcast(p & jnp.uint32(HIGH16), F32))


SUBLANES = 8
assert HALF_D == SUBLANES * LANES


def _store_token_tiles(ref, index, packed, first=0):
    m = packed.shape[0]
    for s in range(SUBLANES):
        rows = pl.ds(first * SUBLANES + s, m, stride=SUBLANES)
        ref[index + (rows, slice(None))] = packed[:, s * LANES:(s + 1) * LANES]


def _load_token_tiles(ref, index, m):
    return jnp.concatenate([ref[index + (pl.ds(s, m, stride=SUBLANES), slice(None))] for s in range(SUBLANES)],
                           axis=1)


def _mod_kernel(c_ref, w_ref, b_ref, o_ref):
    c = c_ref[...]
    o_ref[0] = jnp.dot(_silu(c), w_ref[0], precision=lax.Precision.HIGHEST,
                       preferred_element_type=F32) + b_ref[0]


def _modulation(cond8, w_mod, b_mod):
    tn = 1024
    n = w_mod.shape[-1]
    return pl.pallas_call(
        _mod_kernel,
        grid=(DEPTH, n // tn),
        in_specs=[pl.BlockSpec((8, D_MODEL), lambda l, j: (0, 0)),
                  pl.BlockSpec((1, D_MODEL, tn), lambda l, j: (l, 0, j)),
                  pl.BlockSpec((1, 1, tn), lambda l, j: (l, 0, j))],
        out_specs=pl.BlockSpec((1, 8, tn), lambda l, j: (l, 0, j)),
        out_shape=jax.ShapeDtypeStruct((DEPTH, 8, n), F32),
        compiler_params=_params("parallel", "parallel"),
        name="modulation",
    )(cond8, w_mod, b_mod.reshape(DEPTH, 1, n))


def _mod_spec(which, tm):
    return pl.BlockSpec((None, None, 1, D_MODEL), lambda i, *_: (i * tm // GROUP_ROWS, which, 0, 0))


def _inproj_kernel(x_ref, sc_ref, sh_ref, w_ref, o_ref, xb_ref):
    @pl.when(pl.program_id(1) == 0)
    def _():
        xb_ref[...] = (x_ref[...] * (1.0 + sc_ref[...]) + sh_ref[...]).astype(BF16)

    o_ref[...] = _mm(xb_ref[...], w_ref[...])


def _in_projection(x, mod, w_in_b):
    tm, tn = TM_PROJ, TN_PROJ
    return pl.pallas_call(
        _inproj_kernel,
        grid=(N_TOK // tm, IN_WIDTH // tn),
        in_specs=[pl.BlockSpec((tm, D_MODEL), lambda i, j: (i, 0)),
                  _mod_spec(1, tm), _mod_spec(0, tm),
                  pl.BlockSpec((D_MODEL, tn), lambda i, j: (0, j))],
        out_specs=pl.BlockSpec((tm, tn), lambda i, j: (i, j)),
        out_shape=jax.ShapeDtypeStruct((N_TOK, IN_WIDTH), F32),
        scratch_shapes=[pltpu.VMEM((tm, D_MODEL), BF16)],
        compiler_params=_params("parallel", "arbitrary"),
        name="in_projection",
    )(x, mod, mod, w_in_b)


def _ctx_attn_kernel(scal_ref, qa_ref, ka_ref, va_ref, qb_ref, kb_ref, vb_ref, qd_ref, kd_ref, vd_ref,
                     ga_ref, gb_ref, gd_ref, oa_ref, ob_ref, od_ref, *, lam_init):
    L = SEQ
    lo = _low_half((L, LANES))

    ka = ka_ref[...].astype(BF16)
    va = va_ref[...].astype(BF16)
    qa = qa_ref[...]
    chunks = []
    for c in range(A_HEADS // 2):
        hk = c // 2
        chunk = qa[:, c * LANES:(c + 1) * LANES]
        keep = lo if hk == 0 else jnp.logical_not(lo)
        halves = []
        for half in range(2):
            x = chunk if half == hk else pltpu.roll(chunk, HEAD_DIM, 1)
            qh = jnp.where(keep, x, 0.0).astype(BF16)
            s = _nt(qh, ka) * SCALE
            (e,), den = _softmax_unnorm([s], scal_ref[2 * c + half])
            o = _mm(e.astype(BF16), va) / den
            halves.append(o if half == hk else pltpu.roll(o, HEAD_DIM, 1))
        chunks.append(jnp.where(lo, halves[0], halves[1]))
    oa = jnp.concatenate(chunks, axis=1)
    oa_ref[...] = _rms_gain(oa, ga_ref[...]).astype(BF16)

    lam = scal_ref[A_HEADS]
    qb = qb_ref[...]
    chunks = []
    for h in range(B_HEADS):
        sl = slice(h * LANES, (h + 1) * LANES)
        qc = qb[:, sl]
        kc = kb_ref[:, sl].astype(BF16)
        vh = vb_ref[:, sl].astype(BF16)
        (e1,), d1 = _softmax_unnorm([_nt(jnp.where(lo, qc, 0.0).astype(BF16), kc) * SCALE])
        (e2,), d2 = _softmax_unnorm([_nt(jnp.where(lo, 0.0, qc).astype(BF16), kc) * SCALE])
        w = e1 * (1.0 / d1) - e2 * (lam / d2)
        o = _mm(w.astype(BF16), vh)
        chunks.append(_rms_gain(o, gb_ref[:, sl]) * (1.0 - lam_init))
    ob_ref[...] = jnp.concatenate(chunks, axis=1).astype(BF16)

    qd = qd_ref[...]
    chunks = []
    for c in range(D_HEADS // 2):
        sl = slice(c * LANES, (c + 1) * LANES)
        qc = qd[:, sl]
        kc = kd_ref[:, sl].astype(BF16)
        vc = vd_ref[:, sl].astype(BF16)
        halves = []
        for half in range(2):
            qh = jnp.where(lo if half == 0 else jnp.logical_not(lo), qc, 0.0).astype(BF16)
            (e,), den = _softmax_unnorm([_nt(qh, kc) * SCALE])
            halves.append(_mm(e.astype(BF16), vc) / den)
        chunks.append(jnp.where(lo, halves[0], halves[1]))
    od = jnp.concatenate(chunks, axis=1)
    od_ref[...] = _rms_gain(od, gd_ref[...]).astype(BF16)


def _ctx_attention(slab, scal, g_mix2d, lam_init):
    L = SEQ

    def wide(col):
        return pl.BlockSpec((L, 512), lambda b: (b, col))

    def narrow(col):
        return pl.BlockSpec((L, LANES), lambda b: (b, col))

    def gain(col):
        return pl.BlockSpec((1, 512), lambda b: (0, col))

    out = jax.ShapeDtypeStruct((N_TOK, 512), BF16)
    ospec = pl.BlockSpec((L, 512), lambda b: (b, 0))
    return pl.pallas_call(
        functools.partial(_ctx_attn_kernel, lam_init=lam_init),
        grid=(BATCH,),
        in_specs=[pl.BlockSpec(memory_space=pltpu.SMEM),
                  wide(COL_QA), narrow(COL_KA), narrow(COL_VA),
                  wide(COL_QB), wide(COL_KB), wide(COL_VB),
                  wide(COL_QD), wide(COL_KD), wide(COL_VD),
                  gain(0), gain(1), gain(3)],
        out_specs=[ospec, ospec, ospec],
        out_shape=[out, out, out],
        compiler_params=_params("parallel"),
        name="ctx_attention",
    )(scal, slab, slab, slab, slab, slab, slab, slab, slab, slab, g_mix2d, g_mix2d, g_mix2d)


def _prep_kernel(qa_ref, ka_ref, va_ref, qb_ref, kb_ref, vb_ref, qd_ref, kd_ref, vd_ref,
                 cos_ref, sa_ref, sb_ref,
                 qa_o, ka_o, va_o, qb_o, kb_o, vb_o, qd_o, kd_o, vd_o):
    cos, sa, sb = cos_ref[...], sa_ref[...], sb_ref[...]
    lo = _low_half(cos.shape)
    hi = jnp.logical_not(lo)

    def rope(x):
        return (x * cos + pltpu.roll(x, LANES - HEAD_DIM // 4, 1) * sa
                + pltpu.roll(x, HEAD_DIM // 4, 1) * sb)

    ka_o[...] = rope(ka_ref[...]).astype(BF16)
    va_o[...] = va_ref[...].astype(BF16)
    vb_o[...] = vb_ref[...].astype(BF16)
    kd_o[...] = kd_ref[...].astype(BF16)
    vd_o[...] = vd_ref[...].astype(BF16)

    for c in range(A_HEADS // 2):
        hk = c // 2
        r = rope(qa_ref[:, c * LANES:(c + 1) * LANES])
        keep = lo if hk == 0 else hi
        for half in range(2):
            h = 2 * c + half
            x = r if half == hk else pltpu.roll(r, HEAD_DIM, 1)
            qa_o[:, h * LANES:(h + 1) * LANES] = jnp.where(keep, x, 0.0).astype(BF16)
    for h in range(B_HEADS):
        sl = slice(h * LANES, (h + 1) * LANES)
        r = rope(qb_ref[:, sl])
        kb_o[:, sl] = rope(kb_ref[:, sl]).astype(BF16)
        qb_o[:, (2 * h) * LANES:(2 * h + 1) * LANES] = jnp.where(lo, r, 0.0).astype(BF16)
        qb_o[:, (2 * h + 1) * LANES:(2 * h + 2) * LANES] = jnp.where(hi, r, 0.0).astype(BF16)
    for c in range(D_HEADS // 2):
        x = qd_ref[:, c * LANES:(c + 1) * LANES]
        qd_o[:, (2 * c) * LANES:(2 * c + 1) * LANES] = jnp.where(lo, x, 0.0).astype(BF16)
        qd_o[:, (2 * c + 1) * LANES:(2 * c + 2) * LANES] = jnp.where(hi, x, 0.0).astype(BF16)


def _latent_prep(slab, cos, sa, sb):
    tm = TM_PREP
    off = N_CTX // tm
    per_seq = DEC_SEQ // tm

    def wide(col):
        return pl.BlockSpec((tm, 512), lambda i: (i + off, col))

    def narrow(col):
        return pl.BlockSpec((tm, LANES), lambda i: (i + off, col))

    tab = pl.BlockSpec((tm, LANES), lambda i: (i % per_seq, 0))

    def out(width):
        return (pl.BlockSpec((tm, width), lambda i: (i, 0)), jax.ShapeDtypeStruct((N_LAT, width), BF16))

    outs = [out(1024), out(LANES), out(LANES), out(1024), out(512), out(512), out(1024), out(512), out(512)]
    return pl.pallas_call(
        _prep_kernel,
        grid=(N_LAT // tm,),
        in_specs=[wide(COL_QA), narrow(COL_KA), narrow(COL_VA), wide(COL_QB), wide(COL_KB), wide(COL_VB),
                  wide(COL_QD), wide(COL_KD), wide(COL_VD), tab, tab, tab],
        out_specs=[o[0] for o in outs],
        out_shape=[o[1] for o in outs],
        compiler_params=_params("parallel"),
        name="latent_prep",
    )(slab, slab, slab, slab, slab, slab, slab, slab, slab, cos, sa, sb)


def _win_attn_kernel(scal_ref, q_ref, kp_ref, kc_ref, kn_ref, vp_ref, vc_ref, vn_ref, kctx_ref, vctx_ref,
                     g_ref, _all_rows_ref, o_ref):
    n = pl.program_id(1)
    nb = pl.num_programs(1)
    tq = TQ_A
    group = A_HEADS // A_KV_HEADS
    k = jnp.concatenate([kp_ref[...], kc_ref[...], kn_ref[...], kctx_ref[...].astype(BF16)], axis=0)
    v = jnp.concatenate([vp_ref[...], vc_ref[...], vn_ref[...], vctx_ref[...].astype(BF16)], axis=0)
    nk = 3 * tq + PAST_LEN
    row = lax.broadcasted_iota(jnp.int32, (group * tq, nk), 0) & (tq - 1)
    col = lax.broadcasted_iota(jnp.int32, (group * tq, nk), 1)
    prev_thr = row + jnp.where(n > 0, 0, tq)
    next_thr = row + 2 * tq - jnp.where(n < nb - 1, 0, tq)
    masked = ((col < tq) & (col < prev_thr)) | ((col >= 2 * tq) & (col < 3 * tq) & (col > next_thr))
    valid = jnp.logical_not(masked)
    lo = _low_half((tq, LANES))
    heads = []
    for hk in range(A_KV_HEADS):
        q4 = jnp.concatenate([q_ref[:, (hk * group + g) * LANES:(hk * group + g + 1) * LANES]
                              for g in range(group)], axis=0)
        s = jnp.where(valid, _nt(q4, k) * SCALE, NEG_INF)
        sink = jnp.concatenate([jnp.full((tq, 1), scal_ref[hk * group + g], F32) for g in range(group)], axis=0)
        (e,), den = _softmax_unnorm([s], sink)
        o = _mm(e.astype(BF16), v) / den
        for g in range(group):
            h = hk * group + g
            og = o[g * tq:(g + 1) * tq]
            heads.append(og if (h % 2) == hk else pltpu.roll(og, HEAD_DIM, 1))
    oa = jnp.concatenate([jnp.where(lo, heads[2 * c], heads[2 * c + 1]) for c in range(A_HEADS // 2)], axis=1)
    o_ref[...] = _rms_gain(oa, g_ref[...]).astype(BF16)


def _window_attention(qa_p, ka_r, va_b, cache_k, cache_v, scal, g_mix2d, o_all):
    tq = TQ_A
    nb = DEC_SEQ // tq

    def band(d):
        return pl.BlockSpec((tq, LANES), lambda b, n: (b * nb + jnp.clip(n + d, 0, nb - 1), 0))

    ctx = pl.BlockSpec((None, PAST_LEN, LANES), lambda b, n: (b, 0, 0))
    return pl.pallas_call(
        _win_attn_kernel,
        grid=(DEC_BATCH, nb),
        in_specs=[pl.BlockSpec(memory_space=pltpu.SMEM),
                  pl.BlockSpec((tq, A_HEADS * LANES), lambda b, n: (b * nb + n, 0)),
                  band(-1), band(0), band(1), band(-1), band(0), band(1), ctx, ctx,
                  pl.BlockSpec((1, 512), lambda b, n: (0, 0)),
                  pl.BlockSpec(memory_space=pl.ANY)],
        out_specs=pl.BlockSpec((tq, 512), lambda b, n: (N_CTX // tq + b * nb + n, 0)),
        out_shape=jax.ShapeDtypeStruct((N_TOK, 512), BF16),
        input_output_aliases={11: 0},
        compiler_params=_params("parallel", "parallel"),
        name="window_attention",
    )(scal, qa_p, ka_r, ka_r, ka_r, va_b, va_b, va_b, cache_k, cache_v, g_mix2d, o_all)


def _diff_attn_kernel(scal_ref, q_ref, k_ref, v_ref, kctx_ref, vctx_ref, g_ref, _all_rows_ref, o_ref, *,
                      lam_init):
    lam = scal_ref[A_HEADS]
    chunks = []
    for h in range(B_HEADS):
        sl = slice(h * LANES, (h + 1) * LANES)
        kh = k_ref[:, sl]
        kc = kctx_ref[:, sl].astype(BF16)
        es, dens = [], []
        for m in range(2):
            q = q_ref[:, (2 * h + m) * LANES:(2 * h + m + 1) * LANES]
            t_lat = _nt(q, kh) * (SCALE * LOG2E)
            t_ctx = _nt(q, kc) * (SCALE * LOG2E)
            mx = jnp.maximum(t_lat.max(axis=-1, keepdims=True), t_ctx.max(axis=-1, keepdims=True))
            e_lat = jnp.exp2(t_lat - mx)
            e_ctx = jnp.exp2(t_ctx - mx)
            es.append((e_lat, e_ctx))
            dens.append(e_lat.sum(axis=-1, keepdims=True) + e_ctx.sum(axis=-1, keepdims=True))
        c1 = 1.0 / dens[0]
        c2 = lam / dens[1]
        w_lat = (es[0][0] * c1 - es[1][0] * c2).astype(BF16)
        w_ctx = (es[0][1] * c1 - es[1][1] * c2).astype(BF16)
        o = _mm(w_lat, v_ref[:, sl]) + _mm(w_ctx, vctx_ref[:, sl].astype(BF16))
        chunks.append(_rms_gain(o, g_ref[:, sl]) * (1.0 - lam_init))
    o_ref[...] = jnp.concatenate(chunks, axis=1).astype(BF16)


def _diff_attention(qb_p, kb_r, vb_b, cache_k, cache_v, scal, g_mix2d, lam_init, o_all):
    tq = TQ_B
    nb = DEC_SEQ // tq
    full = pl.BlockSpec((DEC_SEQ, 512), lambda b, n: (b, 0))
    ctx = pl.BlockSpec((None, PAST_LEN, 512), lambda b, n: (b, 0, 0))
    return pl.pallas_call(
        functools.partial(_diff_attn_kernel, lam_init=lam_init),
        grid=(DEC_BATCH, nb),
        in_specs=[pl.BlockSpec(memory_space=pltpu.SMEM),
                  pl.BlockSpec((tq, 2 * B_HEADS * LANES), lambda b, n: (b * nb + n, 0)),
                  full, full, ctx, ctx,
                  pl.BlockSpec((1, 512), lambda b, n: (0, 1)),
                  pl.BlockSpec(memory_space=pl.ANY)],
        out_specs=pl.BlockSpec((tq, 512), lambda b, n: (N_CTX // tq + b * nb + n, 0)),
        out_shape=jax.ShapeDtypeStruct((N_TOK, 512), BF16),
        input_output_aliases={7: 0},
        compiler_params=_params("parallel", "parallel"),
        name="diff_attention",
    )(scal, qb_p, kb_r, vb_b, cache_k, cache_v, g_mix2d, o_all)


def _na_window_start(step):
    return np.clip(step * NA_QR - NA_ROWS // 2, 0, GRID_W - NA_KR)


def _na_bias_indices():
    rows = DEC_SEQ // GRID_W
    steps = rows // NA_QR
    pats = []
    for step in range(steps):
        w0 = _na_window_start(step)
        r = step * NA_QR + np.arange(NA_QR)[:, None, None, None]
        c = np.arange(GRID_W)[None, :, None, None]
        kr = w0 + np.arange(NA_KR)[None, None, :, None]
        kc = np.arange(GRID_W)[None, None, None, :]
        kr0 = np.clip(r - NA_ROWS // 2, 0, rows - NA_ROWS)
        kc0 = np.clip(c - NA_COLS // 2, 0, GRID_W - NA_COLS)
        valid = (kr >= kr0) & (kr < kr0 + NA_ROWS) & (kc >= kc0) & (kc < kc0 + NA_COLS)
        drow = np.clip(kr - r + NA_ROWS - 1, 0, 2 * NA_ROWS - 2)
        dcol = np.clip(kc - c + NA_COLS - 1, 0, 2 * NA_COLS - 2)
        shape = (NA_QR * GRID_W, NA_KR * GRID_W)
        full = np.broadcast_to
        pats.append((full(drow, valid.shape).reshape(shape), full(dcol, valid.shape).reshape(shape),
                     valid.reshape(shape)))
    for step in range(2, steps - 1):
        for a, b in zip(pats[1], pats[step]):
            assert np.array_equal(a, b)
    kinds = [pats[0], pats[1], pats[-1]]
    drow, dcol, valid = (np.stack([k[i] for k in kinds]) for i in range(3))
    shape6 = (len(kinds), NA_QR, GRID_W, NA_KR, GRID_W)
    row_sel = np.eye(2 * NA_ROWS - 1, dtype=np.float32)[drow.reshape(shape6)[:, :, 0, :, 0]]
    col_sel = np.eye(2 * NA_COLS - 1, dtype=np.float32)[dcol.reshape(shape6)[0, 0, :, 0, :]]
    return row_sel, col_sel, valid


def _na_bias(rpb, row_sel, col_sel, valid):
    hp = lax.Precision.HIGHEST
    cols = jnp.einsum('hab,cdb->hacd', rpb, col_sel, precision=hp)
    bias = jnp.einsum('kria,hacd->khrcid', row_sel, cols, precision=hp)
    bias = bias.reshape(valid.shape[0], D_HEADS, NA_QR * GRID_W, NA_KR * GRID_W)
    return jnp.where(valid[:, None], bias, NEG_INF)


def _na_kernel(q_ref, k_ref, v_ref, kctx_ref, vctx_ref, bias_ref, g_ref, _all_rows_ref, o_ref):
    step = pl.program_id(1)
    nq = NA_QR * GRID_W
    nk = NA_KR * GRID_W
    w0 = jnp.clip(step * NA_QR - NA_ROWS // 2, 0, GRID_W - NA_KR)
    start = pl.multiple_of(w0 * GRID_W, GRID_W)
    lo = _low_half((nq, LANES))
    chunks = []
    for c in range(D_HEADS // 2):
        sl = slice(c * LANES, (c + 1) * LANES)
        kw = k_ref[pl.ds(start, nk), sl]
        vw = v_ref[pl.ds(start, nk), sl]
        kc = kctx_ref[:, sl].astype(BF16)
        vc = vctx_ref[:, sl].astype(BF16)
        halves = []
        for half in range(2):
            h = 2 * c + half
            q = q_ref[:, h * LANES:(h + 1) * LANES]
            s_loc = _nt(q, kw) * SCALE + bias_ref[h]
            s_ctx = _nt(q, kc) * SCALE
            (e_loc, e_ctx), den = _softmax_unnorm([s_loc, s_ctx])
            halves.append((_mm(e_loc.astype(BF16), vw) + _mm(e_ctx.astype(BF16), vc)) / den)
        chunks.append(jnp.where(lo, halves[0], halves[1]))
    od = jnp.concatenate(chunks, axis=1)
    o_ref[...] = _rms_gain(od, g_ref[...]).astype(BF16)


def _neighbourhood_attention(qd_p, kd_b, vd_b, cache_k, cache_v, bias, g_mix2d, o_all):
    nq = NA_QR * GRID_W
    steps = DEC_SEQ // nq
    full = pl.BlockSpec((DEC_SEQ, 512), lambda b, j: (b, 0))
    ctx = pl.BlockSpec((None, PAST_LEN, 512), lambda b, j: (b, 0, 0))

    def kind(b, j):
        return (jnp.where(j == 0, 0, jnp.where(j == steps - 1, 2, 1)), 0, 0, 0)

    return pl.pallas_call(
        _na_kernel,
        grid=(DEC_BATCH, steps),
        in_specs=[pl.BlockSpec((nq, D_HEADS * LANES), lambda b, j: (b * steps + j, 0)),
                  full, full, ctx, ctx,
                  pl.BlockSpec((None, D_HEADS, nq, NA_KR * GRID_W), kind),
                  pl.BlockSpec((1, 512), lambda b, j: (0, 3)),
                  pl.BlockSpec(memory_space=pl.ANY)],
        out_specs=pl.BlockSpec((nq, 512), lambda b, j: (N_CTX // nq + b * steps + j, 0)),
        out_shape=jax.ShapeDtypeStruct((N_TOK, 512), BF16),
        input_output_aliases={7: 0},
        compiler_params=_params("parallel", "arbitrary"),
        name="neighbourhood_attention",
    )(qd_p, kd_b, vd_b, cache_k, cache_v, bias, g_mix2d, o_all)


N_SCAN_TILES = N_TOK // T_SCAN
CTX_TILES = N_CTX // T_SCAN
TILES_PER_LAT = DEC_SEQ // T_SCAN


def _scan_tile_flags(i):
    is_ctx = i < CTX_TILES
    pos = (i - CTX_TILES) % TILES_PER_LAT
    return is_ctx | (pos == 0), is_ctx | (pos == TILES_PER_LAT - 1)


def _scan_seq(i):
    return jnp.where(i < CTX_TILES, i, CTX_TILES + (i - CTX_TILES) // TILES_PER_LAT)


def _conv_gates(x_ref, prev_ref, next_ref, cw_ref, cb_ref, wgh_ref, wgl_ref, bg_ref, lam_ref, first, last):
    x = x_ref[...]
    t = T_SCAN
    row = lax.broadcasted_iota(jnp.int32, x.shape, 0)
    pm1 = jnp.where(first, 0.0, prev_ref[7:8, :])
    n0 = jnp.where(last, 0.0, next_ref[0:1, :])
    n1 = jnp.where(last, 0.0, next_ref[1:2, :])
    x_m1 = jnp.where(row == 0, pm1, pltpu.roll(x, 1, 0))
    x_p1 = jnp.where(row == t - 1, n0, pltpu.roll(x, t - 1, 0))
    x_p2 = jnp.where(row == t - 2, n0, jnp.where(row == t - 1, n1, pltpu.roll(x, t - 2, 0)))
    y = (cw_ref[0:1, :] * x_m1 + cw_ref[1:2, :] * x + cw_ref[2:3, :] * x_p1 + cw_ref[3:4, :] * x_p2
         + cb_ref[...])
    y_hi = y.astype(BF16)
    y_lo = (y - y_hi.astype(F32)).astype(BF16)
    zr, zi = [], []
    for c in range(C_WIDTH // LANES):
        sl = slice(c * LANES, (c + 1) * LANES)
        w_hi = wgh_ref[0, sl, :]
        z = _mm(y_hi[:, sl], w_hi) + _mm(y_lo[:, sl], w_hi) + _mm(y_hi[:, sl], wgl_ref[0, sl, :])
        zr.append(z[:, :LANES])
        zi.append(z[:, LANES:])
    bias = bg_ref[0]
    r = jax.nn.sigmoid(jnp.concatenate(zr, axis=1) + bias[:, :C_WIDTH])
    gate_i = jax.nn.sigmoid(jnp.concatenate(zi, axis=1) + bias[:, C_WIDTH:])
    nl = -lam_ref[0]
    softplus = jnp.maximum(nl, 0.0) + jnp.log1p(jnp.exp(-jnp.abs(nl)))
    log_a = -C_POW * r * softplus
    a = jnp.exp(log_a)
    u = jnp.sqrt(1.0 - jnp.exp(2.0 * log_a)) * gate_i * y
    return a, u


def _scan_rows(a_s, u_s, h_s, h, reverse):
    groups = T_SCAN // 8

    def body(i, h):
        g = (groups - 1 - i) if reverse else i
        base = pl.multiple_of(g * 8, 8)
        a8 = a_s[pl.ds(base, 8), :]
        u8 = u_s[pl.ds(base, 8), :]
        rows = [None] * 8
        for k in (range(7, -1, -1) if reverse else range(8)):
            h = a8[k:k + 1, :] * h + u8[k:k + 1, :]
            rows[k] = h
        h_s[pl.ds(base, 8), :] = jnp.concatenate(rows, axis=0)
        return h

    return lax.fori_loop(0, groups, body, h)


def _scan_fwd_kernel(x_ref, prev_ref, next_ref, cw_ref, cb_ref, wgh_ref, wgl_ref, bg_ref, lam_ref, h0_ref,
                     hf_ref, fin_ref, a_s, u_s, carry_s):
    i = pl.program_id(0)
    first, last = _scan_tile_flags(i)
    a, u = _conv_gates(x_ref, prev_ref, next_ref, cw_ref, cb_ref, wgh_ref, wgl_ref, bg_ref, lam_ref, first, last)
    a_s[...] = a
    u_s[...] = u

    @pl.when(first)
    def _():
        carry_s[...] = h0_ref[...]

    h = _scan_rows(a_s, u_s, hf_ref, carry_s[...], reverse=False)
    carry_s[...] = h
    fin_ref[...] = h


def _scan_bwd_kernel(x_ref, prev_ref, next_ref, cw_ref, cb_ref, wgh_ref, wgl_ref, bg_ref, lam_ref, h0_ref,
                     hf_ref, gc_ref, g_ref, oc_ref, fin_ref, a_s, u_s, hb_s, carry_s):
    i = N_SCAN_TILES - 1 - pl.program_id(0)
    first, last = _scan_tile_flags(i)
    a, u = _conv_gates(x_ref, prev_ref, next_ref, cw_ref, cb_ref, wgh_ref, wgl_ref, bg_ref, lam_ref, first, last)
    a_s[...] = a
    u_s[...] = u

    @pl.when(last)
    def _():
        carry_s[...] = h0_ref[...]

    h = _scan_rows(a_s, u_s, hb_s, carry_s[...], reverse=True)
    carry_s[...] = h
    fin_ref[...] = h
    g = gc_ref[...]
    gelu = 0.5 * g * (1.0 + jnp.tanh(math.sqrt(2.0 / math.pi) * (g + 0.044715 * (g * g * g))))
    oc_ref[...] = _rms_gain(gelu * (hf_ref[...] + hb_s[...]), g_ref[...]).astype(BF16)


def _scan_common_specs(tile_of, direction):
    halo = T_SCAN // 8
    last_block = N_TOK // 8 - 1
    return [pl.BlockSpec((T_SCAN, 512), lambda j: (tile_of(j), COL_XC)),
            pl.BlockSpec((8, 512), lambda j: (jnp.maximum(tile_of(j) * halo - 1, 0), COL_XC)),
            pl.BlockSpec((8, 512), lambda j: (jnp.minimum((tile_of(j) + 1) * halo, last_block), COL_XC)),
            pl.BlockSpec((4, C_WIDTH), lambda j: (0, 0)),
            pl.BlockSpec((1, C_WIDTH), lambda j: (0, 0)),
            pl.BlockSpec((1, C_WIDTH, 2 * LANES), lambda j: (direction, 0, 0)),
            pl.BlockSpec((1, C_WIDTH, 2 * LANES), lambda j: (direction, 0, 0)),
            pl.BlockSpec((1, 1, 2 * C_WIDTH), lambda j: (direction, 0, 0)),
            pl.BlockSpec((1, 1, C_WIDTH), lambda j: (direction, 0, 0)),
            pl.BlockSpec((None, 1, C_WIDTH), lambda j: (_scan_seq(tile_of(j)), 0, 0))]


def _rglru(slab, conv_w, conv_b, w_gates, b_gates, lam, h0_f, h0_b, g_mix2d):
    wg_hi = w_gates.astype(BF16)
    wg_lo = (w_gates - wg_hi.astype(F32)).astype(BF16)
    fin = jax.ShapeDtypeStruct((N_SCAN_TILES, 1, C_WIDTH), F32)
    fwd_tile = lambda j: j
    h_f, fin_f = pl.pallas_call(
        _scan_fwd_kernel,
        grid=(N_SCAN_TILES,),
        in_specs=_scan_common_specs(fwd_tile, 0),
        out_specs=[pl.BlockSpec((T_SCAN, C_WIDTH), lambda j: (j, 0)),
                   pl.BlockSpec((None, 1, C_WIDTH), lambda j: (j, 0, 0))],
        out_shape=[jax.ShapeDtypeStruct((N_TOK, C_WIDTH), F32), fin],
        scratch_shapes=[pltpu.VMEM((T_SCAN, C_WIDTH), F32), pltpu.VMEM((T_SCAN, C_WIDTH), F32),
                        pltpu.VMEM((1, C_WIDTH), F32)],
        compiler_params=_params("arbitrary"),
        name="rglru_forward",
    )(slab, slab, slab, conv_w, conv_b, wg_hi, wg_lo, b_gates, lam, h0_f)
    bwd_tile = lambda j: N_SCAN_TILES - 1 - j
    oc, fin_b = pl.pallas_call(
        _scan_bwd_kernel,
        grid=(N_SCAN_TILES,),
        in_specs=_scan_common_specs(bwd_tile, 1) + [
            pl.BlockSpec((T_SCAN, C_WIDTH), lambda j: (bwd_tile(j), 0)),
            pl.BlockSpec((T_SCAN, 512), lambda j: (bwd_tile(j), COL_GC)),
            pl.BlockSpec((1, 512), lambda j: (0, 2))],
        out_specs=[pl.BlockSpec((T_SCAN, C_WIDTH), lambda j: (bwd_tile(j), 0)),
                   pl.BlockSpec((None, 1, C_WIDTH), lambda j: (bwd_tile(j), 0, 0))],
        out_shape=[jax.ShapeDtypeStruct((N_TOK, C_WIDTH), BF16), fin],
        scratch_shapes=[pltpu.VMEM((T_SCAN, C_WIDTH), F32), pltpu.VMEM((T_SCAN, C_WIDTH), F32),
                        pltpu.VMEM((T_SCAN, C_WIDTH), F32), pltpu.VMEM((1, C_WIDTH), F32)],
        compiler_params=_params("arbitrary"),
        name="rglru_backward",
    )(slab, slab, slab, conv_w, conv_b, wg_hi, wg_lo, b_gates, lam, h0_b, h_f, slab, g_mix2d)
    return oc, fin_f, fin_b


def _mixout_kernel(oa_ref, ob_ref, oc_ref, od_ref, w_ref, x_ref, g1_ref, sc2_ref, sh2_ref, lng_ref, lnb_ref,
                   wrh_ref, wrl_ref, br_ref, x1_ref, h2_ref, idx_ref, wt_ref, rank_ref, cnt_ref, cnt_s):
    @pl.when(pl.program_id(0) == 0)
    def _():
        cnt_s[...] = jnp.zeros(cnt_s.shape, F32)

    parts = 2
    pm = x_ref.shape[0] // parts
    for part in range(parts):
        _mixout_rows(pl.ds(part * pm, pm), part * pm, pm, oa_ref, ob_ref, oc_ref, od_ref, w_ref, x_ref, g1_ref,
                     sc2_ref, sh2_ref, lng_ref, lnb_ref, wrh_ref, wrl_ref, br_ref, x1_ref, h2_ref, idx_ref,
                     wt_ref, rank_ref, cnt_s)
    cnt_ref[...] = cnt_s[...]


def _mixout_rows(rows, first, tm, oa_ref, ob_ref, oc_ref, od_ref, w_ref, x_ref, g1_ref, sc2_ref, sh2_ref,
                 lng_ref, lnb_ref, wrh_ref, wrl_ref, br_ref, x1_ref, h2_ref, idx_ref, wt_ref, rank_ref, cnt_s):
    y = _mm(oa_ref[rows, :], w_ref[0:512, :])
    y += _mm(ob_ref[rows, :], w_ref[512:1024, :])
    y += _mm(oc_ref[rows, :], w_ref[1024:1536, :])
    y += _mm(od_ref[rows, :], w_ref[1536:2048, :])
    x1 = _layernorm(ALPHA * x_ref[rows, :] + g1_ref[...] * y, lng_ref[...], lnb_ref[...])
    x1_ref[rows, :] = x1
    h2 = x1 * (1.0 + sc2_ref[...]) + sh2_ref[...]
    _store_token_tiles(h2_ref, (), _pack_rows(h2), first)

    h_hi = h2.astype(BF16)
    h_lo = (h2 - h_hi.astype(F32)).astype(BF16)
    scores = jax.nn.sigmoid(_mm(h_hi, wrh_ref[...]) + _mm(h_lo, wrh_ref[...]) + _mm(h_hi, wrl_ref[...]))
    sel = scores + br_ref[...]
    lane = lax.broadcasted_iota(jnp.int32, sel.shape, 1).astype(F32)
    slot = lax.broadcasted_iota(jnp.int32, (tm, 8), 1)
    wide = lax.broadcasted_iota(jnp.int32, (tm, LANES), 1)
    idx_out = jnp.zeros((tm, LANES), F32)
    wt_out = jnp.zeros((tm, 8), F32)
    total = jnp.zeros((sel.shape[0], 1), F32)
    hits = []
    for k in range(TOP_K):
        m = sel.max(axis=-1, keepdims=True)
        idx = jnp.where(sel == m, lane, float(N_EXPERTS)).min(axis=-1, keepdims=True)
        hit = lane == idx
        hits.append(hit)
        w = jnp.where(hit, scores, 0.0).sum(axis=-1, keepdims=True)
        total = total + w
        idx_out = jnp.where(wide == k, idx, idx_out)
        wt_out = jnp.where(slot == k, w, wt_out)
        sel = jnp.where(hit, -jnp.inf, sel)
    cols = pl.ds(first, tm)
    idx_ref[:, cols] = idx_out.T[:8, :].astype(jnp.int32)
    wt_ref[rows, :] = wt_out / total * ROUTED_SCALE

    chosen = jnp.zeros(sel.shape, F32)
    for hit in hits:
        chosen = chosen + hit.astype(F32)
    earlier = (lax.broadcasted_iota(jnp.int32, (tm, tm), 1) < lax.broadcasted_iota(jnp.int32, (tm, tm), 0))
    before = _mm(earlier.astype(F32).astype(BF16), chosen.astype(BF16)) + cnt_s[...]
    rank_out = jnp.zeros((tm, LANES), F32)
    for k, hit in enumerate(hits):
        rank_out = jnp.where(wide == k, jnp.where(hit, before, 0.0).sum(axis=-1, keepdims=True), rank_out)
    rank_ref[:, cols] = rank_out.T[:8, :].astype(jnp.int32)
    cnt_s[...] = cnt_s[...] + chosen.sum(axis=0, keepdims=True)


def _mixer_out(oa, ob, oc, od, w_out_b, x, mod, ln_g, ln_b, w_router, b_router):
    tm = TM_OUT
    cat = pl.BlockSpec((tm, 512), lambda i: (i, 0))
    row = pl.BlockSpec((tm, D_MODEL), lambda i: (i, 0))
    vec = pl.BlockSpec((1, D_MODEL), lambda i: (0, 0))
    k8 = pl.BlockSpec((tm, 8), lambda i: (i, 0))
    k_major = pl.BlockSpec((8, tm), lambda i: (0, i))
    per_expert = pl.BlockSpec((1, N_EXPERTS), lambda i: (0, 0))
    router = pl.BlockSpec((D_MODEL, N_EXPERTS), lambda i: (0, 0))
    wr_hi = w_router.astype(BF16)
    wr_lo = (w_router - wr_hi.astype(F32)).astype(BF16)
    return pl.pallas_call(
        _mixout_kernel,
        grid=(N_TOK // tm,),
        in_specs=[cat, cat, cat, cat,
                  pl.BlockSpec((D_MODEL, D_MODEL), lambda i: (0, 0)),
                  row, _mod_spec(2, tm), _mod_spec(4, tm), _mod_spec(3, tm), vec, vec,
                  router, router, per_expert],
        out_specs=[row, pl.BlockSpec((tm * SUBLANES, LANES), lambda i: (i, 0)), k_major, k8, k_major, per_expert],
        out_shape=[jax.ShapeDtypeStruct((N_TOK, D_MODEL), F32),
                   jax.ShapeDtypeStruct((N_TOK * SUBLANES, LANES), jnp.uint32),
                   jax.ShapeDtypeStruct((8, N_TOK), jnp.int32), jax.ShapeDtypeStruct((N_TOK, 8), F32),
                   jax.ShapeDtypeStruct((8, N_TOK), jnp.int32), jax.ShapeDtypeStruct((1, N_EXPERTS), F32)],
        scratch_shapes=[pltpu.VMEM((1, N_EXPERTS), F32)],
        compiler_params=_params("arbitrary"),
        name="mixer_out_router",
    )(oa, ob, oc, od, w_out_b, x, mod, mod, mod, ln_g, ln_b, wr_hi, wr_lo, b_router)


MOE_BLOCKS = N_TOK * TOP_K // MOE_BM + N_EXPERTS
MOE_ROWS = MOE_BLOCKS * MOE_BM


def _row_copy(src, src_row, dst, dst_row, sem):
    def tile(row):
        return pl.ds(row * SUBLANES if isinstance(row, int) else pl.multiple_of(row * SUBLANES, SUBLANES), SUBLANES)

    return pltpu.make_async_copy(src.at[tile(src_row)], dst.at[tile(dst_row)], sem)


def _dispatch_kernel(slot_ref, h_ref, xg_ref, sem):
    tm = h_ref.shape[0] // SUBLANES
    base = pl.program_id(0) * tm

    for r in range(tm):
        for k in range(TOP_K):
            _row_copy(h_ref, r, xg_ref, slot_ref[base + (k * N_TOK + r)], sem).start()
    for k in range(TOP_K):
        pltpu.make_async_copy(h_ref, xg_ref.at[pl.ds(0, tm * SUBLANES)], sem).wait()


def _dispatch_rows(slot_flat, h2):
    tm = TM_DISPATCH
    grid_spec = pltpu.PrefetchScalarGridSpec(
        num_scalar_prefetch=1,
        grid=(N_TOK // tm,),
        in_specs=[pl.BlockSpec((tm * SUBLANES, LANES), lambda i, s: (i, 0))],
        out_specs=pl.BlockSpec(memory_space=pl.ANY),
        scratch_shapes=[pltpu.SemaphoreType.DMA])
    return pl.pallas_call(
        _dispatch_kernel,
        grid_spec=grid_spec,
        out_shape=jax.ShapeDtypeStruct((MOE_ROWS * SUBLANES, LANES), jnp.uint32),
        compiler_params=_params("arbitrary"),
        name="moe_dispatch",
    )(slot_flat, h2)


BLOCK_TILE_ROWS = MOE_BM * SUBLANES


def _expert_kernel(blk0_ref, nblk_ref, cnt_ref, wg_ref, wu_ref, wd_ref, x_hbm, y_hbm,
                   wg_s, wu_s, wd_s, xbuf, ybuf, xsem, ysem):
    e = pl.program_id(0)
    first_blk = blk0_ref[e]
    n_blk = nblk_ref[e]
    count = cnt_ref[e]
    total = blk0_ref[N_EXPERTS - 1] + nblk_ref[N_EXPERTS - 1]

    def rows_of(g):
        return pl.ds(pl.multiple_of(g * BLOCK_TILE_ROWS, BLOCK_TILE_ROWS), BLOCK_TILE_ROWS)

    def x_copy(g, buf):
        return pltpu.make_async_copy(x_hbm.at[rows_of(g)], xbuf.at[buf], xsem.at[buf])

    def y_copy(g, buf):
        return pltpu.make_async_copy(ybuf.at[buf], y_hbm.at[rows_of(g)], ysem.at[buf])

    @pl.when((e == 0) & (total > 0))
    def _():
        x_copy(0, 0).start()

    wg_s[...] = wg_ref[...].astype(BF16)
    wu_s[...] = wu_ref[...].astype(BF16)
    wd_s[...] = wd_ref[...].astype(BF16)

    def block(j, carry):
        g = first_blk + j
        buf = g % 2

        @pl.when(g + 1 < total)
        def _():
            x_copy(g + 1, 1 - buf).start()

        x_copy(g, buf).wait()

        @pl.when(g >= 2)
        def _():
            y_copy(g - 2, buf).wait()

        x = _load_token_tiles(xbuf, (buf,), MOE_BM)
        row = lax.broadcasted_iota(jnp.int32, x.shape, 0)
        lo, hi = _unpack_rows(jnp.where(row < count - j * MOE_BM, x, jnp.uint32(0)))
        lo, hi = lo.astype(BF16), hi.astype(BF16)
        gate = _mm(lo, wg_s[:HALF_D, :]) + _mm(hi, wg_s[HALF_D:, :])
        up = _mm(lo, wu_s[:HALF_D, :]) + _mm(hi, wu_s[HALF_D:, :])
        _store_token_tiles(ybuf, (buf,), _pack_rows(_mm((_silu(gate) * up).astype(BF16), wd_s[...])))
        y_copy(g, buf).start()
        return carry

    lax.fori_loop(0, n_blk, block, 0)

    @pl.when((e == N_EXPERTS - 1) & (total >= 2))
    def _():
        y_copy(total - 2, total % 2).wait()

    @pl.when((e == N_EXPERTS - 1) & (total >= 1))
    def _():
        y_copy(total - 1, (total - 1) % 2).wait()


def _routed_experts(xg, first_blk, n_blk, counts, layer, w_g, w_u, w_d):
    def weight(shape):
        return pl.BlockSpec((None, None) + shape, lambda e, *_: (layer, e, 0, 0))

    any_space = pl.BlockSpec(memory_space=pl.ANY)
    block_buf = pltpu.VMEM((2, BLOCK_TILE_ROWS, LANES), jnp.uint32)
    grid_spec = pltpu.PrefetchScalarGridSpec(
        num_scalar_prefetch=3,
        grid=(N_EXPERTS,),
        in_specs=[weight((D_MODEL, D_EXPERT)), weight((D_MODEL, D_EXPERT)), weight((D_EXPERT, D_MODEL)), any_space],
        out_specs=any_space,
        scratch_shapes=[pltpu.VMEM((D_MODEL, D_EXPERT), BF16), pltpu.VMEM((D_MODEL, D_EXPERT), BF16),
                        pltpu.VMEM((D_EXPERT, D_MODEL), BF16), block_buf, block_buf,
                        pltpu.SemaphoreType.DMA((2,)), pltpu.SemaphoreType.DMA((2,))])
    return pl.pallas_call(
        _expert_kernel,
        grid_spec=grid_spec,
        out_shape=jax.ShapeDtypeStruct((MOE_ROWS * SUBLANES, LANES), jnp.uint32),
        compiler_params=_params("arbitrary"),
        name="routed_experts",
    )(first_blk, n_blk, counts, w_g, w_u, w_d, xg)


def _ffn_out_kernel(slot_ref, h_ref, wt_ref, x1_ref, wg_ref, wu_ref, wd_ref, g2_ref, lng_ref, lnb_ref, y_ref,
                    o_ref, ybuf, sems):
    tm = h_ref.shape[0] // SUBLANES
    i = pl.program_id(0)
    n = pl.num_programs(0)

    def gather(tile, buf):
        base = tile * tm

        for r in range(tm):
            for k in range(TOP_K):
                _row_copy(y_ref, slot_ref[base + (k * N_TOK + r)], ybuf.at[buf, k], r, sems.at[buf]).start()

    @pl.when(i == 0)
    def _():
        gather(0, 0)

    @pl.when(i + 1 < n)
    def _():
        gather(i + 1, (i + 1) % 2)

    lo, hi = _unpack_rows(_load_token_tiles(h_ref, (), tm))
    lo, hi = lo.astype(BF16), hi.astype(BF16)
    gate = _mm(lo, wg_ref[:HALF_D, :]) + _mm(hi, wg_ref[HALF_D:, :])
    up = _mm(lo, wu_ref[:HALF_D, :]) + _mm(hi, wu_ref[HALF_D:, :])
    y = _mm((_silu(gate) * up).astype(BF16), wd_ref[...])
    buf = i % 2
    for k in range(TOP_K):
        pltpu.make_async_copy(y_ref.at[pl.ds(0, tm * SUBLANES)], ybuf.at[buf, k], sems.at[buf]).wait()
    wt = wt_ref[...]
    y_lo, y_hi = y[:, :HALF_D], y[:, HALF_D:]
    for k in range(TOP_K):
        r_lo, r_hi = _unpack_rows(_load_token_tiles(ybuf, (buf, k), tm))
        y_lo = y_lo + r_lo * wt[:, k:k + 1]
        y_hi = y_hi + r_hi * wt[:, k:k + 1]
    y = jnp.concatenate([y_lo, y_hi], axis=1)
    o_ref[...] = _layernorm(ALPHA * x1_ref[...] + g2_ref[...] * y, lng_ref[...], lnb_ref[...])


def _ffn_out(slot_flat, h2, wt, x1, ws_g, ws_u, ws_d, mod, ln_g, ln_b, y_blk):
    tm = TM_FFN
    row = pl.BlockSpec((tm, D_MODEL), lambda i, s: (i, 0))
    vec = pl.BlockSpec((1, D_MODEL), lambda i, s: (0, 0))
    grid_spec = pltpu.PrefetchScalarGridSpec(
        num_scalar_prefetch=1,
        grid=(N_TOK // tm,),
        in_specs=[pl.BlockSpec((tm * SUBLANES, LANES), lambda i, s: (i, 0)),
                  pl.BlockSpec((tm, 8), lambda i, s: (i, 0)), row,
                  pl.BlockSpec((D_MODEL, D_EXPERT), lambda i, s: (0, 0)),
                  pl.BlockSpec((D_MODEL, D_EXPERT), lambda i, s: (0, 0)),
                  pl.BlockSpec((D_EXPERT, D_MODEL), lambda i, s: (0, 0)),
                  _mod_spec(5, tm), vec, vec,
                  pl.BlockSpec(memory_space=pl.ANY)],
        out_specs=row,
        scratch_shapes=[pltpu.VMEM((2, TOP_K, tm * SUBLANES, LANES), jnp.uint32),
                        pltpu.SemaphoreType.DMA((2,))])
    return pl.pallas_call(
        _ffn_out_kernel,
        grid_spec=grid_spec,
        out_shape=jax.ShapeDtypeStruct((N_TOK, D_MODEL), F32),
        compiler_params=_params("arbitrary"),
        name="shared_expert_out",
    )(slot_flat, h2, wt, x1, ws_g, ws_u, ws_d, mod, ln_g, ln_b, y_blk)


def _dispatch_plan(idx, rank, counts):
    bm = MOE_BM
    counts = counts.reshape(N_EXPERTS).astype(jnp.int32)
    padded = (counts + bm - 1) // bm * bm
    pad_end = jnp.cumsum(padded)
    pad_start = pad_end - padded
    experts = jnp.arange(N_EXPERTS, dtype=jnp.int32)[:, None, None]
    first_row = jnp.sum(jnp.where(idx[None, :TOP_K] == experts, pad_start[:, None, None], 0), axis=0)
    slot = first_row + rank[:TOP_K]
    return slot.reshape(-1).astype(jnp.int32), pad_start // bm, padded // bm, counts


def _rope_tables():
    t = jnp.arange(DEC_SEQ)
    row = (t // GRID_W).astype(F32)
    col = (t % GRID_W).astype(F32)
    nf = HEAD_DIM // 4
    inv = ROPE_BASE ** (-jnp.arange(nf, dtype=F32) / nf)
    ar = row[:, None] * inv[None, :]
    ac = col[:, None] * inv[None, :]
    ang = jnp.concatenate([ar, ar, ac, ac], axis=-1)
    ang = jnp.concatenate([ang, ang], axis=-1)
    cos, sin = jnp.cos(ang), jnp.sin(ang)
    first = (jnp.arange(LANES) % (2 * nf)) < nf
    return cos, jnp.where(first, -sin, 0.0), jnp.where(first, 0.0, sin)


def _lambda_init(l):
    return 0.8 - 0.6 * math.exp(-0.3 * l)


def _gate_pairs(w_r, w_i):
    per = LANES // C_BW
    eye = jnp.eye(per, dtype=w_r.dtype)

    def chunk_diag(w):
        w = w.reshape(C_BLOCKS // per, per, C_BW, C_BW)
        return (eye[None, :, None, :, None] * w[:, :, :, None, :]).reshape(C_BLOCKS // per, LANES, LANES)

    return jnp.concatenate([chunk_diag(w_r), chunk_diag(w_i)], axis=2).reshape(C_WIDTH, 2 * LANES)


def kernel(x_prompt, x_sample, c, cache_a_k, cache_a_v, cache_b_k, cache_b_v, state_c, cache_d_k, cache_d_v, c_ctx, w_mod, b_mod, w_in, a_sink, b_lambda, c_conv_w, c_conv_b, c_w_rgate, c_b_rgate, c_w_igate, c_b_igate, c_lambda, d_rpb, g_mix, w_out, ln_g, ln_b, w_router, b_router, w_e_gate, w_e_up, w_e_down, w_s_gate, w_s_up, w_s_down):
    x = jnp.concatenate([x_prompt.reshape(N_CTX, D_MODEL), x_sample.reshape(N_LAT, D_MODEL)], axis=0)
    cond8 = jnp.zeros((8, D_MODEL), F32).at[0].set(c_ctx).at[1:1 + DEC_BATCH].set(c)
    mod_all = _modulation(cond8, w_mod, b_mod)
    cos, sin_a, sin_b = _rope_tables()
    row_sel, col_sel, na_valid = _na_bias_indices()

    order = np.concatenate([np.arange(0, 512), np.arange(768, 4864), np.arange(512, 768)])

    ctx_out = [[] for _ in range(7)]
    for l in range(DEPTH):
        lam_init = _lambda_init(l)
        mod = mod_all[l, :N_GROUPS].reshape(N_GROUPS, 6, 1, D_MODEL)
        w_in_b = w_in[l][:, order].astype(BF16)
        slab = _in_projection(x, mod, w_in_b)

        lp = b_lambda[l]
        lam = jnp.exp(jnp.sum(lp[0] * lp[1])) - jnp.exp(jnp.sum(lp[2] * lp[3])) + lam_init
        scal = jnp.zeros((16,), F32).at[:A_HEADS].set(a_sink[l]).at[A_HEADS].set(lam)
        g2d = g_mix[l].reshape(1, -1)

        oa, ob, od = _ctx_attention(slab, scal, g2d, lam_init)

        qa_p, ka_r, va_b, qb_p, kb_r, vb_b, qd_p, kd_b, vd_b = _latent_prep(slab, cos, sin_a, sin_b)
        oa = _window_attention(qa_p, ka_r, va_b, cache_a_k[:, l].reshape(DEC_BATCH, PAST_LEN, LANES),
                               cache_a_v[:, l].reshape(DEC_BATCH, PAST_LEN, LANES), scal, g2d, oa)
        ob = _diff_attention(qb_p, kb_r, vb_b, cache_b_k[:, l].reshape(DEC_BATCH, PAST_LEN, 512),
                             cache_b_v[:, l].reshape(DEC_BATCH, PAST_LEN, 512), scal, g2d, lam_init, ob)
        bias = _na_bias(d_rpb[l], row_sel, col_sel, na_valid)
        od = _neighbourhood_attention(qd_p, kd_b, vd_b, cache_d_k[:, l].reshape(DEC_BATCH, PAST_LEN, 512),
                                      cache_d_v[:, l].reshape(DEC_BATCH, PAST_LEN, 512), bias, g2d, od)

        w_gates = jnp.stack([_gate_pairs(c_w_rgate[l, d], c_w_igate[l, d]) for d in range(2)])
        b_gates = jnp.concatenate([c_b_rgate[l], c_b_igate[l]], axis=-1).reshape(2, 1, 2 * C_WIDTH)
        zeros = jnp.zeros((BATCH, 1, C_WIDTH), F32)
        h0_f = jnp.concatenate([zeros, state_c[:, l, 0][:, None, :]], axis=0)
        h0_b = jnp.concatenate([zeros, state_c[:, l, 1][:, None, :]], axis=0)
        oc, fin_f, fin_b = _rglru(slab, c_conv_w[l], c_conv_b[l].reshape(1, -1), w_gates, b_gates,
                                  c_lambda[l].reshape(2, 1, C_WIDTH), h0_f, h0_b, g2d)

        x1, h2, idx, wt, rank, counts = _mixer_out(oa, ob, oc, od, w_out[l].astype(BF16), x, mod,
                                                   ln_g[l, 0].reshape(1, -1), ln_b[l, 0].reshape(1, -1),
                                                   w_router[l], b_router[l].reshape(1, -1))

        slot, first_blk, n_blk, counts = _dispatch_plan(idx, rank, counts)
        xg = _dispatch_rows(slot, h2)
        y_blk = _routed_experts(xg, first_blk, n_blk, counts, l, w_e_gate, w_e_up, w_e_down)
        x = _ffn_out(slot, h2, wt, x1, w_s_gate[l].astype(BF16), w_s_up[l].astype(BF16),
                     w_s_down[l].astype(BF16), mod, ln_g[l, 1].reshape(1, -1), ln_b[l, 1].reshape(1, -1), y_blk)

        ctx = slab[:N_CTX]
        ctx_out[0].append(ctx[:, COL_KA * LANES:(COL_KA + 1) * LANES].reshape(BATCH, SEQ, A_KV_HEADS, HEAD_DIM))
        ctx_out[1].append(ctx[:, COL_VA * LANES:(COL_VA + 1) * LANES].reshape(BATCH, SEQ, A_KV_HEADS, HEAD_DIM))
        ctx_out[2].append(ctx[:, COL_KB * 512:(COL_KB + 1) * 512].reshape(BATCH, SEQ, B_HEADS, 2, HEAD_DIM))
        ctx_out[3].append(ctx[:, COL_VB * 512:(COL_VB + 1) * 512].reshape(BATCH, SEQ, B_HEADS, 2 * HEAD_DIM))
        ctx_out[4].append(jnp.concatenate([fin_f[:BATCH], fin_b[:BATCH]], axis=1))
        ctx_out[5].append(ctx[:, COL_KD * 512:(COL_KD + 1) * 512].reshape(BATCH, SEQ, D_HEADS, HEAD_DIM))
        ctx_out[6].append(ctx[:, COL_VD * 512:(COL_VD + 1) * 512].reshape(BATCH, SEQ, D_HEADS, HEAD_DIM))

    new = [jnp.stack(t, axis=1) for t in ctx_out]
    return (x[:N_CTX].reshape(BATCH, SEQ, D_MODEL), x[N_CTX:].reshape(DEC_BATCH, DEC_SEQ, D_MODEL), *new)
```

```python
import functools
import math

import numpy as np
import jax
import jax.numpy as jnp
from jax import lax
from jax.experimental import pallas as pl
from jax.experimental.pallas import tpu as pltpu

F32 = jnp.float32
BF16 = jnp.bfloat16

D_MODEL = 2048
BATCH = 16
SEQ = 256
DEPTH = 2
DEC_BATCH = 2
DEC_SEQ = 4096
PAST_LEN = 256
GRID_W = 64
HEAD_DIM = 64
ROPE_BASE = 10000.0
A_HEADS = 8
A_KV_HEADS = 2
A_WINDOW = 128
B_HEADS = 4
C_WIDTH = 512
C_BLOCKS = 8
C_BW = C_WIDTH // C_BLOCKS
C_POW = 8.0
D_HEADS = 8
NA_ROWS = 8
NA_COLS = 16
N_EXPERTS = 64
TOP_K = 6
D_EXPERT = 512
ROUTED_SCALE = 2.5
ALPHA = (2.0 * DEPTH) ** 0.25
LN_EPS = 1e-5
NORM_EPS = 1e-6
NEG_INF = -1e30
SCALE = HEAD_DIM ** -0.5
LOG2E = math.log2(math.e)

N_CTX = BATCH * SEQ
N_LAT = DEC_BATCH * DEC_SEQ
N_TOK = N_CTX + N_LAT
GROUP_ROWS = 4096
N_GROUPS = N_TOK // GROUP_ROWS
IN_WIDTH = 4864
LANES = 128
VMEM_LIMIT = 56 * 1024 * 1024

COL_QA, COL_QB, COL_KB, COL_VB, COL_XC, COL_GC, COL_QD, COL_KD, COL_VD = range(9)
COL_KA, COL_VA = 36, 37

TM_PROJ = 512
TN_PROJ = IN_WIDTH // 2
TM_PREP = 256
TQ_A = 128
TQ_B = 256
NA_QR = 4
NA_KR = 12
T_SCAN = 256
TM_OUT = 256
MOE_BM = 256
TM_DISPATCH = 128
TM_FFN = 128


def _params(*sem):
    return pltpu.CompilerParams(dimension_semantics=sem, vmem_limit_bytes=VMEM_LIMIT)


def _nt(a, b):
    return lax.dot_general(a, b, (((1,), (1,)), ((), ())), preferred_element_type=F32)


def _mm(a, b):
    return jnp.dot(a, b, preferred_element_type=F32)


def _softmax_unnorm(parts, sink=None):
    m = parts[0].max(axis=-1, keepdims=True)
    for p in parts[1:]:
        m = jnp.maximum(m, p.max(axis=-1, keepdims=True))
    if sink is not None:
        m = jnp.maximum(m, sink)
    es = [jnp.exp(p - m) for p in parts]
    den = es[0].sum(axis=-1, keepdims=True)
    for e in es[1:]:
        den = den + e.sum(axis=-1, keepdims=True)
    if sink is not None:
        den = den + jnp.exp(sink - m)
    return es, den


def _rms_gain(x, g):
    return x * lax.rsqrt(jnp.mean(x * x, axis=-1, keepdims=True) + NORM_EPS) * g


def _layernorm(z, g, b):
    zc = z - jnp.mean(z, axis=-1, keepdims=True)
    var = jnp.mean(zc * zc, axis=-1, keepdims=True)
    return zc * lax.rsqrt(var + LN_EPS) * g + b


def _silu(x):
    return x * jax.nn.sigmoid(x)


def _low_half(shape):
    return lax.broadcasted_iota(jnp.int32, shape, 1) < HEAD_DIM


HALF_D = D_MODEL // 2
HIGH16 = 0xFFFF0000


def _pack_rows(x):
    bits = pltpu.bitcast(x.astype(BF16).astype(F32), jnp.uint32)
    return (bits[:, :HALF_D] >> 16) | (bits[:, HALF_D:] & jnp.uint32(HIGH16))


def _unpack_rows(p):
    return (pltpu.bitcast(p << 16, F32), pltpu.bitcast(p & jnp.uint32(HIGH16), F32))


SUBLANES = 8
assert HALF_D == SUBLANES * LANES


def _store_token_tiles(ref, index, packed, first=0):
    m = packed.shape[0]
    for s in range(SUBLANES):
        rows = pl.ds(first * SUBLANES + s, m, stride=SUBLANES)
        ref[index + (rows, slice(None))] = packed[:, s * LANES:(s + 1) * LANES]


def _load_token_tiles(ref, index, m):
    return jnp.concatenate([ref[index + (pl.ds(s, m, stride=SUBLANES), slice(None))] for s in range(SUBLANES)],
                           axis=1)


def _mod_kernel(c_ref, w_ref, b_ref, o_ref):
    c = c_ref[...]
    o_ref[0] = jnp.dot(_silu(c), w_ref[0], precision=lax.Precision.HIGHEST,
                       preferred_element_type=F32) + b_ref[0]


def _modulation(cond8, w_mod, b_mod):
    tn = 1024
    n = w_mod.shape[-1]
    return pl.pallas_call(
        _mod_kernel,
        grid=(DEPTH, n // tn),
        in_specs=[pl.BlockSpec((8, D_MODEL), lambda l, j: (0, 0)),
                  pl.BlockSpec((1, D_MODEL, tn), lambda l, j: (l, 0, j)),
                  pl.BlockSpec((1, 1, tn), lambda l, j: (l, 0, j))],
        out_specs=pl.BlockSpec((1, 8, tn), lambda l, j: (l, 0, j)),
        out_shape=jax.ShapeDtypeStruct((DEPTH, 8, n), F32),
        compiler_params=_params("parallel", "parallel"),
        name="modulation",
    )(cond8, w_mod, b_mod.reshape(DEPTH, 1, n))


def _mod_spec(which, tm):
    return pl.BlockSpec((None, None, 1, D_MODEL), lambda i, *_: (i * tm // GROUP_ROWS, which, 0, 0))


def _ctx_rows(tm):
    return pl.BlockSpec((tm, D_MODEL), lambda i, *_: (jnp.minimum(i, N_CTX // tm - 1), 0))


def _lat_rows(tm):
    return pl.BlockSpec((tm, D_MODEL), lambda i, *_: (jnp.maximum(i - N_CTX // tm, 0), 0))


def _is_ctx_tile(tm):
    return pl.program_id(0) * tm < N_CTX


def _inproj_kernel(xc_ref, xl_ref, sc_ref, sh_ref, w_ref, o_ref, xb_ref):
    @pl.when(pl.program_id(1) == 0)
    def _():
        x = jnp.where(_is_ctx_tile(xc_ref.shape[0]), xc_ref[...], xl_ref[...])
        xb_ref[...] = (x * (1.0 + sc_ref[...]) + sh_ref[...]).astype(BF16)

    o_ref[...] = _mm(xb_ref[...], w_ref[...])


def _in_projection(x_ctx, x_lat, mod, w_in_b):
    tm, tn = TM_PROJ, TN_PROJ
    return pl.pallas_call(
        _inproj_kernel,
        grid=(N_TOK // tm, IN_WIDTH // tn),
        in_specs=[_ctx_rows(tm), _lat_rows(tm),
                  _mod_spec(1, tm), _mod_spec(0, tm),
                  pl.BlockSpec((D_MODEL, tn), lambda i, j: (0, j))],
        out_specs=pl.BlockSpec((tm, tn), lambda i, j: (i, j)),
        out_shape=jax.ShapeDtypeStruct((N_TOK, IN_WIDTH), F32),
        scratch_shapes=[pltpu.VMEM((tm, D_MODEL), BF16)],
        compiler_params=_params("parallel", "arbitrary"),
        name="in_projection",
    )(x_ctx, x_lat, mod, mod, w_in_b)


def _ctx_attn_kernel(scal_ref, qa_ref, ka_ref, va_ref, qb_ref, kb_ref, vb_ref, qd_ref, kd_ref, vd_ref,
                     ga_ref, gb_ref, gd_ref, oa_ref, ob_ref, od_ref, *, lam_init):
    L = SEQ
    lo = _low_half((L, LANES))

    ka = ka_ref[...].astype(BF16)
    va = va_ref[...].astype(BF16)
    qa = qa_ref[...]
    chunks = []
    for c in range(A_HEADS // 2):
        hk = c // 2
        chunk = qa[:, c * LANES:(c + 1) * LANES]
        keep = lo if hk == 0 else jnp.logical_not(lo)
        halves = []
        for half in range(2):
            x = chunk if half == hk else pltpu.roll(chunk, HEAD_DIM, 1)
            qh = jnp.where(keep, x, 0.0).astype(BF16)
            s = _nt(qh, ka) * SCALE
            (e,), den = _softmax_unnorm([s], scal_ref[2 * c + half])
            o = _mm(e.astype(BF16), va) / den
            halves.append(o if half == hk else pltpu.roll(o, HEAD_DIM, 1))
        chunks.append(jnp.where(lo, halves[0], halves[1]))
    oa = jnp.concatenate(chunks, axis=1)
    oa_ref[...] = _rms_gain(oa, ga_ref[...]).astype(BF16)

    lam = scal_ref[A_HEADS]
    qb = qb_ref[...]
    chunks = []
    for h in range(B_HEADS):
        sl = slice(h * LANES, (h + 1) * LANES)
        qc = qb[:, sl]
        kc = kb_ref[:, sl].astype(BF16)
        vh = vb_ref[:, sl].astype(BF16)
        (e1,), d1 = _softmax_unnorm([_nt(jnp.where(lo, qc, 0.0).astype(BF16), kc) * SCALE])
        (e2,), d2 = _softmax_unnorm([_nt(jnp.where(lo, 0.0, qc).astype(BF16), kc) * SCALE])
        w = e1 * (1.0 / d1) - e2 * (lam / d2)
        o = _mm(w.astype(BF16), vh)
        chunks.append(_rms_gain(o, gb_ref[:, sl]) * (1.0 - lam_init))
    ob_ref[...] = jnp.concatenate(chunks, axis=1).astype(BF16)

    qd = qd_ref[...]
    chunks = []
    for c in range(D_HEADS // 2):
        sl = slice(c * LANES, (c + 1) * LANES)
        qc = qd[:, sl]
        kc = kd_ref[:, sl].astype(BF16)
        vc = vd_ref[:, sl].astype(BF16)
        halves = []
        for half in range(2):
            qh = jnp.where(lo if half == 0 else jnp.logical_not(lo), qc, 0.0).astype(BF16)
            (e,), den = _softmax_unnorm([_nt(qh, kc) * SCALE])
            halves.append(_mm(e.astype(BF16), vc) / den)
        chunks.append(jnp.where(lo, halves[0], halves[1]))
    od = jnp.concatenate(chunks, axis=1)
    od_ref[...] = _rms_gain(od, gd_ref[...]).astype(BF16)


def _ctx_attention(slab, scal, g_mix2d, lam_init):
    L = SEQ

    def wide(col):
        return pl.BlockSpec((L, 512), lambda b: (b, col))

    def narrow(col):
        return pl.BlockSpec((L, LANES), lambda b: (b, col))

    def gain(col):
        return pl.BlockSpec((1, 512), lambda b: (0, col))

    out = jax.ShapeDtypeStruct((N_TOK, 512), BF16)
    ospec = pl.BlockSpec((L, 512), lambda b: (b, 0))
    return pl.pallas_call(
        functools.partial(_ctx_attn_kernel, lam_init=lam_init),
        grid=(BATCH,),
        in_specs=[pl.BlockSpec(memory_space=pltpu.SMEM),
                  wide(COL_QA), narrow(COL_KA), narrow(COL_VA),
                  wide(COL_QB), wide(COL_KB), wide(COL_VB),
                  wide(COL_QD), wide(COL_KD), wide(COL_VD),
                  gain(0), gain(1), gain(3)],
        out_specs=[ospec, ospec, ospec],
        out_shape=[out, out, out],
        compiler_params=_params("parallel"),
        name="ctx_attention",
    )(scal, slab, slab, slab, slab, slab, slab, slab, slab, slab, g_mix2d, g_mix2d, g_mix2d)


def _prep_kernel(qa_ref, ka_ref, va_ref, qb_ref, kb_ref, vb_ref, qd_ref, kd_ref, vd_ref,
                 cos_ref, sa_ref, sb_ref,
                 qa_o, ka_o, va_o, qb_o, kb_o, vb_o, qd_o, kd_o, vd_o):
    cos, sa, sb = cos_ref[...], sa_ref[...], sb_ref[...]
    lo = _low_half(cos.shape)
    hi = jnp.logical_not(lo)

    def rope(x):
        return (x * cos + pltpu.roll(x, LANES - HEAD_DIM // 4, 1) * sa
                + pltpu.roll(x, HEAD_DIM // 4, 1) * sb)

    ka_o[...] = rope(ka_ref[...]).astype(BF16)
    va_o[...] = va_ref[...].astype(BF16)
    vb_o[...] = vb_ref[...].astype(BF16)
    kd_o[...] = kd_ref[...].astype(BF16)
    vd_o[...] = vd_ref[...].astype(BF16)

    for c in range(A_HEADS // 2):
        hk = c // 2
        r = rope(qa_ref[:, c * LANES:(c + 1) * LANES])
        keep = lo if hk == 0 else hi
        for half in range(2):
            h = 2 * c + half
            x = r if half == hk else pltpu.roll(r, HEAD_DIM, 1)
            qa_o[:, h * LANES:(h + 1) * LANES] = jnp.where(keep, x, 0.0).astype(BF16)
    for h in range(B_HEADS):
        sl = slice(h * LANES, (h + 1) * LANES)
        r = rope(qb_ref[:, sl])
        kb_o[:, sl] = rope(kb_ref[:, sl]).astype(BF16)
        qb_o[:, (2 * h) * LANES:(2 * h + 1) * LANES] = jnp.where(lo, r, 0.0).astype(BF16)
        qb_o[:, (2 * h + 1) * LANES:(2 * h + 2) * LANES] = jnp.where(hi, r, 0.0).astype(BF16)
    for c in range(D_HEADS // 2):
        x = qd_ref[:, c * LANES:(c + 1) * LANES]
        qd_o[:, (2 * c) * LANES:(2 * c + 1) * LANES] = jnp.where(lo, x, 0.0).astype(BF16)
        qd_o[:, (2 * c + 1) * LANES:(2 * c + 2) * LANES] = jnp.where(hi, x, 0.0).astype(BF16)


def _latent_prep(slab, cos, sa, sb):
    tm = TM_PREP
    off = N_CTX // tm
    per_seq = DEC_SEQ // tm

    def wide(col):
        return pl.BlockSpec((tm, 512), lambda i: (i + off, col))

    def narrow(col):
        return pl.BlockSpec((tm, LANES), lambda i: (i + off, col))

    tab = pl.BlockSpec((tm, LANES), lambda i: (i % per_seq, 0))

    def out(width):
        return (pl.BlockSpec((tm, width), lambda i: (i, 0)), jax.ShapeDtypeStruct((N_LAT, width), BF16))

    outs = [out(1024), out(LANES), out(LANES), out(1024), out(512), out(512), out(1024), out(512), out(512)]
    return pl.pallas_call(
        _prep_kernel,
        grid=(N_LAT // tm,),
        in_specs=[wide(COL_QA), narrow(COL_KA), narrow(COL_VA), wide(COL_QB), wide(COL_KB), wide(COL_VB),
                  wide(COL_QD), wide(COL_KD), wide(COL_VD), tab, tab, tab],
        out_specs=[o[0] for o in outs],
        out_shape=[o[1] for o in outs],
        compiler_params=_params("parallel"),
        name="latent_prep",
    )(slab, slab, slab, slab, slab, slab, slab, slab, slab, cos, sa, sb)


def _win_attn_kernel(scal_ref, q_ref, kp_ref, kc_ref, kn_ref, vp_ref, vc_ref, vn_ref, kctx_ref, vctx_ref,
                     g_ref, _all_rows_ref, o_ref):
    n = pl.program_id(1)
    nb = pl.num_programs(1)
    tq = TQ_A
    group = A_HEADS // A_KV_HEADS
    k = jnp.concatenate([kp_ref[...], kc_ref[...], kn_ref[...], kctx_ref[...].astype(BF16)], axis=0)
    v = jnp.concatenate([vp_ref[...], vc_ref[...], vn_ref[...], vctx_ref[...].astype(BF16)], axis=0)
    nk = 3 * tq + PAST_LEN
    row = lax.broadcasted_iota(jnp.int32, (group * tq, nk), 0) & (tq - 1)
    col = lax.broadcasted_iota(jnp.int32, (group * tq, nk), 1)
    prev_thr = row + jnp.where(n > 0, 0, tq)
    next_thr = row + 2 * tq - jnp.where(n < nb - 1, 0, tq)
    masked = ((col < tq) & (col < prev_thr)) | ((col >= 2 * tq) & (col < 3 * tq) & (col > next_thr))
    valid = jnp.logical_not(masked)
    lo = _low_half((tq, LANES))
    heads = []
    for hk in range(A_KV_HEADS):
        q4 = jnp.concatenate([q_ref[:, (hk * group + g) * LANES:(hk * group + g + 1) * LANES]
                              for g in range(group)], axis=0)
        s = jnp.where(valid, _nt(q4, k) * SCALE, NEG_INF)
        sink = jnp.concatenate([jnp.full((tq, 1), scal_ref[hk * group + g], F32) for g in range(group)], axis=0)
        (e,), den = _softmax_unnorm([s], sink)
        o = _mm(e.astype(BF16), v) / den
        for g in range(group):
            h = hk * group + g
            og = o[g * tq:(g + 1) * tq]
            heads.append(og if (h % 2) == hk else pltpu.roll(og, HEAD_DIM, 1))
    oa = jnp.concatenate([jnp.where(lo, heads[2 * c], heads[2 * c + 1]) for c in range(A_HEADS // 2)], axis=1)
    o_ref[...] = _rms_gain(oa, g_ref[...]).astype(BF16)


def _window_attention(qa_p, ka_r, va_b, cache_k, cache_v, scal, g_mix2d, o_all):
    tq = TQ_A
    nb = DEC_SEQ // tq

    def band(d):
        return pl.BlockSpec((tq, LANES), lambda b, n: (b * nb + jnp.clip(n + d, 0, nb - 1), 0))

    ctx = pl.BlockSpec((None, PAST_LEN, LANES), lambda b, n: (b, 0, 0))
    return pl.pallas_call(
        _win_attn_kernel,
        grid=(DEC_BATCH, nb),
        in_specs=[pl.BlockSpec(memory_space=pltpu.SMEM),
                  pl.BlockSpec((tq, A_HEADS * LANES), lambda b, n: (b * nb + n, 0)),
                  band(-1), band(0), band(1), band(-1), band(0), band(1), ctx, ctx,
                  pl.BlockSpec((1, 512), lambda b, n: (0, 0)),
                  pl.BlockSpec(memory_space=pl.ANY)],
        out_specs=pl.BlockSpec((tq, 512), lambda b, n: (N_CTX // tq + b * nb + n, 0)),
        out_shape=jax.ShapeDtypeStruct((N_TOK, 512), BF16),
        input_output_aliases={11: 0},
        compiler_params=_params("parallel", "parallel"),
        name="window_attention",
    )(scal, qa_p, ka_r, ka_r, ka_r, va_b, va_b, va_b, cache_k, cache_v, g_mix2d, o_all)


def _diff_attn_kernel(scal_ref, q_ref, k_ref, v_ref, kctx_ref, vctx_ref, g_ref, _all_rows_ref, o_ref, *,
                      lam_init):
    lam = scal_ref[A_HEADS]
    chunks = []
    for h in range(B_HEADS):
        sl = slice(h * LANES, (h + 1) * LANES)
        kh = k_ref[:, sl]
        kc = kctx_ref[:, sl].astype(BF16)
        es, dens = [], []
        for m in range(2):
            q = q_ref[:, (2 * h + m) * LANES:(2 * h + m + 1) * LANES]
            t_lat = _nt(q, kh) * (SCALE * LOG2E)
            t_ctx = _nt(q, kc) * (SCALE * LOG2E)
            mx = jnp.maximum(t_lat.max(axis=-1, keepdims=True), t_ctx.max(axis=-1, keepdims=True))
            e_lat = jnp.exp2(t_lat - mx)
            e_ctx = jnp.exp2(t_ctx - mx)
            es.append((e_lat, e_ctx))
            dens.append(e_lat.sum(axis=-1, keepdims=True) + e_ctx.sum(axis=-1, keepdims=True))
        c1 = 1.0 / dens[0]
        c2 = lam / dens[1]
        w_lat = (es[0][0] * c1 - es[1][0] * c2).astype(BF16)
        w_ctx = (es[0][1] * c1 - es[1][1] * c2).astype(BF16)
        o = _mm(w_lat, v_ref[:, sl]) + _mm(w_ctx, vctx_ref[:, sl].astype(BF16))
        chunks.append(_rms_gain(o, g_ref[:, sl]) * (1.0 - lam_init))
    o_ref[...] = jnp.concatenate(chunks, axis=1).astype(BF16)


def _diff_attention(qb_p, kb_r, vb_b, cache_k, cache_v, scal, g_mix2d, lam_init, o_all):
    tq = TQ_B
    nb = DEC_SEQ // tq
    full = pl.BlockSpec((DEC_SEQ, 512), lambda b, n: (b, 0))
    ctx = pl.BlockSpec((None, PAST_LEN, 512), lambda b, n: (b, 0, 0))
    return pl.pallas_call(
        functools.partial(_diff_attn_kernel, lam_init=lam_init),
        grid=(DEC_BATCH, nb),
        in_specs=[pl.BlockSpec(memory_space=pltpu.SMEM),
                  pl.BlockSpec((tq, 2 * B_HEADS * LANES), lambda b, n: (b * nb + n, 0)),
                  full, full, ctx, ctx,
                  pl.BlockSpec((1, 512), lambda b, n: (0, 1)),
                  pl.BlockSpec(memory_space=pl.ANY)],
        out_specs=pl.BlockSpec((tq, 512), lambda b, n: (N_CTX // tq + b * nb + n, 0)),
        out_shape=jax.ShapeDtypeStruct((N_TOK, 512), BF16),
        input_output_aliases={7: 0},
        compiler_params=_params("parallel", "parallel"),
        name="diff_attention",
    )(scal, qb_p, kb_r, vb_b, cache_k, cache_v, g_mix2d, o_all)


def _na_window_start(step):
    return np.clip(step * NA_QR - NA_ROWS // 2, 0, GRID_W - NA_KR)


def _na_bias_indices():
    rows = DEC_SEQ // GRID_W
    steps = rows // NA_QR
    pats = []
    for step in range(steps):
        w0 = _na_window_start(step)
        r = step * NA_QR + np.arange(NA_QR)[:, None, None, None]
        c = np.arange(GRID_W)[None, :, None, None]
        kr = w0 + np.arange(NA_KR)[None, None, :, None]
        kc = np.arange(GRID_W)[None, None, None, :]
        kr0 = np.clip(r - NA_ROWS // 2, 0, rows - NA_ROWS)
        kc0 = np.clip(c - NA_COLS // 2, 0, GRID_W - NA_COLS)
        valid = (kr >= kr0) & (kr < kr0 + NA_ROWS) & (kc >= kc0) & (kc < kc0 + NA_COLS)
        drow = np.clip(kr - r + NA_ROWS - 1, 0, 2 * NA_ROWS - 2)
        dcol = np.clip(kc - c + NA_COLS - 1, 0, 2 * NA_COLS - 2)
        shape = (NA_QR * GRID_W, NA_KR * GRID_W)
        full = np.broadcast_to
        pats.append((full(drow, valid.shape).reshape(shape), full(dcol, valid.shape).reshape(shape),
                     valid.reshape(shape)))
    for step in range(2, steps - 1):
        for a, b in zip(pats[1], pats[step]):
            assert np.array_equal(a, b)
    kinds = [pats[0], pats[1], pats[-1]]
    drow, dcol, valid = (np.stack([k[i] for k in kinds]) for i in range(3))
    shape6 = (len(kinds), NA_QR, GRID_W, NA_KR, GRID_W)
    row_sel = np.eye(2 * NA_ROWS - 1, dtype=np.float32)[drow.reshape(shape6)[:, :, 0, :, 0]]
    col_sel = np.eye(2 * NA_COLS - 1, dtype=np.float32)[dcol.reshape(shape6)[0, 0, :, 0, :]]
    return row_sel, col_sel, valid


def _na_bias(rpb, row_sel, col_sel, valid):
    hp = lax.Precision.HIGHEST
    cols = jnp.einsum('hab,cdb->hacd', rpb, col_sel, precision=hp)
    bias = jnp.einsum('kria,hacd->khrcid', row_sel, cols, precision=hp)
    bias = bias.reshape(valid.shape[0], D_HEADS, NA_QR * GRID_W, NA_KR * GRID_W)
    return jnp.where(valid[:, None], bias, NEG_INF)


def _na_kernel(q_ref, k_ref, v_ref, kctx_ref, vctx_ref, bias_ref, g_ref, _all_rows_ref, o_ref):
    step = pl.program_id(1)
    nq = NA_QR * GRID_W
    nk = NA_KR * GRID_W
    w0 = jnp.clip(step * NA_QR - NA_ROWS // 2, 0, GRID_W - NA_KR)
    start = pl.multiple_of(w0 * GRID_W, GRID_W)
    lo = _low_half((nq, LANES))
    chunks = []
    for c in range(D_HEADS // 2):
        sl = slice(c * LANES, (c + 1) * LANES)
        kw = k_ref[pl.ds(start, nk), sl]
        vw = v_ref[pl.ds(start, nk), sl]
        kc = kctx_ref[:, sl].astype(BF16)
        vc = vctx_ref[:, sl].astype(BF16)
        halves = []
        for half in range(2):
            h = 2 * c + half
            q = q_ref[:, h * LANES:(h + 1) * LANES]
            s_loc = _nt(q, kw) * SCALE + bias_ref[h]
            s_ctx = _nt(q, kc) * SCALE
            (e_loc, e_ctx), den = _softmax_unnorm([s_loc, s_ctx])
            halves.append((_mm(e_loc.astype(BF16), vw) + _mm(e_ctx.astype(BF16), vc)) / den)
        chunks.append(jnp.where(lo, halves[0], halves[1]))
    od = jnp.concatenate(chunks, axis=1)
    o_ref[...] = _rms_gain(od, g_ref[...]).astype(BF16)


def _neighbourhood_attention(qd_p, kd_b, vd_b, cache_k, cache_v, bias, g_mix2d, o_all):
    nq = NA_QR * GRID_W
    steps = DEC_SEQ // nq
    full = pl.BlockSpec((DEC_SEQ, 512), lambda b, j: (b, 0))
    ctx = pl.BlockSpec((None, PAST_LEN, 512), lambda b, j: (b, 0, 0))

    def kind(b, j):
        return (jnp.where(j == 0, 0, jnp.where(j == steps - 1, 2, 1)), 0, 0, 0)

    return pl.pallas_call(
        _na_kernel,
        grid=(DEC_BATCH, steps),
        in_specs=[pl.BlockSpec((nq, D_HEADS * LANES), lambda b, j: (b * steps + j, 0)),
                  full, full, ctx, ctx,
                  pl.BlockSpec((None, D_HEADS, nq, NA_KR * GRID_W), kind),
                  pl.BlockSpec((1, 512), lambda b, j: (0, 3)),
                  pl.BlockSpec(memory_space=pl.ANY)],
        out_specs=pl.BlockSpec((nq, 512), lambda b, j: (N_CTX // nq + b * steps + j, 0)),
        out_shape=jax.ShapeDtypeStruct((N_TOK, 512), BF16),
        input_output_aliases={7: 0},
        compiler_params=_params("parallel", "arbitrary"),
        name="neighbourhood_attention",
    )(qd_p, kd_b, vd_b, cache_k, cache_v, bias, g_mix2d, o_all)


N_SCAN_TILES = N_TOK // T_SCAN
CTX_TILES = N_CTX // T_SCAN
TILES_PER_LAT = DEC_SEQ // T_SCAN


def _scan_tile_flags(i):
    is_ctx = i < CTX_TILES
    pos = (i - CTX_TILES) % TILES_PER_LAT
    return is_ctx | (pos == 0), is_ctx | (pos == TILES_PER_LAT - 1)


def _scan_seq(i):
    return jnp.where(i < CTX_TILES, i, CTX_TILES + (i - CTX_TILES) // TILES_PER_LAT)


def _conv_gates(x_ref, prev_ref, next_ref, cw_ref, cb_ref, wgh_ref, wgl_ref, bg_ref, lam_ref, first, last):
    x = x_ref[...]
    t = T_SCAN
    row = lax.broadcasted_iota(jnp.int32, x.shape, 0)
    pm1 = jnp.where(first, 0.0, prev_ref[7:8, :])
    n0 = jnp.where(last, 0.0, next_ref[0:1, :])
    n1 = jnp.where(last, 0.0, next_ref[1:2, :])
    x_m1 = jnp.where(row == 0, pm1, pltpu.roll(x, 1, 0))
    x_p1 = jnp.where(row == t - 1, n0, pltpu.roll(x, t - 1, 0))
    x_p2 = jnp.where(row == t - 2, n0, jnp.where(row == t - 1, n1, pltpu.roll(x, t - 2, 0)))
    y = (cw_ref[0:1, :] * x_m1 + cw_ref[1:2, :] * x + cw_ref[2:3, :] * x_p1 + cw_ref[3:4, :] * x_p2
         + cb_ref[...])
    y_hi = y.astype(BF16)
    y_lo = (y - y_hi.astype(F32)).astype(BF16)
    zr, zi = [], []
    for c in range(C_WIDTH // LANES):
        sl = slice(c * LANES, (c + 1) * LANES)
        w_hi = wgh_ref[0, sl, :]
        z = _mm(y_hi[:, sl], w_hi) + _mm(y_lo[:, sl], w_hi) + _mm(y_hi[:, sl], wgl_ref[0, sl, :])
        zr.append(z[:, :LANES])
        zi.append(z[:, LANES:])
    bias = bg_ref[0]
    r = jax.nn.sigmoid(jnp.concatenate(zr, axis=1) + bias[:, :C_WIDTH])
    gate_i = jax.nn.sigmoid(jnp.concatenate(zi, axis=1) + bias[:, C_WIDTH:])
    nl = -lam_ref[0]
    softplus = jnp.maximum(nl, 0.0) + jnp.log1p(jnp.exp(-jnp.abs(nl)))
    log_a = -C_POW * r * softplus
    a = jnp.exp(log_a)
    u = jnp.sqrt(1.0 - jnp.exp(2.0 * log_a)) * gate_i * y
    return a, u


def _scan_rows(a_s, u_s, h_s, h, reverse):
    groups = T_SCAN // 8

    def body(i, h):
        g = (groups - 1 - i) if reverse else i
        base = pl.multiple_of(g * 8, 8)
        a8 = a_s[pl.ds(base, 8), :]
        u8 = u_s[pl.ds(base, 8), :]
        rows = [None] * 8
        for k in (range(7, -1, -1) if reverse else range(8)):
            h = a8[k:k + 1, :] * h + u8[k:k + 1, :]
            rows[k] = h
        h_s[pl.ds(base, 8), :] = jnp.concatenate(rows, axis=0)
        return h

    return lax.fori_loop(0, groups, body, h)


def _scan_fwd_kernel(x_ref, prev_ref, next_ref, cw_ref, cb_ref, wgh_ref, wgl_ref, bg_ref, lam_ref, h0_ref,
                     hf_ref, fin_ref, a_s, u_s, carry_s):
    i = pl.program_id(0)
    first, last = _scan_tile_flags(i)
    a, u = _conv_gates(x_ref, prev_ref, next_ref, cw_ref, cb_ref, wgh_ref, wgl_ref, bg_ref, lam_ref, first, last)
    a_s[...] = a
    u_s[...] = u

    @pl.when(first)
    def _():
        carry_s[...] = h0_ref[...]

    h = _scan_rows(a_s, u_s, hf_ref, carry_s[...], reverse=False)
    carry_s[...] = h
    fin_ref[...] = h


def _scan_bwd_kernel(x_ref, prev_ref, next_ref, cw_ref, cb_ref, wgh_ref, wgl_ref, bg_ref, lam_ref, h0_ref,
                     hf_ref, gc_ref, g_ref, oc_ref, fin_ref, a_s, u_s, hb_s, carry_s):
    i = N_SCAN_TILES - 1 - pl.program_id(0)
    first, last = _scan_tile_flags(i)
    a, u = _conv_gates(x_ref, prev_ref, next_ref, cw_ref, cb_ref, wgh_ref, wgl_ref, bg_ref, lam_ref, first, last)
    a_s[...] = a
    u_s[...] = u

    @pl.when(last)
    def _():
        carry_s[...] = h0_ref[...]

    h = _scan_rows(a_s, u_s, hb_s, carry_s[...], reverse=True)
    carry_s[...] = h
    fin_ref[...] = h
    g = gc_ref[...]
    gelu = 0.5 * g * (1.0 + jnp.tanh(math.sqrt(2.0 / math.pi) * (g + 0.044715 * (g * g * g))))
    oc_ref[...] = _rms_gain(gelu * (hf_ref[...] + hb_s[...]), g_ref[...]).astype(BF16)


def _scan_common_specs(tile_of, direction):
    halo = T_SCAN // 8
    last_block = N_TOK // 8 - 1
    return [pl.BlockSpec((T_SCAN, 512), lambda j: (tile_of(j), COL_XC)),
            pl.BlockSpec((8, 512), lambda j: (jnp.maximum(tile_of(j) * halo - 1, 0), COL_XC)),
            pl.BlockSpec((8, 512), lambda j: (jnp.minimum((tile_of(j) + 1) * halo, last_block), COL_XC)),
            pl.BlockSpec((4, C_WIDTH), lambda j: (0, 0)),
            pl.BlockSpec((1, C_WIDTH), lambda j: (0, 0)),
            pl.BlockSpec((1, C_WIDTH, 2 * LANES), lambda j: (direction, 0, 0)),
            pl.BlockSpec((1, C_WIDTH, 2 * LANES), lambda j: (direction, 0, 0)),
            pl.BlockSpec((1, 1, 2 * C_WIDTH), lambda j: (direction, 0, 0)),
            pl.BlockSpec((1, 1, C_WIDTH), lambda j: (direction, 0, 0)),
            pl.BlockSpec((None, 1, C_WIDTH), lambda j: (_scan_seq(tile_of(j)), 0, 0))]


def _rglru(slab, conv_w, conv_b, w_gates, b_gates, lam, h0_f, h0_b, g_mix2d):
    wg_hi = w_gates.astype(BF16)
    wg_lo = (w_gates - wg_hi.astype(F32)).astype(BF16)
    fin = jax.ShapeDtypeStruct((N_SCAN_TILES, 1, C_WIDTH), F32)
    fwd_tile = lambda j: j
    h_f, fin_f = pl.pallas_call(
        _scan_fwd_kernel,
        grid=(N_SCAN_TILES,),
        in_specs=_scan_common_specs(fwd_tile, 0),
        out_specs=[pl.BlockSpec((T_SCAN, C_WIDTH), lambda j: (j, 0)),
                   pl.BlockSpec((None, 1, C_WIDTH), lambda j: (j, 0, 0))],
        out_shape=[jax.ShapeDtypeStruct((N_TOK, C_WIDTH), F32), fin],
        scratch_shapes=[pltpu.VMEM((T_SCAN, C_WIDTH), F32), pltpu.VMEM((T_SCAN, C_WIDTH), F32),
                        pltpu.VMEM((1, C_WIDTH), F32)],
        compiler_params=_params("arbitrary"),
        name="rglru_forward",
    )(slab, slab, slab, conv_w, conv_b, wg_hi, wg_lo, b_gates, lam, h0_f)
    bwd_tile = lambda j: N_SCAN_TILES - 1 - j
    oc, fin_b = pl.pallas_call(
        _scan_bwd_kernel,
        grid=(N_SCAN_TILES,),
        in_specs=_scan_common_specs(bwd_tile, 1) + [
            pl.BlockSpec((T_SCAN, C_WIDTH), lambda j: (bwd_tile(j), 0)),
            pl.BlockSpec((T_SCAN, 512), lambda j: (bwd_tile(j), COL_GC)),
            pl.BlockSpec((1, 512), lambda j: (0, 2))],
        out_specs=[pl.BlockSpec((T_SCAN, C_WIDTH), lambda j: (bwd_tile(j), 0)),
                   pl.BlockSpec((None, 1, C_WIDTH), lambda j: (bwd_tile(j), 0, 0))],
        out_shape=[jax.ShapeDtypeStruct((N_TOK, C_WIDTH), BF16), fin],
        scratch_shapes=[pltpu.VMEM((T_SCAN, C_WIDTH), F32), pltpu.VMEM((T_SCAN, C_WIDTH), F32),
                        pltpu.VMEM((T_SCAN, C_WIDTH), F32), pltpu.VMEM((1, C_WIDTH), F32)],
        compiler_params=_params("arbitrary"),
        name="rglru_backward",
    )(slab, slab, slab, conv_w, conv_b, wg_hi, wg_lo, b_gates, lam, h0_b, h_f, slab, g_mix2d)
    return oc, fin_f, fin_b


def _mixout_kernel(oa_ref, ob_ref, oc_ref, od_ref, w_ref, xc_ref, xl_ref, g1_ref, sc2_ref, sh2_ref, lng_ref,
                   lnb_ref, wrh_ref, wrl_ref, br_ref, x1_ref, h2_ref, idx_ref, wt_ref, rank_ref, cnt_ref, cnt_s):
    @pl.when(pl.program_id(0) == 0)
    def _():
        cnt_s[...] = jnp.zeros(cnt_s.shape, F32)

    parts = 2
    pm = xc_ref.shape[0] // parts
    is_ctx = _is_ctx_tile(xc_ref.shape[0])
    for part in range(parts):
        rows = pl.ds(part * pm, pm)
        x = jnp.where(is_ctx, xc_ref[rows, :], xl_ref[rows, :])
        _mixout_rows(rows, part * pm, pm, oa_ref, ob_ref, oc_ref, od_ref, w_ref, x, g1_ref,
                     sc2_ref, sh2_ref, lng_ref, lnb_ref, wrh_ref, wrl_ref, br_ref, x1_ref, h2_ref, idx_ref,
                     wt_ref, rank_ref, cnt_s)
    cnt_ref[...] = cnt_s[...]


def _mixout_rows(rows, first, tm, oa_ref, ob_ref, oc_ref, od_ref, w_ref, x, g1_ref, sc2_ref, sh2_ref,
                 lng_ref, lnb_ref, wrh_ref, wrl_ref, br_ref, x1_ref, h2_ref, idx_ref, wt_ref, rank_ref, cnt_s):
    y = _mm(oa_ref[rows, :], w_ref[0:512, :])
    y += _mm(ob_ref[rows, :], w_ref[512:1024, :])
    y += _mm(oc_ref[rows, :], w_ref[1024:1536, :])
    y += _mm(od_ref[rows, :], w_ref[1536:2048, :])
    x1 = _layernorm(ALPHA * x + g1_ref[...] * y, lng_ref[...], lnb_ref[...])
    x1_ref[rows, :] = x1
    h2 = x1 * (1.0 + sc2_ref[...]) + sh2_ref[...]
    _store_token_tiles(h2_ref, (), _pack_rows(h2), first)

    h_hi = h2.astype(BF16)
    h_lo = (h2 - h_hi.astype(F32)).astype(BF16)
    scores = jax.nn.sigmoid(_mm(h_hi, wrh_ref[...]) + _mm(h_lo, wrh_ref[...]) + _mm(h_hi, wrl_ref[...]))
    sel = scores + br_ref[...]
    lane = lax.broadcasted_iota(jnp.int32, sel.shape, 1).astype(F32)
    slot = lax.broadcasted_iota(jnp.int32, (tm, 8), 1)
    wide = lax.broadcasted_iota(jnp.int32, (tm, LANES), 1)
    idx_out = jnp.zeros((tm, LANES), F32)
    wt_out = jnp.zeros((tm, 8), F32)
    total = jnp.zeros((sel.shape[0], 1), F32)
    hits = []
    for k in range(TOP_K):
        m = sel.max(axis=-1, keepdims=True)
        idx = jnp.where(sel == m, lane, float(N_EXPERTS)).min(axis=-1, keepdims=True)
        hit = lane == idx
        hits.append(hit)
        w = jnp.where(hit, scores, 0.0).sum(axis=-1, keepdims=True)
        total = total + w
        idx_out = jnp.where(wide == k, idx, idx_out)
        wt_out = jnp.where(slot == k, w, wt_out)
        sel = jnp.where(hit, -jnp.inf, sel)
    cols = pl.ds(first, tm)
    idx_ref[:, cols] = idx_out.T[:8, :].astype(jnp.int32)
    wt_ref[rows, :] = wt_out / total * ROUTED_SCALE

    chosen = jnp.zeros(sel.shape, F32)
    for hit in hits:
        chosen = chosen + hit.astype(F32)
    earlier = (lax.broadcasted_iota(jnp.int32, (tm, tm), 1) < lax.broadcasted_iota(jnp.int32, (tm, tm), 0))
    before = _mm(earlier.astype(F32).astype(BF16), chosen.astype(BF16)) + cnt_s[...]
    rank_out = jnp.zeros((tm, LANES), F32)
    for k, hit in enumerate(hits):
        rank_out = jnp.where(wide == k, jnp.where(hit, before, 0.0).sum(axis=-1, keepdims=True), rank_out)
    rank_ref[:, cols] = rank_out.T[:8, :].astype(jnp.int32)
    cnt_s[...] = cnt_s[...] + chosen.sum(axis=0, keepdims=True)


def _mixer_out(oa, ob, oc, od, w_out_b, x_ctx, x_lat, mod, ln_g, ln_b, w_router, b_router):
    tm = TM_OUT
    cat = pl.BlockSpec((tm, 512), lambda i: (i, 0))
    row = pl.BlockSpec((tm, D_MODEL), lambda i: (i, 0))
    vec = pl.BlockSpec((1, D_MODEL), lambda i: (0, 0))
    k8 = pl.BlockSpec((tm, 8), lambda i: (i, 0))
    k_major = pl.BlockSpec((8, tm), lambda i: (0, i))
    per_expert = pl.BlockSpec((1, N_EXPERTS), lambda i: (0, 0))
    router = pl.BlockSpec((D_MODEL, N_EXPERTS), lambda i: (0, 0))
    wr_hi = w_router.astype(BF16)
    wr_lo = (w_router - wr_hi.astype(F32)).astype(BF16)
    return pl.pallas_call(
        _mixout_kernel,
        grid=(N_TOK // tm,),
        in_specs=[cat, cat, cat, cat,
                  pl.BlockSpec((D_MODEL, D_MODEL), lambda i: (0, 0)),
                  _ctx_rows(tm), _lat_rows(tm), _mod_spec(2, tm), _mod_spec(4, tm), _mod_spec(3, tm), vec, vec,
                  router, router, per_expert],
        out_specs=[row, pl.BlockSpec((tm * SUBLANES, LANES), lambda i: (i, 0)), k_major, k8, k_major, per_expert],
        out_shape=[jax.ShapeDtypeStruct((N_TOK, D_MODEL), F32),
                   jax.ShapeDtypeStruct((N_TOK * SUBLANES, LANES), jnp.uint32),
                   jax.ShapeDtypeStruct((8, N_TOK), jnp.int32), jax.ShapeDtypeStruct((N_TOK, 8), F32),
                   jax.ShapeDtypeStruct((8, N_TOK), jnp.int32), jax.ShapeDtypeStruct((1, N_EXPERTS), F32)],
        scratch_shapes=[pltpu.VMEM((1, N_EXPERTS), F32)],
        compiler_params=_params("arbitrary"),
        name="mixer_out_router",
    )(oa, ob, oc, od, w_out_b, x_ctx, x_lat, mod, mod, mod, ln_g, ln_b, wr_hi, wr_lo, b_router)


MOE_BLOCKS = N_TOK * TOP_K // MOE_BM + N_EXPERTS
MOE_ROWS = MOE_BLOCKS * MOE_BM


def _row_copy(src, src_row, dst, dst_row, sem):
    def tile(row):
        return pl.ds(row * SUBLANES if isinstance(row, int) else pl.multiple_of(row * SUBLANES, SUBLANES), SUBLANES)

    return pltpu.make_async_copy(src.at[tile(src_row)], dst.at[tile(dst_row)], sem)


def _dispatch_kernel(slot_ref, h_ref, xg_ref, sem):
    tm = h_ref.shape[0] // SUBLANES
    base = pl.program_id(0) * tm

    for r in range(tm):
        for k in range(TOP_K):
            _row_copy(h_ref, r, xg_ref, slot_ref[base + (k * N_TOK + r)], sem).start()
    for k in range(TOP_K):
        pltpu.make_async_copy(h_ref, xg_ref.at[pl.ds(0, tm * SUBLANES)], sem).wait()


def _dispatch_rows(slot_flat, h2):
    tm = TM_DISPATCH
    grid_spec = pltpu.PrefetchScalarGridSpec(
        num_scalar_prefetch=1,
        grid=(N_TOK // tm,),
        in_specs=[pl.BlockSpec((tm * SUBLANES, LANES), lambda i, s: (i, 0))],
        out_specs=pl.BlockSpec(memory_space=pl.ANY),
        scratch_shapes=[pltpu.SemaphoreType.DMA])
    return pl.pallas_call(
        _dispatch_kernel,
        grid_spec=grid_spec,
        out_shape=jax.ShapeDtypeStruct((MOE_ROWS * SUBLANES, LANES), jnp.uint32),
        compiler_params=_params("arbitrary"),
        name="moe_dispatch",
    )(slot_flat, h2)


BLOCK_TILE_ROWS = MOE_BM * SUBLANES


def _expert_kernel(blk0_ref, nblk_ref, cnt_ref, wg_ref, wu_ref, wd_ref, x_hbm, y_hbm,
                   wg_s, wu_s, wd_s, xbuf, ybuf, xsem, ysem):
    e = pl.program_id(0)
    first_blk = blk0_ref[e]
    n_blk = nblk_ref[e]
    count = cnt_ref[e]
    total = blk0_ref[N_EXPERTS - 1] + nblk_ref[N_EXPERTS - 1]

    def rows_of(g):
        return pl.ds(pl.multiple_of(g * BLOCK_TILE_ROWS, BLOCK_TILE_ROWS), BLOCK_TILE_ROWS)

    def x_copy(g, buf):
        return pltpu.make_async_copy(x_hbm.at[rows_of(g)], xbuf.at[buf], xsem.at[buf])

    def y_copy(g, buf):
        return pltpu.make_async_copy(ybuf.at[buf], y_hbm.at[rows_of(g)], ysem.at[buf])

    @pl.when((e == 0) & (total > 0))
    def _():
        x_copy(0, 0).start(priority=1)

    wg_s[...] = wg_ref[...].astype(BF16)
    wu_s[...] = wu_ref[...].astype(BF16)
    wd_s[...] = wd_ref[...].astype(BF16)

    def block(j, carry):
        g = first_blk + j
        buf = g % 2

        @pl.when(g + 1 < total)
        def _():
            x_copy(g + 1, 1 - buf).start(priority=1)

        x_copy(g, buf).wait()

        @pl.when(g >= 2)
        def _():
            y_copy(g - 2, buf).wait()

        x = _load_token_tiles(xbuf, (buf,), MOE_BM)
        row = lax.broadcasted_iota(jnp.int32, x.shape, 0)
        lo, hi = _unpack_rows(jnp.where(row < count - j * MOE_BM, x, jnp.uint32(0)))
        lo, hi = lo.astype(BF16), hi.astype(BF16)
        gate = _mm(lo, wg_s[:HALF_D, :]) + _mm(hi, wg_s[HALF_D:, :])
        up = _mm(lo, wu_s[:HALF_D, :]) + _mm(hi, wu_s[HALF_D:, :])
        _store_token_tiles(ybuf, (buf,), _pack_rows(_mm((_silu(gate) * up).astype(BF16), wd_s[...])))
        y_copy(g, buf).start(priority=1)
        return carry

    lax.fori_loop(0, n_blk, block, 0)

    @pl.when((e == N_EXPERTS - 1) & (total >= 2))
    def _():
        y_copy(total - 2, total % 2).wait()

    @pl.when((e == N_EXPERTS - 1) & (total >= 1))
    def _():
        y_copy(total - 1, (total - 1) % 2).wait()


def _routed_experts(xg, first_blk, n_blk, counts, layer, w_g, w_u, w_d):
    def weight(shape):
        return pl.BlockSpec((None, None) + shape, lambda e, *_: (layer, e, 0, 0))

    any_space = pl.BlockSpec(memory_space=pl.ANY)
    block_buf = pltpu.VMEM((2, BLOCK_TILE_ROWS, LANES), jnp.uint32)
    grid_spec = pltpu.PrefetchScalarGridSpec(
        num_scalar_prefetch=3,
        grid=(N_EXPERTS,),
        in_specs=[weight((D_MODEL, D_EXPERT)), weight((D_MODEL, D_EXPERT)), weight((D_EXPERT, D_MODEL)), any_space],
        out_specs=any_space,
        scratch_shapes=[pltpu.VMEM((D_MODEL, D_EXPERT), BF16), pltpu.VMEM((D_MODEL, D_EXPERT), BF16),
                        pltpu.VMEM((D_EXPERT, D_MODEL), BF16), block_buf, block_buf,
                        pltpu.SemaphoreType.DMA((2,)), pltpu.SemaphoreType.DMA((2,))])
    return pl.pallas_call(
        _expert_kernel,
        grid_spec=grid_spec,
        out_shape=jax.ShapeDtypeStruct((MOE_ROWS * SUBLANES, LANES), jnp.uint32),
        compiler_params=_params("arbitrary"),
        name="routed_experts",
    )(first_blk, n_blk, counts, w_g, w_u, w_d, xg)


def _ffn_out_kernel(slot_ref, h_ref, wt_ref, x1_ref, wg_ref, wu_ref, wd_ref, g2_ref, lng_ref, lnb_ref, y_ref,
                    oc_ref, ol_ref, ybuf, sems):
    tm = h_ref.shape[0] // SUBLANES
    i = pl.program_id(0)
    n = pl.num_programs(0)

    def gather(tile, buf):
        base = tile * tm

        for r in range(tm):
            for k in range(TOP_K):
                _row_copy(y_ref, slot_ref[base + (k * N_TOK + r)], ybuf.at[buf, k], r, sems.at[buf]).start()

    @pl.when(i == 0)
    def _():
        gather(0, 0)

    @pl.when(i + 1 < n)
    def _():
        gather(i + 1, (i + 1) % 2)

    lo, hi = _unpack_rows(_load_token_tiles(h_ref, (), tm))
    lo, hi = lo.astype(BF16), hi.astype(BF16)
    gate = _mm(lo, wg_ref[:HALF_D, :]) + _mm(hi, wg_ref[HALF_D:, :])
    up = _mm(lo, wu_ref[:HALF_D, :]) + _mm(hi, wu_ref[HALF_D:, :])
    y = _mm((_silu(gate) * up).astype(BF16), wd_ref[...])
    buf = i % 2
    for k in range(TOP_K):
        pltpu.make_async_copy(y_ref.at[pl.ds(0, tm * SUBLANES)], ybuf.at[buf, k], sems.at[buf]).wait()
    wt = wt_ref[...]
    y_lo, y_hi = y[:, :HALF_D], y[:, HALF_D:]
    for k in range(TOP_K):
        r_lo, r_hi = _unpack_rows(_load_token_tiles(ybuf, (buf, k), tm))
        y_lo = y_lo + r_lo * wt[:, k:k + 1]
        y_hi = y_hi + r_hi * wt[:, k:k + 1]
    y = jnp.concatenate([y_lo, y_hi], axis=1)
    out = _layernorm(ALPHA * x1_ref[...] + g2_ref[...] * y, lng_ref[...], lnb_ref[...])
    is_ctx = _is_ctx_tile(tm)

    @pl.when(is_ctx)
    def _():
        oc_ref[...] = out

    @pl.when(jnp.logical_not(is_ctx))
    def _():
        ol_ref[...] = out


def _ffn_out(slot_flat, h2, wt, x1, ws_g, ws_u, ws_d, mod, ln_g, ln_b, y_blk):
    tm = TM_FFN
    row = pl.BlockSpec((tm, D_MODEL), lambda i, s: (i, 0))
    vec = pl.BlockSpec((1, D_MODEL), lambda i, s: (0, 0))
    grid_spec = pltpu.PrefetchScalarGridSpec(
        num_scalar_prefetch=1,
        grid=(N_TOK // tm,),
        in_specs=[pl.BlockSpec((tm * SUBLANES, LANES), lambda i, s: (i, 0)),
                  pl.BlockSpec((tm, 8), lambda i, s: (i, 0)), row,
                  pl.BlockSpec((D_MODEL, D_EXPERT), lambda i, s: (0, 0)),
                  pl.BlockSpec((D_MODEL, D_EXPERT), lambda i, s: (0, 0)),
                  pl.BlockSpec((D_EXPERT, D_MODEL), lambda i, s: (0, 0)),
                  _mod_spec(5, tm), vec, vec,
                  pl.BlockSpec(memory_space=pl.ANY)],
        out_specs=[_ctx_rows(tm), _lat_rows(tm)],
        scratch_shapes=[pltpu.VMEM((2, TOP_K, tm * SUBLANES, LANES), jnp.uint32),
                        pltpu.SemaphoreType.DMA((2,))])
    return pl.pallas_call(
        _ffn_out_kernel,
        grid_spec=grid_spec,
        out_shape=[jax.ShapeDtypeStruct((N_CTX, D_MODEL), F32), jax.ShapeDtypeStruct((N_LAT, D_MODEL), F32)],
        compiler_params=_params("arbitrary"),
        name="shared_expert_out",
    )(slot_flat, h2, wt, x1, ws_g, ws_u, ws_d, mod, ln_g, ln_b, y_blk)


def _dispatch_plan(idx, rank, counts):
    bm = MOE_BM
    counts = counts.reshape(N_EXPERTS).astype(jnp.int32)
    padded = (counts + bm - 1) // bm * bm
    pad_end = jnp.cumsum(padded)
    pad_start = pad_end - padded
    experts = jnp.arange(N_EXPERTS, dtype=jnp.int32)[:, None, None]
    first_row = jnp.sum(jnp.where(idx[None, :TOP_K] == experts, pad_start[:, None, None], 0), axis=0)
    slot = first_row + rank[:TOP_K]
    return slot.reshape(-1).astype(jnp.int32), pad_start // bm, padded // bm, counts


def _rope_tables():
    t = jnp.arange(DEC_SEQ)
    row = (t // GRID_W).astype(F32)
    col = (t % GRID_W).astype(F32)
    nf = HEAD_DIM // 4
    inv = ROPE_BASE ** (-jnp.arange(nf, dtype=F32) / nf)
    ar = row[:, None] * inv[None, :]
    ac = col[:, None] * inv[None, :]
    ang = jnp.concatenate([ar, ar, ac, ac], axis=-1)
    ang = jnp.concatenate([ang, ang], axis=-1)
    cos, sin = jnp.cos(ang), jnp.sin(ang)
    first = (jnp.arange(LANES) % (2 * nf)) < nf
    return cos, jnp.where(first, -sin, 0.0), jnp.where(first, 0.0, sin)


def _lambda_init(l):
    return 0.8 - 0.6 * math.exp(-0.3 * l)


def _gate_pairs(w_r, w_i):
    per = LANES // C_BW
    eye = jnp.eye(per, dtype=w_r.dtype)

    def chunk_diag(w):
        w = w.reshape(C_BLOCKS // per, per, C_BW, C_BW)
        return (eye[None, :, None, :, None] * w[:, :, :, None, :]).reshape(C_BLOCKS // per, LANES, LANES)

    return jnp.concatenate([chunk_diag(w_r), chunk_diag(w_i)], axis=2).reshape(C_WIDTH, 2 * LANES)


def kernel(x_prompt, x_sample, c, cache_a_k, cache_a_v, cache_b_k, cache_b_v, state_c, cache_d_k, cache_d_v, c_ctx, w_mod, b_mod, w_in, a_sink, b_lambda, c_conv_w, c_conv_b, c_w_rgate, c_b_rgate, c_w_igate, c_b_igate, c_lambda, d_rpb, g_mix, w_out, ln_g, ln_b, w_router, b_router, w_e_gate, w_e_up, w_e_down, w_s_gate, w_s_up, w_s_down):
    x_ctx, x_lat = x_prompt.reshape(N_CTX, D_MODEL), x_sample.reshape(N_LAT, D_MODEL)
    cond8 = jnp.zeros((8, D_MODEL), F32).at[0].set(c_ctx).at[1:1 + DEC_BATCH].set(c)
    mod_all = _modulation(cond8, w_mod, b_mod)
    cos, sin_a, sin_b = _rope_tables()
    row_sel, col_sel, na_valid = _na_bias_indices()

    order = np.concatenate([np.arange(0, 512), np.arange(768, 4864), np.arange(512, 768)])

    ctx_out = [[] for _ in range(7)]
    for l in range(DEPTH):
        lam_init = _lambda_init(l)
        mod = mod_all[l, :N_GROUPS].reshape(N_GROUPS, 6, 1, D_MODEL)
        w_in_b = w_in[l][:, order].astype(BF16)
        slab = _in_projection(x_ctx, x_lat, mod, w_in_b)

        lp = b_lambda[l]
        lam = jnp.exp(jnp.sum(lp[0] * lp[1])) - jnp.exp(jnp.sum(lp[2] * lp[3])) + lam_init
        scal = jnp.zeros((16,), F32).at[:A_HEADS].set(a_sink[l]).at[A_HEADS].set(lam)
        g2d = g_mix[l].reshape(1, -1)

        oa, ob, od = _ctx_attention(slab, scal, g2d, lam_init)

        qa_p, ka_r, va_b, qb_p, kb_r, vb_b, qd_p, kd_b, vd_b = _latent_prep(slab, cos, sin_a, sin_b)
        oa = _window_attention(qa_p, ka_r, va_b, cache_a_k[:, l].reshape(DEC_BATCH, PAST_LEN, LANES),
                               cache_a_v[:, l].reshape(DEC_BATCH, PAST_LEN, LANES), scal, g2d, oa)
        ob = _diff_attention(qb_p, kb_r, vb_b, cache_b_k[:, l].reshape(DEC_BATCH, PAST_LEN, 512),
                             cache_b_v[:, l].reshape(DEC_BATCH, PAST_LEN, 512), scal, g2d, lam_init, ob)
        bias = _na_bias(d_rpb[l], row_sel, col_sel, na_valid)
        od = _neighbourhood_attention(qd_p, kd_b, vd_b, cache_d_k[:, l].reshape(DEC_BATCH, PAST_LEN, 512),
                                      cache_d_v[:, l].reshape(DEC_BATCH, PAST_LEN, 512), bias, g2d, od)

        w_gates = jnp.stack([_gate_pairs(c_w_rgate[l, d], c_w_igate[l, d]) for d in range(2)])
        b_gates = jnp.concatenate([c_b_rgate[l], c_b_igate[l]], axis=-1).reshape(2, 1, 2 * C_WIDTH)
        zeros = jnp.zeros((BATCH, 1, C_WIDTH), F32)
        h0_f = jnp.concatenate([zeros, state_c[:, l, 0][:, None, :]], axis=0)
        h0_b = jnp.concatenate([zeros, state_c[:, l, 1][:, None, :]], axis=0)
        oc, fin_f, fin_b = _rglru(slab, c_conv_w[l], c_conv_b[l].reshape(1, -1), w_gates, b_gates,
                                  c_lambda[l].reshape(2, 1, C_WIDTH), h0_f, h0_b, g2d)

        x1, h2, idx, wt, rank, counts = _mixer_out(oa, ob, oc, od, w_out[l].astype(BF16), x_ctx, x_lat, mod,
                                                   ln_g[l, 0].reshape(1, -1), ln_b[l, 0].reshape(1, -1),
                                                   w_router[l], b_router[l].reshape(1, -1))

        slot, first_blk, n_blk, counts = _dispatch_plan(idx, rank, counts)
        xg = _dispatch_rows(slot, h2)
        y_blk = _routed_experts(xg, first_blk, n_blk, counts, l, w_e_gate, w_e_up, w_e_down)
        x_ctx, x_lat = _ffn_out(slot, h2, wt, x1, w_s_gate[l].astype(BF16), w_s_up[l].astype(BF16),
                                w_s_down[l].astype(BF16), mod, ln_g[l, 1].reshape(1, -1),
                                ln_b[l, 1].reshape(1, -1), y_blk)

        ctx = slab[:N_CTX]
        ctx_out[0].append(ctx[:, COL_KA * LANES:(COL_KA + 1) * LANES].reshape(BATCH, SEQ, A_KV_HEADS, HEAD_DIM))
        ctx_out[1].append(ctx[:, COL_VA * LANES:(COL_VA + 1) * LANES].reshape(BATCH, SEQ, A_KV_HEADS, HEAD_DIM))
        ctx_out[2].append(ctx[:, COL_KB * 512:(COL_KB + 1) * 512].reshape(BATCH, SEQ, B_HEADS, 2, HEAD_DIM))
        ctx_out[3].append(ctx[:, COL_VB * 512:(COL_VB + 1) * 512].reshape(BATCH, SEQ, B_HEADS, 2 * HEAD_DIM))
        ctx_out[4].append(jnp.concatenate([fin_f[:BATCH], fin_b[:BATCH]], axis=1))
        ctx_out[5].append(ctx[:, COL_KD * 512:(COL_KD + 1) * 512].reshape(BATCH, SEQ, D_HEADS, HEAD_DIM))
        ctx_out[6].append(ctx[:, COL_VD * 512:(COL_VD + 1) * 512].reshape(BATCH, SEQ, D_HEADS, HEAD_DIM))

    new = [jnp.stack(t, axis=1) for t in ctx_out]
    return (x_ctx.reshape(BATCH, SEQ, D_MODEL), x_lat.reshape(DEC_BATCH, DEC_SEQ, D_MODEL), *new)
```

```python
import functools
import math

import numpy as np
import jax
import jax.numpy as jnp
from jax import lax
from jax.experimental import pallas as pl
from jax.experimental.pallas import tpu as pltpu

F32 = jnp.float32
BF16 = jnp.bfloat16

D_MODEL = 2048
BATCH = 16
SEQ = 256
DEPTH = 2
DEC_BATCH = 2
DEC_SEQ = 4096
PAST_LEN = 256
GRID_W = 64
HEAD_DIM = 64
ROPE_BASE = 10000.0
A_HEADS = 8
A_KV_HEADS = 2
A_WINDOW = 128
B_HEADS = 4
C_WIDTH = 512
C_BLOCKS = 8
C_BW = C_WIDTH // C_BLOCKS
C_POW = 8.0
D_HEADS = 8
NA_ROWS = 8
NA_COLS = 16
N_EXPERTS = 64
TOP_K = 6
D_EXPERT = 512
ROUTED_SCALE = 2.5
ALPHA = (2.0 * DEPTH) ** 0.25
LN_EPS = 1e-5
NORM_EPS = 1e-6
NEG_INF = -1e30
SCALE = HEAD_DIM ** -0.5
LOG2E = math.log2(math.e)

N_CTX = BATCH * SEQ
N_LAT = DEC_BATCH * DEC_SEQ
N_TOK = N_CTX + N_LAT
GROUP_ROWS = 4096
N_GROUPS = N_TOK // GROUP_ROWS
IN_WIDTH = 4864
LANES = 128
VMEM_LIMIT = 56 * 1024 * 1024

COL_QA, COL_QB, COL_KB, COL_VB, COL_XC, COL_GC, COL_QD, COL_KD, COL_VD = range(9)
COL_KA, COL_VA = 36, 37

TM_PROJ = 512
TN_PROJ = IN_WIDTH // 2
TM_PREP = 256
TQ_A = 128
TQ_B = 256
NA_QR = 4
NA_KR = 12
T_SCAN = 256
TM_OUT = 256
MOE_BM = 256
TM_DISPATCH = 128
TM_FFN = 128


def _params(*sem):
    return pltpu.CompilerParams(dimension_semantics=sem, vmem_limit_bytes=VMEM_LIMIT)


def _nt(a, b):
    return lax.dot_general(a, b, (((1,), (1,)), ((), ())), preferred_element_type=F32)


def _mm(a, b):
    return jnp.dot(a, b, preferred_element_type=F32)


def _softmax_unnorm(parts, sink=None):
    m = parts[0].max(axis=-1, keepdims=True)
    for p in parts[1:]:
        m = jnp.maximum(m, p.max(axis=-1, keepdims=True))
    if sink is not None:
        m = jnp.maximum(m, sink)
    es = [jnp.exp(p - m) for p in parts]
    den = es[0].sum(axis=-1, keepdims=True)
    for e in es[1:]:
        den = den + e.sum(axis=-1, keepdims=True)
    if sink is not None:
        den = den + jnp.exp(sink - m)
    return es, den


def _rms_gain(x, g):
    return x * lax.rsqrt(jnp.mean(x * x, axis=-1, keepdims=True) + NORM_EPS) * g


def _layernorm(z, g, b):
    zc = z - jnp.mean(z, axis=-1, keepdims=True)
    var = jnp.mean(zc * zc, axis=-1, keepdims=True)
    return zc * lax.rsqrt(var + LN_EPS) * g + b


def _silu(x):
    return x * jax.nn.sigmoid(x)


def _low_half(shape):
    return lax.broadcasted_iota(jnp.int32, shape, 1) < HEAD_DIM


HALF_D = D_MODEL // 2
HIGH16 = 0xFFFF0000


def _pack_rows(x):
    bits = pltpu.bitcast(x.astype(BF16).astype(F32), jnp.uint32)
    return (bits[:, :HALF_D] >> 16) | (bits[:, HALF_D:] & jnp.uint32(HIGH16))


def _unpack_rows(p):
    return (pltpu.bitcast(p << 16, F32), pltpu.bitcast(p & jnp.uint32(HIGH16), F32))


SUBLANES = 8
assert HALF_D == SUBLANES * LANES


def _store_token_tiles(ref, index, packed, first=0):
    m = packed.shape[0]
    for s in range(SUBLANES):
        rows = pl.ds(first * SUBLANES + s, m, stride=SUBLANES)
        ref[index + (rows, slice(None))] = packed[:, s * LANES:(s + 1) * LANES]


def _load_token_tiles(ref, index, m):
    return jnp.concatenate([ref[index + (pl.ds(s, m, stride=SUBLANES), slice(None))] for s in range(SUBLANES)],
                           axis=1)


def _mod_kernel(c_ref, w_ref, b_ref, o_ref):
    c = c_ref[...]
    o_ref[0] = jnp.dot(_silu(c), w_ref[0], precision=lax.Precision.HIGHEST,
                       preferred_element_type=F32) + b_ref[0]


def _modulation(cond8, w_mod, b_mod):
    tn = 1024
    n = w_mod.shape[-1]
    return pl.pallas_call(
        _mod_kernel,
        grid=(DEPTH, n // tn),
        in_specs=[pl.BlockSpec((8, D_MODEL), lambda l, j: (0, 0)),
                  pl.BlockSpec((1, D_MODEL, tn), lambda l, j: (l, 0, j)),
                  pl.BlockSpec((1, 1, tn), lambda l, j: (l, 0, j))],
        out_specs=pl.BlockSpec((1, 8, tn), lambda l, j: (l, 0, j)),
        out_shape=jax.ShapeDtypeStruct((DEPTH, 8, n), F32),
        compiler_params=_params("parallel", "parallel"),
        name="modulation",
    )(cond8, w_mod, b_mod.reshape(DEPTH, 1, n))


def _mod_spec(which, tm):
    return pl.BlockSpec((None, None, 1, D_MODEL), lambda i, *_: (i * tm // GROUP_ROWS, which, 0, 0))


def _ctx_rows(tm):
    return pl.BlockSpec((tm, D_MODEL), lambda i, *_: (jnp.minimum(i, N_CTX // tm - 1), 0))


def _lat_rows(tm):
    return pl.BlockSpec((tm, D_MODEL), lambda i, *_: (jnp.maximum(i - N_CTX // tm, 0), 0))


def _is_ctx_tile(tm):
    return pl.program_id(0) * tm < N_CTX


def _inproj_kernel(xc_ref, xl_ref, sc_ref, sh_ref, w_ref, o_ref, xb_ref):
    @pl.when(pl.program_id(1) == 0)
    def _():
        x = jnp.where(_is_ctx_tile(xc_ref.shape[0]), xc_ref[...], xl_ref[...])
        xb_ref[...] = (x * (1.0 + sc_ref[...]) + sh_ref[...]).astype(BF16)

    o_ref[...] = _mm(xb_ref[...], w_ref[...])


def _in_projection(x_ctx, x_lat, mod, w_in_b):
    tm, tn = TM_PROJ, TN_PROJ
    return pl.pallas_call(
        _inproj_kernel,
        grid=(N_TOK // tm, IN_WIDTH // tn),
        in_specs=[_ctx_rows(tm), _lat_rows(tm),
                  _mod_spec(1, tm), _mod_spec(0, tm),
                  pl.BlockSpec((D_MODEL, tn), lambda i, j: (0, j))],
        out_specs=pl.BlockSpec((tm, tn), lambda i, j: (i, j)),
        out_shape=jax.ShapeDtypeStruct((N_TOK, IN_WIDTH), F32),
        scratch_shapes=[pltpu.VMEM((tm, D_MODEL), BF16)],
        compiler_params=_params("parallel", "arbitrary"),
        name="in_projection",
    )(x_ctx, x_lat, mod, mod, w_in_b)


def _ctx_attn_kernel(scal_ref, qa_ref, ka_ref, va_ref, qb_ref, kb_ref, vb_ref, qd_ref, kd_ref, vd_ref,
                     ga_ref, gb_ref, gd_ref, oa_ref, ob_ref, od_ref, *, lam_init):
    L = SEQ
    lo = _low_half((L, LANES))

    ka = ka_ref[...].astype(BF16)
    va = va_ref[...].astype(BF16)
    qa = qa_ref[...]
    chunks = []
    for c in range(A_HEADS // 2):
        hk = c // 2
        chunk = qa[:, c * LANES:(c + 1) * LANES]
        keep = lo if hk == 0 else jnp.logical_not(lo)
        halves = []
        for half in range(2):
            x = chunk if half == hk else pltpu.roll(chunk, HEAD_DIM, 1)
            qh = jnp.where(keep, x, 0.0).astype(BF16)
            s = _nt(qh, ka) * SCALE
            (e,), den = _softmax_unnorm([s], scal_ref[2 * c + half])
            o = _mm(e.astype(BF16), va) / den
            halves.append(o if half == hk else pltpu.roll(o, HEAD_DIM, 1))
        chunks.append(jnp.where(lo, halves[0], halves[1]))
    oa = jnp.concatenate(chunks, axis=1)
    oa_ref[...] = _rms_gain(oa, ga_ref[...]).astype(BF16)

    lam = scal_ref[A_HEADS]
    qb = qb_ref[...]
    chunks = []
    for h in range(B_HEADS):
        sl = slice(h * LANES, (h + 1) * LANES)
        qc = qb[:, sl]
        kc = kb_ref[:, sl].astype(BF16)
        vh = vb_ref[:, sl].astype(BF16)
        (e1,), d1 = _softmax_unnorm([_nt(jnp.where(lo, qc, 0.0).astype(BF16), kc) * SCALE])
        (e2,), d2 = _softmax_unnorm([_nt(jnp.where(lo, 0.0, qc).astype(BF16), kc) * SCALE])
        w = e1 * (1.0 / d1) - e2 * (lam / d2)
        o = _mm(w.astype(BF16), vh)
        chunks.append(_rms_gain(o, gb_ref[:, sl]) * (1.0 - lam_init))
    ob_ref[...] = jnp.concatenate(chunks, axis=1).astype(BF16)

    qd = qd_ref[...]
    chunks = []
    for c in range(D_HEADS // 2):
        sl = slice(c * LANES, (c + 1) * LANES)
        qc = qd[:, sl]
        kc = kd_ref[:, sl].astype(BF16)
        vc = vd_ref[:, sl].astype(BF16)
        halves = []
        for half in range(2):
            qh = jnp.where(lo if half == 0 else jnp.logical_not(lo), qc, 0.0).astype(BF16)
            (e,), den = _softmax_unnorm([_nt(qh, kc) * SCALE])
            halves.append(_mm(e.astype(BF16), vc) / den)
        chunks.append(jnp.where(lo, halves[0], halves[1]))
    od = jnp.concatenate(chunks, axis=1)
    od_ref[...] = _rms_gain(od, gd_ref[...]).astype(BF16)


def _ctx_attention(slab, scal, g_mix2d, lam_init):
    L = SEQ

    def wide(col):
        return pl.BlockSpec((L, 512), lambda b: (b, col))

    def narrow(col):
        return pl.BlockSpec((L, LANES), lambda b: (b, col))

    def gain(col):
        return pl.BlockSpec((1, 512), lambda b: (0, col))

    out = jax.ShapeDtypeStruct((N_TOK, 512), BF16)
    ospec = pl.BlockSpec((L, 512), lambda b: (b, 0))
    return pl.pallas_call(
        functools.partial(_ctx_attn_kernel, lam_init=lam_init),
        grid=(BATCH,),
        in_specs=[pl.BlockSpec(memory_space=pltpu.SMEM),
                  wide(COL_QA), narrow(COL_KA), narrow(COL_VA),
                  wide(COL_QB), wide(COL_KB), wide(COL_VB),
                  wide(COL_QD), wide(COL_KD), wide(COL_VD),
                  gain(0), gain(1), gain(3)],
        out_specs=[ospec, ospec, ospec],
        out_shape=[out, out, out],
        compiler_params=_params("parallel"),
        name="ctx_attention",
    )(scal, slab, slab, slab, slab, slab, slab, slab, slab, slab, g_mix2d, g_mix2d, g_mix2d)


def _prep_kernel(qa_ref, ka_ref, va_ref, qb_ref, kb_ref, vb_ref, qd_ref, kd_ref, vd_ref,
                 cos_ref, sa_ref, sb_ref,
                 qa_o, ka_o, va_o, qb_o, kb_o, vb_o, qd_o, kd_o, vd_o):
    cos, sa, sb = cos_ref[...], sa_ref[...], sb_ref[...]
    lo = _low_half(cos.shape)
    hi = jnp.logical_not(lo)

    def rope(x):
        return (x * cos + pltpu.roll(x, LANES - HEAD_DIM // 4, 1) * sa
                + pltpu.roll(x, HEAD_DIM // 4, 1) * sb)

    ka_o[...] = rope(ka_ref[...]).astype(BF16)
    va_o[...] = va_ref[...].astype(BF16)
    vb_o[...] = vb_ref[...].astype(BF16)
    kd_o[...] = kd_ref[...].astype(BF16)
    vd_o[...] = vd_ref[...].astype(BF16)

    for c in range(A_HEADS // 2):
        hk = c // 2
        r = rope(qa_ref[:, c * LANES:(c + 1) * LANES])
        keep = lo if hk == 0 else hi
        for half in range(2):
            h = 2 * c + half
            x = r if half == hk else pltpu.roll(r, HEAD_DIM, 1)
            qa_o[:, h * LANES:(h + 1) * LANES] = jnp.where(keep, x, 0.0).astype(BF16)
    for h in range(B_HEADS):
        sl = slice(h * LANES, (h + 1) * LANES)
        r = rope(qb_ref[:, sl])
        kb_o[:, sl] = rope(kb_ref[:, sl]).astype(BF16)
        qb_o[:, (2 * h) * LANES:(2 * h + 1) * LANES] = jnp.where(lo, r, 0.0).astype(BF16)
        qb_o[:, (2 * h + 1) * LANES:(2 * h + 2) * LANES] = jnp.where(hi, r, 0.0).astype(BF16)
    for c in range(D_HEADS // 2):
        x = qd_ref[:, c * LANES:(c + 1) * LANES]
        qd_o[:, (2 * c) * LANES:(2 * c + 1) * LANES] = jnp.where(lo, x, 0.0).astype(BF16)
        qd_o[:, (2 * c + 1) * LANES:(2 * c + 2) * LANES] = jnp.where(hi, x, 0.0).astype(BF16)


def _latent_prep(slab, cos, sa, sb):
    tm = TM_PREP
    off = N_CTX // tm
    per_seq = DEC_SEQ // tm

    def wide(col):
        return pl.BlockSpec((tm, 512), lambda i: (i + off, col))

    def narrow(col):
        return pl.BlockSpec((tm, LANES), lambda i: (i + off, col))

    tab = pl.BlockSpec((tm, LANES), lambda i: (i % per_seq, 0))

    def out(width):
        return (pl.BlockSpec((tm, width), lambda i: (i, 0)), jax.ShapeDtypeStruct((N_LAT, width), BF16))

    outs = [out(1024), out(LANES), out(LANES), out(1024), out(512), out(512), out(1024), out(512), out(512)]
    return pl.pallas_call(
        _prep_kernel,
        grid=(N_LAT // tm,),
        in_specs=[wide(COL_QA), narrow(COL_KA), narrow(COL_VA), wide(COL_QB), wide(COL_KB), wide(COL_VB),
                  wide(COL_QD), wide(COL_KD), wide(COL_VD), tab, tab, tab],
        out_specs=[o[0] for o in outs],
        out_shape=[o[1] for o in outs],
        compiler_params=_params("parallel"),
        name="latent_prep",
    )(slab, slab, slab, slab, slab, slab, slab, slab, slab, cos, sa, sb)


def _win_attn_kernel(scal_ref, q_ref, kp_ref, kc_ref, kn_ref, vp_ref, vc_ref, vn_ref, kctx_ref, vctx_ref,
                     g_ref, _all_rows_ref, o_ref):
    n = pl.program_id(1)
    nb = pl.num_programs(1)
    tq = TQ_A
    group = A_HEADS // A_KV_HEADS
    k = jnp.concatenate([kp_ref[...], kc_ref[...], kn_ref[...], kctx_ref[...].astype(BF16)], axis=0)
    v = jnp.concatenate([vp_ref[...], vc_ref[...], vn_ref[...], vctx_ref[...].astype(BF16)], axis=0)
    nk = 3 * tq + PAST_LEN
    row = lax.broadcasted_iota(jnp.int32, (group * tq, nk), 0) & (tq - 1)
    col = lax.broadcasted_iota(jnp.int32, (group * tq, nk), 1)
    prev_thr = row + jnp.where(n > 0, 0, tq)
    next_thr = row + 2 * tq - jnp.where(n < nb - 1, 0, tq)
    masked = ((col < tq) & (col < prev_thr)) | ((col >= 2 * tq) & (col < 3 * tq) & (col > next_thr))
    valid = jnp.logical_not(masked)
    lo = _low_half((tq, LANES))
    heads = []
    for hk in range(A_KV_HEADS):
        q4 = jnp.concatenate([q_ref[:, (hk * group + g) * LANES:(hk * group + g + 1) * LANES]
                              for g in range(group)], axis=0)
        s = jnp.where(valid, _nt(q4, k) * SCALE, NEG_INF)
        sink = jnp.concatenate([jnp.full((tq, 1), scal_ref[hk * group + g], F32) for g in range(group)], axis=0)
        (e,), den = _softmax_unnorm([s], sink)
        o = _mm(e.astype(BF16), v) / den
        for g in range(group):
            h = hk * group + g
            og = o[g * tq:(g + 1) * tq]
            heads.append(og if (h % 2) == hk else pltpu.roll(og, HEAD_DIM, 1))
    oa = jnp.concatenate([jnp.where(lo, heads[2 * c], heads[2 * c + 1]) for c in range(A_HEADS // 2)], axis=1)
    o_ref[...] = _rms_gain(oa, g_ref[...]).astype(BF16)


def _window_attention(qa_p, ka_r, va_b, cache_k, cache_v, scal, g_mix2d, o_all):
    tq = TQ_A
    nb = DEC_SEQ // tq

    def band(d):
        return pl.BlockSpec((tq, LANES), lambda b, n: (b * nb + jnp.clip(n + d, 0, nb - 1), 0))

    ctx = pl.BlockSpec((None, PAST_LEN, LANES), lambda b, n: (b, 0, 0))
    return pl.pallas_call(
        _win_attn_kernel,
        grid=(DEC_BATCH, nb),
        in_specs=[pl.BlockSpec(memory_space=pltpu.SMEM),
                  pl.BlockSpec((tq, A_HEADS * LANES), lambda b, n: (b * nb + n, 0)),
                  band(-1), band(0), band(1), band(-1), band(0), band(1), ctx, ctx,
                  pl.BlockSpec((1, 512), lambda b, n: (0, 0)),
                  pl.BlockSpec(memory_space=pl.ANY)],
        out_specs=pl.BlockSpec((tq, 512), lambda b, n: (N_CTX // tq + b * nb + n, 0)),
        out_shape=jax.ShapeDtypeStruct((N_TOK, 512), BF16),
        input_output_aliases={11: 0},
        compiler_params=_params("parallel", "parallel"),
        name="window_attention",
    )(scal, qa_p, ka_r, ka_r, ka_r, va_b, va_b, va_b, cache_k, cache_v, g_mix2d, o_all)


def _diff_attn_kernel(scal_ref, q_ref, k_ref, v_ref, kctx_ref, vctx_ref, g_ref, _all_rows_ref, o_ref, *,
                      lam_init):
    lam = scal_ref[A_HEADS]
    chunks = []
    for h in range(B_HEADS):
        sl = slice(h * LANES, (h + 1) * LANES)
        kh = k_ref[:, sl]
        kc = kctx_ref[:, sl].astype(BF16)
        es, dens = [], []
        for m in range(2):
            q = q_ref[:, (2 * h + m) * LANES:(2 * h + m + 1) * LANES]
            t_lat = _nt(q, kh) * (SCALE * LOG2E)
            t_ctx = _nt(q, kc) * (SCALE * LOG2E)
            mx = jnp.maximum(t_lat.max(axis=-1, keepdims=True), t_ctx.max(axis=-1, keepdims=True))
            e_lat = jnp.exp2(t_lat - mx)
            e_ctx = jnp.exp2(t_ctx - mx)
            es.append((e_lat, e_ctx))
            dens.append(e_lat.sum(axis=-1, keepdims=True) + e_ctx.sum(axis=-1, keepdims=True))
        c1 = 1.0 / dens[0]
        c2 = lam / dens[1]
        w_lat = (es[0][0] * c1 - es[1][0] * c2).astype(BF16)
        w_ctx = (es[0][1] * c1 - es[1][1] * c2).astype(BF16)
        o = _mm(w_lat, v_ref[:, sl]) + _mm(w_ctx, vctx_ref[:, sl].astype(BF16))
        chunks.append(_rms_gain(o, g_ref[:, sl]) * (1.0 - lam_init))
    o_ref[...] = jnp.concatenate(chunks, axis=1).astype(BF16)


def _diff_attention(qb_p, kb_r, vb_b, cache_k, cache_v, scal, g_mix2d, lam_init, o_all):
    tq = TQ_B
    nb = DEC_SEQ // tq
    full = pl.BlockSpec((DEC_SEQ, 512), lambda b, n: (b, 0))
    ctx = pl.BlockSpec((None, PAST_LEN, 512), lambda b, n: (b, 0, 0))
    return pl.pallas_call(
        functools.partial(_diff_attn_kernel, lam_init=lam_init),
        grid=(DEC_BATCH, nb),
        in_specs=[pl.BlockSpec(memory_space=pltpu.SMEM),
                  pl.BlockSpec((tq, 2 * B_HEADS * LANES), lambda b, n: (b * nb + n, 0)),
                  full, full, ctx, ctx,
                  pl.BlockSpec((1, 512), lambda b, n: (0, 1)),
                  pl.BlockSpec(memory_space=pl.ANY)],
        out_specs=pl.BlockSpec((tq, 512), lambda b, n: (N_CTX // tq + b * nb + n, 0)),
        out_shape=jax.ShapeDtypeStruct((N_TOK, 512), BF16),
        input_output_aliases={7: 0},
        compiler_params=_params("parallel", "parallel"),
        name="diff_attention",
    )(scal, qb_p, kb_r, vb_b, cache_k, cache_v, g_mix2d, o_all)


def _na_window_start(step):
    return np.clip(step * NA_QR - NA_ROWS // 2, 0, GRID_W - NA_KR)


def _na_bias_indices():
    rows = DEC_SEQ // GRID_W
    steps = rows // NA_QR
    pats = []
    for step in range(steps):
        w0 = _na_window_start(step)
        r = step * NA_QR + np.arange(NA_QR)[:, None, None, None]
        c = np.arange(GRID_W)[None, :, None, None]
        kr = w0 + np.arange(NA_KR)[None, None, :, None]
        kc = np.arange(GRID_W)[None, None, None, :]
        kr0 = np.clip(r - NA_ROWS // 2, 0, rows - NA_ROWS)
        kc0 = np.clip(c - NA_COLS // 2, 0, GRID_W - NA_COLS)
        valid = (kr >= kr0) & (kr < kr0 + NA_ROWS) & (kc >= kc0) & (kc < kc0 + NA_COLS)
        drow = np.clip(kr - r + NA_ROWS - 1, 0, 2 * NA_ROWS - 2)
        dcol = np.clip(kc - c + NA_COLS - 1, 0, 2 * NA_COLS - 2)
        shape = (NA_QR * GRID_W, NA_KR * GRID_W)
        full = np.broadcast_to
        pats.append((full(drow, valid.shape).reshape(shape), full(dcol, valid.shape).reshape(shape),
                     valid.reshape(shape)))
    for step in range(2, steps - 1):
        for a, b in zip(pats[1], pats[step]):
            assert np.array_equal(a, b)
    kinds = [pats[0], pats[1], pats[-1]]
    drow, dcol, valid = (np.stack([k[i] for k in kinds]) for i in range(3))
    shape6 = (len(kinds), NA_QR, GRID_W, NA_KR, GRID_W)
    row_sel = np.eye(2 * NA_ROWS - 1, dtype=np.float32)[drow.reshape(shape6)[:, :, 0, :, 0]]
    col_sel = np.eye(2 * NA_COLS - 1, dtype=np.float32)[dcol.reshape(shape6)[0, 0, :, 0, :]]
    return row_sel, col_sel, valid


def _na_bias(rpb, row_sel, col_sel, valid):
    hp = lax.Precision.HIGHEST
    cols = jnp.einsum('hab,cdb->hacd', rpb, col_sel, precision=hp)
    bias = jnp.einsum('kria,hacd->khrcid', row_sel, cols, precision=hp)
    bias = bias.reshape(valid.shape[0], D_HEADS, NA_QR * GRID_W, NA_KR * GRID_W)
    return jnp.where(valid[:, None], bias, NEG_INF)


def _na_kernel(q_ref, k_ref, v_ref, kctx_ref, vctx_ref, bias_ref, g_ref, _all_rows_ref, o_ref):
    step = pl.program_id(1)
    nq = NA_QR * GRID_W
    nk = NA_KR * GRID_W
    w0 = jnp.clip(step * NA_QR - NA_ROWS // 2, 0, GRID_W - NA_KR)
    start = pl.multiple_of(w0 * GRID_W, GRID_W)
    lo = _low_half((nq, LANES))
    chunks = []
    for c in range(D_HEADS // 2):
        sl = slice(c * LANES, (c + 1) * LANES)
        kw = k_ref[pl.ds(start, nk), sl]
        vw = v_ref[pl.ds(start, nk), sl]
        kc = kctx_ref[:, sl].astype(BF16)
        vc = vctx_ref[:, sl].astype(BF16)
        halves = []
        for half in range(2):
            h = 2 * c + half
            q = q_ref[:, h * LANES:(h + 1) * LANES]
            s_loc = _nt(q, kw) * SCALE + bias_ref[h]
            s_ctx = _nt(q, kc) * SCALE
            (e_loc, e_ctx), den = _softmax_unnorm([s_loc, s_ctx])
            halves.append((_mm(e_loc.astype(BF16), vw) + _mm(e_ctx.astype(BF16), vc)) / den)
        chunks.append(jnp.where(lo, halves[0], halves[1]))
    od = jnp.concatenate(chunks, axis=1)
    o_ref[...] = _rms_gain(od, g_ref[...]).astype(BF16)


def _neighbourhood_attention(qd_p, kd_b, vd_b, cache_k, cache_v, bias, g_mix2d, o_all):
    nq = NA_QR * GRID_W
    steps = DEC_SEQ // nq
    full = pl.BlockSpec((DEC_SEQ, 512), lambda b, j: (b, 0))
    ctx = pl.BlockSpec((None, PAST_LEN, 512), lambda b, j: (b, 0, 0))

    def kind(b, j):
        return (jnp.where(j == 0, 0, jnp.where(j == steps - 1, 2, 1)), 0, 0, 0)

    return pl.pallas_call(
        _na_kernel,
        grid=(DEC_BATCH, steps),
        in_specs=[pl.BlockSpec((nq, D_HEADS * LANES), lambda b, j: (b * steps + j, 0)),
                  full, full, ctx, ctx,
                  pl.BlockSpec((None, D_HEADS, nq, NA_KR * GRID_W), kind),
                  pl.BlockSpec((1, 512), lambda b, j: (0, 3)),
                  pl.BlockSpec(memory_space=pl.ANY)],
        out_specs=pl.BlockSpec((nq, 512), lambda b, j: (N_CTX // nq + b * steps + j, 0)),
        out_shape=jax.ShapeDtypeStruct((N_TOK, 512), BF16),
        input_output_aliases={7: 0},
        compiler_params=_params("parallel", "arbitrary"),
        name="neighbourhood_attention",
    )(qd_p, kd_b, vd_b, cache_k, cache_v, bias, g_mix2d, o_all)


N_SCAN_TILES = N_TOK // T_SCAN
CTX_TILES = N_CTX // T_SCAN
TILES_PER_LAT = DEC_SEQ // T_SCAN


def _scan_tile_flags(i):
    is_ctx = i < CTX_TILES
    pos = (i - CTX_TILES) % TILES_PER_LAT
    return is_ctx | (pos == 0), is_ctx | (pos == TILES_PER_LAT - 1)


def _scan_seq(i):
    return jnp.where(i < CTX_TILES, i, CTX_TILES + (i - CTX_TILES) // TILES_PER_LAT)


def _conv_gates(x_ref, prev_ref, next_ref, cw_ref, cb_ref, wgh_ref, wgl_ref, bg_ref, lam_ref, first, last):
    x = x_ref[...]
    t = T_SCAN
    row = lax.broadcasted_iota(jnp.int32, x.shape, 0)
    pm1 = jnp.where(first, 0.0, prev_ref[7:8, :])
    n0 = jnp.where(last, 0.0, next_ref[0:1, :])
    n1 = jnp.where(last, 0.0, next_ref[1:2, :])
    x_m1 = jnp.where(row == 0, pm1, pltpu.roll(x, 1, 0))
    x_p1 = jnp.where(row == t - 1, n0, pltpu.roll(x, t - 1, 0))
    x_p2 = jnp.where(row == t - 2, n0, jnp.where(row == t - 1, n1, pltpu.roll(x, t - 2, 0)))
    y = (cw_ref[0:1, :] * x_m1 + cw_ref[1:2, :] * x + cw_ref[2:3, :] * x_p1 + cw_ref[3:4, :] * x_p2
         + cb_ref[...])
    y_hi = y.astype(BF16)
    y_lo = (y - y_hi.astype(F32)).astype(BF16)
    zr, zi = [], []
    for c in range(C_WIDTH // LANES):
        sl = slice(c * LANES, (c + 1) * LANES)
        w_hi = wgh_ref[0, sl, :]
        z = _mm(y_hi[:, sl], w_hi) + _mm(y_lo[:, sl], w_hi) + _mm(y_hi[:, sl], wgl_ref[0, sl, :])
        zr.append(z[:, :LANES])
        zi.append(z[:, LANES:])
    bias = bg_ref[0]
    r = jax.nn.sigmoid(jnp.concatenate(zr, axis=1) + bias[:, :C_WIDTH])
    gate_i = jax.nn.sigmoid(jnp.concatenate(zi, axis=1) + bias[:, C_WIDTH:])
    nl = -lam_ref[0]
    softplus = jnp.maximum(nl, 0.0) + jnp.log1p(jnp.exp(-jnp.abs(nl)))
    log_a = -C_POW * r * softplus
    a = jnp.exp(log_a)
    u = jnp.sqrt(1.0 - jnp.exp(2.0 * log_a)) * gate_i * y
    return a, u


def _scan_rows(a_s, u_s, h_s, h, reverse):
    groups = T_SCAN // 8

    def body(i, h):
        g = (groups - 1 - i) if reverse else i
        base = pl.multiple_of(g * 8, 8)
        a8 = a_s[pl.ds(base, 8), :]
        u8 = u_s[pl.ds(base, 8), :]
        rows = [None] * 8
        for k in (range(7, -1, -1) if reverse else range(8)):
            h = a8[k:k + 1, :] * h + u8[k:k + 1, :]
            rows[k] = h
        h_s[pl.ds(base, 8), :] = jnp.concatenate(rows, axis=0)
        return h

    return lax.fori_loop(0, groups, body, h)


def _scan_fwd_kernel(x_ref, prev_ref, next_ref, cw_ref, cb_ref, wgh_ref, wgl_ref, bg_ref, lam_ref, h0_ref,
                     hf_ref, fin_ref, a_s, u_s, carry_s):
    i = pl.program_id(0)
    first, last = _scan_tile_flags(i)
    a, u = _conv_gates(x_ref, prev_ref, next_ref, cw_ref, cb_ref, wgh_ref, wgl_ref, bg_ref, lam_ref, first, last)
    a_s[...] = a
    u_s[...] = u

    @pl.when(first)
    def _():
        carry_s[...] = h0_ref[...]

    h = _scan_rows(a_s, u_s, hf_ref, carry_s[...], reverse=False)
    carry_s[...] = h
    fin_ref[...] = h


def _scan_bwd_kernel(x_ref, prev_ref, next_ref, cw_ref, cb_ref, wgh_ref, wgl_ref, bg_ref, lam_ref, h0_ref,
                     hf_ref, gc_ref, g_ref, oc_ref, fin_ref, a_s, u_s, hb_s, carry_s):
    i = N_SCAN_TILES - 1 - pl.program_id(0)
    first, last = _scan_tile_flags(i)
    a, u = _conv_gates(x_ref, prev_ref, next_ref, cw_ref, cb_ref, wgh_ref, wgl_ref, bg_ref, lam_ref, first, last)
    a_s[...] = a
    u_s[...] = u

    @pl.when(last)
    def _():
        carry_s[...] = h0_ref[...]

    h = _scan_rows(a_s, u_s, hb_s, carry_s[...], reverse=True)
    carry_s[...] = h
    fin_ref[...] = h
    g = gc_ref[...]
    gelu = 0.5 * g * (1.0 + jnp.tanh(math.sqrt(2.0 / math.pi) * (g + 0.044715 * (g * g * g))))
    oc_ref[...] = _rms_gain(gelu * (hf_ref[...] + hb_s[...]), g_ref[...]).astype(BF16)


def _scan_common_specs(tile_of, direction):
    halo = T_SCAN // 8
    last_block = N_TOK // 8 - 1
    return [pl.BlockSpec((T_SCAN, 512), lambda j: (tile_of(j), COL_XC)),
            pl.BlockSpec((8, 512), lambda j: (jnp.maximum(tile_of(j) * halo - 1, 0), COL_XC)),
            pl.BlockSpec((8, 512), lambda j: (jnp.minimum((tile_of(j) + 1) * halo, last_block), COL_XC)),
            pl.BlockSpec((4, C_WIDTH), lambda j: (0, 0)),
            pl.BlockSpec((1, C_WIDTH), lambda j: (0, 0)),
            pl.BlockSpec((1, C_WIDTH, 2 * LANES), lambda j: (direction, 0, 0)),
            pl.BlockSpec((1, C_WIDTH, 2 * LANES), lambda j: (direction, 0, 0)),
            pl.BlockSpec((1, 1, 2 * C_WIDTH), lambda j: (direction, 0, 0)),
            pl.BlockSpec((1, 1, C_WIDTH), lambda j: (direction, 0, 0)),
            pl.BlockSpec((None, 1, C_WIDTH), lambda j: (_scan_seq(tile_of(j)), 0, 0))]


def _rglru(slab, conv_w, conv_b, w_gates, b_gates, lam, h0_f, h0_b, g_mix2d):
    wg_hi = w_gates.astype(BF16)
    wg_lo = (w_gates - wg_hi.astype(F32)).astype(BF16)
    fin = jax.ShapeDtypeStruct((N_SCAN_TILES, 1, C_WIDTH), F32)
    fwd_tile = lambda j: j
    h_f, fin_f = pl.pallas_call(
        _scan_fwd_kernel,
        grid=(N_SCAN_TILES,),
        in_specs=_scan_common_specs(fwd_tile, 0),
        out_specs=[pl.BlockSpec((T_SCAN, C_WIDTH), lambda j: (j, 0)),
                   pl.BlockSpec((None, 1, C_WIDTH), lambda j: (j, 0, 0))],
        out_shape=[jax.ShapeDtypeStruct((N_TOK, C_WIDTH), F32), fin],
        scratch_shapes=[pltpu.VMEM((T_SCAN, C_WIDTH), F32), pltpu.VMEM((T_SCAN, C_WIDTH), F32),
                        pltpu.VMEM((1, C_WIDTH), F32)],
        compiler_params=_params("arbitrary"),
        name="rglru_forward",
    )(slab, slab, slab, conv_w, conv_b, wg_hi, wg_lo, b_gates, lam, h0_f)
    bwd_tile = lambda j: N_SCAN_TILES - 1 - j
    oc, fin_b = pl.pallas_call(
        _scan_bwd_kernel,
        grid=(N_SCAN_TILES,),
        in_specs=_scan_common_specs(bwd_tile, 1) + [
            pl.BlockSpec((T_SCAN, C_WIDTH), lambda j: (bwd_tile(j), 0)),
            pl.BlockSpec((T_SCAN, 512), lambda j: (bwd_tile(j), COL_GC)),
            pl.BlockSpec((1, 512), lambda j: (0, 2))],
        out_specs=[pl.BlockSpec((T_SCAN, C_WIDTH), lambda j: (bwd_tile(j), 0)),
                   pl.BlockSpec((None, 1, C_WIDTH), lambda j: (bwd_tile(j), 0, 0))],
        out_shape=[jax.ShapeDtypeStruct((N_TOK, C_WIDTH), BF16), fin],
        scratch_shapes=[pltpu.VMEM((T_SCAN, C_WIDTH), F32), pltpu.VMEM((T_SCAN, C_WIDTH), F32),
                        pltpu.VMEM((T_SCAN, C_WIDTH), F32), pltpu.VMEM((1, C_WIDTH), F32)],
        compiler_params=_params("arbitrary"),
        name="rglru_backward",
    )(slab, slab, slab, conv_w, conv_b, wg_hi, wg_lo, b_gates, lam, h0_b, h_f, slab, g_mix2d)
    return oc, fin_f, fin_b


def _mixout_kernel(oa_ref, ob_ref, oc_ref, od_ref, w_ref, xc_ref, xl_ref, g1_ref, sc2_ref, sh2_ref, lng_ref,
                   lnb_ref, wrh_ref, wrl_ref, br_ref, x1_ref, h2_ref, idx_ref, wt_ref, rank_ref, cnt_ref, cnt_s):
    @pl.when(pl.program_id(0) == 0)
    def _():
        cnt_s[...] = jnp.zeros(cnt_s.shape, F32)

    parts = 2
    pm = xc_ref.shape[0] // parts
    is_ctx = _is_ctx_tile(xc_ref.shape[0])
    for part in range(parts):
        rows = pl.ds(part * pm, pm)
        x = jnp.where(is_ctx, xc_ref[rows, :], xl_ref[rows, :])
        _mixout_rows(rows, part * pm, pm, oa_ref, ob_ref, oc_ref, od_ref, w_ref, x, g1_ref,
                     sc2_ref, sh2_ref, lng_ref, lnb_ref, wrh_ref, wrl_ref, br_ref, x1_ref, h2_ref, idx_ref,
                     wt_ref, rank_ref, cnt_s)
    cnt_ref[...] = cnt_s[...]


def _mixout_rows(rows, first, tm, oa_ref, ob_ref, oc_ref, od_ref, w_ref, x, g1_ref, sc2_ref, sh2_ref,
                 lng_ref, lnb_ref, wrh_ref, wrl_ref, br_ref, x1_ref, h2_ref, idx_ref, wt_ref, rank_ref, cnt_s):
    y = _mm(oa_ref[rows, :], w_ref[0:512, :])
    y += _mm(ob_ref[rows, :], w_ref[512:1024, :])
    y += _mm(oc_ref[rows, :], w_ref[1024:1536, :])
    y += _mm(od_ref[rows, :], w_ref[1536:2048, :])
    x1 = _layernorm(ALPHA * x + g1_ref[...] * y, lng_ref[...], lnb_ref[...])
    x1_ref[rows, :] = x1
    h2 = x1 * (1.0 + sc2_ref[...]) + sh2_ref[...]
    _store_token_tiles(h2_ref, (), _pack_rows(h2), first)

    h_hi = h2.astype(BF16)
    h_lo = (h2 - h_hi.astype(F32)).astype(BF16)
    scores = jax.nn.sigmoid(_mm(h_hi, wrh_ref[...]) + _mm(h_lo, wrh_ref[...]) + _mm(h_hi, wrl_ref[...]))
    sel = scores + br_ref[...]
    lane = lax.broadcasted_iota(jnp.int32, sel.shape, 1).astype(F32)
    slot = lax.broadcasted_iota(jnp.int32, (tm, 8), 1)
    wide = lax.broadcasted_iota(jnp.int32, (tm, LANES), 1)
    idx_out = jnp.zeros((tm, LANES), F32)
    wt_out = jnp.zeros((tm, 8), F32)
    total = jnp.zeros((sel.shape[0], 1), F32)
    hits = []
    for k in range(TOP_K):
        m = sel.max(axis=-1, keepdims=True)
        idx = jnp.where(sel == m, lane, float(N_EXPERTS)).min(axis=-1, keepdims=True)
        hit = lane == idx
        hits.append(hit)
        w = jnp.where(hit, scores, 0.0).sum(axis=-1, keepdims=True)
        total = total + w
        idx_out = jnp.where(wide == k, idx, idx_out)
        wt_out = jnp.where(slot == k, w, wt_out)
        sel = jnp.where(hit, -jnp.inf, sel)
    cols = pl.ds(first, tm)
    idx_ref[:, cols] = idx_out.T[:8, :].astype(jnp.int32)
    wt_ref[rows, :] = wt_out / total * ROUTED_SCALE

    chosen = jnp.zeros(sel.shape, F32)
    for hit in hits:
        chosen = chosen + hit.astype(F32)
    earlier = (lax.broadcasted_iota(jnp.int32, (tm, tm), 1) < lax.broadcasted_iota(jnp.int32, (tm, tm), 0))
    before = _mm(earlier.astype(F32).astype(BF16), chosen.astype(BF16)) + cnt_s[...]
    rank_out = jnp.zeros((tm, LANES), F32)
    for k, hit in enumerate(hits):
        rank_out = jnp.where(wide == k, jnp.where(hit, before, 0.0).sum(axis=-1, keepdims=True), rank_out)
    rank_ref[:, cols] = rank_out.T[:8, :].astype(jnp.int32)
    cnt_s[...] = cnt_s[...] + chosen.sum(axis=0, keepdims=True)


def _mixer_out(oa, ob, oc, od, w_out_b, x_ctx, x_lat, mod, ln_g, ln_b, w_router, b_router):
    tm = TM_OUT
    cat = pl.BlockSpec((tm, 512), lambda i: (i, 0))
    row = pl.BlockSpec((tm, D_MODEL), lambda i: (i, 0))
    vec = pl.BlockSpec((1, D_MODEL), lambda i: (0, 0))
    k8 = pl.BlockSpec((tm, 8), lambda i: (i, 0))
    k_major = pl.BlockSpec((8, tm), lambda i: (0, i))
    per_expert = pl.BlockSpec((1, N_EXPERTS), lambda i: (0, 0))
    router = pl.BlockSpec((D_MODEL, N_EXPERTS), lambda i: (0, 0))
    wr_hi = w_router.astype(BF16)
    wr_lo = (w_router - wr_hi.astype(F32)).astype(BF16)
    return pl.pallas_call(
        _mixout_kernel,
        grid=(N_TOK // tm,),
        in_specs=[cat, cat, cat, cat,
                  pl.BlockSpec((D_MODEL, D_MODEL), lambda i: (0, 0)),
                  _ctx_rows(tm), _lat_rows(tm), _mod_spec(2, tm), _mod_spec(4, tm), _mod_spec(3, tm), vec, vec,
                  router, router, per_expert],
        out_specs=[row, pl.BlockSpec((tm * SUBLANES, LANES), lambda i: (i, 0)), k_major, k8, k_major, per_expert],
        out_shape=[jax.ShapeDtypeStruct((N_TOK, D_MODEL), F32),
                   jax.ShapeDtypeStruct((N_TOK * SUBLANES, LANES), jnp.uint32),
                   jax.ShapeDtypeStruct((8, N_TOK), jnp.int32), jax.ShapeDtypeStruct((N_TOK, 8), F32),
                   jax.ShapeDtypeStruct((8, N_TOK), jnp.int32), jax.ShapeDtypeStruct((1, N_EXPERTS), F32)],
        scratch_shapes=[pltpu.VMEM((1, N_EXPERTS), F32)],
        compiler_params=_params("arbitrary"),
        name="mixer_out_router",
    )(oa, ob, oc, od, w_out_b, x_ctx, x_lat, mod, mod, mod, ln_g, ln_b, wr_hi, wr_lo, b_router)


MOE_BLOCKS = N_TOK * TOP_K // MOE_BM + N_EXPERTS
MOE_ROWS = MOE_BLOCKS * MOE_BM


def _row_copy(src, src_row, dst, dst_row, sem):
    def tile(row):
        return pl.ds(row * SUBLANES if isinstance(row, int) else pl.multiple_of(row * SUBLANES, SUBLANES), SUBLANES)

    return pltpu.make_async_copy(src.at[tile(src_row)], dst.at[tile(dst_row)], sem)


def _dispatch_kernel(slot_ref, h_ref, xg_ref, sem):
    tm = h_ref.shape[0] // SUBLANES
    base = pl.program_id(0) * tm

    for r in range(tm):
        for k in range(TOP_K):
            _row_copy(h_ref, r, xg_ref, slot_ref[base + (k * N_TOK + r)], sem).start()
    for k in range(TOP_K):
        pltpu.make_async_copy(h_ref, xg_ref.at[pl.ds(0, tm * SUBLANES)], sem).wait()


def _dispatch_rows(slot_flat, h2):
    tm = TM_DISPATCH
    grid_spec = pltpu.PrefetchScalarGridSpec(
        num_scalar_prefetch=1,
        grid=(N_TOK // tm,),
        in_specs=[pl.BlockSpec((tm * SUBLANES, LANES), lambda i, s: (i, 0))],
        out_specs=pl.BlockSpec(memory_space=pl.ANY),
        scratch_shapes=[pltpu.SemaphoreType.DMA])
    return pl.pallas_call(
        _dispatch_kernel,
        grid_spec=grid_spec,
        out_shape=jax.ShapeDtypeStruct((MOE_ROWS * SUBLANES, LANES), jnp.uint32),
        compiler_params=_params("arbitrary"),
        name="moe_dispatch",
    )(slot_flat, h2)


BLOCK_TILE_ROWS = MOE_BM * SUBLANES
EXPERT_X_BUFFERS = 3


def _expert_kernel(blk0_ref, nblk_ref, cnt_ref, wg_ref, wu_ref, wd_ref, x_hbm, y_hbm,
                   wg_s, wu_s, wd_s, xbuf, ybuf, xsem, ysem):
    e = pl.program_id(0)
    first_blk = blk0_ref[e]
    n_blk = nblk_ref[e]
    count = cnt_ref[e]
    total = blk0_ref[N_EXPERTS - 1] + nblk_ref[N_EXPERTS - 1]

    def rows_of(g):
        return pl.ds(pl.multiple_of(g * BLOCK_TILE_ROWS, BLOCK_TILE_ROWS), BLOCK_TILE_ROWS)

    def x_copy(g, buf):
        return pltpu.make_async_copy(x_hbm.at[rows_of(g)], xbuf.at[buf], xsem.at[buf])

    def y_copy(g, buf):
        return pltpu.make_async_copy(ybuf.at[buf], y_hbm.at[rows_of(g)], ysem.at[buf])

    for ahead in range(EXPERT_X_BUFFERS - 1):
        @pl.when((e == 0) & (total > ahead))
        def _():
            x_copy(ahead, ahead).start()

    wg_s[...] = wg_ref[...].astype(BF16)
    wu_s[...] = wu_ref[...].astype(BF16)
    wd_s[...] = wd_ref[...].astype(BF16)

    def block(j, carry):
        g = first_blk + j
        buf = g % 2
        xb = lax.rem(g, EXPERT_X_BUFFERS)
        nxt = g + (EXPERT_X_BUFFERS - 1)

        @pl.when(nxt < total)
        def _():
            x_copy(nxt, lax.rem(nxt, EXPERT_X_BUFFERS)).start()

        x_copy(g, xb).wait()

        @pl.when(g >= 2)
        def _():
            y_copy(g - 2, buf).wait()

        x = _load_token_tiles(xbuf, (xb,), MOE_BM)
        row = lax.broadcasted_iota(jnp.int32, x.shape, 0)
        lo, hi = _unpack_rows(jnp.where(row < count - j * MOE_BM, x, jnp.uint32(0)))
        lo, hi = lo.astype(BF16), hi.astype(BF16)
        gate = _mm(lo, wg_s[:HALF_D, :]) + _mm(hi, wg_s[HALF_D:, :])
        up = _mm(lo, wu_s[:HALF_D, :]) + _mm(hi, wu_s[HALF_D:, :])
        _store_token_tiles(ybuf, (buf,), _pack_rows(_mm((_silu(gate) * up).astype(BF16), wd_s[...])))
        y_copy(g, buf).start()
        return carry

    lax.fori_loop(0, n_blk, block, 0)

    @pl.when((e == N_EXPERTS - 1) & (total >= 2))
    def _():
        y_copy(total - 2, total % 2).wait()

    @pl.when((e == N_EXPERTS - 1) & (total >= 1))
    def _():
        y_copy(total - 1, (total - 1) % 2).wait()


def _routed_experts(xg, first_blk, n_blk, counts, layer, w_g, w_u, w_d):
    def weight(shape):
        return pl.BlockSpec((None, None) + shape, lambda e, *_: (layer, e, 0, 0))

    any_space = pl.BlockSpec(memory_space=pl.ANY)
    x_bufs = pltpu.VMEM((EXPERT_X_BUFFERS, BLOCK_TILE_ROWS, LANES), jnp.uint32)
    y_bufs = pltpu.VMEM((2, BLOCK_TILE_ROWS, LANES), jnp.uint32)
    grid_spec = pltpu.PrefetchScalarGridSpec(
        num_scalar_prefetch=3,
        grid=(N_EXPERTS,),
        in_specs=[weight((D_MODEL, D_EXPERT)), weight((D_MODEL, D_EXPERT)), weight((D_EXPERT, D_MODEL)), any_space],
        out_specs=any_space,
        scratch_shapes=[pltpu.VMEM((D_MODEL, D_EXPERT), BF16), pltpu.VMEM((D_MODEL, D_EXPERT), BF16),
                        pltpu.VMEM((D_EXPERT, D_MODEL), BF16), x_bufs, y_bufs,
                        pltpu.SemaphoreType.DMA((EXPERT_X_BUFFERS,)), pltpu.SemaphoreType.DMA((2,))])
    return pl.pallas_call(
        _expert_kernel,
        grid_spec=grid_spec,
        out_shape=jax.ShapeDtypeStruct((MOE_ROWS * SUBLANES, LANES), jnp.uint32),
        compiler_params=_params("arbitrary"),
        name="routed_experts",
    )(first_blk, n_blk, counts, w_g, w_u, w_d, xg)


def _ffn_out_kernel(slot_ref, h_ref, wt_ref, x1_ref, wg_ref, wu_ref, wd_ref, g2_ref, lng_ref, lnb_ref, y_ref,
                    oc_ref, ol_ref, ybuf, sems):
    tm = h_ref.shape[0] // SUBLANES
    i = pl.program_id(0)
    n = pl.num_programs(0)

    def gather(tile, buf):
        base = tile * tm

        for r in range(tm):
            for k in range(TOP_K):
                _row_copy(y_ref, slot_ref[base + (k * N_TOK + r)], ybuf.at[buf, k], r, sems.at[buf]).start()

    @pl.when(i == 0)
    def _():
        gather(0, 0)

    @pl.when(i + 1 < n)
    def _():
        gather(i + 1, (i + 1) % 2)

    lo, hi = _unpack_rows(_load_token_tiles(h_ref, (), tm))
    lo, hi = lo.astype(BF16), hi.astype(BF16)
    gate = _mm(lo, wg_ref[:HALF_D, :]) + _mm(hi, wg_ref[HALF_D:, :])
    up = _mm(lo, wu_ref[:HALF_D, :]) + _mm(hi, wu_ref[HALF_D:, :])
    y = _mm((_silu(gate) * up).astype(BF16), wd_ref[...])
    buf = i % 2
    for k in range(TOP_K):
        pltpu.make_async_copy(y_ref.at[pl.ds(0, tm * SUBLANES)], ybuf.at[buf, k], sems.at[buf]).wait()
    wt = wt_ref[...]
    y_lo, y_hi = y[:, :HALF_D], y[:, HALF_D:]
    for k in range(TOP_K):
        r_lo, r_hi = _unpack_rows(_load_token_tiles(ybuf, (buf, k), tm))
        y_lo = y_lo + r_lo * wt[:, k:k + 1]
        y_hi = y_hi + r_hi * wt[:, k:k + 1]
    y = jnp.concatenate([y_lo, y_hi], axis=1)
    out = _layernorm(ALPHA * x1_ref[...] + g2_ref[...] * y, lng_ref[...], lnb_ref[...])
    is_ctx = _is_ctx_tile(tm)

    @pl.when(is_ctx)
    def _():
        oc_ref[...] = out

    @pl.when(jnp.logical_not(is_ctx))
    def _():
        ol_ref[...] = out


def _ffn_out(slot_flat, h2, wt, x1, ws_g, ws_u, ws_d, mod, ln_g, ln_b, y_blk):
    tm = TM_FFN
    row = pl.BlockSpec((tm, D_MODEL), lambda i, s: (i, 0))
    vec = pl.BlockSpec((1, D_MODEL), lambda i, s: (0, 0))
    grid_spec = pltpu.PrefetchScalarGridSpec(
        num_scalar_prefetch=1,
        grid=(N_TOK // tm,),
        in_specs=[pl.BlockSpec((tm * SUBLANES, LANES), lambda i, s: (i, 0)),
                  pl.BlockSpec((tm, 8), lambda i, s: (i, 0)), row,
                  pl.BlockSpec((D_MODEL, D_EXPERT), lambda i, s: (0, 0)),
                  pl.BlockSpec((D_MODEL, D_EXPERT), lambda i, s: (0, 0)),
                  pl.BlockSpec((D_EXPERT, D_MODEL), lambda i, s: (0, 0)),
                  _mod_spec(5, tm), vec, vec,
                  pl.BlockSpec(memory_space=pl.ANY)],
        out_specs=[_ctx_rows(tm), _lat_rows(tm)],
        scratch_shapes=[pltpu.VMEM((2, TOP_K, tm * SUBLANES, LANES), jnp.uint32),
                        pltpu.SemaphoreType.DMA((2,))])
    return pl.pallas_call(
        _ffn_out_kernel,
        grid_spec=grid_spec,
        out_shape=[jax.ShapeDtypeStruct((N_CTX, D_MODEL), F32), jax.ShapeDtypeStruct((N_LAT, D_MODEL), F32)],
        compiler_params=_params("arbitrary"),
        name="shared_expert_out",
    )(slot_flat, h2, wt, x1, ws_g, ws_u, ws_d, mod, ln_g, ln_b, y_blk)


def _dispatch_plan(idx, rank, counts):
    bm = MOE_BM
    counts = counts.reshape(N_EXPERTS).astype(jnp.int32)
    padded = (counts + bm - 1) // bm * bm
    pad_end = jnp.cumsum(padded)
    pad_start = pad_end - padded
    experts = jnp.arange(N_EXPERTS, dtype=jnp.int32)[:, None, None]
    first_row = jnp.sum(jnp.where(idx[None, :TOP_K] == experts, pad_start[:, None, None], 0), axis=0)
    slot = first_row + rank[:TOP_K]
    return slot.reshape(-1).astype(jnp.int32), pad_start // bm, padded // bm, counts


def _rope_tables():
    t = jnp.arange(DEC_SEQ)
    row = (t // GRID_W).astype(F32)
    col = (t % GRID_W).astype(F32)
    nf = HEAD_DIM // 4
    inv = ROPE_BASE ** (-jnp.arange(nf, dtype=F32) / nf)
    ar = row[:, None] * inv[None, :]
    ac = col[:, None] * inv[None, :]
    ang = jnp.concatenate([ar, ar, ac, ac], axis=-1)
    ang = jnp.concatenate([ang, ang], axis=-1)
    cos, sin = jnp.cos(ang), jnp.sin(ang)
    first = (jnp.arange(LANES) % (2 * nf)) < nf
    return cos, jnp.where(first, -sin, 0.0), jnp.where(first, 0.0, sin)


def _lambda_init(l):
    return 0.8 - 0.6 * math.exp(-0.3 * l)


def _gate_pairs(w_r, w_i):
    per = LANES // C_BW
    eye = jnp.eye(per, dtype=w_r.dtype)

    def chunk_diag(w):
        w = w.reshape(C_BLOCKS // per, per, C_BW, C_BW)
        return (eye[None, :, None, :, None] * w[:, :, :, None, :]).reshape(C_BLOCKS // per, LANES, LANES)

    return jnp.concatenate([chunk_diag(w_r), chunk_diag(w_i)], axis=2).reshape(C_WIDTH, 2 * LANES)


def kernel(x_prompt, x_sample, c, cache_a_k, cache_a_v, cache_b_k, cache_b_v, state_c, cache_d_k, cache_d_v, c_ctx, w_mod, b_mod, w_in, a_sink, b_lambda, c_conv_w, c_conv_b, c_w_rgate, c_b_rgate, c_w_igate, c_b_igate, c_lambda, d_rpb, g_mix, w_out, ln_g, ln_b, w_router, b_router, w_e_gate, w_e_up, w_e_down, w_s_gate, w_s_up, w_s_down):
    x_ctx, x_lat = x_prompt.reshape(N_CTX, D_MODEL), x_sample.reshape(N_LAT, D_MODEL)
    cond8 = jnp.zeros((8, D_MODEL), F32).at[0].set(c_ctx).at[1:1 + DEC_BATCH].set(c)
    mod_all = _modulation(cond8, w_mod, b_mod)
    cos, sin_a, sin_b = _rope_tables()
    row_sel, col_sel, na_valid = _na_bias_indices()

    order = ((0, 512), (768, IN_WIDTH), (512, 768))

    ctx_out = [[] for _ in range(7)]
    for l in range(DEPTH):
        lam_init = _lambda_init(l)
        mod = mod_all[l, :N_GROUPS].reshape(N_GROUPS, 6, 1, D_MODEL)
        w_l = w_in[l]
        w_in_b = jnp.concatenate([w_l[:, a:b] for a, b in order], axis=1).astype(BF16)
        slab = _in_projection(x_ctx, x_lat, mod, w_in_b)

        lp = b_lambda[l]
        lam = jnp.exp(jnp.sum(lp[0] * lp[1])) - jnp.exp(jnp.sum(lp[2] * lp[3])) + lam_init
        scal = jnp.zeros((16,), F32).at[:A_HEADS].set(a_sink[l]).at[A_HEADS].set(lam)
        g2d = g_mix[l].reshape(1, -1)

        oa, ob, od = _ctx_attention(slab, scal, g2d, lam_init)

        qa_p, ka_r, va_b, qb_p, kb_r, vb_b, qd_p, kd_b, vd_b = _latent_prep(slab, cos, sin_a, sin_b)
        oa = _window_attention(qa_p, ka_r, va_b, cache_a_k[:, l].reshape(DEC_BATCH, PAST_LEN, LANES),
                               cache_a_v[:, l].reshape(DEC_BATCH, PAST_LEN, LANES), scal, g2d, oa)
        ob = _diff_attention(qb_p, kb_r, vb_b, cache_b_k[:, l].reshape(DEC_BATCH, PAST_LEN, 512),
                             cache_b_v[:, l].reshape(DEC_BATCH, PAST_LEN, 512), scal, g2d, lam_init, ob)
        bias = _na_bias(d_rpb[l], row_sel, col_sel, na_valid)
        od = _neighbourhood_attention(qd_p, kd_b, vd_b, cache_d_k[:, l].reshape(DEC_BATCH, PAST_LEN, 512),
                                      cache_d_v[:, l].reshape(DEC_BATCH, PAST_LEN, 512), bias, g2d, od)

        w_gates = jnp.stack([_gate_pairs(c_w_rgate[l, d], c_w_igate[l, d]) for d in range(2)])
        b_gates = jnp.concatenate([c_b_rgate[l], c_b_igate[l]], axis=-1).reshape(2, 1, 2 * C_WIDTH)
        zeros = jnp.zeros((BATCH, 1, C_WIDTH), F32)
        h0_f = jnp.concatenate([zeros, state_c[:, l, 0][:, None, :]], axis=0)
        h0_b = jnp.concatenate([zeros, state_c[:, l, 1][:, None, :]], axis=0)
        oc, fin_f, fin_b = _rglru(slab, c_conv_w[l], c_conv_b[l].reshape(1, -1), w_gates, b_gates,
                                  c_lambda[l].reshape(2, 1, C_WIDTH), h0_f, h0_b, g2d)

        x1, h2, idx, wt, rank, counts = _mixer_out(oa, ob, oc, od, w_out[l].astype(BF16), x_ctx, x_lat, mod,
                                                   ln_g[l, 0].reshape(1, -1), ln_b[l, 0].reshape(1, -1),
                                                   w_router[l], b_router[l].reshape(1, -1))

        slot, first_blk, n_blk, counts = _dispatch_plan(idx, rank, counts)
        xg = _dispatch_rows(slot, h2)
        y_blk = _routed_experts(xg, first_blk, n_blk, counts, l, w_e_gate, w_e_up, w_e_down)
        x_ctx, x_lat = _ffn_out(slot, h2, wt, x1, w_s_gate[l].astype(BF16), w_s_up[l].astype(BF16),
                                w_s_down[l].astype(BF16), mod, ln_g[l, 1].reshape(1, -1),
                                ln_b[l, 1].reshape(1, -1), y_blk)

        ctx = slab[:N_CTX]
        ctx_out[0].append(ctx[:, COL_KA * LANES:(COL_KA + 1) * LANES].reshape(BATCH, SEQ, A_KV_HEADS, HEAD_DIM))
        ctx_out[1].append(ctx[:, COL_VA * LANES:(COL_VA + 1) * LANES].reshape(BATCH, SEQ, A_KV_HEADS, HEAD_DIM))
        ctx_out[2].append(ctx[:, COL_KB * 512:(COL_KB + 1) * 512].reshape(BATCH, SEQ, B_HEADS, 2, HEAD_DIM))
        ctx_out[3].append(ctx[:, COL_VB * 512:(COL_VB + 1) * 512].reshape(BATCH, SEQ, B_HEADS, 2 * HEAD_DIM))
        ctx_out[4].append(jnp.concatenate([fin_f[:BATCH], fin_b[:BATCH]], axis=1))
        ctx_out[5].append(ctx[:, COL_KD * 512:(COL_KD + 1) * 512].reshape(BATCH, SEQ, D_HEADS, HEAD_DIM))
        ctx_out[6].append(ctx[:, COL_VD * 512:(COL_VD + 1) * 512].reshape(BATCH, SEQ, D_HEADS, HEAD_DIM))

    new = [jnp.stack(t, axis=1) for t in ctx_out]
    return (x_ctx.reshape(BATCH, SEQ, D_MODEL), x_lat.reshape(DEC_BATCH, DEC_SEQ, D_MODEL), *new)
```

```python
import functools
import math

import numpy as np
import jax
import jax.numpy as jnp
from jax import lax
from jax.experimental import pallas as pl
from jax.experimental.pallas import tpu as pltpu

F32 = jnp.float32
BF16 = jnp.bfloat16

D_MODEL = 2048
BATCH = 16
SEQ = 256
DEPTH = 2
DEC_BATCH = 2
DEC_SEQ = 4096
PAST_LEN = 256
GRID_W = 64
HEAD_DIM = 64
ROPE_BASE = 10000.0
A_HEADS = 8
A_KV_HEADS = 2
A_WINDOW = 128
B_HEADS = 4
C_WIDTH = 512
C_BLOCKS = 8
C_BW = C_WIDTH // C_BLOCKS
C_POW = 8.0
D_HEADS = 8
NA_ROWS = 8
NA_COLS = 16
N_EXPERTS = 64
TOP_K = 6
D_EXPERT = 512
ROUTED_SCALE = 2.5
ALPHA = (2.0 * DEPTH) ** 0.25
LN_EPS = 1e-5
NORM_EPS = 1e-6
NEG_INF = -1e30
SCALE = HEAD_DIM ** -0.5
LOG2E = math.log2(math.e)

N_CTX = BATCH * SEQ
N_LAT = DEC_BATCH * DEC_SEQ
N_TOK = N_CTX + N_LAT
GROUP_ROWS = 4096
N_GROUPS = N_TOK // GROUP_ROWS
IN_WIDTH = 4864
LANES = 128
VMEM_LIMIT = 56 * 1024 * 1024

COL_QA, COL_QB, COL_KB, COL_VB, COL_XC, COL_GC, COL_QD, COL_KD, COL_VD = range(9)
COL_KA, COL_VA = 36, 37

TM_PROJ = 512
TN_PROJ = IN_WIDTH // 2
TM_PREP = 256
TQ_A = 128
TQ_B = 256
NA_QR = 4
NA_KR = 12
T_SCAN = 256
TM_OUT = 256
MOE_BM = 256
TM_DISPATCH = 128
TM_FFN = 128


def _params(*sem):
    return pltpu.CompilerParams(dimension_semantics=sem, vmem_limit_bytes=VMEM_LIMIT)


def _nt(a, b):
    return lax.dot_general(a, b, (((1,), (1,)), ((), ())), preferred_element_type=F32)


def _mm(a, b):
    return jnp.dot(a, b, preferred_element_type=F32)


def _softmax_unnorm(parts, sink=None):
    m = parts[0].max(axis=-1, keepdims=True)
    for p in parts[1:]:
        m = jnp.maximum(m, p.max(axis=-1, keepdims=True))
    if sink is not None:
        m = jnp.maximum(m, sink)
    es = [jnp.exp(p - m) for p in parts]
    den = es[0].sum(axis=-1, keepdims=True)
    for e in es[1:]:
        den = den + e.sum(axis=-1, keepdims=True)
    if sink is not None:
        den = den + jnp.exp(sink - m)
    return es, den


def _rms_gain(x, g):
    return x * lax.rsqrt(jnp.mean(x * x, axis=-1, keepdims=True) + NORM_EPS) * g


def _layernorm(z, g, b):
    zc = z - jnp.mean(z, axis=-1, keepdims=True)
    var = jnp.mean(zc * zc, axis=-1, keepdims=True)
    return zc * lax.rsqrt(var + LN_EPS) * g + b


def _silu(x):
    return x * jax.nn.sigmoid(x)


def _low_half(shape):
    return lax.broadcasted_iota(jnp.int32, shape, 1) < HEAD_DIM


HALF_D = D_MODEL // 2
HIGH16 = 0xFFFF0000


def _pack_rows(x):
    bits = pltpu.bitcast(x.astype(BF16).astype(F32), jnp.uint32)
    return (bits[:, :HALF_D] >> 16) | (bits[:, HALF_D:] & jnp.uint32(HIGH16))


def _unpack_rows(p):
    return (pltpu.bitcast(p << 16, F32), pltpu.bitcast(p & jnp.uint32(HIGH16), F32))


SUBLANES = 8
assert HALF_D == SUBLANES * LANES


def _store_token_tiles(ref, index, packed, first=0):
    m = packed.shape[0]
    for s in range(SUBLANES):
        rows = pl.ds(first * SUBLANES + s, m, stride=SUBLANES)
        ref[index + (rows, slice(None))] = packed[:, s * LANES:(s + 1) * LANES]


def _load_token_tiles(ref, index, m):
    return jnp.concatenate([ref[index + (pl.ds(s, m, stride=SUBLANES), slice(None))] for s in range(SUBLANES)],
                           axis=1)


def _mod_kernel(c_ref, w_ref, b_ref, o_ref):
    c = c_ref[...]
    o_ref[0] = jnp.dot(_silu(c), w_ref[0], precision=lax.Precision.HIGHEST,
                       preferred_element_type=F32) + b_ref[0]


def _modulation(cond8, w_mod, b_mod):
    tn = 1024
    n = w_mod.shape[-1]
    return pl.pallas_call(
        _mod_kernel,
        grid=(DEPTH, n // tn),
        in_specs=[pl.BlockSpec((8, D_MODEL), lambda l, j: (0, 0)),
                  pl.BlockSpec((1, D_MODEL, tn), lambda l, j: (l, 0, j)),
                  pl.BlockSpec((1, 1, tn), lambda l, j: (l, 0, j))],
        out_specs=pl.BlockSpec((1, 8, tn), lambda l, j: (l, 0, j)),
        out_shape=jax.ShapeDtypeStruct((DEPTH, 8, n), F32),
        compiler_params=_params("parallel", "parallel"),
        name="modulation",
    )(cond8, w_mod, b_mod.reshape(DEPTH, 1, n))


def _mod_spec(which, tm):
    return pl.BlockSpec((None, None, 1, D_MODEL), lambda i, *_: (i * tm // GROUP_ROWS, which, 0, 0))


def _ctx_rows(tm):
    return pl.BlockSpec((tm, D_MODEL), lambda i, *_: (jnp.minimum(i, N_CTX // tm - 1), 0))


def _lat_rows(tm):
    return pl.BlockSpec((tm, D_MODEL), lambda i, *_: (jnp.maximum(i - N_CTX // tm, 0), 0))


def _is_ctx_tile(tm):
    return pl.program_id(0) * tm < N_CTX


def _inproj_kernel(xc_ref, xl_ref, sc_ref, sh_ref, w_ref, o_ref, xb_ref):
    @pl.when(pl.program_id(1) == 0)
    def _():
        x = jnp.where(_is_ctx_tile(xc_ref.shape[0]), xc_ref[...], xl_ref[...])
        xb_ref[...] = (x * (1.0 + sc_ref[...]) + sh_ref[...]).astype(BF16)

    o_ref[...] = _mm(xb_ref[...], w_ref[...])


def _in_projection(x_ctx, x_lat, mod, w_in_b):
    tm, tn = TM_PROJ, TN_PROJ
    return pl.pallas_call(
        _inproj_kernel,
        grid=(N_TOK // tm, IN_WIDTH // tn),
        in_specs=[_ctx_rows(tm), _lat_rows(tm),
                  _mod_spec(1, tm), _mod_spec(0, tm),
                  pl.BlockSpec((D_MODEL, tn), lambda i, j: (0, j))],
        out_specs=pl.BlockSpec((tm, tn), lambda i, j: (i, j)),
        out_shape=jax.ShapeDtypeStruct((N_TOK, IN_WIDTH), F32),
        scratch_shapes=[pltpu.VMEM((tm, D_MODEL), BF16)],
        compiler_params=_params("parallel", "arbitrary"),
        name="in_projection",
    )(x_ctx, x_lat, mod, mod, w_in_b)


def _ctx_attn_kernel(scal_ref, qa_ref, ka_ref, va_ref, qb_ref, kb_ref, vb_ref, qd_ref, kd_ref, vd_ref,
                     ga_ref, gb_ref, gd_ref, oa_ref, ob_ref, od_ref, *, lam_init):
    L = SEQ
    lo = _low_half((L, LANES))

    ka = ka_ref[...].astype(BF16)
    va = va_ref[...].astype(BF16)
    qa = qa_ref[...]
    chunks = []
    for c in range(A_HEADS // 2):
        hk = c // 2
        chunk = qa[:, c * LANES:(c + 1) * LANES]
        keep = lo if hk == 0 else jnp.logical_not(lo)
        halves = []
        for half in range(2):
            x = chunk if half == hk else pltpu.roll(chunk, HEAD_DIM, 1)
            qh = jnp.where(keep, x, 0.0).astype(BF16)
            s = _nt(qh, ka) * SCALE
            (e,), den = _softmax_unnorm([s], scal_ref[2 * c + half])
            o = _mm(e.astype(BF16), va) / den
            halves.append(o if half == hk else pltpu.roll(o, HEAD_DIM, 1))
        chunks.append(jnp.where(lo, halves[0], halves[1]))
    oa = jnp.concatenate(chunks, axis=1)
    oa_ref[...] = _rms_gain(oa, ga_ref[...]).astype(BF16)

    lam = scal_ref[A_HEADS]
    qb = qb_ref[...]
    chunks = []
    for h in range(B_HEADS):
        sl = slice(h * LANES, (h + 1) * LANES)
        qc = qb[:, sl]
        kc = kb_ref[:, sl].astype(BF16)
        vh = vb_ref[:, sl].astype(BF16)
        (e1,), d1 = _softmax_unnorm([_nt(jnp.where(lo, qc, 0.0).astype(BF16), kc) * SCALE])
        (e2,), d2 = _softmax_unnorm([_nt(jnp.where(lo, 0.0, qc).astype(BF16), kc) * SCALE])
        w = e1 * (1.0 / d1) - e2 * (lam / d2)
        o = _mm(w.astype(BF16), vh)
        chunks.append(_rms_gain(o, gb_ref[:, sl]) * (1.0 - lam_init))
    ob_ref[...] = jnp.concatenate(chunks, axis=1).astype(BF16)

    qd = qd_ref[...]
    chunks = []
    for c in range(D_HEADS // 2):
        sl = slice(c * LANES, (c + 1) * LANES)
        qc = qd[:, sl]
        kc = kd_ref[:, sl].astype(BF16)
        vc = vd_ref[:, sl].astype(BF16)
        halves = []
        for half in range(2):
            qh = jnp.where(lo if half == 0 else jnp.logical_not(lo), qc, 0.0).astype(BF16)
            (e,), den = _softmax_unnorm([_nt(qh, kc) * SCALE])
            halves.append(_mm(e.astype(BF16), vc) / den)
        chunks.append(jnp.where(lo, halves[0], halves[1]))
    od = jnp.concatenate(chunks, axis=1)
    od_ref[...] = _rms_gain(od, gd_ref[...]).astype(BF16)


def _ctx_attention(slab, scal, g_mix2d, lam_init):
    L = SEQ

    def wide(col):
        return pl.BlockSpec((L, 512), lambda b: (b, col))

    def narrow(col):
        return pl.BlockSpec((L, LANES), lambda b: (b, col))

    def gain(col):
        return pl.BlockSpec((1, 512), lambda b: (0, col))

    out = jax.ShapeDtypeStruct((N_TOK, 512), BF16)
    ospec = pl.BlockSpec((L, 512), lambda b: (b, 0))
    return pl.pallas_call(
        functools.partial(_ctx_attn_kernel, lam_init=lam_init),
        grid=(BATCH,),
        in_specs=[pl.BlockSpec(memory_space=pltpu.SMEM),
                  wide(COL_QA), narrow(COL_KA), narrow(COL_VA),
                  wide(COL_QB), wide(COL_KB), wide(COL_VB),
                  wide(COL_QD), wide(COL_KD), wide(COL_VD),
                  gain(0), gain(1), gain(3)],
        out_specs=[ospec, ospec, ospec],
        out_shape=[out, out, out],
        compiler_params=_params("parallel"),
        name="ctx_attention",
    )(scal, slab, slab, slab, slab, slab, slab, slab, slab, slab, g_mix2d, g_mix2d, g_mix2d)


def _prep_kernel(qa_ref, ka_ref, va_ref, qb_ref, kb_ref, vb_ref, qd_ref, kd_ref, vd_ref,
                 cos_ref, sa_ref, sb_ref,
                 qa_o, ka_o, va_o, qb_o, kb_o, vb_o, qd_o, kd_o, vd_o):
    cos, sa, sb = cos_ref[...], sa_ref[...], sb_ref[...]
    lo = _low_half(cos.shape)
    hi = jnp.logical_not(lo)

    def rope(x):
        return (x * cos + pltpu.roll(x, LANES - HEAD_DIM // 4, 1) * sa
                + pltpu.roll(x, HEAD_DIM // 4, 1) * sb)

    ka_o[...] = rope(ka_ref[...]).astype(BF16)
    va_o[...] = va_ref[...].astype(BF16)
    vb_o[...] = vb_ref[...].astype(BF16)
    kd_o[...] = kd_ref[...].astype(BF16)
    vd_o[...] = vd_ref[...].astype(BF16)

    for c in range(A_HEADS // 2):
        hk = c // 2
        r = rope(qa_ref[:, c * LANES:(c + 1) * LANES])
        keep = lo if hk == 0 else hi
        for half in range(2):
            h = 2 * c + half
            x = r if half == hk else pltpu.roll(r, HEAD_DIM, 1)
            qa_o[:, h * LANES:(h + 1) * LANES] = jnp.where(keep, x, 0.0).astype(BF16)
    for h in range(B_HEADS):
        sl = slice(h * LANES, (h + 1) * LANES)
        r = rope(qb_ref[:, sl])
        kb_o[:, sl] = rope(kb_ref[:, sl]).astype(BF16)
        qb_o[:, (2 * h) * LANES:(2 * h + 1) * LANES] = jnp.where(lo, r, 0.0).astype(BF16)
        qb_o[:, (2 * h + 1) * LANES:(2 * h + 2) * LANES] = jnp.where(hi, r, 0.0).astype(BF16)
    for c in range(D_HEADS // 2):
        x = qd_ref[:, c * LANES:(c + 1) * LANES]
        qd_o[:, (2 * c) * LANES:(2 * c + 1) * LANES] = jnp.where(lo, x, 0.0).astype(BF16)
        qd_o[:, (2 * c + 1) * LANES:(2 * c + 2) * LANES] = jnp.where(hi, x, 0.0).astype(BF16)


def _latent_prep(slab, cos, sa, sb):
    tm = TM_PREP
    off = N_CTX // tm
    per_seq = DEC_SEQ // tm

    def wide(col):
        return pl.BlockSpec((tm, 512), lambda i: (i + off, col))

    def narrow(col):
        return pl.BlockSpec((tm, LANES), lambda i: (i + off, col))

    tab = pl.BlockSpec((tm, LANES), lambda i: (i % per_seq, 0))

    def out(width):
        return (pl.BlockSpec((tm, width), lambda i: (i, 0)), jax.ShapeDtypeStruct((N_LAT, width), BF16))

    outs = [out(1024), out(LANES), out(LANES), out(1024), out(512), out(512), out(1024), out(512), out(512)]
    return pl.pallas_call(
        _prep_kernel,
        grid=(N_LAT // tm,),
        in_specs=[wide(COL_QA), narrow(COL_KA), narrow(COL_VA), wide(COL_QB), wide(COL_KB), wide(COL_VB),
                  wide(COL_QD), wide(COL_KD), wide(COL_VD), tab, tab, tab],
        out_specs=[o[0] for o in outs],
        out_shape=[o[1] for o in outs],
        compiler_params=_params("parallel"),
        name="latent_prep",
    )(slab, slab, slab, slab, slab, slab, slab, slab, slab, cos, sa, sb)


def _win_attn_kernel(scal_ref, q_ref, kp_ref, kc_ref, kn_ref, vp_ref, vc_ref, vn_ref, kctx_ref, vctx_ref,
                     g_ref, _all_rows_ref, o_ref):
    n = pl.program_id(1)
    nb = pl.num_programs(1)
    tq = TQ_A
    group = A_HEADS // A_KV_HEADS
    k = jnp.concatenate([kp_ref[...], kc_ref[...], kn_ref[...], kctx_ref[...].astype(BF16)], axis=0)
    v = jnp.concatenate([vp_ref[...], vc_ref[...], vn_ref[...], vctx_ref[...].astype(BF16)], axis=0)
    nk = 3 * tq + PAST_LEN
    row = lax.broadcasted_iota(jnp.int32, (group * tq, nk), 0) & (tq - 1)
    col = lax.broadcasted_iota(jnp.int32, (group * tq, nk), 1)
    prev_thr = row + jnp.where(n > 0, 0, tq)
    next_thr = row + 2 * tq - jnp.where(n < nb - 1, 0, tq)
    masked = ((col < tq) & (col < prev_thr)) | ((col >= 2 * tq) & (col < 3 * tq) & (col > next_thr))
    valid = jnp.logical_not(masked)
    lo = _low_half((tq, LANES))
    heads = []
    for hk in range(A_KV_HEADS):
        q4 = jnp.concatenate([q_ref[:, (hk * group + g) * LANES:(hk * group + g + 1) * LANES]
                              for g in range(group)], axis=0)
        s = jnp.where(valid, _nt(q4, k) * SCALE, NEG_INF)
        sink = jnp.concatenate([jnp.full((tq, 1), scal_ref[hk * group + g], F32) for g in range(group)], axis=0)
        (e,), den = _softmax_unnorm([s], sink)
        o = _mm(e.astype(BF16), v) / den
        for g in range(group):
            h = hk * group + g
            og = o[g * tq:(g + 1) * tq]
            heads.append(og if (h % 2) == hk else pltpu.roll(og, HEAD_DIM, 1))
    oa = jnp.concatenate([jnp.where(lo, heads[2 * c], heads[2 * c + 1]) for c in range(A_HEADS // 2)], axis=1)
    o_ref[...] = _rms_gain(oa, g_ref[...]).astype(BF16)


def _window_attention(qa_p, ka_r, va_b, cache_k, cache_v, scal, g_mix2d, o_all):
    tq = TQ_A
    nb = DEC_SEQ // tq

    def band(d):
        return pl.BlockSpec((tq, LANES), lambda b, n: (b * nb + jnp.clip(n + d, 0, nb - 1), 0))

    ctx = pl.BlockSpec((None, PAST_LEN, LANES), lambda b, n: (b, 0, 0))
    return pl.pallas_call(
        _win_attn_kernel,
        grid=(DEC_BATCH, nb),
        in_specs=[pl.BlockSpec(memory_space=pltpu.SMEM),
                  pl.BlockSpec((tq, A_HEADS * LANES), lambda b, n: (b * nb + n, 0)),
                  band(-1), band(0), band(1), band(-1), band(0), band(1), ctx, ctx,
                  pl.BlockSpec((1, 512), lambda b, n: (0, 0)),
                  pl.BlockSpec(memory_space=pl.ANY)],
        out_specs=pl.BlockSpec((tq, 512), lambda b, n: (N_CTX // tq + b * nb + n, 0)),
        out_shape=jax.ShapeDtypeStruct((N_TOK, 512), BF16),
        input_output_aliases={11: 0},
        compiler_params=_params("parallel", "parallel"),
        name="window_attention",
    )(scal, qa_p, ka_r, ka_r, ka_r, va_b, va_b, va_b, cache_k, cache_v, g_mix2d, o_all)


def _diff_attn_kernel(scal_ref, q_ref, k_ref, v_ref, kctx_ref, vctx_ref, g_ref, _all_rows_ref, o_ref, *,
                      lam_init):
    lam = scal_ref[A_HEADS]
    chunks = []
    for h in range(B_HEADS):
        sl = slice(h * LANES, (h + 1) * LANES)
        kh = k_ref[:, sl]
        kc = kctx_ref[:, sl].astype(BF16)
        es, dens = [], []
        for m in range(2):
            q = q_ref[:, (2 * h + m) * LANES:(2 * h + m + 1) * LANES]
            t_lat = _nt(q, kh) * (SCALE * LOG2E)
            t_ctx = _nt(q, kc) * (SCALE * LOG2E)
            mx = jnp.maximum(t_lat.max(axis=-1, keepdims=True), t_ctx.max(axis=-1, keepdims=True))
            e_lat = jnp.exp2(t_lat - mx)
            e_ctx = jnp.exp2(t_ctx - mx)
            es.append((e_lat, e_ctx))
            dens.append(e_lat.sum(axis=-1, keepdims=True) + e_ctx.sum(axis=-1, keepdims=True))
        c1 = 1.0 / dens[0]
        c2 = lam / dens[1]
        w_lat = (es[0][0] * c1 - es[1][0] * c2).astype(BF16)
        w_ctx = (es[0][1] * c1 - es[1][1] * c2).astype(BF16)
        o = _mm(w_lat, v_ref[:, sl]) + _mm(w_ctx, vctx_ref[:, sl].astype(BF16))
        chunks.append(_rms_gain(o, g_ref[:, sl]) * (1.0 - lam_init))
    o_ref[...] = jnp.concatenate(chunks, axis=1).astype(BF16)


def _diff_attention(qb_p, kb_r, vb_b, cache_k, cache_v, scal, g_mix2d, lam_init, o_all):
    tq = TQ_B
    nb = DEC_SEQ // tq
    full = pl.BlockSpec((DEC_SEQ, 512), lambda b, n: (b, 0))
    ctx = pl.BlockSpec((None, PAST_LEN, 512), lambda b, n: (b, 0, 0))
    return pl.pallas_call(
        functools.partial(_diff_attn_kernel, lam_init=lam_init),
        grid=(DEC_BATCH, nb),
        in_specs=[pl.BlockSpec(memory_space=pltpu.SMEM),
                  pl.BlockSpec((tq, 2 * B_HEADS * LANES), lambda b, n: (b * nb + n, 0)),
                  full, full, ctx, ctx,
                  pl.BlockSpec((1, 512), lambda b, n: (0, 1)),
                  pl.BlockSpec(memory_space=pl.ANY)],
        out_specs=pl.BlockSpec((tq, 512), lambda b, n: (N_CTX // tq + b * nb + n, 0)),
        out_shape=jax.ShapeDtypeStruct((N_TOK, 512), BF16),
        input_output_aliases={7: 0},
        compiler_params=_params("parallel", "parallel"),
        name="diff_attention",
    )(scal, qb_p, kb_r, vb_b, cache_k, cache_v, g_mix2d, o_all)


def _na_window_start(step):
    return np.clip(step * NA_QR - NA_ROWS // 2, 0, GRID_W - NA_KR)


def _na_bias_indices():
    rows = DEC_SEQ // GRID_W
    steps = rows // NA_QR
    pats = []
    for step in range(steps):
        w0 = _na_window_start(step)
        r = step * NA_QR + np.arange(NA_QR)[:, None, None, None]
        c = np.arange(GRID_W)[None, :, None, None]
        kr = w0 + np.arange(NA_KR)[None, None, :, None]
        kc = np.arange(GRID_W)[None, None, None, :]
        kr0 = np.clip(r - NA_ROWS // 2, 0, rows - NA_ROWS)
        kc0 = np.clip(c - NA_COLS // 2, 0, GRID_W - NA_COLS)
        valid = (kr >= kr0) & (kr < kr0 + NA_ROWS) & (kc >= kc0) & (kc < kc0 + NA_COLS)
        drow = np.clip(kr - r + NA_ROWS - 1, 0, 2 * NA_ROWS - 2)
        dcol = np.clip(kc - c + NA_COLS - 1, 0, 2 * NA_COLS - 2)
        shape = (NA_QR * GRID_W, NA_KR * GRID_W)
        full = np.broadcast_to
        pats.append((full(drow, valid.shape).reshape(shape), full(dcol, valid.shape).reshape(shape),
                     valid.reshape(shape)))
    for step in range(2, steps - 1):
        for a, b in zip(pats[1], pats[step]):
            assert np.array_equal(a, b)
    kinds = [pats[0], pats[1], pats[-1]]
    drow, dcol, valid = (np.stack([k[i] for k in kinds]) for i in range(3))
    shape6 = (len(kinds), NA_QR, GRID_W, NA_KR, GRID_W)
    valid6 = valid.reshape(shape6)
    row_ok = valid6.any(axis=(2, 4))
    col_ok = valid6.any(axis=(0, 1, 3))
    assert np.array_equal(valid6, row_ok[:, :, None, :, None] & col_ok[None, None, :, None, :])
    outside = 2 * NA_ROWS - 1
    row_idx = np.where(row_ok, drow.reshape(shape6)[:, :, 0, :, 0], outside)
    col_sel = np.eye(2 * NA_COLS - 1, dtype=np.float32)[dcol.reshape(shape6)[0, 0, :, 0, :]]
    return row_idx, col_sel, col_ok


def _na_bias(rpb, row_idx, col_sel, col_ok):
    cols = jnp.einsum('hab,cdb->hacd', rpb, col_sel, precision=lax.Precision.HIGHEST)
    cols = jnp.where(col_ok[None, None], cols, NEG_INF)
    cols = jnp.concatenate([cols, jnp.full((D_HEADS, 1, GRID_W, GRID_W), NEG_INF, F32)], axis=1)
    kinds = []
    for kind in row_idx:
        kinds.append(jnp.concatenate(
            [jnp.concatenate([cols[:, int(a)] for a in per_row], axis=-1) for per_row in kind], axis=1))
    return jnp.stack(kinds)


def _na_kernel(q_ref, k_ref, v_ref, kctx_ref, vctx_ref, bias_ref, g_ref, _all_rows_ref, o_ref):
    step = pl.program_id(1)
    nq = NA_QR * GRID_W
    nk = NA_KR * GRID_W
    w0 = jnp.clip(step * NA_QR - NA_ROWS // 2, 0, GRID_W - NA_KR)
    start = pl.multiple_of(w0 * GRID_W, GRID_W)
    lo = _low_half((nq, LANES))
    chunks = []
    for c in range(D_HEADS // 2):
        sl = slice(c * LANES, (c + 1) * LANES)
        kw = k_ref[pl.ds(start, nk), sl]
        vw = v_ref[pl.ds(start, nk), sl]
        kc = kctx_ref[:, sl].astype(BF16)
        vc = vctx_ref[:, sl].astype(BF16)
        halves = []
        for half in range(2):
            h = 2 * c + half
            q = q_ref[:, h * LANES:(h + 1) * LANES]
            s_loc = _nt(q, kw) * SCALE + bias_ref[h]
            s_ctx = _nt(q, kc) * SCALE
            (e_loc, e_ctx), den = _softmax_unnorm([s_loc, s_ctx])
            halves.append((_mm(e_loc.astype(BF16), vw) + _mm(e_ctx.astype(BF16), vc)) / den)
        chunks.append(jnp.where(lo, halves[0], halves[1]))
    od = jnp.concatenate(chunks, axis=1)
    o_ref[...] = _rms_gain(od, g_ref[...]).astype(BF16)


def _neighbourhood_attention(qd_p, kd_b, vd_b, cache_k, cache_v, bias, g_mix2d, o_all):
    nq = NA_QR * GRID_W
    steps = DEC_SEQ // nq
    full = pl.BlockSpec((DEC_SEQ, 512), lambda b, j: (b, 0))
    ctx = pl.BlockSpec((None, PAST_LEN, 512), lambda b, j: (b, 0, 0))

    def kind(b, j):
        return (jnp.where(j == 0, 0, jnp.where(j == steps - 1, 2, 1)), 0, 0, 0)

    return pl.pallas_call(
        _na_kernel,
        grid=(DEC_BATCH, steps),
        in_specs=[pl.BlockSpec((nq, D_HEADS * LANES), lambda b, j: (b * steps + j, 0)),
                  full, full, ctx, ctx,
                  pl.BlockSpec((None, D_HEADS, nq, NA_KR * GRID_W), kind),
                  pl.BlockSpec((1, 512), lambda b, j: (0, 3)),
                  pl.BlockSpec(memory_space=pl.ANY)],
        out_specs=pl.BlockSpec((nq, 512), lambda b, j: (N_CTX // nq + b * steps + j, 0)),
        out_shape=jax.ShapeDtypeStruct((N_TOK, 512), BF16),
        input_output_aliases={7: 0},
        compiler_params=_params("parallel", "arbitrary"),
        name="neighbourhood_attention",
    )(qd_p, kd_b, vd_b, cache_k, cache_v, bias, g_mix2d, o_all)


N_SCAN_TILES = N_TOK // T_SCAN
CTX_TILES = N_CTX // T_SCAN
TILES_PER_LAT = DEC_SEQ // T_SCAN


def _scan_tile_flags(i):
    is_ctx = i < CTX_TILES
    pos = (i - CTX_TILES) % TILES_PER_LAT
    return is_ctx | (pos == 0), is_ctx | (pos == TILES_PER_LAT - 1)


def _scan_seq(i):
    return jnp.where(i < CTX_TILES, i, CTX_TILES + (i - CTX_TILES) // TILES_PER_LAT)


def _conv_gates(x_ref, prev_ref, next_ref, cw_ref, cb_ref, wgh_ref, wgl_ref, bg_ref, lam_ref, first, last):
    x = x_ref[...]
    t = T_SCAN
    row = lax.broadcasted_iota(jnp.int32, x.shape, 0)
    pm1 = jnp.where(first, 0.0, prev_ref[7:8, :])
    n0 = jnp.where(last, 0.0, next_ref[0:1, :])
    n1 = jnp.where(last, 0.0, next_ref[1:2, :])
    x_m1 = jnp.where(row == 0, pm1, pltpu.roll(x, 1, 0))
    x_p1 = jnp.where(row == t - 1, n0, pltpu.roll(x, t - 1, 0))
    x_p2 = jnp.where(row == t - 2, n0, jnp.where(row == t - 1, n1, pltpu.roll(x, t - 2, 0)))
    y = (cw_ref[0:1, :] * x_m1 + cw_ref[1:2, :] * x + cw_ref[2:3, :] * x_p1 + cw_ref[3:4, :] * x_p2
         + cb_ref[...])
    y_hi = y.astype(BF16)
    y_lo = (y - y_hi.astype(F32)).astype(BF16)
    zr, zi = [], []
    for c in range(C_WIDTH // LANES):
        sl = slice(c * LANES, (c + 1) * LANES)
        w_hi = wgh_ref[0, sl, :]
        z = _mm(y_hi[:, sl], w_hi) + _mm(y_lo[:, sl], w_hi) + _mm(y_hi[:, sl], wgl_ref[0, sl, :])
        zr.append(z[:, :LANES])
        zi.append(z[:, LANES:])
    bias = bg_ref[0]
    r = jax.nn.sigmoid(jnp.concatenate(zr, axis=1) + bias[:, :C_WIDTH])
    gate_i = jax.nn.sigmoid(jnp.concatenate(zi, axis=1) + bias[:, C_WIDTH:])
    nl = -lam_ref[0]
    softplus = jnp.maximum(nl, 0.0) + jnp.log1p(jnp.exp(-jnp.abs(nl)))
    log_a = -C_POW * r * softplus
    a = jnp.exp(log_a)
    u = jnp.sqrt(1.0 - jnp.exp(2.0 * log_a)) * gate_i * y
    return a, u


def _scan_rows(a_s, u_s, h_s, h, reverse):
    groups = T_SCAN // 8

    def body(i, h):
        g = (groups - 1 - i) if reverse else i
        base = pl.multiple_of(g * 8, 8)
        a8 = a_s[pl.ds(base, 8), :]
        u8 = u_s[pl.ds(base, 8), :]
        rows = [None] * 8
        for k in (range(7, -1, -1) if reverse else range(8)):
            h = a8[k:k + 1, :] * h + u8[k:k + 1, :]
            rows[k] = h
        h_s[pl.ds(base, 8), :] = jnp.concatenate(rows, axis=0)
        return h

    return lax.fori_loop(0, groups, body, h)


def _scan_fwd_kernel(x_ref, prev_ref, next_ref, cw_ref, cb_ref, wgh_ref, wgl_ref, bg_ref, lam_ref, h0_ref,
                     hf_ref, fin_ref, a_s, u_s, carry_s):
    i = pl.program_id(0)
    first, last = _scan_tile_flags(i)
    a, u = _conv_gates(x_ref, prev_ref, next_ref, cw_ref, cb_ref, wgh_ref, wgl_ref, bg_ref, lam_ref, first, last)
    a_s[...] = a
    u_s[...] = u

    @pl.when(first)
    def _():
        carry_s[...] = h0_ref[...]

    h = _scan_rows(a_s, u_s, hf_ref, carry_s[...], reverse=False)
    carry_s[...] = h
    fin_ref[...] = h


def _scan_bwd_kernel(x_ref, prev_ref, next_ref, cw_ref, cb_ref, wgh_ref, wgl_ref, bg_ref, lam_ref, h0_ref,
                     hf_ref, gc_ref, g_ref, oc_ref, fin_ref, a_s, u_s, hb_s, carry_s):
    i = N_SCAN_TILES - 1 - pl.program_id(0)
    first, last = _scan_tile_flags(i)
    a, u = _conv_gates(x_ref, prev_ref, next_ref, cw_ref, cb_ref, wgh_ref, wgl_ref, bg_ref, lam_ref, first, last)
    a_s[...] = a
    u_s[...] = u

    @pl.when(last)
    def _():
        carry_s[...] = h0_ref[...]

    h = _scan_rows(a_s, u_s, hb_s, carry_s[...], reverse=True)
    carry_s[...] = h
    fin_ref[...] = h
    g = gc_ref[...]
    gelu = 0.5 * g * (1.0 + jnp.tanh(math.sqrt(2.0 / math.pi) * (g + 0.044715 * (g * g * g))))
    oc_ref[...] = _rms_gain(gelu * (hf_ref[...] + hb_s[...]), g_ref[...]).astype(BF16)


def _scan_common_specs(tile_of, direction):
    halo = T_SCAN // 8
    last_block = N_TOK // 8 - 1
    return [pl.BlockSpec((T_SCAN, 512), lambda j: (tile_of(j), COL_XC)),
            pl.BlockSpec((8, 512), lambda j: (jnp.maximum(tile_of(j) * halo - 1, 0), COL_XC)),
            pl.BlockSpec((8, 512), lambda j: (jnp.minimum((tile_of(j) + 1) * halo, last_block), COL_XC)),
            pl.BlockSpec((4, C_WIDTH), lambda j: (0, 0)),
            pl.BlockSpec((1, C_WIDTH), lambda j: (0, 0)),
            pl.BlockSpec((1, C_WIDTH, 2 * LANES), lambda j: (direction, 0, 0)),
            pl.BlockSpec((1, C_WIDTH, 2 * LANES), lambda j: (direction, 0, 0)),
            pl.BlockSpec((1, 1, 2 * C_WIDTH), lambda j: (direction, 0, 0)),
            pl.BlockSpec((1, 1, C_WIDTH), lambda j: (direction, 0, 0)),
            pl.BlockSpec((None, 1, C_WIDTH), lambda j: (_scan_seq(tile_of(j)), 0, 0))]


def _rglru(slab, conv_w, conv_b, w_gates, b_gates, lam, h0_f, h0_b, g_mix2d):
    wg_hi = w_gates.astype(BF16)
    wg_lo = (w_gates - wg_hi.astype(F32)).astype(BF16)
    fin = jax.ShapeDtypeStruct((N_SCAN_TILES, 1, C_WIDTH), F32)
    fwd_tile = lambda j: j
    h_f, fin_f = pl.pallas_call(
        _scan_fwd_kernel,
        grid=(N_SCAN_TILES,),
        in_specs=_scan_common_specs(fwd_tile, 0),
        out_specs=[pl.BlockSpec((T_SCAN, C_WIDTH), lambda j: (j, 0)),
                   pl.BlockSpec((None, 1, C_WIDTH), lambda j: (j, 0, 0))],
        out_shape=[jax.ShapeDtypeStruct((N_TOK, C_WIDTH), F32), fin],
        scratch_shapes=[pltpu.VMEM((T_SCAN, C_WIDTH), F32), pltpu.VMEM((T_SCAN, C_WIDTH), F32),
                        pltpu.VMEM((1, C_WIDTH), F32)],
        compiler_params=_params("arbitrary"),
        name="rglru_forward",
    )(slab, slab, slab, conv_w, conv_b, wg_hi, wg_lo, b_gates, lam, h0_f)
    bwd_tile = lambda j: N_SCAN_TILES - 1 - j
    oc, fin_b = pl.pallas_call(
        _scan_bwd_kernel,
        grid=(N_SCAN_TILES,),
        in_specs=_scan_common_specs(bwd_tile, 1) + [
            pl.BlockSpec((T_SCAN, C_WIDTH), lambda j: (bwd_tile(j), 0)),
            pl.BlockSpec((T_SCAN, 512), lambda j: (bwd_tile(j), COL_GC)),
            pl.BlockSpec((1, 512), lambda j: (0, 2))],
        out_specs=[pl.BlockSpec((T_SCAN, C_WIDTH), lambda j: (bwd_tile(j), 0)),
                   pl.BlockSpec((None, 1, C_WIDTH), lambda j: (bwd_tile(j), 0, 0))],
        out_shape=[jax.ShapeDtypeStruct((N_TOK, C_WIDTH), BF16), fin],
        scratch_shapes=[pltpu.VMEM((T_SCAN, C_WIDTH), F32), pltpu.VMEM((T_SCAN, C_WIDTH), F32),
                        pltpu.VMEM((T_SCAN, C_WIDTH), F32), pltpu.VMEM((1, C_WIDTH), F32)],
        compiler_params=_params("arbitrary"),
        name="rglru_backward",
    )(slab, slab, slab, conv_w, conv_b, wg_hi, wg_lo, b_gates, lam, h0_b, h_f, slab, g_mix2d)
    return oc, fin_f, fin_b


def _mixout_kernel(oa_ref, ob_ref, oc_ref, od_ref, w_ref, xc_ref, xl_ref, g1_ref, sc2_ref, sh2_ref, lng_ref,
                   lnb_ref, wrh_ref, wrl_ref, br_ref, x1_ref, h2_ref, idx_ref, wt_ref, rank_ref, cnt_ref, cnt_s):
    @pl.when(pl.program_id(0) == 0)
    def _():
        cnt_s[...] = jnp.zeros(cnt_s.shape, F32)

    parts = 2
    pm = xc_ref.shape[0] // parts
    is_ctx = _is_ctx_tile(xc_ref.shape[0])
    for part in range(parts):
        rows = pl.ds(part * pm, pm)
        x = jnp.where(is_ctx, xc_ref[rows, :], xl_ref[rows, :])
        _mixout_rows(rows, part * pm, pm, oa_ref, ob_ref, oc_ref, od_ref, w_ref, x, g1_ref,
                     sc2_ref, sh2_ref, lng_ref, lnb_ref, wrh_ref, wrl_ref, br_ref, x1_ref, h2_ref, idx_ref,
                     wt_ref, rank_ref, cnt_s)
    cnt_ref[...] = cnt_s[...]


def _mixout_rows(rows, first, tm, oa_ref, ob_ref, oc_ref, od_ref, w_ref, x, g1_ref, sc2_ref, sh2_ref,
                 lng_ref, lnb_ref, wrh_ref, wrl_ref, br_ref, x1_ref, h2_ref, idx_ref, wt_ref, rank_ref, cnt_s):
    y = _mm(oa_ref[rows, :], w_ref[0:512, :])
    y += _mm(ob_ref[rows, :], w_ref[512:1024, :])
    y += _mm(oc_ref[rows, :], w_ref[1024:1536, :])
    y += _mm(od_ref[rows, :], w_ref[1536:2048, :])
    x1 = _layernorm(ALPHA * x + g1_ref[...] * y, lng_ref[...], lnb_ref[...])
    x1_ref[rows, :] = x1
    h2 = x1 * (1.0 + sc2_ref[...]) + sh2_ref[...]
    _store_token_tiles(h2_ref, (), _pack_rows(h2), first)

    h_hi = h2.astype(BF16)
    h_lo = (h2 - h_hi.astype(F32)).astype(BF16)
    scores = jax.nn.sigmoid(_mm(h_hi, wrh_ref[...]) + _mm(h_lo, wrh_ref[...]) + _mm(h_hi, wrl_ref[...]))
    sel = scores + br_ref[...]
    lane = lax.broadcasted_iota(jnp.int32, sel.shape, 1).astype(F32)
    slot = lax.broadcasted_iota(jnp.int32, (tm, 8), 1)
    wide = lax.broadcasted_iota(jnp.int32, (tm, LANES), 1)
    idx_out = jnp.zeros((tm, LANES), F32)
    wt_out = jnp.zeros((tm, 8), F32)
    total = jnp.zeros((sel.shape[0], 1), F32)
    hits = []
    for k in range(TOP_K):
        m = sel.max(axis=-1, keepdims=True)
        idx = jnp.where(sel == m, lane, float(N_EXPERTS)).min(axis=-1, keepdims=True)
        hit = lane == idx
        hits.append(hit)
        w = jnp.where(hit, scores, 0.0).sum(axis=-1, keepdims=True)
        total = total + w
        idx_out = jnp.where(wide == k, idx, idx_out)
        wt_out = jnp.where(slot == k, w, wt_out)
        sel = jnp.where(hit, -jnp.inf, sel)
    cols = pl.ds(first, tm)
    idx_ref[:, cols] = idx_out.T[:8, :].astype(jnp.int32)
    wt_ref[rows, :] = wt_out / total * ROUTED_SCALE

    chosen = jnp.zeros(sel.shape, F32)
    for hit in hits:
        chosen = chosen + hit.astype(F32)
    earlier = (lax.broadcasted_iota(jnp.int32, (tm, tm), 1) < lax.broadcasted_iota(jnp.int32, (tm, tm), 0))
    before = _mm(earlier.astype(F32).astype(BF16), chosen.astype(BF16)) + cnt_s[...]
    rank_out = jnp.zeros((tm, LANES), F32)
    for k, hit in enumerate(hits):
        rank_out = jnp.where(wide == k, jnp.where(hit, before, 0.0).sum(axis=-1, keepdims=True), rank_out)
    rank_ref[:, cols] = rank_out.T[:8, :].astype(jnp.int32)
    cnt_s[...] = cnt_s[...] + chosen.sum(axis=0, keepdims=True)


def _mixer_out(oa, ob, oc, od, w_out_b, x_ctx, x_lat, mod, ln_g, ln_b, w_router, b_router):
    tm = TM_OUT
    cat = pl.BlockSpec((tm, 512), lambda i: (i, 0))
    row = pl.BlockSpec((tm, D_MODEL), lambda i: (i, 0))
    vec = pl.BlockSpec((1, D_MODEL), lambda i: (0, 0))
    k8 = pl.BlockSpec((tm, 8), lambda i: (i, 0))
    k_major = pl.BlockSpec((8, tm), lambda i: (0, i))
    per_expert = pl.BlockSpec((1, N_EXPERTS), lambda i: (0, 0))
    router = pl.BlockSpec((D_MODEL, N_EXPERTS), lambda i: (0, 0))
    wr_hi = w_router.astype(BF16)
    wr_lo = (w_router - wr_hi.astype(F32)).astype(BF16)
    return pl.pallas_call(
        _mixout_kernel,
        grid=(N_TOK // tm,),
        in_specs=[cat, cat, cat, cat,
                  pl.BlockSpec((D_MODEL, D_MODEL), lambda i: (0, 0)),
                  _ctx_rows(tm), _lat_rows(tm), _mod_spec(2, tm), _mod_spec(4, tm), _mod_spec(3, tm), vec, vec,
                  router, router, per_expert],
        out_specs=[row, pl.BlockSpec((tm * SUBLANES, LANES), lambda i: (i, 0)), k_major, k8, k_major, per_expert],
        out_shape=[jax.ShapeDtypeStruct((N_TOK, D_MODEL), F32),
                   jax.ShapeDtypeStruct((N_TOK * SUBLANES, LANES), jnp.uint32),
                   jax.ShapeDtypeStruct((8, N_TOK), jnp.int32), jax.ShapeDtypeStruct((N_TOK, 8), F32),
                   jax.ShapeDtypeStruct((8, N_TOK), jnp.int32), jax.ShapeDtypeStruct((1, N_EXPERTS), F32)],
        scratch_shapes=[pltpu.VMEM((1, N_EXPERTS), F32)],
        compiler_params=_params("arbitrary"),
        name="mixer_out_router",
    )(oa, ob, oc, od, w_out_b, x_ctx, x_lat, mod, mod, mod, ln_g, ln_b, wr_hi, wr_lo, b_router)


MOE_BLOCKS = N_TOK * TOP_K // MOE_BM + N_EXPERTS
MOE_ROWS = MOE_BLOCKS * MOE_BM


def _row_copy(src, src_row, dst, dst_row, sem):
    def tile(row):
        return pl.ds(row * SUBLANES if isinstance(row, int) else pl.multiple_of(row * SUBLANES, SUBLANES), SUBLANES)

    return pltpu.make_async_copy(src.at[tile(src_row)], dst.at[tile(dst_row)], sem)


def _dispatch_kernel(slot_ref, h_ref, xg_ref, sem):
    tm = h_ref.shape[0] // SUBLANES
    base = pl.program_id(0) * tm

    for r in range(tm):
        for k in range(TOP_K):
            _row_copy(h_ref, r, xg_ref, slot_ref[base + (k * N_TOK + r)], sem).start(priority=k % 2)
    for k in range(TOP_K):
        pltpu.make_async_copy(h_ref, xg_ref.at[pl.ds(0, tm * SUBLANES)], sem).wait()


def _dispatch_rows(slot_flat, h2):
    tm = TM_DISPATCH
    grid_spec = pltpu.PrefetchScalarGridSpec(
        num_scalar_prefetch=1,
        grid=(N_TOK // tm,),
        in_specs=[pl.BlockSpec((tm * SUBLANES, LANES), lambda i, s: (i, 0))],
        out_specs=pl.BlockSpec(memory_space=pl.ANY),
        scratch_shapes=[pltpu.SemaphoreType.DMA])
    return pl.pallas_call(
        _dispatch_kernel,
        grid_spec=grid_spec,
        out_shape=jax.ShapeDtypeStruct((MOE_ROWS * SUBLANES, LANES), jnp.uint32),
        compiler_params=_params("arbitrary"),
        name="moe_dispatch",
    )(slot_flat, h2)


BLOCK_TILE_ROWS = MOE_BM * SUBLANES
EXPERT_X_BUFFERS = 3


def _expert_kernel(blk0_ref, nblk_ref, cnt_ref, wg_ref, wu_ref, wd_ref, x_hbm, y_hbm,
                   wg_s, wu_s, wd_s, xbuf, ybuf, xsem, ysem):
    e = pl.program_id(0)
    first_blk = blk0_ref[e]
    n_blk = nblk_ref[e]
    count = cnt_ref[e]
    total = blk0_ref[N_EXPERTS - 1] + nblk_ref[N_EXPERTS - 1]

    def rows_of(g):
        return pl.ds(pl.multiple_of(g * BLOCK_TILE_ROWS, BLOCK_TILE_ROWS), BLOCK_TILE_ROWS)

    def x_copy(g, buf):
        return pltpu.make_async_copy(x_hbm.at[rows_of(g)], xbuf.at[buf], xsem.at[buf])

    def y_copy(g, buf):
        return pltpu.make_async_copy(ybuf.at[buf], y_hbm.at[rows_of(g)], ysem.at[buf])

    for ahead in range(EXPERT_X_BUFFERS - 1):
        @pl.when((e == 0) & (total > ahead))
        def _():
            x_copy(ahead, ahead).start()

    wg_s[...] = wg_ref[...].astype(BF16)
    wu_s[...] = wu_ref[...].astype(BF16)
    wd_s[...] = wd_ref[...].astype(BF16)

    def block(j, carry):
        g = first_blk + j
        buf = g % 2
        xb = lax.rem(g, EXPERT_X_BUFFERS)
        nxt = g + (EXPERT_X_BUFFERS - 1)

        @pl.when(nxt < total)
        def _():
            x_copy(nxt, lax.rem(nxt, EXPERT_X_BUFFERS)).start()

        x_copy(g, xb).wait()

        @pl.when(g >= 2)
        def _():
            y_copy(g - 2, buf).wait()

        x = _load_token_tiles(xbuf, (xb,), MOE_BM)
        row = lax.broadcasted_iota(jnp.int32, x.shape, 0)
        lo, hi = _unpack_rows(jnp.where(row < count - j * MOE_BM, x, jnp.uint32(0)))
        lo, hi = lo.astype(BF16), hi.astype(BF16)
        gate = _mm(lo, wg_s[:HALF_D, :]) + _mm(hi, wg_s[HALF_D:, :])
        up = _mm(lo, wu_s[:HALF_D, :]) + _mm(hi, wu_s[HALF_D:, :])
        _store_token_tiles(ybuf, (buf,), _pack_rows(_mm((_silu(gate) * up).astype(BF16), wd_s[...])))
        y_copy(g, buf).start()
        return carry

    lax.fori_loop(0, n_blk, block, 0)

    @pl.when((e == N_EXPERTS - 1) & (total >= 2))
    def _():
        y_copy(total - 2, total % 2).wait()

    @pl.when((e == N_EXPERTS - 1) & (total >= 1))
    def _():
        y_copy(total - 1, (total - 1) % 2).wait()


def _routed_experts(xg, first_blk, n_blk, counts, layer, w_g, w_u, w_d):
    def weight(shape):
        return pl.BlockSpec((None, None) + shape, lambda e, *_: (layer, e, 0, 0))

    any_space = pl.BlockSpec(memory_space=pl.ANY)
    x_bufs = pltpu.VMEM((EXPERT_X_BUFFERS, BLOCK_TILE_ROWS, LANES), jnp.uint32)
    y_bufs = pltpu.VMEM((2, BLOCK_TILE_ROWS, LANES), jnp.uint32)
    grid_spec = pltpu.PrefetchScalarGridSpec(
        num_scalar_prefetch=3,
        grid=(N_EXPERTS,),
        in_specs=[weight((D_MODEL, D_EXPERT)), weight((D_MODEL, D_EXPERT)), weight((D_EXPERT, D_MODEL)), any_space],
        out_specs=any_space,
        scratch_shapes=[pltpu.VMEM((D_MODEL, D_EXPERT), BF16), pltpu.VMEM((D_MODEL, D_EXPERT), BF16),
                        pltpu.VMEM((D_EXPERT, D_MODEL), BF16), x_bufs, y_bufs,
                        pltpu.SemaphoreType.DMA((EXPERT_X_BUFFERS,)), pltpu.SemaphoreType.DMA((2,))])
    return pl.pallas_call(
        _expert_kernel,
        grid_spec=grid_spec,
        out_shape=jax.ShapeDtypeStruct((MOE_ROWS * SUBLANES, LANES), jnp.uint32),
        compiler_params=_params("arbitrary"),
        name="routed_experts",
    )(first_blk, n_blk, counts, w_g, w_u, w_d, xg)


def _ffn_out_kernel(slot_ref, h_ref, wt_ref, x1_ref, wg_ref, wu_ref, wd_ref, g2_ref, lng_ref, lnb_ref, y_ref,
                    oc_ref, ol_ref, ybuf, sems):
    tm = h_ref.shape[0] // SUBLANES
    i = pl.program_id(0)
    n = pl.num_programs(0)

    def gather(tile, buf):
        base = tile * tm

        for r in range(tm):
            for k in range(TOP_K):
                _row_copy(y_ref, slot_ref[base + (k * N_TOK + r)], ybuf.at[buf, k], r,
                          sems.at[buf]).start(priority=k % 2)

    @pl.when(i == 0)
    def _():
        gather(0, 0)

    @pl.when(i + 1 < n)
    def _():
        gather(i + 1, (i + 1) % 2)

    lo, hi = _unpack_rows(_load_token_tiles(h_ref, (), tm))
    lo, hi = lo.astype(BF16), hi.astype(BF16)
    gate = _mm(lo, wg_ref[:HALF_D, :]) + _mm(hi, wg_ref[HALF_D:, :])
    up = _mm(lo, wu_ref[:HALF_D, :]) + _mm(hi, wu_ref[HALF_D:, :])
    y = _mm((_silu(gate) * up).astype(BF16), wd_ref[...])
    buf = i % 2
    for k in range(TOP_K):
        pltpu.make_async_copy(y_ref.at[pl.ds(0, tm * SUBLANES)], ybuf.at[buf, k], sems.at[buf]).wait()
    wt = wt_ref[...]
    y_lo, y_hi = y[:, :HALF_D], y[:, HALF_D:]
    for k in range(TOP_K):
        r_lo, r_hi = _unpack_rows(_load_token_tiles(ybuf, (buf, k), tm))
        y_lo = y_lo + r_lo * wt[:, k:k + 1]
        y_hi = y_hi + r_hi * wt[:, k:k + 1]
    y = jnp.concatenate([y_lo, y_hi], axis=1)
    out = _layernorm(ALPHA * x1_ref[...] + g2_ref[...] * y, lng_ref[...], lnb_ref[...])
    is_ctx = _is_ctx_tile(tm)

    @pl.when(is_ctx)
    def _():
        oc_ref[...] = out

    @pl.when(jnp.logical_not(is_ctx))
    def _():
        ol_ref[...] = out


def _ffn_out(slot_flat, h2, wt, x1, ws_g, ws_u, ws_d, mod, ln_g, ln_b, y_blk):
    tm = TM_FFN
    row = pl.BlockSpec((tm, D_MODEL), lambda i, s: (i, 0))
    vec = pl.BlockSpec((1, D_MODEL), lambda i, s: (0, 0))
    grid_spec = pltpu.PrefetchScalarGridSpec(
        num_scalar_prefetch=1,
        grid=(N_TOK // tm,),
        in_specs=[pl.BlockSpec((tm * SUBLANES, LANES), lambda i, s: (i, 0)),
                  pl.BlockSpec((tm, 8), lambda i, s: (i, 0)), row,
                  pl.BlockSpec((D_MODEL, D_EXPERT), lambda i, s: (0, 0)),
                  pl.BlockSpec((D_MODEL, D_EXPERT), lambda i, s: (0, 0)),
                  pl.BlockSpec((D_EXPERT, D_MODEL), lambda i, s: (0, 0)),
                  _mod_spec(5, tm), vec, vec,
                  pl.BlockSpec(memory_space=pl.ANY)],
        out_specs=[_ctx_rows(tm), _lat_rows(tm)],
        scratch_shapes=[pltpu.VMEM((2, TOP_K, tm * SUBLANES, LANES), jnp.uint32),
                        pltpu.SemaphoreType.DMA((2,))])
    return pl.pallas_call(
        _ffn_out_kernel,
        grid_spec=grid_spec,
        out_shape=[jax.ShapeDtypeStruct((N_CTX, D_MODEL), F32), jax.ShapeDtypeStruct((N_LAT, D_MODEL), F32)],
        compiler_params=_params("arbitrary"),
        name="shared_expert_out",
    )(slot_flat, h2, wt, x1, ws_g, ws_u, ws_d, mod, ln_g, ln_b, y_blk)


def _dispatch_plan(idx, rank, counts):
    bm = MOE_BM
    counts = counts.reshape(N_EXPERTS).astype(jnp.int32)
    padded = (counts + bm - 1) // bm * bm
    pad_end = jnp.cumsum(padded)
    pad_start = pad_end - padded
    experts = jnp.arange(N_EXPERTS, dtype=jnp.int32)[:, None, None]
    first_row = jnp.sum(jnp.where(idx[None, :TOP_K] == experts, pad_start[:, None, None], 0), axis=0)
    slot = first_row + rank[:TOP_K]
    return slot.reshape(-1).astype(jnp.int32), pad_start // bm, padded // bm, counts


def _rope_tables():
    t = jnp.arange(DEC_SEQ)
    row = (t // GRID_W).astype(F32)
    col = (t % GRID_W).astype(F32)
    nf = HEAD_DIM // 4
    inv = ROPE_BASE ** (-jnp.arange(nf, dtype=F32) / nf)
    ar = row[:, None] * inv[None, :]
    ac = col[:, None] * inv[None, :]
    ang = jnp.concatenate([ar, ar, ac, ac], axis=-1)
    ang = jnp.concatenate([ang, ang], axis=-1)
    cos, sin = jnp.cos(ang), jnp.sin(ang)
    first = (jnp.arange(LANES) % (2 * nf)) < nf
    return cos, jnp.where(first, -sin, 0.0), jnp.where(first, 0.0, sin)


def _lambda_init(l):
    return 0.8 - 0.6 * math.exp(-0.3 * l)


def _gate_pairs(w_r, w_i):
    per = LANES // C_BW
    eye = jnp.eye(per, dtype=w_r.dtype)

    def chunk_diag(w):
        w = w.reshape(C_BLOCKS // per, per, C_BW, C_BW)
        return (eye[None, :, None, :, None] * w[:, :, :, None, :]).reshape(C_BLOCKS // per, LANES, LANES)

    return jnp.concatenate([chunk_diag(w_r), chunk_diag(w_i)], axis=2).reshape(C_WIDTH, 2 * LANES)


def kernel(x_prompt, x_sample, c, cache_a_k, cache_a_v, cache_b_k, cache_b_v, state_c, cache_d_k, cache_d_v, c_ctx, w_mod, b_mod, w_in, a_sink, b_lambda, c_conv_w, c_conv_b, c_w_rgate, c_b_rgate, c_w_igate, c_b_igate, c_lambda, d_rpb, g_mix, w_out, ln_g, ln_b, w_router, b_router, w_e_gate, w_e_up, w_e_down, w_s_gate, w_s_up, w_s_down):
    x_ctx, x_lat = x_prompt.reshape(N_CTX, D_MODEL), x_sample.reshape(N_LAT, D_MODEL)
    cond8 = jnp.zeros((8, D_MODEL), F32).at[0].set(c_ctx).at[1:1 + DEC_BATCH].set(c)
    mod_all = _modulation(cond8, w_mod, b_mod)
    cos, sin_a, sin_b = _rope_tables()
    na_row_idx, na_col_sel, na_col_ok = _na_bias_indices()

    order = ((0, 512), (768, IN_WIDTH), (512, 768))

    ctx_out = [[] for _ in range(7)]
    for l in range(DEPTH):
        lam_init = _lambda_init(l)
        mod = mod_all[l, :N_GROUPS].reshape(N_GROUPS, 6, 1, D_MODEL)
        w_l = w_in[l]
        w_in_b = jnp.concatenate([w_l[:, a:b] for a, b in order], axis=1).astype(BF16)
        slab = _in_projection(x_ctx, x_lat, mod, w_in_b)

        lp = b_lambda[l]
        lam = jnp.exp(jnp.sum(lp[0] * lp[1])) - jnp.exp(jnp.sum(lp[2] * lp[3])) + lam_init
        scal = jnp.zeros((16,), F32).at[:A_HEADS].set(a_sink[l]).at[A_HEADS].set(lam)
        g2d = g_mix[l].reshape(1, -1)

        oa, ob, od = _ctx_attention(slab, scal, g2d, lam_init)

        qa_p, ka_r, va_b, qb_p, kb_r, vb_b, qd_p, kd_b, vd_b = _latent_prep(slab, cos, sin_a, sin_b)
        oa = _window_attention(qa_p, ka_r, va_b, cache_a_k[:, l].reshape(DEC_BATCH, PAST_LEN, LANES),
                               cache_a_v[:, l].reshape(DEC_BATCH, PAST_LEN, LANES), scal, g2d, oa)
        ob = _diff_attention(qb_p, kb_r, vb_b, cache_b_k[:, l].reshape(DEC_BATCH, PAST_LEN, 512),
                             cache_b_v[:, l].reshape(DEC_BATCH, PAST_LEN, 512), scal, g2d, lam_init, ob)
        bias = _na_bias(d_rpb[l], na_row_idx, na_col_sel, na_col_ok)
        od = _neighbourhood_attention(qd_p, kd_b, vd_b, cache_d_k[:, l].reshape(DEC_BATCH, PAST_LEN, 512),
                                      cache_d_v[:, l].reshape(DEC_BATCH, PAST_LEN, 512), bias, g2d, od)

        w_gates = jnp.stack([_gate_pairs(c_w_rgate[l, d], c_w_igate[l, d]) for d in range(2)])
        b_gates = jnp.concatenate([c_b_rgate[l], c_b_igate[l]], axis=-1).reshape(2, 1, 2 * C_WIDTH)
        zeros = jnp.zeros((BATCH, 1, C_WIDTH), F32)
        h0_f = jnp.concatenate([zeros, state_c[:, l, 0][:, None, :]], axis=0)
        h0_b = jnp.concatenate([zeros, state_c[:, l, 1][:, None, :]], axis=0)
        oc, fin_f, fin_b = _rglru(slab, c_conv_w[l], c_conv_b[l].reshape(1, -1), w_gates, b_gates,
                                  c_lambda[l].reshape(2, 1, C_WIDTH), h0_f, h0_b, g2d)

        x1, h2, idx, wt, rank, counts = _mixer_out(oa, ob, oc, od, w_out[l].astype(BF16), x_ctx, x_lat, mod,
                                                   ln_g[l, 0].reshape(1, -1), ln_b[l, 0].reshape(1, -1),
                                                   w_router[l], b_router[l].reshape(1, -1))

        slot, first_blk, n_blk, counts = _dispatch_plan(idx, rank, counts)
        xg = _dispatch_rows(slot, h2)
        y_blk = _routed_experts(xg, first_blk, n_blk, counts, l, w_e_gate, w_e_up, w_e_down)
        x_ctx, x_lat = _ffn_out(slot, h2, wt, x1, w_s_gate[l].astype(BF16), w_s_up[l].astype(BF16),
                                w_s_down[l].astype(BF16), mod, ln_g[l, 1].reshape(1, -1),
                                ln_b[l, 1].reshape(1, -1), y_blk)

        ctx = slab[:N_CTX]
        ctx_out[0].append(ctx[:, COL_KA * LANES:(COL_KA + 1) * LANES].reshape(BATCH, SEQ, A_KV_HEADS, HEAD_DIM))
        ctx_out[1].append(ctx[:, COL_VA * LANES:(COL_VA + 1) * LANES].reshape(BATCH, SEQ, A_KV_HEADS, HEAD_DIM))
        ctx_out[2].append(ctx[:, COL_KB * 512:(COL_KB + 1) * 512].reshape(BATCH, SEQ, B_HEADS, 2, HEAD_DIM))
        ctx_out[3].append(ctx[:, COL_VB * 512:(COL_VB + 1) * 512].reshape(BATCH, SEQ, B_HEADS, 2 * HEAD_DIM))
        ctx_out[4].append(jnp.concatenate([fin_f[:BATCH], fin_b[:BATCH]], axis=1))
        ctx_out[5].append(ctx[:, COL_KD * 512:(COL_KD + 1) * 512].reshape(BATCH, SEQ, D_HEADS, HEAD_DIM))
        ctx_out[6].append(ctx[:, COL_VD * 512:(COL_VD + 1) * 512].reshape(BATCH, SEQ, D_HEADS, HEAD_DIM))

    new = [jnp.stack(t, axis=1) for t in ctx_out]
    return (x_ctx.reshape(BATCH, SEQ, D_MODEL), x_lat.reshape(DEC_BATCH, DEC_SEQ, D_MODEL), *new)
```

```python
import functools
import math

import numpy as np
import jax
import jax.numpy as jnp
from jax import lax
from jax.experimental import pallas as pl
from jax.experimental.pallas import tpu as pltpu

F32 = jnp.float32
BF16 = jnp.bfloat16

D_MODEL = 2048
BATCH = 16
SEQ = 256
DEPTH = 2
DEC_BATCH = 2
DEC_SEQ = 4096
PAST_LEN = 256
GRID_W = 64
HEAD_DIM = 64
ROPE_BASE = 10000.0
A_HEADS = 8
A_KV_HEADS = 2
A_WINDOW = 128
B_HEADS = 4
C_WIDTH = 512
C_BLOCKS = 8
C_BW = C_WIDTH // C_BLOCKS
C_POW = 8.0
D_HEADS = 8
NA_ROWS = 8
NA_COLS = 16
N_EXPERTS = 64
TOP_K = 6
D_EXPERT = 512
ROUTED_SCALE = 2.5
ALPHA = (2.0 * DEPTH) ** 0.25
LN_EPS = 1e-5
NORM_EPS = 1e-6
NEG_INF = -1e30
SCALE = HEAD_DIM ** -0.5
LOG2E = math.log2(math.e)

N_CTX = BATCH * SEQ
N_LAT = DEC_BATCH * DEC_SEQ
N_TOK = N_CTX + N_LAT
GROUP_ROWS = 4096
N_GROUPS = N_TOK // GROUP_ROWS
IN_WIDTH = 4864
LANES = 128
VMEM_LIMIT = 56 * 1024 * 1024

COL_QA, COL_QB, COL_KB, COL_VB, COL_XC, COL_GC, COL_QD, COL_KD, COL_VD = range(9)
COL_KA, COL_VA = 36, 37

TM_PROJ = 512
TN_PROJ = IN_WIDTH // 2
TM_PREP = 256
TQ_A = 128
TQ_B = 256
NA_QR = 4
NA_KR = 12
T_SCAN = 256
TM_OUT = 256
MOE_BM = 256
TM_DISPATCH = 256
TM_FFN = 256


def _params(*sem):
    return pltpu.CompilerParams(dimension_semantics=sem, vmem_limit_bytes=VMEM_LIMIT)


def _nt(a, b):
    return lax.dot_general(a, b, (((1,), (1,)), ((), ())), preferred_element_type=F32)


def _mm(a, b):
    return jnp.dot(a, b, preferred_element_type=F32)


def _softmax_unnorm(parts, sink=None):
    m = parts[0].max(axis=-1, keepdims=True)
    for p in parts[1:]:
        m = jnp.maximum(m, p.max(axis=-1, keepdims=True))
    if sink is not None:
        m = jnp.maximum(m, sink)
    es = [jnp.exp(p - m) for p in parts]
    den = es[0].sum(axis=-1, keepdims=True)
    for e in es[1:]:
        den = den + e.sum(axis=-1, keepdims=True)
    if sink is not None:
        den = den + jnp.exp(sink - m)
    return es, den


def _rms_gain(x, g):
    return x * lax.rsqrt(jnp.mean(x * x, axis=-1, keepdims=True) + NORM_EPS) * g


def _layernorm(z, g, b):
    zc = z - jnp.mean(z, axis=-1, keepdims=True)
    var = jnp.mean(zc * zc, axis=-1, keepdims=True)
    return zc * lax.rsqrt(var + LN_EPS) * g + b


def _silu(x):
    return x * jax.nn.sigmoid(x)


def _low_half(shape):
    return lax.broadcasted_iota(jnp.int32, shape, 1) < HEAD_DIM


HALF_D = D_MODEL // 2
HIGH16 = 0xFFFF0000


def _pack_rows(x):
    bits = pltpu.bitcast(x.astype(BF16).astype(F32), jnp.uint32)
    return (bits[:, :HALF_D] >> 16) | (bits[:, HALF_D:] & jnp.uint32(HIGH16))


def _unpack_rows(p):
    return (pltpu.bitcast(p << 16, F32), pltpu.bitcast(p & jnp.uint32(HIGH16), F32))


SUBLANES = 8
assert HALF_D == SUBLANES * LANES


def _store_token_tiles(ref, index, packed, first=0):
    m = packed.shape[0]
    for s in range(SUBLANES):
        rows = pl.ds(first * SUBLANES + s, m, stride=SUBLANES)
        ref[index + (rows, slice(None))] = packed[:, s * LANES:(s + 1) * LANES]


def _load_token_tiles(ref, index, m):
    return jnp.concatenate([ref[index + (pl.ds(s, m, stride=SUBLANES), slice(None))] for s in range(SUBLANES)],
                           axis=1)


def _mod_kernel(c_ref, w_ref, b_ref, o_ref):
    c = c_ref[...]
    o_ref[0] = jnp.dot(_silu(c), w_ref[0], precision=lax.Precision.HIGHEST,
                       preferred_element_type=F32) + b_ref[0]


def _modulation(cond8, w_mod, b_mod):
    tn = 1024
    n = w_mod.shape[-1]
    return pl.pallas_call(
        _mod_kernel,
        grid=(DEPTH, n // tn),
        in_specs=[pl.BlockSpec((8, D_MODEL), lambda l, j: (0, 0)),
                  pl.BlockSpec((1, D_MODEL, tn), lambda l, j: (l, 0, j)),
                  pl.BlockSpec((1, 1, tn), lambda l, j: (l, 0, j))],
        out_specs=pl.BlockSpec((1, 8, tn), lambda l, j: (l, 0, j)),
        out_shape=jax.ShapeDtypeStruct((DEPTH, 8, n), F32),
        compiler_params=_params("parallel", "parallel"),
        name="modulation",
    )(cond8, w_mod, b_mod.reshape(DEPTH, 1, n))


def _mod_spec(which, tm):
    return pl.BlockSpec((None, None, 1, D_MODEL), lambda i, *_: (i * tm // GROUP_ROWS, which, 0, 0))


def _ctx_rows(tm):
    return pl.BlockSpec((tm, D_MODEL), lambda i, *_: (jnp.minimum(i, N_CTX // tm - 1), 0))


def _lat_rows(tm):
    return pl.BlockSpec((tm, D_MODEL), lambda i, *_: (jnp.maximum(i - N_CTX // tm, 0), 0))


def _is_ctx_tile(tm):
    return pl.program_id(0) * tm < N_CTX


def _inproj_kernel(xc_ref, xl_ref, sc_ref, sh_ref, w_ref, o_ref, xb_ref):
    @pl.when(pl.program_id(1) == 0)
    def _():
        x = jnp.where(_is_ctx_tile(xc_ref.shape[0]), xc_ref[...], xl_ref[...])
        xb_ref[...] = (x * (1.0 + sc_ref[...]) + sh_ref[...]).astype(BF16)

    o_ref[...] = _mm(xb_ref[...], w_ref[...])


def _in_projection(x_ctx, x_lat, mod, w_in_b):
    tm, tn = TM_PROJ, TN_PROJ
    return pl.pallas_call(
        _inproj_kernel,
        grid=(N_TOK // tm, IN_WIDTH // tn),
        in_specs=[_ctx_rows(tm), _lat_rows(tm),
                  _mod_spec(1, tm), _mod_spec(0, tm),
                  pl.BlockSpec((D_MODEL, tn), lambda i, j: (0, j))],
        out_specs=pl.BlockSpec((tm, tn), lambda i, j: (i, j)),
        out_shape=jax.ShapeDtypeStruct((N_TOK, IN_WIDTH), F32),
        scratch_shapes=[pltpu.VMEM((tm, D_MODEL), BF16)],
        compiler_params=_params("parallel", "arbitrary"),
        name="in_projection",
    )(x_ctx, x_lat, mod, mod, w_in_b)


def _ctx_attn_kernel(scal_ref, qa_ref, ka_ref, va_ref, qb_ref, kb_ref, vb_ref, qd_ref, kd_ref, vd_ref,
                     ga_ref, gb_ref, gd_ref, oa_ref, ob_ref, od_ref, *, lam_init):
    L = SEQ
    lo = _low_half((L, LANES))

    ka = ka_ref[...].astype(BF16)
    va = va_ref[...].astype(BF16)
    qa = qa_ref[...]
    chunks = []
    for c in range(A_HEADS // 2):
        hk = c // 2
        chunk = qa[:, c * LANES:(c + 1) * LANES]
        keep = lo if hk == 0 else jnp.logical_not(lo)
        halves = []
        for half in range(2):
            x = chunk if half == hk else pltpu.roll(chunk, HEAD_DIM, 1)
            qh = jnp.where(keep, x, 0.0).astype(BF16)
            s = _nt(qh, ka) * SCALE
            (e,), den = _softmax_unnorm([s], scal_ref[2 * c + half])
            o = _mm(e.astype(BF16), va) / den
            halves.append(o if half == hk else pltpu.roll(o, HEAD_DIM, 1))
        chunks.append(jnp.where(lo, halves[0], halves[1]))
    oa = jnp.concatenate(chunks, axis=1)
    oa_ref[...] = _rms_gain(oa, ga_ref[...]).astype(BF16)

    lam = scal_ref[A_HEADS]
    qb = qb_ref[...]
    chunks = []
    for h in range(B_HEADS):
        sl = slice(h * LANES, (h + 1) * LANES)
        qc = qb[:, sl]
        kc = kb_ref[:, sl].astype(BF16)
        vh = vb_ref[:, sl].astype(BF16)
        (e1,), d1 = _softmax_unnorm([_nt(jnp.where(lo, qc, 0.0).astype(BF16), kc) * SCALE])
        (e2,), d2 = _softmax_unnorm([_nt(jnp.where(lo, 0.0, qc).astype(BF16), kc) * SCALE])
        w = e1 * (1.0 / d1) - e2 * (lam / d2)
        o = _mm(w.astype(BF16), vh)
        chunks.append(_rms_gain(o, gb_ref[:, sl]) * (1.0 - lam_init))
    ob_ref[...] = jnp.concatenate(chunks, axis=1).astype(BF16)

    qd = qd_ref[...]
    chunks = []
    for c in range(D_HEADS // 2):
        sl = slice(c * LANES, (c + 1) * LANES)
        qc = qd[:, sl]
        kc = kd_ref[:, sl].astype(BF16)
        vc = vd_ref[:, sl].astype(BF16)
        halves = []
        for half in range(2):
            qh = jnp.where(lo if half == 0 else jnp.logical_not(lo), qc, 0.0).astype(BF16)
            (e,), den = _softmax_unnorm([_nt(qh, kc) * SCALE])
            halves.append(_mm(e.astype(BF16), vc) / den)
        chunks.append(jnp.where(lo, halves[0], halves[1]))
    od = jnp.concatenate(chunks, axis=1)
    od_ref[...] = _rms_gain(od, gd_ref[...]).astype(BF16)


def _ctx_attention(slab, scal, g_mix2d, lam_init):
    L = SEQ

    def wide(col):
        return pl.BlockSpec((L, 512), lambda b: (b, col))

    def narrow(col):
        return pl.BlockSpec((L, LANES), lambda b: (b, col))

    def gain(col):
        return pl.BlockSpec((1, 512), lambda b: (0, col))

    out = jax.ShapeDtypeStruct((N_TOK, 512), BF16)
    ospec = pl.BlockSpec((L, 512), lambda b: (b, 0))
    return pl.pallas_call(
        functools.partial(_ctx_attn_kernel, lam_init=lam_init),
        grid=(BATCH,),
        in_specs=[pl.BlockSpec(memory_space=pltpu.SMEM),
                  wide(COL_QA), narrow(COL_KA), narrow(COL_VA),
                  wide(COL_QB), wide(COL_KB), wide(COL_VB),
                  wide(COL_QD), wide(COL_KD), wide(COL_VD),
                  gain(0), gain(1), gain(3)],
        out_specs=[ospec, ospec, ospec],
        out_shape=[out, out, out],
        compiler_params=_params("parallel"),
        name="ctx_attention",
    )(scal, slab, slab, slab, slab, slab, slab, slab, slab, slab, g_mix2d, g_mix2d, g_mix2d)


def _prep_kernel(qa_ref, ka_ref, va_ref, qb_ref, kb_ref, vb_ref, qd_ref, kd_ref, vd_ref,
                 cos_ref, sa_ref, sb_ref,
                 qa_o, ka_o, va_o, qb_o, kb_o, vb_o, qd_o, kd_o, vd_o):
    cos, sa, sb = cos_ref[...], sa_ref[...], sb_ref[...]
    lo = _low_half(cos.shape)
    hi = jnp.logical_not(lo)

    def rope(x):
        return (x * cos + pltpu.roll(x, LANES - HEAD_DIM // 4, 1) * sa
                + pltpu.roll(x, HEAD_DIM // 4, 1) * sb)

    ka_o[...] = rope(ka_ref[...]).astype(BF16)
    va_o[...] = va_ref[...].astype(BF16)
    vb_o[...] = vb_ref[...].astype(BF16)
    kd_o[...] = kd_ref[...].astype(BF16)
    vd_o[...] = vd_ref[...].astype(BF16)

    for c in range(A_HEADS // 2):
        hk = c // 2
        r = rope(qa_ref[:, c * LANES:(c + 1) * LANES])
        keep = lo if hk == 0 else hi
        for half in range(2):
            h = 2 * c + half
            x = r if half == hk else pltpu.roll(r, HEAD_DIM, 1)
            qa_o[:, h * LANES:(h + 1) * LANES] = jnp.where(keep, x, 0.0).astype(BF16)
    for h in range(B_HEADS):
        sl = slice(h * LANES, (h + 1) * LANES)
        r = rope(qb_ref[:, sl])
        kb_o[:, sl] = rope(kb_ref[:, sl]).astype(BF16)
        qb_o[:, (2 * h) * LANES:(2 * h + 1) * LANES] = jnp.where(lo, r, 0.0).astype(BF16)
        qb_o[:, (2 * h + 1) * LANES:(2 * h + 2) * LANES] = jnp.where(hi, r, 0.0).astype(BF16)
    for c in range(D_HEADS // 2):
        x = qd_ref[:, c * LANES:(c + 1) * LANES]
        qd_o[:, (2 * c) * LANES:(2 * c + 1) * LANES] = jnp.where(lo, x, 0.0).astype(BF16)
        qd_o[:, (2 * c + 1) * LANES:(2 * c + 2) * LANES] = jnp.where(hi, x, 0.0).astype(BF16)


def _latent_prep(slab, cos, sa, sb):
    tm = TM_PREP
    off = N_CTX // tm
    per_seq = DEC_SEQ // tm

    def wide(col):
        return pl.BlockSpec((tm, 512), lambda i: (i + off, col))

    def narrow(col):
        return pl.BlockSpec((tm, LANES), lambda i: (i + off, col))

    tab = pl.BlockSpec((tm, LANES), lambda i: (i % per_seq, 0))

    def out(width):
        return (pl.BlockSpec((tm, width), lambda i: (i, 0)), jax.ShapeDtypeStruct((N_LAT, width), BF16))

    outs = [out(1024), out(LANES), out(LANES), out(1024), out(512), out(512), out(1024), out(512), out(512)]
    return pl.pallas_call(
        _prep_kernel,
        grid=(N_LAT // tm,),
        in_specs=[wide(COL_QA), narrow(COL_KA), narrow(COL_VA), wide(COL_QB), wide(COL_KB), wide(COL_VB),
                  wide(COL_QD), wide(COL_KD), wide(COL_VD), tab, tab, tab],
        out_specs=[o[0] for o in outs],
        out_shape=[o[1] for o in outs],
        compiler_params=_params("parallel"),
        name="latent_prep",
    )(slab, slab, slab, slab, slab, slab, slab, slab, slab, cos, sa, sb)


def _win_attn_kernel(scal_ref, q_ref, kp_ref, kc_ref, kn_ref, vp_ref, vc_ref, vn_ref, kctx_ref, vctx_ref,
                     g_ref, _all_rows_ref, o_ref):
    n = pl.program_id(1)
    nb = pl.num_programs(1)
    tq = TQ_A
    group = A_HEADS // A_KV_HEADS
    k = jnp.concatenate([kp_ref[...], kc_ref[...], kn_ref[...], kctx_ref[...].astype(BF16)], axis=0)
    v = jnp.concatenate([vp_ref[...], vc_ref[...], vn_ref[...], vctx_ref[...].astype(BF16)], axis=0)
    nk = 3 * tq + PAST_LEN
    row = lax.broadcasted_iota(jnp.int32, (group * tq, nk), 0) & (tq - 1)
    col = lax.broadcasted_iota(jnp.int32, (group * tq, nk), 1)
    prev_thr = row + jnp.where(n > 0, 0, tq)
    next_thr = row + 2 * tq - jnp.where(n < nb - 1, 0, tq)
    masked = ((col < tq) & (col < prev_thr)) | ((col >= 2 * tq) & (col < 3 * tq) & (col > next_thr))
    valid = jnp.logical_not(masked)
    lo = _low_half((tq, LANES))
    heads = []
    for hk in range(A_KV_HEADS):
        q4 = jnp.concatenate([q_ref[:, (hk * group + g) * LANES:(hk * group + g + 1) * LANES]
                              for g in range(group)], axis=0)
        s = jnp.where(valid, _nt(q4, k) * SCALE, NEG_INF)
        sink = jnp.concatenate([jnp.full((tq, 1), scal_ref[hk * group + g], F32) for g in range(group)], axis=0)
        (e,), den = _softmax_unnorm([s], sink)
        o = _mm(e.astype(BF16), v) / den
        for g in range(group):
            h = hk * group + g
            og = o[g * tq:(g + 1) * tq]
            heads.append(og if (h % 2) == hk else pltpu.roll(og, HEAD_DIM, 1))
    oa = jnp.concatenate([jnp.where(lo, heads[2 * c], heads[2 * c + 1]) for c in range(A_HEADS // 2)], axis=1)
    o_ref[...] = _rms_gain(oa, g_ref[...]).astype(BF16)


def _window_attention(qa_p, ka_r, va_b, cache_k, cache_v, scal, g_mix2d, o_all):
    tq = TQ_A
    nb = DEC_SEQ // tq

    def band(d):
        return pl.BlockSpec((tq, LANES), lambda b, n: (b * nb + jnp.clip(n + d, 0, nb - 1), 0))

    ctx = pl.BlockSpec((None, PAST_LEN, LANES), lambda b, n: (b, 0, 0))
    return pl.pallas_call(
        _win_attn_kernel,
        grid=(DEC_BATCH, nb),
        in_specs=[pl.BlockSpec(memory_space=pltpu.SMEM),
                  pl.BlockSpec((tq, A_HEADS * LANES), lambda b, n: (b * nb + n, 0)),
                  band(-1), band(0), band(1), band(-1), band(0), band(1), ctx, ctx,
                  pl.BlockSpec((1, 512), lambda b, n: (0, 0)),
                  pl.BlockSpec(memory_space=pl.ANY)],
        out_specs=pl.BlockSpec((tq, 512), lambda b, n: (N_CTX // tq + b * nb + n, 0)),
        out_shape=jax.ShapeDtypeStruct((N_TOK, 512), BF16),
        input_output_aliases={11: 0},
        compiler_params=_params("parallel", "parallel"),
        name="window_attention",
    )(scal, qa_p, ka_r, ka_r, ka_r, va_b, va_b, va_b, cache_k, cache_v, g_mix2d, o_all)


def _diff_attn_kernel(scal_ref, q_ref, k_ref, v_ref, kctx_ref, vctx_ref, g_ref, _all_rows_ref, o_ref, *,
                      lam_init):
    lam = scal_ref[A_HEADS]
    chunks = []
    for h in range(B_HEADS):
        sl = slice(h * LANES, (h + 1) * LANES)
        kh = k_ref[:, sl]
        kc = kctx_ref[:, sl].astype(BF16)
        es, dens = [], []
        for m in range(2):
            q = q_ref[:, (2 * h + m) * LANES:(2 * h + m + 1) * LANES]
            t_lat = _nt(q, kh) * (SCALE * LOG2E)
            t_ctx = _nt(q, kc) * (SCALE * LOG2E)
            mx = jnp.maximum(t_lat.max(axis=-1, keepdims=True), t_ctx.max(axis=-1, keepdims=True))
            e_lat = jnp.exp2(t_lat - mx)
            e_ctx = jnp.exp2(t_ctx - mx)
            es.append((e_lat, e_ctx))
            dens.append(e_lat.sum(axis=-1, keepdims=True) + e_ctx.sum(axis=-1, keepdims=True))
        c1 = 1.0 / dens[0]
        c2 = lam / dens[1]
        w_lat = (es[0][0] * c1 - es[1][0] * c2).astype(BF16)
        w_ctx = (es[0][1] * c1 - es[1][1] * c2).astype(BF16)
        o = _mm(w_lat, v_ref[:, sl]) + _mm(w_ctx, vctx_ref[:, sl].astype(BF16))
        chunks.append(_rms_gain(o, g_ref[:, sl]) * (1.0 - lam_init))
    o_ref[...] = jnp.concatenate(chunks, axis=1).astype(BF16)


def _diff_attention(qb_p, kb_r, vb_b, cache_k, cache_v, scal, g_mix2d, lam_init, o_all):
    tq = TQ_B
    nb = DEC_SEQ // tq
    full = pl.BlockSpec((DEC_SEQ, 512), lambda b, n: (b, 0))
    ctx = pl.BlockSpec((None, PAST_LEN, 512), lambda b, n: (b, 0, 0))
    return pl.pallas_call(
        functools.partial(_diff_attn_kernel, lam_init=lam_init),
        grid=(DEC_BATCH, nb),
        in_specs=[pl.BlockSpec(memory_space=pltpu.SMEM),
                  pl.BlockSpec((tq, 2 * B_HEADS * LANES), lambda b, n: (b * nb + n, 0)),
                  full, full, ctx, ctx,
                  pl.BlockSpec((1, 512), lambda b, n: (0, 1)),
                  pl.BlockSpec(memory_space=pl.ANY)],
        out_specs=pl.BlockSpec((tq, 512), lambda b, n: (N_CTX // tq + b * nb + n, 0)),
        out_shape=jax.ShapeDtypeStruct((N_TOK, 512), BF16),
        input_output_aliases={7: 0},
        compiler_params=_params("parallel", "parallel"),
        name="diff_attention",
    )(scal, qb_p, kb_r, vb_b, cache_k, cache_v, g_mix2d, o_all)


def _na_window_start(step):
    return np.clip(step * NA_QR - NA_ROWS // 2, 0, GRID_W - NA_KR)


def _na_bias_indices():
    rows = DEC_SEQ // GRID_W
    steps = rows // NA_QR
    pats = []
    for step in range(steps):
        w0 = _na_window_start(step)
        r = step * NA_QR + np.arange(NA_QR)[:, None, None, None]
        c = np.arange(GRID_W)[None, :, None, None]
        kr = w0 + np.arange(NA_KR)[None, None, :, None]
        kc = np.arange(GRID_W)[None, None, None, :]
        kr0 = np.clip(r - NA_ROWS // 2, 0, rows - NA_ROWS)
        kc0 = np.clip(c - NA_COLS // 2, 0, GRID_W - NA_COLS)
        valid = (kr >= kr0) & (kr < kr0 + NA_ROWS) & (kc >= kc0) & (kc < kc0 + NA_COLS)
        drow = np.clip(kr - r + NA_ROWS - 1, 0, 2 * NA_ROWS - 2)
        dcol = np.clip(kc - c + NA_COLS - 1, 0, 2 * NA_COLS - 2)
        shape = (NA_QR * GRID_W, NA_KR * GRID_W)
        full = np.broadcast_to
        pats.append((full(drow, valid.shape).reshape(shape), full(dcol, valid.shape).reshape(shape),
                     valid.reshape(shape)))
    for step in range(2, steps - 1):
        for a, b in zip(pats[1], pats[step]):
            assert np.array_equal(a, b)
    kinds = [pats[0], pats[1], pats[-1]]
    drow, dcol, valid = (np.stack([k[i] for k in kinds]) for i in range(3))
    shape6 = (len(kinds), NA_QR, GRID_W, NA_KR, GRID_W)
    valid6 = valid.reshape(shape6)
    row_ok = valid6.any(axis=(2, 4))
    col_ok = valid6.any(axis=(0, 1, 3))
    assert np.array_equal(valid6, row_ok[:, :, None, :, None] & col_ok[None, None, :, None, :])
    outside = 2 * NA_ROWS - 1
    row_idx = np.where(row_ok, drow.reshape(shape6)[:, :, 0, :, 0], outside)
    col_sel = np.eye(2 * NA_COLS - 1, dtype=np.float32)[dcol.reshape(shape6)[0, 0, :, 0, :]]
    return row_idx, col_sel, col_ok


def _na_bias(rpb, row_idx, col_sel, col_ok):
    cols = jnp.einsum('hab,cdb->hacd', rpb, col_sel, precision=lax.Precision.HIGHEST)
    cols = jnp.where(col_ok[None, None], cols, NEG_INF)
    cols = jnp.concatenate([cols, jnp.full((D_HEADS, 1, GRID_W, GRID_W), NEG_INF, F32)], axis=1)
    kinds = []
    for kind in row_idx:
        kinds.append(jnp.concatenate(
            [jnp.concatenate([cols[:, int(a)] for a in per_row], axis=-1) for per_row in kind], axis=1))
    return jnp.stack(kinds)


def _na_kernel(q_ref, k_ref, v_ref, kctx_ref, vctx_ref, bias_ref, g_ref, _all_rows_ref, o_ref):
    step = pl.program_id(1)
    nq = NA_QR * GRID_W
    nk = NA_KR * GRID_W
    w0 = jnp.clip(step * NA_QR - NA_ROWS // 2, 0, GRID_W - NA_KR)
    start = pl.multiple_of(w0 * GRID_W, GRID_W)
    lo = _low_half((nq, LANES))
    chunks = []
    for c in range(D_HEADS // 2):
        sl = slice(c * LANES, (c + 1) * LANES)
        kw = k_ref[pl.ds(start, nk), sl]
        vw = v_ref[pl.ds(start, nk), sl]
        kc = kctx_ref[:, sl].astype(BF16)
        vc = vctx_ref[:, sl].astype(BF16)
        halves = []
        for half in range(2):
            h = 2 * c + half
            q = q_ref[:, h * LANES:(h + 1) * LANES]
            s_loc = _nt(q, kw) * SCALE + bias_ref[h]
            s_ctx = _nt(q, kc) * SCALE
            (e_loc, e_ctx), den = _softmax_unnorm([s_loc, s_ctx])
            halves.append((_mm(e_loc.astype(BF16), vw) + _mm(e_ctx.astype(BF16), vc)) / den)
        chunks.append(jnp.where(lo, halves[0], halves[1]))
    od = jnp.concatenate(chunks, axis=1)
    o_ref[...] = _rms_gain(od, g_ref[...]).astype(BF16)


def _neighbourhood_attention(qd_p, kd_b, vd_b, cache_k, cache_v, bias, g_mix2d, o_all):
    nq = NA_QR * GRID_W
    steps = DEC_SEQ // nq
    full = pl.BlockSpec((DEC_SEQ, 512), lambda b, j: (b, 0))
    ctx = pl.BlockSpec((None, PAST_LEN, 512), lambda b, j: (b, 0, 0))

    def kind(b, j):
        return (jnp.where(j == 0, 0, jnp.where(j == steps - 1, 2, 1)), 0, 0, 0)

    return pl.pallas_call(
        _na_kernel,
        grid=(DEC_BATCH, steps),
        in_specs=[pl.BlockSpec((nq, D_HEADS * LANES), lambda b, j: (b * steps + j, 0)),
                  full, full, ctx, ctx,
                  pl.BlockSpec((None, D_HEADS, nq, NA_KR * GRID_W), kind),
                  pl.BlockSpec((1, 512), lambda b, j: (0, 3)),
                  pl.BlockSpec(memory_space=pl.ANY)],
        out_specs=pl.BlockSpec((nq, 512), lambda b, j: (N_CTX // nq + b * steps + j, 0)),
        out_shape=jax.ShapeDtypeStruct((N_TOK, 512), BF16),
        input_output_aliases={7: 0},
        compiler_params=_params("parallel", "arbitrary"),
        name="neighbourhood_attention",
    )(qd_p, kd_b, vd_b, cache_k, cache_v, bias, g_mix2d, o_all)


N_SCAN_TILES = N_TOK // T_SCAN
CTX_TILES = N_CTX // T_SCAN
TILES_PER_LAT = DEC_SEQ // T_SCAN


def _scan_tile_flags(i):
    is_ctx = i < CTX_TILES
    pos = (i - CTX_TILES) % TILES_PER_LAT
    return is_ctx | (pos == 0), is_ctx | (pos == TILES_PER_LAT - 1)


def _scan_seq(i):
    return jnp.where(i < CTX_TILES, i, CTX_TILES + (i - CTX_TILES) // TILES_PER_LAT)


def _conv_gates(x_ref, prev_ref, next_ref, cw_ref, cb_ref, wgh_ref, wgl_ref, bg_ref, lam_ref, first, last):
    x = x_ref[...]
    t = T_SCAN
    row = lax.broadcasted_iota(jnp.int32, x.shape, 0)
    pm1 = jnp.where(first, 0.0, prev_ref[7:8, :])
    n0 = jnp.where(last, 0.0, next_ref[0:1, :])
    n1 = jnp.where(last, 0.0, next_ref[1:2, :])
    x_m1 = jnp.where(row == 0, pm1, pltpu.roll(x, 1, 0))
    x_p1 = jnp.where(row == t - 1, n0, pltpu.roll(x, t - 1, 0))
    x_p2 = jnp.where(row == t - 2, n0, jnp.where(row == t - 1, n1, pltpu.roll(x, t - 2, 0)))
    y = (cw_ref[0:1, :] * x_m1 + cw_ref[1:2, :] * x + cw_ref[2:3, :] * x_p1 + cw_ref[3:4, :] * x_p2
         + cb_ref[...])
    y_hi = y.astype(BF16)
    y_lo = (y - y_hi.astype(F32)).astype(BF16)
    zr, zi = [], []
    for c in range(C_WIDTH // LANES):
        sl = slice(c * LANES, (c + 1) * LANES)
        w_hi = wgh_ref[0, sl, :]
        z = _mm(y_hi[:, sl], w_hi) + _mm(y_lo[:, sl], w_hi) + _mm(y_hi[:, sl], wgl_ref[0, sl, :])
        zr.append(z[:, :LANES])
        zi.append(z[:, LANES:])
    bias = bg_ref[0]
    r = jax.nn.sigmoid(jnp.concatenate(zr, axis=1) + bias[:, :C_WIDTH])
    gate_i = jax.nn.sigmoid(jnp.concatenate(zi, axis=1) + bias[:, C_WIDTH:])
    nl = -lam_ref[0]
    softplus = jnp.maximum(nl, 0.0) + jnp.log1p(jnp.exp(-jnp.abs(nl)))
    log_a = -C_POW * r * softplus
    a = jnp.exp(log_a)
    u = jnp.sqrt(1.0 - jnp.exp(2.0 * log_a)) * gate_i * y
    return a, u


def _scan_rows(a_s, u_s, h_s, h, reverse):
    groups = T_SCAN // 8

    def body(i, h):
        g = (groups - 1 - i) if reverse else i
        base = pl.multiple_of(g * 8, 8)
        a8 = a_s[pl.ds(base, 8), :]
        u8 = u_s[pl.ds(base, 8), :]
        rows = [None] * 8
        for k in (range(7, -1, -1) if reverse else range(8)):
            h = a8[k:k + 1, :] * h + u8[k:k + 1, :]
            rows[k] = h
        h_s[pl.ds(base, 8), :] = jnp.concatenate(rows, axis=0)
        return h

    return lax.fori_loop(0, groups, body, h)


def _scan_fwd_kernel(x_ref, prev_ref, next_ref, cw_ref, cb_ref, wgh_ref, wgl_ref, bg_ref, lam_ref, h0_ref,
                     hf_ref, fin_ref, a_s, u_s, carry_s):
    i = pl.program_id(0)
    first, last = _scan_tile_flags(i)
    a, u = _conv_gates(x_ref, prev_ref, next_ref, cw_ref, cb_ref, wgh_ref, wgl_ref, bg_ref, lam_ref, first, last)
    a_s[...] = a
    u_s[...] = u

    @pl.when(first)
    def _():
        carry_s[...] = h0_ref[...]

    h = _scan_rows(a_s, u_s, hf_ref, carry_s[...], reverse=False)
    carry_s[...] = h
    fin_ref[...] = h


def _scan_bwd_kernel(x_ref, prev_ref, next_ref, cw_ref, cb_ref, wgh_ref, wgl_ref, bg_ref, lam_ref, h0_ref,
                     hf_ref, gc_ref, g_ref, oc_ref, fin_ref, a_s, u_s, hb_s, carry_s):
    i = N_SCAN_TILES - 1 - pl.program_id(0)
    first, last = _scan_tile_flags(i)
    a, u = _conv_gates(x_ref, prev_ref, next_ref, cw_ref, cb_ref, wgh_ref, wgl_ref, bg_ref, lam_ref, first, last)
    a_s[...] = a
    u_s[...] = u

    @pl.when(last)
    def _():
        carry_s[...] = h0_ref[...]

    h = _scan_rows(a_s, u_s, hb_s, carry_s[...], reverse=True)
    carry_s[...] = h
    fin_ref[...] = h
    g = gc_ref[...]
    gelu = 0.5 * g * (1.0 + jnp.tanh(math.sqrt(2.0 / math.pi) * (g + 0.044715 * (g * g * g))))
    oc_ref[...] = _rms_gain(gelu * (hf_ref[...] + hb_s[...]), g_ref[...]).astype(BF16)


def _scan_common_specs(tile_of, direction):
    halo = T_SCAN // 8
    last_block = N_TOK // 8 - 1
    return [pl.BlockSpec((T_SCAN, 512), lambda j: (tile_of(j), COL_XC)),
            pl.BlockSpec((8, 512), lambda j: (jnp.maximum(tile_of(j) * halo - 1, 0), COL_XC)),
            pl.BlockSpec((8, 512), lambda j: (jnp.minimum((tile_of(j) + 1) * halo, last_block), COL_XC)),
            pl.BlockSpec((4, C_WIDTH), lambda j: (0, 0)),
            pl.BlockSpec((1, C_WIDTH), lambda j: (0, 0)),
            pl.BlockSpec((1, C_WIDTH, 2 * LANES), lambda j: (direction, 0, 0)),
            pl.BlockSpec((1, C_WIDTH, 2 * LANES), lambda j: (direction, 0, 0)),
            pl.BlockSpec((1, 1, 2 * C_WIDTH), lambda j: (direction, 0, 0)),
            pl.BlockSpec((1, 1, C_WIDTH), lambda j: (direction, 0, 0)),
            pl.BlockSpec((None, 1, C_WIDTH), lambda j: (_scan_seq(tile_of(j)), 0, 0))]


def _rglru(slab, conv_w, conv_b, w_gates, b_gates, lam, h0_f, h0_b, g_mix2d):
    wg_hi = w_gates.astype(BF16)
    wg_lo = (w_gates - wg_hi.astype(F32)).astype(BF16)
    fin = jax.ShapeDtypeStruct((N_SCAN_TILES, 1, C_WIDTH), F32)
    fwd_tile = lambda j: j
    h_f, fin_f = pl.pallas_call(
        _scan_fwd_kernel,
        grid=(N_SCAN_TILES,),
        in_specs=_scan_common_specs(fwd_tile, 0),
        out_specs=[pl.BlockSpec((T_SCAN, C_WIDTH), lambda j: (j, 0)),
                   pl.BlockSpec((None, 1, C_WIDTH), lambda j: (j, 0, 0))],
        out_shape=[jax.ShapeDtypeStruct((N_TOK, C_WIDTH), F32), fin],
        scratch_shapes=[pltpu.VMEM((T_SCAN, C_WIDTH), F32), pltpu.VMEM((T_SCAN, C_WIDTH), F32),
                        pltpu.VMEM((1, C_WIDTH), F32)],
        compiler_params=_params("arbitrary"),
        name="rglru_forward",
    )(slab, slab, slab, conv_w, conv_b, wg_hi, wg_lo, b_gates, lam, h0_f)
    bwd_tile = lambda j: N_SCAN_TILES - 1 - j
    oc, fin_b = pl.pallas_call(
        _scan_bwd_kernel,
        grid=(N_SCAN_TILES,),
        in_specs=_scan_common_specs(bwd_tile, 1) + [
            pl.BlockSpec((T_SCAN, C_WIDTH), lambda j: (bwd_tile(j), 0)),
            pl.BlockSpec((T_SCAN, 512), lambda j: (bwd_tile(j), COL_GC)),
            pl.BlockSpec((1, 512), lambda j: (0, 2))],
        out_specs=[pl.BlockSpec((T_SCAN, C_WIDTH), lambda j: (bwd_tile(j), 0)),
                   pl.BlockSpec((None, 1, C_WIDTH), lambda j: (bwd_tile(j), 0, 0))],
        out_shape=[jax.ShapeDtypeStruct((N_TOK, C_WIDTH), BF16), fin],
        scratch_shapes=[pltpu.VMEM((T_SCAN, C_WIDTH), F32), pltpu.VMEM((T_SCAN, C_WIDTH), F32),
                        pltpu.VMEM((T_SCAN, C_WIDTH), F32), pltpu.VMEM((1, C_WIDTH), F32)],
        compiler_params=_params("arbitrary"),
        name="rglru_backward",
    )(slab, slab, slab, conv_w, conv_b, wg_hi, wg_lo, b_gates, lam, h0_b, h_f, slab, g_mix2d)
    return oc, fin_f, fin_b


def _mixout_kernel(oa_ref, ob_ref, oc_ref, od_ref, w_ref, xc_ref, xl_ref, g1_ref, sc2_ref, sh2_ref, lng_ref,
                   lnb_ref, wrh_ref, wrl_ref, br_ref, x1_ref, h2_ref, idx_ref, wt_ref, rank_ref, cnt_ref, cnt_s):
    @pl.when(pl.program_id(0) == 0)
    def _():
        cnt_s[...] = jnp.zeros(cnt_s.shape, F32)

    parts = 2
    pm = xc_ref.shape[0] // parts
    is_ctx = _is_ctx_tile(xc_ref.shape[0])
    for part in range(parts):
        rows = pl.ds(part * pm, pm)
        x = jnp.where(is_ctx, xc_ref[rows, :], xl_ref[rows, :])
        _mixout_rows(rows, part * pm, pm, oa_ref, ob_ref, oc_ref, od_ref, w_ref, x, g1_ref,
                     sc2_ref, sh2_ref, lng_ref, lnb_ref, wrh_ref, wrl_ref, br_ref, x1_ref, h2_ref, idx_ref,
                     wt_ref, rank_ref, cnt_s)
    cnt_ref[...] = cnt_s[...]


def _mixout_rows(rows, first, tm, oa_ref, ob_ref, oc_ref, od_ref, w_ref, x, g1_ref, sc2_ref, sh2_ref,
                 lng_ref, lnb_ref, wrh_ref, wrl_ref, br_ref, x1_ref, h2_ref, idx_ref, wt_ref, rank_ref, cnt_s):
    y = _mm(oa_ref[rows, :], w_ref[0:512, :])
    y += _mm(ob_ref[rows, :], w_ref[512:1024, :])
    y += _mm(oc_ref[rows, :], w_ref[1024:1536, :])
    y += _mm(od_ref[rows, :], w_ref[1536:2048, :])
    x1 = _layernorm(ALPHA * x + g1_ref[...] * y, lng_ref[...], lnb_ref[...])
    x1_ref[rows, :] = x1
    h2 = x1 * (1.0 + sc2_ref[...]) + sh2_ref[...]
    _store_token_tiles(h2_ref, (), _pack_rows(h2), first)

    h_hi = h2.astype(BF16)
    h_lo = (h2 - h_hi.astype(F32)).astype(BF16)
    scores = jax.nn.sigmoid(_mm(h_hi, wrh_ref[...]) + _mm(h_lo, wrh_ref[...]) + _mm(h_hi, wrl_ref[...]))
    sel = scores + br_ref[...]
    lane = lax.broadcasted_iota(jnp.int32, sel.shape, 1).astype(F32)
    slot = lax.broadcasted_iota(jnp.int32, (tm, 8), 1)
    wide = lax.broadcasted_iota(jnp.int32, (tm, LANES), 1)
    idx_out = jnp.zeros((tm, LANES), F32)
    wt_out = jnp.zeros((tm, 8), F32)
    total = jnp.zeros((sel.shape[0], 1), F32)
    hits = []
    for k in range(TOP_K):
        m = sel.max(axis=-1, keepdims=True)
        idx = jnp.where(sel == m, lane, float(N_EXPERTS)).min(axis=-1, keepdims=True)
        hit = lane == idx
        hits.append(hit)
        w = jnp.where(hit, scores, 0.0).sum(axis=-1, keepdims=True)
        total = total + w
        idx_out = jnp.where(wide == k, idx, idx_out)
        wt_out = jnp.where(slot == k, w, wt_out)
        sel = jnp.where(hit, -jnp.inf, sel)
    cols = pl.ds(first, tm)
    idx_ref[:, cols] = idx_out.T[:8, :].astype(jnp.int32)
    wt_ref[rows, :] = wt_out / total * ROUTED_SCALE

    chosen = jnp.zeros(sel.shape, F32)
    for hit in hits:
        chosen = chosen + hit.astype(F32)
    earlier = (lax.broadcasted_iota(jnp.int32, (tm, tm), 1) < lax.broadcasted_iota(jnp.int32, (tm, tm), 0))
    before = _mm(earlier.astype(F32).astype(BF16), chosen.astype(BF16)) + cnt_s[...]
    rank_out = jnp.zeros((tm, LANES), F32)
    for k, hit in enumerate(hits):
        rank_out = jnp.where(wide == k, jnp.where(hit, before, 0.0).sum(axis=-1, keepdims=True), rank_out)
    rank_ref[:, cols] = rank_out.T[:8, :].astype(jnp.int32)
    cnt_s[...] = cnt_s[...] + chosen.sum(axis=0, keepdims=True)


def _mixer_out(oa, ob, oc, od, w_out_b, x_ctx, x_lat, mod, ln_g, ln_b, w_router, b_router):
    tm = TM_OUT
    cat = pl.BlockSpec((tm, 512), lambda i: (i, 0))
    row = pl.BlockSpec((tm, D_MODEL), lambda i: (i, 0))
    vec = pl.BlockSpec((1, D_MODEL), lambda i: (0, 0))
    k8 = pl.BlockSpec((tm, 8), lambda i: (i, 0))
    k_major = pl.BlockSpec((8, tm), lambda i: (0, i))
    per_expert = pl.BlockSpec((1, N_EXPERTS), lambda i: (0, 0))
    router = pl.BlockSpec((D_MODEL, N_EXPERTS), lambda i: (0, 0))
    wr_hi = w_router.astype(BF16)
    wr_lo = (w_router - wr_hi.astype(F32)).astype(BF16)
    return pl.pallas_call(
        _mixout_kernel,
        grid=(N_TOK // tm,),
        in_specs=[cat, cat, cat, cat,
                  pl.BlockSpec((D_MODEL, D_MODEL), lambda i: (0, 0)),
                  _ctx_rows(tm), _lat_rows(tm), _mod_spec(2, tm), _mod_spec(4, tm), _mod_spec(3, tm), vec, vec,
                  router, router, per_expert],
        out_specs=[row, pl.BlockSpec((tm * SUBLANES, LANES), lambda i: (i, 0)), k_major, k8, k_major, per_expert],
        out_shape=[jax.ShapeDtypeStruct((N_TOK, D_MODEL), F32),
                   jax.ShapeDtypeStruct((N_TOK * SUBLANES, LANES), jnp.uint32),
                   jax.ShapeDtypeStruct((8, N_TOK), jnp.int32), jax.ShapeDtypeStruct((N_TOK, 8), F32),
                   jax.ShapeDtypeStruct((8, N_TOK), jnp.int32), jax.ShapeDtypeStruct((1, N_EXPERTS), F32)],
        scratch_shapes=[pltpu.VMEM((1, N_EXPERTS), F32)],
        compiler_params=_params("arbitrary"),
        name="mixer_out_router",
    )(oa, ob, oc, od, w_out_b, x_ctx, x_lat, mod, mod, mod, ln_g, ln_b, wr_hi, wr_lo, b_router)


ROW_DMA_QUEUES = 2
MOE_BLOCKS = N_TOK * TOP_K // MOE_BM + N_EXPERTS
MOE_ROWS = MOE_BLOCKS * MOE_BM


def _row_copy(src, src_row, dst, dst_row, sem):
    def tile(row):
        return pl.ds(row * SUBLANES if isinstance(row, int) else pl.multiple_of(row * SUBLANES, SUBLANES), SUBLANES)

    return pltpu.make_async_copy(src.at[tile(src_row)], dst.at[tile(dst_row)], sem)


def _dispatch_kernel(slot_ref, h_ref, xg_ref, sem):
    tm = h_ref.shape[0] // SUBLANES
    base = pl.program_id(0) * tm

    for r in range(tm):
        for k in range(TOP_K):
            _row_copy(h_ref, r, xg_ref, slot_ref[base + (k * N_TOK + r)], sem).start(priority=k % ROW_DMA_QUEUES)
    for k in range(TOP_K):
        pltpu.make_async_copy(h_ref, xg_ref.at[pl.ds(0, tm * SUBLANES)], sem).wait()


def _dispatch_rows(slot_flat, h2):
    tm = TM_DISPATCH
    grid_spec = pltpu.PrefetchScalarGridSpec(
        num_scalar_prefetch=1,
        grid=(N_TOK // tm,),
        in_specs=[pl.BlockSpec((tm * SUBLANES, LANES), lambda i, s: (i, 0))],
        out_specs=pl.BlockSpec(memory_space=pl.ANY),
        scratch_shapes=[pltpu.SemaphoreType.DMA])
    return pl.pallas_call(
        _dispatch_kernel,
        grid_spec=grid_spec,
        out_shape=jax.ShapeDtypeStruct((MOE_ROWS * SUBLANES, LANES), jnp.uint32),
        compiler_params=_params("arbitrary"),
        name="moe_dispatch",
    )(slot_flat, h2)


BLOCK_TILE_ROWS = MOE_BM * SUBLANES
EXPERT_X_BUFFERS = 3


def _expert_kernel(blk0_ref, nblk_ref, cnt_ref, wg_ref, wu_ref, wd_ref, x_hbm, y_hbm,
                   wg_s, wu_s, wd_s, xbuf, ybuf, xsem, ysem):
    e = pl.program_id(0)
    first_blk = blk0_ref[e]
    n_blk = nblk_ref[e]
    count = cnt_ref[e]
    total = blk0_ref[N_EXPERTS - 1] + nblk_ref[N_EXPERTS - 1]

    def rows_of(g):
        return pl.ds(pl.multiple_of(g * BLOCK_TILE_ROWS, BLOCK_TILE_ROWS), BLOCK_TILE_ROWS)

    def x_copy(g, buf):
        return pltpu.make_async_copy(x_hbm.at[rows_of(g)], xbuf.at[buf], xsem.at[buf])

    def y_copy(g, buf):
        return pltpu.make_async_copy(ybuf.at[buf], y_hbm.at[rows_of(g)], ysem.at[buf])

    for ahead in range(EXPERT_X_BUFFERS - 1):
        @pl.when((e == 0) & (total > ahead))
        def _():
            x_copy(ahead, ahead).start()

    wg_s[...] = wg_ref[...].astype(BF16)
    wu_s[...] = wu_ref[...].astype(BF16)
    wd_s[...] = wd_ref[...].astype(BF16)

    def block(j, carry):
        g = first_blk + j
        buf = g % 2
        xb = lax.rem(g, EXPERT_X_BUFFERS)
        nxt = g + (EXPERT_X_BUFFERS - 1)

        @pl.when(nxt < total)
        def _():
            x_copy(nxt, lax.rem(nxt, EXPERT_X_BUFFERS)).start()

        x_copy(g, xb).wait()

        @pl.when(g >= 2)
        def _():
            y_copy(g - 2, buf).wait()

        x = _load_token_tiles(xbuf, (xb,), MOE_BM)
        row = lax.broadcasted_iota(jnp.int32, x.shape, 0)
        lo, hi = _unpack_rows(jnp.where(row < count - j * MOE_BM, x, jnp.uint32(0)))
        lo, hi = lo.astype(BF16), hi.astype(BF16)
        gate = _mm(lo, wg_s[:HALF_D, :]) + _mm(hi, wg_s[HALF_D:, :])
        up = _mm(lo, wu_s[:HALF_D, :]) + _mm(hi, wu_s[HALF_D:, :])
        _store_token_tiles(ybuf, (buf,), _pack_rows(_mm((_silu(gate) * up).astype(BF16), wd_s[...])))
        y_copy(g, buf).start()
        return carry

    lax.fori_loop(0, n_blk, block, 0)

    @pl.when((e == N_EXPERTS - 1) & (total >= 2))
    def _():
        y_copy(total - 2, total % 2).wait()

    @pl.when((e == N_EXPERTS - 1) & (total >= 1))
    def _():
        y_copy(total - 1, (total - 1) % 2).wait()


def _routed_experts(xg, first_blk, n_blk, counts, layer, w_g, w_u, w_d):
    def weight(shape):
        return pl.BlockSpec((None, None) + shape, lambda e, *_: (layer, e, 0, 0))

    any_space = pl.BlockSpec(memory_space=pl.ANY)
    x_bufs = pltpu.VMEM((EXPERT_X_BUFFERS, BLOCK_TILE_ROWS, LANES), jnp.uint32)
    y_bufs = pltpu.VMEM((2, BLOCK_TILE_ROWS, LANES), jnp.uint32)
    grid_spec = pltpu.PrefetchScalarGridSpec(
        num_scalar_prefetch=3,
        grid=(N_EXPERTS,),
        in_specs=[weight((D_MODEL, D_EXPERT)), weight((D_MODEL, D_EXPERT)), weight((D_EXPERT, D_MODEL)), any_space],
        out_specs=any_space,
        scratch_shapes=[pltpu.VMEM((D_MODEL, D_EXPERT), BF16), pltpu.VMEM((D_MODEL, D_EXPERT), BF16),
                        pltpu.VMEM((D_EXPERT, D_MODEL), BF16), x_bufs, y_bufs,
                        pltpu.SemaphoreType.DMA((EXPERT_X_BUFFERS,)), pltpu.SemaphoreType.DMA((2,))])
    return pl.pallas_call(
        _expert_kernel,
        grid_spec=grid_spec,
        out_shape=jax.ShapeDtypeStruct((MOE_ROWS * SUBLANES, LANES), jnp.uint32),
        compiler_params=_params("arbitrary"),
        name="routed_experts",
    )(first_blk, n_blk, counts, w_g, w_u, w_d, xg)


def _ffn_out_kernel(slot_ref, h_ref, wt_ref, x1_ref, wg_ref, wu_ref, wd_ref, g2_ref, lng_ref, lnb_ref, y_ref,
                    oc_ref, ol_ref, ybuf_a, ybuf_b, sems):
    tm = h_ref.shape[0] // SUBLANES
    i = pl.program_id(0)
    n = pl.num_programs(0)

    def gather(tile, dst, sem):
        base = tile * tm
        for r in range(tm):
            for k in range(TOP_K):
                _row_copy(y_ref, slot_ref[base + (k * N_TOK + r)], dst.at[k], r,
                          sem).start(priority=k % ROW_DMA_QUEUES)

    def wait(dst, sem):
        for k in range(TOP_K):
            pltpu.make_async_copy(y_ref.at[pl.ds(0, tm * SUBLANES)], dst.at[k], sem).wait()

    @pl.when(i == 0)
    def _():
        gather(0, ybuf_a, sems.at[0])

    def step(cur, cur_sem, nxt, nxt_sem):
        gather(jnp.minimum(i + 1, n - 1), nxt, nxt_sem)
        lo, hi = _unpack_rows(_load_token_tiles(h_ref, (), tm))
        lo, hi = lo.astype(BF16), hi.astype(BF16)
        gate = _mm(lo, wg_ref[:HALF_D, :]) + _mm(hi, wg_ref[HALF_D:, :])
        up = _mm(lo, wu_ref[:HALF_D, :]) + _mm(hi, wu_ref[HALF_D:, :])
        y = _mm((_silu(gate) * up).astype(BF16), wd_ref[...])
        wait(cur, cur_sem)
        wt = wt_ref[...]
        y_lo, y_hi = y[:, :HALF_D], y[:, HALF_D:]
        for k in range(TOP_K):
            r_lo, r_hi = _unpack_rows(_load_token_tiles(cur, (k,), tm))
            y_lo = y_lo + r_lo * wt[:, k:k + 1]
            y_hi = y_hi + r_hi * wt[:, k:k + 1]
        y = jnp.concatenate([y_lo, y_hi], axis=1)
        out = _layernorm(ALPHA * x1_ref[...] + g2_ref[...] * y, lng_ref[...], lnb_ref[...])
        is_ctx = _is_ctx_tile(tm)

        @pl.when(is_ctx)
        def _():
            oc_ref[...] = out

        @pl.when(jnp.logical_not(is_ctx))
        def _():
            ol_ref[...] = out

        @pl.when(i == n - 1)
        def _():
            wait(nxt, nxt_sem)

    @pl.when(i % 2 == 0)
    def _():
        step(ybuf_a, sems.at[0], ybuf_b, sems.at[1])

    @pl.when(i % 2 == 1)
    def _():
        step(ybuf_b, sems.at[1], ybuf_a, sems.at[0])


def _ffn_out(slot_flat, h2, wt, x1, ws_g, ws_u, ws_d, mod, ln_g, ln_b, y_blk):
    tm = TM_FFN
    row = pl.BlockSpec((tm, D_MODEL), lambda i, s: (i, 0))
    vec = pl.BlockSpec((1, D_MODEL), lambda i, s: (0, 0))
    grid_spec = pltpu.PrefetchScalarGridSpec(
        num_scalar_prefetch=1,
        grid=(N_TOK // tm,),
        in_specs=[pl.BlockSpec((tm * SUBLANES, LANES), lambda i, s: (i, 0)),
                  pl.BlockSpec((tm, 8), lambda i, s: (i, 0)), row,
                  pl.BlockSpec((D_MODEL, D_EXPERT), lambda i, s: (0, 0)),
                  pl.BlockSpec((D_MODEL, D_EXPERT), lambda i, s: (0, 0)),
                  pl.BlockSpec((D_EXPERT, D_MODEL), lambda i, s: (0, 0)),
                  _mod_spec(5, tm), vec, vec,
                  pl.BlockSpec(memory_space=pl.ANY)],
        out_specs=[_ctx_rows(tm), _lat_rows(tm)],
        scratch_shapes=[pltpu.VMEM((TOP_K, tm * SUBLANES, LANES), jnp.uint32),
                        pltpu.VMEM((TOP_K, tm * SUBLANES, LANES), jnp.uint32),
                        pltpu.SemaphoreType.DMA((2,))])
    return pl.pallas_call(
        _ffn_out_kernel,
        grid_spec=grid_spec,
        out_shape=[jax.ShapeDtypeStruct((N_CTX, D_MODEL), F32), jax.ShapeDtypeStruct((N_LAT, D_MODEL), F32)],
        compiler_params=_params("arbitrary"),
        name="shared_expert_out",
    )(slot_flat, h2, wt, x1, ws_g, ws_u, ws_d, mod, ln_g, ln_b, y_blk)


def _dispatch_plan(idx, rank, counts):
    bm = MOE_BM
    counts = counts.reshape(N_EXPERTS).astype(jnp.int32)
    padded = (counts + bm - 1) // bm * bm
    pad_end = jnp.cumsum(padded)
    pad_start = pad_end - padded
    experts = jnp.arange(N_EXPERTS, dtype=jnp.int32)[:, None, None]
    first_row = jnp.sum(jnp.where(idx[None, :TOP_K] == experts, pad_start[:, None, None], 0), axis=0)
    slot = first_row + rank[:TOP_K]
    return slot.reshape(-1).astype(jnp.int32), pad_start // bm, padded // bm, counts


def _rope_tables():
    t = jnp.arange(DEC_SEQ)
    row = (t // GRID_W).astype(F32)
    col = (t % GRID_W).astype(F32)
    nf = HEAD_DIM // 4
    inv = ROPE_BASE ** (-jnp.arange(nf, dtype=F32) / nf)
    ar = row[:, None] * inv[None, :]
    ac = col[:, None] * inv[None, :]
    ang = jnp.concatenate([ar, ar, ac, ac], axis=-1)
    ang = jnp.concatenate([ang, ang], axis=-1)
    cos, sin = jnp.cos(ang), jnp.sin(ang)
    first = (jnp.arange(LANES) % (2 * nf)) < nf
    return cos, jnp.where(first, -sin, 0.0), jnp.where(first, 0.0, sin)


def _lambda_init(l):
    return 0.8 - 0.6 * math.exp(-0.3 * l)


def _gate_pairs(w_r, w_i):
    per = LANES // C_BW
    eye = jnp.eye(per, dtype=w_r.dtype)

    def chunk_diag(w):
        w = w.reshape(C_BLOCKS // per, per, C_BW, C_BW)
        return (eye[None, :, None, :, None] * w[:, :, :, None, :]).reshape(C_BLOCKS // per, LANES, LANES)

    return jnp.concatenate([chunk_diag(w_r), chunk_diag(w_i)], axis=2).reshape(C_WIDTH, 2 * LANES)


def kernel(x_prompt, x_sample, c, cache_a_k, cache_a_v, cache_b_k, cache_b_v, state_c, cache_d_k, cache_d_v, c_ctx, w_mod, b_mod, w_in, a_sink, b_lambda, c_conv_w, c_conv_b, c_w_rgate, c_b_rgate, c_w_igate, c_b_igate, c_lambda, d_rpb, g_mix, w_out, ln_g, ln_b, w_router, b_router, w_e_gate, w_e_up, w_e_down, w_s_gate, w_s_up, w_s_down):
    x_ctx, x_lat = x_prompt.reshape(N_CTX, D_MODEL), x_sample.reshape(N_LAT, D_MODEL)
    cond8 = jnp.zeros((8, D_MODEL), F32).at[0].set(c_ctx).at[1:1 + DEC_BATCH].set(c)
    mod_all = _modulation(cond8, w_mod, b_mod)
    cos, sin_a, sin_b = _rope_tables()
    na_row_idx, na_col_sel, na_col_ok = _na_bias_indices()

    order = ((0, 512), (768, IN_WIDTH), (512, 768))

    ctx_out = [[] for _ in range(7)]
    for l in range(DEPTH):
        lam_init = _lambda_init(l)
        mod = mod_all[l, :N_GROUPS].reshape(N_GROUPS, 6, 1, D_MODEL)
        w_l = w_in[l]
        w_in_b = jnp.concatenate([w_l[:, a:b] for a, b in order], axis=1).astype(BF16)
        slab = _in_projection(x_ctx, x_lat, mod, w_in_b)

        lp = b_lambda[l]
        lam = jnp.exp(jnp.sum(lp[0] * lp[1])) - jnp.exp(jnp.sum(lp[2] * lp[3])) + lam_init
        scal = jnp.zeros((16,), F32).at[:A_HEADS].set(a_sink[l]).at[A_HEADS].set(lam)
        g2d = g_mix[l].reshape(1, -1)

        oa, ob, od = _ctx_attention(slab, scal, g2d, lam_init)

        qa_p, ka_r, va_b, qb_p, kb_r, vb_b, qd_p, kd_b, vd_b = _latent_prep(slab, cos, sin_a, sin_b)
        oa = _window_attention(qa_p, ka_r, va_b, cache_a_k[:, l].reshape(DEC_BATCH, PAST_LEN, LANES),
                               cache_a_v[:, l].reshape(DEC_BATCH, PAST_LEN, LANES), scal, g2d, oa)
        ob = _diff_attention(qb_p, kb_r, vb_b, cache_b_k[:, l].reshape(DEC_BATCH, PAST_LEN, 512),
                             cache_b_v[:, l].reshape(DEC_BATCH, PAST_LEN, 512), scal, g2d, lam_init, ob)
        bias = _na_bias(d_rpb[l], na_row_idx, na_col_sel, na_col_ok)
        od = _neighbourhood_attention(qd_p, kd_b, vd_b, cache_d_k[:, l].reshape(DEC_BATCH, PAST_LEN, 512),
                                      cache_d_v[:, l].reshape(DEC_BATCH, PAST_LEN, 512), bias, g2d, od)

        w_gates = jnp.stack([_gate_pairs(c_w_rgate[l, d], c_w_igate[l, d]) for d in range(2)])
        b_gates = jnp.concatenate([c_b_rgate[l], c_b_igate[l]], axis=-1).reshape(2, 1, 2 * C_WIDTH)
        zeros = jnp.zeros((BATCH, 1, C_WIDTH), F32)
        h0_f = jnp.concatenate([zeros, state_c[:, l, 0][:, None, :]], axis=0)
        h0_b = jnp.concatenate([zeros, state_c[:, l, 1][:, None, :]], axis=0)
        oc, fin_f, fin_b = _rglru(slab, c_conv_w[l], c_conv_b[l].reshape(1, -1), w_gates, b_gates,
                                  c_lambda[l].reshape(2, 1, C_WIDTH), h0_f, h0_b, g2d)

        x1, h2, idx, wt, rank, counts = _mixer_out(oa, ob, oc, od, w_out[l].astype(BF16), x_ctx, x_lat, mod,
                                                   ln_g[l, 0].reshape(1, -1), ln_b[l, 0].reshape(1, -1),
                                                   w_router[l], b_router[l].reshape(1, -1))

        slot, first_blk, n_blk, counts = _dispatch_plan(idx, rank, counts)
        xg = _dispatch_rows(slot, h2)
        y_blk = _routed_experts(xg, first_blk, n_blk, counts, l, w_e_gate, w_e_up, w_e_down)
        x_ctx, x_lat = _ffn_out(slot, h2, wt, x1, w_s_gate[l].astype(BF16), w_s_up[l].astype(BF16),
                                w_s_down[l].astype(BF16), mod, ln_g[l, 1].reshape(1, -1),
                                ln_b[l, 1].reshape(1, -1), y_blk)

        ctx = slab[:N_CTX]
        ctx_out[0].append(ctx[:, COL_KA * LANES:(COL_KA + 1) * LANES].reshape(BATCH, SEQ, A_KV_HEADS, HEAD_DIM))
        ctx_out[1].append(ctx[:, COL_VA * LANES:(COL_VA + 1) * LANES].reshape(BATCH, SEQ, A_KV_HEADS, HEAD_DIM))
        ctx_out[2].append(ctx[:, COL_KB * 512:(COL_KB + 1) * 512].reshape(BATCH, SEQ, B_HEADS, 2, HEAD_DIM))
        ctx_out[3].append(ctx[:, COL_VB * 512:(COL_VB + 1) * 512].reshape(BATCH, SEQ, B_HEADS, 2 * HEAD_DIM))
        ctx_out[4].append(jnp.concatenate([fin_f[:BATCH], fin_b[:BATCH]], axis=1))
        ctx_out[5].append(ctx[:, COL_KD * 512:(COL_KD + 1) * 512].reshape(BATCH, SEQ, D_HEADS, HEAD_DIM))
        ctx_out[6].append(ctx[:, COL_VD * 512:(COL_VD + 1) * 512].reshape(BATCH, SEQ, D_HEADS, HEAD_DIM))

    new = [jnp.stack(t, axis=1) for t in ctx_out]
    return (x_ctx.reshape(BATCH, SEQ, D_MODEL), x_lat.reshape(DEC_BATCH, DEC_SEQ, D_MODEL), *new)
```

```python
import functools
import math

import numpy as np
import jax
import jax.numpy as jnp
from jax import lax
from jax.experimental import pallas as pl
from jax.experimental.pallas import tpu as pltpu

F32 = jnp.float32
BF16 = jnp.bfloat16

D_MODEL = 2048
BATCH = 16
SEQ = 256
DEPTH = 2
DEC_BATCH = 2
DEC_SEQ = 4096
PAST_LEN = 256
GRID_W = 64
HEAD_DIM = 64
ROPE_BASE = 10000.0
A_HEADS = 8
A_KV_HEADS = 2
A_WINDOW = 128
B_HEADS = 4
C_WIDTH = 512
C_BLOCKS = 8
C_BW = C_WIDTH // C_BLOCKS
C_POW = 8.0
D_HEADS = 8
NA_ROWS = 8
NA_COLS = 16
N_EXPERTS = 64
TOP_K = 6
D_EXPERT = 512
ROUTED_SCALE = 2.5
ALPHA = (2.0 * DEPTH) ** 0.25
LN_EPS = 1e-5
NORM_EPS = 1e-6
NEG_INF = -1e30
SCALE = HEAD_DIM ** -0.5
LOG2E = math.log2(math.e)

N_CTX = BATCH * SEQ
N_LAT = DEC_BATCH * DEC_SEQ
N_TOK = N_CTX + N_LAT
GROUP_ROWS = 4096
N_GROUPS = N_TOK // GROUP_ROWS
IN_WIDTH = 4864
LANES = 128
VMEM_LIMIT = 56 * 1024 * 1024

COL_QA, COL_QB, COL_KB, COL_VB, COL_XC, COL_GC, COL_QD, COL_KD, COL_VD = range(9)
COL_KA, COL_VA = 36, 37

TM_PROJ = 512
TN_PROJ = IN_WIDTH // 2
TM_PREP = 256
TQ_A = 128
TQ_B = 256
NA_QR = 4
NA_KR = 12
T_SCAN = 256
TM_OUT = 256
MOE_BM = 256
TM_DISPATCH = 256
TM_FFN = 256


def _params(*sem):
    return pltpu.CompilerParams(dimension_semantics=sem, vmem_limit_bytes=VMEM_LIMIT)


def _nt(a, b):
    return lax.dot_general(a, b, (((1,), (1,)), ((), ())), preferred_element_type=F32)


def _mm(a, b):
    return jnp.dot(a, b, preferred_element_type=F32)


def _softmax_unnorm(parts, sink=None):
    m = parts[0].max(axis=-1, keepdims=True)
    for p in parts[1:]:
        m = jnp.maximum(m, p.max(axis=-1, keepdims=True))
    if sink is not None:
        m = jnp.maximum(m, sink)
    es = [jnp.exp(p - m) for p in parts]
    den = es[0].sum(axis=-1, keepdims=True)
    for e in es[1:]:
        den = den + e.sum(axis=-1, keepdims=True)
    if sink is not None:
        den = den + jnp.exp(sink - m)
    return es, den


def _rms_gain(x, g):
    return x * lax.rsqrt(jnp.mean(x * x, axis=-1, keepdims=True) + NORM_EPS) * g


def _layernorm(z, g, b):
    zc = z - jnp.mean(z, axis=-1, keepdims=True)
    var = jnp.mean(zc * zc, axis=-1, keepdims=True)
    return zc * lax.rsqrt(var + LN_EPS) * g + b


def _silu(x):
    return x * jax.nn.sigmoid(x)


def _low_half(shape):
    return lax.broadcasted_iota(jnp.int32, shape, 1) < HEAD_DIM


HALF_D = D_MODEL // 2
HIGH16 = 0xFFFF0000


def _pack_rows(x):
    bits = pltpu.bitcast(x.astype(BF16).astype(F32), jnp.uint32)
    return (bits[:, :HALF_D] >> 16) | (bits[:, HALF_D:] & jnp.uint32(HIGH16))


def _unpack_rows(p):
    return (pltpu.bitcast(p << 16, F32), pltpu.bitcast(p & jnp.uint32(HIGH16), F32))


SUBLANES = 8
assert HALF_D == SUBLANES * LANES


def _store_token_tiles(ref, index, packed, first=0):
    m = packed.shape[0]
    for s in range(SUBLANES):
        rows = pl.ds(first * SUBLANES + s, m, stride=SUBLANES)
        ref[index + (rows, slice(None))] = packed[:, s * LANES:(s + 1) * LANES]


def _load_token_tiles(ref, index, m):
    return jnp.concatenate([ref[index + (pl.ds(s, m, stride=SUBLANES), slice(None))] for s in range(SUBLANES)],
                           axis=1)


def _mod_kernel(c_ref, w_ref, b_ref, o_ref):
    c = c_ref[...]
    o_ref[0] = jnp.dot(_silu(c), w_ref[0], precision=lax.Precision.HIGHEST,
                       preferred_element_type=F32) + b_ref[0]


def _modulation(cond8, w_mod, b_mod):
    tn = 1024
    n = w_mod.shape[-1]
    return pl.pallas_call(
        _mod_kernel,
        grid=(DEPTH, n // tn),
        in_specs=[pl.BlockSpec((8, D_MODEL), lambda l, j: (0, 0)),
                  pl.BlockSpec((1, D_MODEL, tn), lambda l, j: (l, 0, j)),
                  pl.BlockSpec((1, 1, tn), lambda l, j: (l, 0, j))],
        out_specs=pl.BlockSpec((1, 8, tn), lambda l, j: (l, 0, j)),
        out_shape=jax.ShapeDtypeStruct((DEPTH, 8, n), F32),
        compiler_params=_params("parallel", "parallel"),
        name="modulation",
    )(cond8, w_mod, b_mod.reshape(DEPTH, 1, n))


def _mod_spec(which, tm):
    return pl.BlockSpec((None, None, 1, D_MODEL), lambda i, *_: (i * tm // GROUP_ROWS, which, 0, 0))


def _ctx_rows(tm):
    return pl.BlockSpec((tm, D_MODEL), lambda i, *_: (jnp.minimum(i, N_CTX // tm - 1), 0))


def _lat_rows(tm):
    return pl.BlockSpec((tm, D_MODEL), lambda i, *_: (jnp.maximum(i - N_CTX // tm, 0), 0))


def _is_ctx_tile(tm):
    return pl.program_id(0) * tm < N_CTX


def _inproj_kernel(xc_ref, xl_ref, sc_ref, sh_ref, w_ref, o_ref, xb_ref):
    @pl.when(pl.program_id(1) == 0)
    def _():
        x = jnp.where(_is_ctx_tile(xc_ref.shape[0]), xc_ref[...], xl_ref[...])
        xb_ref[...] = (x * (1.0 + sc_ref[...]) + sh_ref[...]).astype(BF16)

    o_ref[...] = _mm(xb_ref[...], w_ref[...])


def _in_projection(x_ctx, x_lat, mod, w_in_b):
    tm, tn = TM_PROJ, TN_PROJ
    return pl.pallas_call(
        _inproj_kernel,
        grid=(N_TOK // tm, IN_WIDTH // tn),
        in_specs=[_ctx_rows(tm), _lat_rows(tm),
                  _mod_spec(1, tm), _mod_spec(0, tm),
                  pl.BlockSpec((D_MODEL, tn), lambda i, j: (0, j))],
        out_specs=pl.BlockSpec((tm, tn), lambda i, j: (i, j)),
        out_shape=jax.ShapeDtypeStruct((N_TOK, IN_WIDTH), F32),
        scratch_shapes=[pltpu.VMEM((tm, D_MODEL), BF16)],
        compiler_params=_params("parallel", "arbitrary"),
        name="in_projection",
    )(x_ctx, x_lat, mod, mod, w_in_b)


def _ctx_attn_kernel(scal_ref, qa_ref, ka_ref, va_ref, qb_ref, kb_ref, vb_ref, qd_ref, kd_ref, vd_ref,
                     ga_ref, gb_ref, gd_ref, oa_ref, ob_ref, od_ref, *, lam_init):
    L = SEQ
    lo = _low_half((L, LANES))

    ka = ka_ref[...].astype(BF16)
    va = va_ref[...].astype(BF16)
    qa = qa_ref[...]
    chunks = []
    for c in range(A_HEADS // 2):
        hk = c // 2
        chunk = qa[:, c * LANES:(c + 1) * LANES]
        keep = lo if hk == 0 else jnp.logical_not(lo)
        halves = []
        for half in range(2):
            x = chunk if half == hk else pltpu.roll(chunk, HEAD_DIM, 1)
            qh = jnp.where(keep, x, 0.0).astype(BF16)
            s = _nt(qh, ka) * SCALE
            (e,), den = _softmax_unnorm([s], scal_ref[2 * c + half])
            o = _mm(e.astype(BF16), va) / den
            halves.append(o if half == hk else pltpu.roll(o, HEAD_DIM, 1))
        chunks.append(jnp.where(lo, halves[0], halves[1]))
    oa = jnp.concatenate(chunks, axis=1)
    oa_ref[...] = _rms_gain(oa, ga_ref[...]).astype(BF16)

    lam = scal_ref[A_HEADS]
    qb = qb_ref[...]
    chunks = []
    for h in range(B_HEADS):
        sl = slice(h * LANES, (h + 1) * LANES)
        qc = qb[:, sl]
        kc = kb_ref[:, sl].astype(BF16)
        vh = vb_ref[:, sl].astype(BF16)
        (e1,), d1 = _softmax_unnorm([_nt(jnp.where(lo, qc, 0.0).astype(BF16), kc) * SCALE])
        (e2,), d2 = _softmax_unnorm([_nt(jnp.where(lo, 0.0, qc).astype(BF16), kc) * SCALE])
        w = e1 * (1.0 / d1) - e2 * (lam / d2)
        o = _mm(w.astype(BF16), vh)
        chunks.append(_rms_gain(o, gb_ref[:, sl]) * (1.0 - lam_init))
    ob_ref[...] = jnp.concatenate(chunks, axis=1).astype(BF16)

    qd = qd_ref[...]
    chunks = []
    for c in range(D_HEADS // 2):
        sl = slice(c * LANES, (c + 1) * LANES)
        qc = qd[:, sl]
        kc = kd_ref[:, sl].astype(BF16)
        vc = vd_ref[:, sl].astype(BF16)
        halves = []
        for half in range(2):
            qh = jnp.where(lo if half == 0 else jnp.logical_not(lo), qc, 0.0).astype(BF16)
            (e,), den = _softmax_unnorm([_nt(qh, kc) * SCALE])
            halves.append(_mm(e.astype(BF16), vc) / den)
        chunks.append(jnp.where(lo, halves[0], halves[1]))
    od = jnp.concatenate(chunks, axis=1)
    od_ref[...] = _rms_gain(od, gd_ref[...]).astype(BF16)


def _ctx_attention(slab, scal, g_mix2d, lam_init):
    L = SEQ

    def wide(col):
        return pl.BlockSpec((L, 512), lambda b: (b, col))

    def narrow(col):
        return pl.BlockSpec((L, LANES), lambda b: (b, col))

    def gain(col):
        return pl.BlockSpec((1, 512), lambda b: (0, col))

    out = jax.ShapeDtypeStruct((N_TOK, 512), BF16)
    ospec = pl.BlockSpec((L, 512), lambda b: (b, 0))
    return pl.pallas_call(
        functools.partial(_ctx_attn_kernel, lam_init=lam_init),
        grid=(BATCH,),
        in_specs=[pl.BlockSpec(memory_space=pltpu.SMEM),
                  wide(COL_QA), narrow(COL_KA), narrow(COL_VA),
                  wide(COL_QB), wide(COL_KB), wide(COL_VB),
                  wide(COL_QD), wide(COL_KD), wide(COL_VD),
                  gain(0), gain(1), gain(3)],
        out_specs=[ospec, ospec, ospec],
        out_shape=[out, out, out],
        compiler_params=_params("parallel"),
        name="ctx_attention",
    )(scal, slab, slab, slab, slab, slab, slab, slab, slab, slab, g_mix2d, g_mix2d, g_mix2d)


def _prep_kernel(qa_ref, ka_ref, va_ref, qb_ref, kb_ref, vb_ref, qd_ref, kd_ref, vd_ref,
                 cos_ref, sa_ref, sb_ref,
                 qa_o, ka_o, va_o, qb_o, kb_o, vb_o, qd_o, kd_o, vd_o):
    cos, sa, sb = cos_ref[...], sa_ref[...], sb_ref[...]
    lo = _low_half(cos.shape)
    hi = jnp.logical_not(lo)

    def rope(x):
        return (x * cos + pltpu.roll(x, LANES - HEAD_DIM // 4, 1) * sa
                + pltpu.roll(x, HEAD_DIM // 4, 1) * sb)

    ka_o[...] = rope(ka_ref[...]).astype(BF16)
    va_o[...] = va_ref[...].astype(BF16)
    vb_o[...] = vb_ref[...].astype(BF16)
    kd_o[...] = kd_ref[...].astype(BF16)
    vd_o[...] = vd_ref[...].astype(BF16)

    for c in range(A_HEADS // 2):
        hk = c // 2
        r = rope(qa_ref[:, c * LANES:(c + 1) * LANES])
        keep = lo if hk == 0 else hi
        for half in range(2):
            h = 2 * c + half
            x = r if half == hk else pltpu.roll(r, HEAD_DIM, 1)
            qa_o[:, h * LANES:(h + 1) * LANES] = jnp.where(keep, x, 0.0).astype(BF16)
    for h in range(B_HEADS):
        sl = slice(h * LANES, (h + 1) * LANES)
        r = rope(qb_ref[:, sl])
        kb_o[:, sl] = rope(kb_ref[:, sl]).astype(BF16)
        qb_o[:, (2 * h) * LANES:(2 * h + 1) * LANES] = jnp.where(lo, r, 0.0).astype(BF16)
        qb_o[:, (2 * h + 1) * LANES:(2 * h + 2) * LANES] = jnp.where(hi, r, 0.0).astype(BF16)
    for c in range(D_HEADS // 2):
        x = qd_ref[:, c * LANES:(c + 1) * LANES]
        qd_o[:, (2 * c) * LANES:(2 * c + 1) * LANES] = jnp.where(lo, x, 0.0).astype(BF16)
        qd_o[:, (2 * c + 1) * LANES:(2 * c + 2) * LANES] = jnp.where(hi, x, 0.0).astype(BF16)


def _latent_prep(slab, cos, sa, sb):
    tm = TM_PREP
    off = N_CTX // tm
    per_seq = DEC_SEQ // tm

    def wide(col):
        return pl.BlockSpec((tm, 512), lambda i: (i + off, col))

    def narrow(col):
        return pl.BlockSpec((tm, LANES), lambda i: (i + off, col))

    tab = pl.BlockSpec((tm, LANES), lambda i: (i % per_seq, 0))

    def out(width):
        return (pl.BlockSpec((tm, width), lambda i: (i, 0)), jax.ShapeDtypeStruct((N_LAT, width), BF16))

    outs = [out(1024), out(LANES), out(LANES), out(1024), out(512), out(512), out(1024), out(512), out(512)]
    return pl.pallas_call(
        _prep_kernel,
        grid=(N_LAT // tm,),
        in_specs=[wide(COL_QA), narrow(COL_KA), narrow(COL_VA), wide(COL_QB), wide(COL_KB), wide(COL_VB),
                  wide(COL_QD), wide(COL_KD), wide(COL_VD), tab, tab, tab],
        out_specs=[o[0] for o in outs],
        out_shape=[o[1] for o in outs],
        compiler_params=_params("parallel"),
        name="latent_prep",
    )(slab, slab, slab, slab, slab, slab, slab, slab, slab, cos, sa, sb)


def _win_attn_kernel(scal_ref, q_ref, kp_ref, kc_ref, kn_ref, vp_ref, vc_ref, vn_ref, kctx_ref, vctx_ref,
                     g_ref, _all_rows_ref, o_ref):
    n = pl.program_id(1)
    nb = pl.num_programs(1)
    tq = TQ_A
    group = A_HEADS // A_KV_HEADS
    k = jnp.concatenate([kp_ref[...], kc_ref[...], kn_ref[...], kctx_ref[...].astype(BF16)], axis=0)
    v = jnp.concatenate([vp_ref[...], vc_ref[...], vn_ref[...], vctx_ref[...].astype(BF16)], axis=0)
    nk = 3 * tq + PAST_LEN
    row = lax.broadcasted_iota(jnp.int32, (group * tq, nk), 0) & (tq - 1)
    col = lax.broadcasted_iota(jnp.int32, (group * tq, nk), 1)
    prev_thr = row + jnp.where(n > 0, 0, tq)
    next_thr = row + 2 * tq - jnp.where(n < nb - 1, 0, tq)
    masked = ((col < tq) & (col < prev_thr)) | ((col >= 2 * tq) & (col < 3 * tq) & (col > next_thr))
    valid = jnp.logical_not(masked)
    lo = _low_half((tq, LANES))
    heads = []
    for hk in range(A_KV_HEADS):
        q4 = jnp.concatenate([q_ref[:, (hk * group + g) * LANES:(hk * group + g + 1) * LANES]
                              for g in range(group)], axis=0)
        s = jnp.where(valid, _nt(q4, k) * SCALE, NEG_INF)
        sink = jnp.concatenate([jnp.full((tq, 1), scal_ref[hk * group + g], F32) for g in range(group)], axis=0)
        (e,), den = _softmax_unnorm([s], sink)
        o = _mm(e.astype(BF16), v) / den
        for g in range(group):
            h = hk * group + g
            og = o[g * tq:(g + 1) * tq]
            heads.append(og if (h % 2) == hk else pltpu.roll(og, HEAD_DIM, 1))
    oa = jnp.concatenate([jnp.where(lo, heads[2 * c], heads[2 * c + 1]) for c in range(A_HEADS // 2)], axis=1)
    o_ref[...] = _rms_gain(oa, g_ref[...]).astype(BF16)


def _window_attention(qa_p, ka_r, va_b, cache_k, cache_v, scal, g_mix2d, o_all):
    tq = TQ_A
    nb = DEC_SEQ // tq

    def band(d):
        return pl.BlockSpec((tq, LANES), lambda b, n: (b * nb + jnp.clip(n + d, 0, nb - 1), 0))

    ctx = pl.BlockSpec((None, PAST_LEN, LANES), lambda b, n: (b, 0, 0))
    return pl.pallas_call(
        _win_attn_kernel,
        grid=(DEC_BATCH, nb),
        in_specs=[pl.BlockSpec(memory_space=pltpu.SMEM),
                  pl.BlockSpec((tq, A_HEADS * LANES), lambda b, n: (b * nb + n, 0)),
                  band(-1), band(0), band(1), band(-1), band(0), band(1), ctx, ctx,
                  pl.BlockSpec((1, 512), lambda b, n: (0, 0)),
                  pl.BlockSpec(memory_space=pl.ANY)],
        out_specs=pl.BlockSpec((tq, 512), lambda b, n: (N_CTX // tq + b * nb + n, 0)),
        out_shape=jax.ShapeDtypeStruct((N_TOK, 512), BF16),
        input_output_aliases={11: 0},
        compiler_params=_params("parallel", "parallel"),
        name="window_attention",
    )(scal, qa_p, ka_r, ka_r, ka_r, va_b, va_b, va_b, cache_k, cache_v, g_mix2d, o_all)


def _diff_attn_kernel(scal_ref, q_ref, k_ref, v_ref, kctx_ref, vctx_ref, g_ref, _all_rows_ref, o_ref, *,
                      lam_init):
    lam = scal_ref[A_HEADS]
    chunks = []
    for h in range(B_HEADS):
        sl = slice(h * LANES, (h + 1) * LANES)
        kh = k_ref[:, sl]
        kc = kctx_ref[:, sl].astype(BF16)
        es, dens = [], []
        for m in range(2):
            q = q_ref[:, (2 * h + m) * LANES:(2 * h + m + 1) * LANES]
            t_lat = _nt(q, kh) * (SCALE * LOG2E)
            t_ctx = _nt(q, kc) * (SCALE * LOG2E)
            mx = jnp.maximum(t_lat.max(axis=-1, keepdims=True), t_ctx.max(axis=-1, keepdims=True))
            e_lat = jnp.exp2(t_lat - mx)
            e_ctx = jnp.exp2(t_ctx - mx)
            es.append((e_lat, e_ctx))
            dens.append(e_lat.sum(axis=-1, keepdims=True) + e_ctx.sum(axis=-1, keepdims=True))
        ratio = lam * dens[0] / dens[1]
        w_lat = (es[0][0] - es[1][0] * ratio).astype(BF16)
        w_ctx = (es[0][1] - es[1][1] * ratio).astype(BF16)
        o = (_mm(w_lat, v_ref[:, sl]) + _mm(w_ctx, vctx_ref[:, sl].astype(BF16))) / dens[0]
        chunks.append(_rms_gain(o, g_ref[:, sl]) * (1.0 - lam_init))
    o_ref[...] = jnp.concatenate(chunks, axis=1).astype(BF16)


def _diff_attention(qb_p, kb_r, vb_b, cache_k, cache_v, scal, g_mix2d, lam_init, o_all):
    tq = TQ_B
    nb = DEC_SEQ // tq
    full = pl.BlockSpec((DEC_SEQ, 512), lambda b, n: (b, 0))
    ctx = pl.BlockSpec((None, PAST_LEN, 512), lambda b, n: (b, 0, 0))
    return pl.pallas_call(
        functools.partial(_diff_attn_kernel, lam_init=lam_init),
        grid=(DEC_BATCH, nb),
        in_specs=[pl.BlockSpec(memory_space=pltpu.SMEM),
                  pl.BlockSpec((tq, 2 * B_HEADS * LANES), lambda b, n: (b * nb + n, 0)),
                  full, full, ctx, ctx,
                  pl.BlockSpec((1, 512), lambda b, n: (0, 1)),
                  pl.BlockSpec(memory_space=pl.ANY)],
        out_specs=pl.BlockSpec((tq, 512), lambda b, n: (N_CTX // tq + b * nb + n, 0)),
        out_shape=jax.ShapeDtypeStruct((N_TOK, 512), BF16),
        input_output_aliases={7: 0},
        compiler_params=_params("parallel", "parallel"),
        name="diff_attention",
    )(scal, qb_p, kb_r, vb_b, cache_k, cache_v, g_mix2d, o_all)


def _na_window_start(step):
    return np.clip(step * NA_QR - NA_ROWS // 2, 0, GRID_W - NA_KR)


def _na_bias_indices():
    rows = DEC_SEQ // GRID_W
    steps = rows // NA_QR
    pats = []
    for step in range(steps):
        w0 = _na_window_start(step)
        r = step * NA_QR + np.arange(NA_QR)[:, None, None, None]
        c = np.arange(GRID_W)[None, :, None, None]
        kr = w0 + np.arange(NA_KR)[None, None, :, None]
        kc = np.arange(GRID_W)[None, None, None, :]
        kr0 = np.clip(r - NA_ROWS // 2, 0, rows - NA_ROWS)
        kc0 = np.clip(c - NA_COLS // 2, 0, GRID_W - NA_COLS)
        valid = (kr >= kr0) & (kr < kr0 + NA_ROWS) & (kc >= kc0) & (kc < kc0 + NA_COLS)
        drow = np.clip(kr - r + NA_ROWS - 1, 0, 2 * NA_ROWS - 2)
        dcol = np.clip(kc - c + NA_COLS - 1, 0, 2 * NA_COLS - 2)
        shape = (NA_QR * GRID_W, NA_KR * GRID_W)
        full = np.broadcast_to
        pats.append((full(drow, valid.shape).reshape(shape), full(dcol, valid.shape).reshape(shape),
                     valid.reshape(shape)))
    for step in range(2, steps - 1):
        for a, b in zip(pats[1], pats[step]):
            assert np.array_equal(a, b)
    kinds = [pats[0], pats[1], pats[-1]]
    drow, dcol, valid = (np.stack([k[i] for k in kinds]) for i in range(3))
    shape6 = (len(kinds), NA_QR, GRID_W, NA_KR, GRID_W)
    valid6 = valid.reshape(shape6)
    row_ok = valid6.any(axis=(2, 4))
    col_ok = valid6.any(axis=(0, 1, 3))
    assert np.array_equal(valid6, row_ok[:, :, None, :, None] & col_ok[None, None, :, None, :])
    outside = 2 * NA_ROWS - 1
    row_idx = np.where(row_ok, drow.reshape(shape6)[:, :, 0, :, 0], outside)
    col_sel = np.eye(2 * NA_COLS - 1, dtype=np.float32)[dcol.reshape(shape6)[0, 0, :, 0, :]]
    return row_idx, col_sel, col_ok


def _na_bias(rpb, row_idx, col_sel, col_ok):
    cols = jnp.einsum('hab,cdb->hacd', rpb, col_sel, precision=lax.Precision.HIGHEST)
    cols = jnp.where(col_ok[None, None], cols, NEG_INF)
    cols = jnp.concatenate([cols, jnp.full((D_HEADS, 1, GRID_W, GRID_W), NEG_INF, F32)], axis=1)
    kinds = []
    for kind in row_idx:
        kinds.append(jnp.concatenate(
            [jnp.concatenate([cols[:, int(a)] for a in per_row], axis=-1) for per_row in kind], axis=1))
    return jnp.stack(kinds)


def _na_kernel(q_ref, k_ref, v_ref, kctx_ref, vctx_ref, bias_ref, g_ref, _all_rows_ref, o_ref):
    step = pl.program_id(1)
    nq = NA_QR * GRID_W
    nk = NA_KR * GRID_W
    w0 = jnp.clip(step * NA_QR - NA_ROWS // 2, 0, GRID_W - NA_KR)
    start = pl.multiple_of(w0 * GRID_W, GRID_W)
    lo = _low_half((nq, LANES))
    chunks = []
    for c in range(D_HEADS // 2):
        sl = slice(c * LANES, (c + 1) * LANES)
        kw = k_ref[pl.ds(start, nk), sl]
        vw = v_ref[pl.ds(start, nk), sl]
        kc = kctx_ref[:, sl].astype(BF16)
        vc = vctx_ref[:, sl].astype(BF16)
        halves = []
        for half in range(2):
            h = 2 * c + half
            q = q_ref[:, h * LANES:(h + 1) * LANES]
            s_loc = _nt(q, kw) * SCALE + bias_ref[h]
            s_ctx = _nt(q, kc) * SCALE
            (e_loc, e_ctx), den = _softmax_unnorm([s_loc, s_ctx])
            halves.append((_mm(e_loc.astype(BF16), vw) + _mm(e_ctx.astype(BF16), vc)) / den)
        chunks.append(jnp.where(lo, halves[0], halves[1]))
    od = jnp.concatenate(chunks, axis=1)
    o_ref[...] = _rms_gain(od, g_ref[...]).astype(BF16)


def _neighbourhood_attention(qd_p, kd_b, vd_b, cache_k, cache_v, bias, g_mix2d, o_all):
    nq = NA_QR * GRID_W
    steps = DEC_SEQ // nq
    full = pl.BlockSpec((DEC_SEQ, 512), lambda b, j: (b, 0))
    ctx = pl.BlockSpec((None, PAST_LEN, 512), lambda b, j: (b, 0, 0))

    def kind(b, j):
        return (jnp.where(j == 0, 0, jnp.where(j == steps - 1, 2, 1)), 0, 0, 0)

    return pl.pallas_call(
        _na_kernel,
        grid=(DEC_BATCH, steps),
        in_specs=[pl.BlockSpec((nq, D_HEADS * LANES), lambda b, j: (b * steps + j, 0)),
                  full, full, ctx, ctx,
                  pl.BlockSpec((None, D_HEADS, nq, NA_KR * GRID_W), kind),
                  pl.BlockSpec((1, 512), lambda b, j: (0, 3)),
                  pl.BlockSpec(memory_space=pl.ANY)],
        out_specs=pl.BlockSpec((nq, 512), lambda b, j: (N_CTX // nq + b * steps + j, 0)),
        out_shape=jax.ShapeDtypeStruct((N_TOK, 512), BF16),
        input_output_aliases={7: 0},
        compiler_params=_params("parallel", "arbitrary"),
        name="neighbourhood_attention",
    )(qd_p, kd_b, vd_b, cache_k, cache_v, bias, g_mix2d, o_all)


N_SCAN_TILES = N_TOK // T_SCAN
CTX_TILES = N_CTX // T_SCAN
TILES_PER_LAT = DEC_SEQ // T_SCAN


def _scan_tile_flags(i):
    is_ctx = i < CTX_TILES
    pos = (i - CTX_TILES) % TILES_PER_LAT
    return is_ctx | (pos == 0), is_ctx | (pos == TILES_PER_LAT - 1)


def _scan_seq(i):
    return jnp.where(i < CTX_TILES, i, CTX_TILES + (i - CTX_TILES) // TILES_PER_LAT)


def _conv_gates(x_ref, prev_ref, next_ref, cw_ref, cb_ref, wgh_ref, wgl_ref, bg_ref, lam_ref, first, last):
    x = x_ref[...]
    t = T_SCAN
    row = lax.broadcasted_iota(jnp.int32, x.shape, 0)
    pm1 = jnp.where(first, 0.0, prev_ref[7:8, :])
    n0 = jnp.where(last, 0.0, next_ref[0:1, :])
    n1 = jnp.where(last, 0.0, next_ref[1:2, :])
    x_m1 = jnp.where(row == 0, pm1, pltpu.roll(x, 1, 0))
    x_p1 = jnp.where(row == t - 1, n0, pltpu.roll(x, t - 1, 0))
    x_p2 = jnp.where(row == t - 2, n0, jnp.where(row == t - 1, n1, pltpu.roll(x, t - 2, 0)))
    y = (cw_ref[0:1, :] * x_m1 + cw_ref[1:2, :] * x + cw_ref[2:3, :] * x_p1 + cw_ref[3:4, :] * x_p2
         + cb_ref[...])
    y_hi = y.astype(BF16)
    y_lo = (y - y_hi.astype(F32)).astype(BF16)
    zr, zi = [], []
    for c in range(C_WIDTH // LANES):
        sl = slice(c * LANES, (c + 1) * LANES)
        w_hi = wgh_ref[0, sl, :]
        z = _mm(y_hi[:, sl], w_hi) + _mm(y_lo[:, sl], w_hi) + _mm(y_hi[:, sl], wgl_ref[0, sl, :])
        zr.append(z[:, :LANES])
        zi.append(z[:, LANES:])
    bias = bg_ref[0]
    r = jax.nn.sigmoid(jnp.concatenate(zr, axis=1) + bias[:, :C_WIDTH])
    gate_i = jax.nn.sigmoid(jnp.concatenate(zi, axis=1) + bias[:, C_WIDTH:])
    nl = -lam_ref[0]
    softplus = jnp.maximum(nl, 0.0) + jnp.log1p(jnp.exp(-jnp.abs(nl)))
    log_a = -C_POW * r * softplus
    a = jnp.exp(log_a)
    u = jnp.sqrt(1.0 - jnp.exp(2.0 * log_a)) * gate_i * y
    return a, u


def _scan_rows(a_s, u_s, h_s, h, reverse):
    groups = T_SCAN // 8

    def body(i, h):
        g = (groups - 1 - i) if reverse else i
        base = pl.multiple_of(g * 8, 8)
        a8 = a_s[pl.ds(base, 8), :]
        u8 = u_s[pl.ds(base, 8), :]
        rows = [None] * 8
        for k in (range(7, -1, -1) if reverse else range(8)):
            h = a8[k:k + 1, :] * h + u8[k:k + 1, :]
            rows[k] = h
        h_s[pl.ds(base, 8), :] = jnp.concatenate(rows, axis=0)
        return h

    return lax.fori_loop(0, groups, body, h)


def _scan_fwd_kernel(x_ref, prev_ref, next_ref, cw_ref, cb_ref, wgh_ref, wgl_ref, bg_ref, lam_ref, h0_ref,
                     hf_ref, fin_ref, a_s, u_s, carry_s):
    i = pl.program_id(0)
    first, last = _scan_tile_flags(i)
    a, u = _conv_gates(x_ref, prev_ref, next_ref, cw_ref, cb_ref, wgh_ref, wgl_ref, bg_ref, lam_ref, first, last)
    a_s[...] = a
    u_s[...] = u

    @pl.when(first)
    def _():
        carry_s[...] = h0_ref[...]

    h = _scan_rows(a_s, u_s, hf_ref, carry_s[...], reverse=False)
    carry_s[...] = h
    fin_ref[...] = h


def _scan_bwd_kernel(x_ref, prev_ref, next_ref, cw_ref, cb_ref, wgh_ref, wgl_ref, bg_ref, lam_ref, h0_ref,
                     hf_ref, gc_ref, g_ref, oc_ref, fin_ref, a_s, u_s, hb_s, carry_s):
    i = N_SCAN_TILES - 1 - pl.program_id(0)
    first, last = _scan_tile_flags(i)
    a, u = _conv_gates(x_ref, prev_ref, next_ref, cw_ref, cb_ref, wgh_ref, wgl_ref, bg_ref, lam_ref, first, last)
    a_s[...] = a
    u_s[...] = u

    @pl.when(last)
    def _():
        carry_s[...] = h0_ref[...]

    h = _scan_rows(a_s, u_s, hb_s, carry_s[...], reverse=True)
    carry_s[...] = h
    fin_ref[...] = h
    g = gc_ref[...]
    gelu = 0.5 * g * (1.0 + jnp.tanh(math.sqrt(2.0 / math.pi) * (g + 0.044715 * (g * g * g))))
    oc_ref[...] = _rms_gain(gelu * (hf_ref[...] + hb_s[...]), g_ref[...]).astype(BF16)


def _scan_common_specs(tile_of, direction):
    halo = T_SCAN // 8
    last_block = N_TOK // 8 - 1
    return [pl.BlockSpec((T_SCAN, 512), lambda j: (tile_of(j), COL_XC)),
            pl.BlockSpec((8, 512), lambda j: (jnp.maximum(tile_of(j) * halo - 1, 0), COL_XC)),
            pl.BlockSpec((8, 512), lambda j: (jnp.minimum((tile_of(j) + 1) * halo, last_block), COL_XC)),
            pl.BlockSpec((4, C_WIDTH), lambda j: (0, 0)),
            pl.BlockSpec((1, C_WIDTH), lambda j: (0, 0)),
            pl.BlockSpec((1, C_WIDTH, 2 * LANES), lambda j: (direction, 0, 0)),
            pl.BlockSpec((1, C_WIDTH, 2 * LANES), lambda j: (direction, 0, 0)),
            pl.BlockSpec((1, 1, 2 * C_WIDTH), lambda j: (direction, 0, 0)),
            pl.BlockSpec((1, 1, C_WIDTH), lambda j: (direction, 0, 0)),
            pl.BlockSpec((None, 1, C_WIDTH), lambda j: (_scan_seq(tile_of(j)), 0, 0))]


def _rglru(slab, conv_w, conv_b, w_gates, b_gates, lam, h0_f, h0_b, g_mix2d):
    wg_hi = w_gates.astype(BF16)
    wg_lo = (w_gates - wg_hi.astype(F32)).astype(BF16)
    fin = jax.ShapeDtypeStruct((N_SCAN_TILES, 1, C_WIDTH), F32)
    fwd_tile = lambda j: j
    h_f, fin_f = pl.pallas_call(
        _scan_fwd_kernel,
        grid=(N_SCAN_TILES,),
        in_specs=_scan_common_specs(fwd_tile, 0),
        out_specs=[pl.BlockSpec((T_SCAN, C_WIDTH), lambda j: (j, 0)),
                   pl.BlockSpec((None, 1, C_WIDTH), lambda j: (j, 0, 0))],
        out_shape=[jax.ShapeDtypeStruct((N_TOK, C_WIDTH), F32), fin],
        scratch_shapes=[pltpu.VMEM((T_SCAN, C_WIDTH), F32), pltpu.VMEM((T_SCAN, C_WIDTH), F32),
                        pltpu.VMEM((1, C_WIDTH), F32)],
        compiler_params=_params("arbitrary"),
        name="rglru_forward",
    )(slab, slab, slab, conv_w, conv_b, wg_hi, wg_lo, b_gates, lam, h0_f)
    bwd_tile = lambda j: N_SCAN_TILES - 1 - j
    oc, fin_b = pl.pallas_call(
        _scan_bwd_kernel,
        grid=(N_SCAN_TILES,),
        in_specs=_scan_common_specs(bwd_tile, 1) + [
            pl.BlockSpec((T_SCAN, C_WIDTH), lambda j: (bwd_tile(j), 0)),
            pl.BlockSpec((T_SCAN, 512), lambda j: (bwd_tile(j), COL_GC)),
            pl.BlockSpec((1, 512), lambda j: (0, 2))],
        out_specs=[pl.BlockSpec((T_SCAN, C_WIDTH), lambda j: (bwd_tile(j), 0)),
                   pl.BlockSpec((None, 1, C_WIDTH), lambda j: (bwd_tile(j), 0, 0))],
        out_shape=[jax.ShapeDtypeStruct((N_TOK, C_WIDTH), BF16), fin],
        scratch_shapes=[pltpu.VMEM((T_SCAN, C_WIDTH), F32), pltpu.VMEM((T_SCAN, C_WIDTH), F32),
                        pltpu.VMEM((T_SCAN, C_WIDTH), F32), pltpu.VMEM((1, C_WIDTH), F32)],
        compiler_params=_params("arbitrary"),
        name="rglru_backward",
    )(slab, slab, slab, conv_w, conv_b, wg_hi, wg_lo, b_gates, lam, h0_b, h_f, slab, g_mix2d)
    return oc, fin_f, fin_b


def _mixout_kernel(oa_ref, ob_ref, oc_ref, od_ref, w_ref, xc_ref, xl_ref, g1_ref, sc2_ref, sh2_ref, lng_ref,
                   lnb_ref, wrh_ref, wrl_ref, br_ref, x1_ref, h2_ref, idx_ref, wt_ref, rank_ref, cnt_ref, cnt_s):
    @pl.when(pl.program_id(0) == 0)
    def _():
        cnt_s[...] = jnp.zeros(cnt_s.shape, F32)

    parts = 2
    pm = xc_ref.shape[0] // parts
    is_ctx = _is_ctx_tile(xc_ref.shape[0])
    for part in range(parts):
        rows = pl.ds(part * pm, pm)
        x = jnp.where(is_ctx, xc_ref[rows, :], xl_ref[rows, :])
        _mixout_rows(rows, part * pm, pm, oa_ref, ob_ref, oc_ref, od_ref, w_ref, x, g1_ref,
                     sc2_ref, sh2_ref, lng_ref, lnb_ref, wrh_ref, wrl_ref, br_ref, x1_ref, h2_ref, idx_ref,
                     wt_ref, rank_ref, cnt_s)
    cnt_ref[...] = cnt_s[...]


def _mixout_rows(rows, first, tm, oa_ref, ob_ref, oc_ref, od_ref, w_ref, x, g1_ref, sc2_ref, sh2_ref,
                 lng_ref, lnb_ref, wrh_ref, wrl_ref, br_ref, x1_ref, h2_ref, idx_ref, wt_ref, rank_ref, cnt_s):
    y = _mm(oa_ref[rows, :], w_ref[0:512, :])
    y += _mm(ob_ref[rows, :], w_ref[512:1024, :])
    y += _mm(oc_ref[rows, :], w_ref[1024:1536, :])
    y += _mm(od_ref[rows, :], w_ref[1536:2048, :])
    x1 = _layernorm(ALPHA * x + g1_ref[...] * y, lng_ref[...], lnb_ref[...])
    x1_ref[rows, :] = x1
    h2 = x1 * (1.0 + sc2_ref[...]) + sh2_ref[...]
    _store_token_tiles(h2_ref, (), _pack_rows(h2), first)

    h_hi = h2.astype(BF16)
    h_lo = (h2 - h_hi.astype(F32)).astype(BF16)
    scores = jax.nn.sigmoid(_mm(h_hi, wrh_ref[...]) + _mm(h_lo, wrh_ref[...]) + _mm(h_hi, wrl_ref[...]))
    sel = scores + br_ref[...]
    lane = lax.broadcasted_iota(jnp.int32, sel.shape, 1).astype(F32)
    slot = lax.broadcasted_iota(jnp.int32, (tm, 8), 1)
    wide = lax.broadcasted_iota(jnp.int32, (tm, LANES), 1)
    idx_out = jnp.zeros((tm, LANES), F32)
    wt_out = jnp.zeros((tm, 8), F32)
    total = jnp.zeros((sel.shape[0], 1), F32)
    hits = []
    for k in range(TOP_K):
        m = sel.max(axis=-1, keepdims=True)
        idx = jnp.where(sel == m, lane, float(N_EXPERTS)).min(axis=-1, keepdims=True)
        hit = lane == idx
        hits.append(hit)
        w = jnp.where(hit, scores, 0.0).sum(axis=-1, keepdims=True)
        total = total + w
        idx_out = jnp.where(wide == k, idx, idx_out)
        wt_out = jnp.where(slot == k, w, wt_out)
        sel = jnp.where(hit, -jnp.inf, sel)
    cols = pl.ds(first, tm)
    idx_ref[:, cols] = idx_out.T[:8, :].astype(jnp.int32)
    wt_ref[rows, :] = wt_out / total * ROUTED_SCALE

    chosen = jnp.zeros(sel.shape, F32)
    for hit in hits:
        chosen = chosen + hit.astype(F32)
    earlier = (lax.broadcasted_iota(jnp.int32, (tm, tm), 1) < lax.broadcasted_iota(jnp.int32, (tm, tm), 0))
    before = _mm(earlier.astype(F32).astype(BF16), chosen.astype(BF16)) + cnt_s[...]
    rank_out = jnp.zeros((tm, LANES), F32)
    for k, hit in enumerate(hits):
        rank_out = jnp.where(wide == k, jnp.where(hit, before, 0.0).sum(axis=-1, keepdims=True), rank_out)
    rank_ref[:, cols] = rank_out.T[:8, :].astype(jnp.int32)
    cnt_s[...] = cnt_s[...] + chosen.sum(axis=0, keepdims=True)


def _mixer_out(oa, ob, oc, od, w_out_b, x_ctx, x_lat, mod, ln_g, ln_b, w_router, b_router):
    tm = TM_OUT
    cat = pl.BlockSpec((tm, 512), lambda i: (i, 0))
    row = pl.BlockSpec((tm, D_MODEL), lambda i: (i, 0))
    vec = pl.BlockSpec((1, D_MODEL), lambda i: (0, 0))
    k8 = pl.BlockSpec((tm, 8), lambda i: (i, 0))
    k_major = pl.BlockSpec((8, tm), lambda i: (0, i))
    per_expert = pl.BlockSpec((1, N_EXPERTS), lambda i: (0, 0))
    router = pl.BlockSpec((D_MODEL, N_EXPERTS), lambda i: (0, 0))
    wr_hi = w_router.astype(BF16)
    wr_lo = (w_router - wr_hi.astype(F32)).astype(BF16)
    return pl.pallas_call(
        _mixout_kernel,
        grid=(N_TOK // tm,),
        in_specs=[cat, cat, cat, cat,
                  pl.BlockSpec((D_MODEL, D_MODEL), lambda i: (0, 0)),
                  _ctx_rows(tm), _lat_rows(tm), _mod_spec(2, tm), _mod_spec(4, tm), _mod_spec(3, tm), vec, vec,
                  router, router, per_expert],
        out_specs=[row, pl.BlockSpec((tm * SUBLANES, LANES), lambda i: (i, 0)), k_major, k8, k_major, per_expert],
        out_shape=[jax.ShapeDtypeStruct((N_TOK, D_MODEL), F32),
                   jax.ShapeDtypeStruct((N_TOK * SUBLANES, LANES), jnp.uint32),
                   jax.ShapeDtypeStruct((8, N_TOK), jnp.int32), jax.ShapeDtypeStruct((N_TOK, 8), F32),
                   jax.ShapeDtypeStruct((8, N_TOK), jnp.int32), jax.ShapeDtypeStruct((1, N_EXPERTS), F32)],
        scratch_shapes=[pltpu.VMEM((1, N_EXPERTS), F32)],
        compiler_params=_params("arbitrary"),
        name="mixer_out_router",
    )(oa, ob, oc, od, w_out_b, x_ctx, x_lat, mod, mod, mod, ln_g, ln_b, wr_hi, wr_lo, b_router)


ROW_DMA_QUEUES = 2
MOE_BLOCKS = N_TOK * TOP_K // MOE_BM + N_EXPERTS
MOE_ROWS = MOE_BLOCKS * MOE_BM


def _row_copy(src, src_row, dst, dst_row, sem):
    def tile(row):
        return pl.ds(row * SUBLANES if isinstance(row, int) else pl.multiple_of(row * SUBLANES, SUBLANES), SUBLANES)

    return pltpu.make_async_copy(src.at[tile(src_row)], dst.at[tile(dst_row)], sem)


def _dispatch_kernel(slot_ref, h_ref, xg_ref, sem):
    tm = h_ref.shape[0] // SUBLANES
    base = pl.program_id(0) * tm

    for r in range(tm):
        for k in range(TOP_K):
            _row_copy(h_ref, r, xg_ref, slot_ref[base + (k * N_TOK + r)], sem).start(priority=k % ROW_DMA_QUEUES)
    for k in range(TOP_K):
        pltpu.make_async_copy(h_ref, xg_ref.at[pl.ds(0, tm * SUBLANES)], sem).wait()


def _dispatch_rows(slot_flat, h2):
    tm = TM_DISPATCH
    grid_spec = pltpu.PrefetchScalarGridSpec(
        num_scalar_prefetch=1,
        grid=(N_TOK // tm,),
        in_specs=[pl.BlockSpec((tm * SUBLANES, LANES), lambda i, s: (i, 0))],
        out_specs=pl.BlockSpec(memory_space=pl.ANY),
        scratch_shapes=[pltpu.SemaphoreType.DMA])
    return pl.pallas_call(
        _dispatch_kernel,
        grid_spec=grid_spec,
        out_shape=jax.ShapeDtypeStruct((MOE_ROWS * SUBLANES, LANES), jnp.uint32),
        compiler_params=_params("arbitrary"),
        name="moe_dispatch",
    )(slot_flat, h2)


BLOCK_TILE_ROWS = MOE_BM * SUBLANES
EXPERT_X_BUFFERS = 3


def _expert_kernel(blk0_ref, nblk_ref, cnt_ref, wg_ref, wu_ref, wd_ref, x_hbm, y_hbm,
                   wg_s, wu_s, wd_s, xbuf, ybuf, xsem, ysem):
    e = pl.program_id(0)
    first_blk = blk0_ref[e]
    n_blk = nblk_ref[e]
    count = cnt_ref[e]
    total = blk0_ref[N_EXPERTS - 1] + nblk_ref[N_EXPERTS - 1]

    def rows_of(g):
        return pl.ds(pl.multiple_of(g * BLOCK_TILE_ROWS, BLOCK_TILE_ROWS), BLOCK_TILE_ROWS)

    def x_copy(g, buf):
        return pltpu.make_async_copy(x_hbm.at[rows_of(g)], xbuf.at[buf], xsem.at[buf])

    def y_copy(g, buf):
        return pltpu.make_async_copy(ybuf.at[buf], y_hbm.at[rows_of(g)], ysem.at[buf])

    for ahead in range(EXPERT_X_BUFFERS - 1):
        @pl.when((e == 0) & (total > ahead))
        def _():
            x_copy(ahead, ahead).start()

    wg_s[...] = wg_ref[...].astype(BF16)
    wu_s[...] = wu_ref[...].astype(BF16)
    wd_s[...] = wd_ref[...].astype(BF16)

    def block(j, carry):
        g = first_blk + j
        buf = g % 2
        xb = lax.rem(g, EXPERT_X_BUFFERS)
        nxt = g + (EXPERT_X_BUFFERS - 1)

        @pl.when(nxt < total)
        def _():
            x_copy(nxt, lax.rem(nxt, EXPERT_X_BUFFERS)).start()

        x_copy(g, xb).wait()

        @pl.when(g >= 2)
        def _():
            y_copy(g - 2, buf).wait()

        n_rows = count - j * MOE_BM

        def swiglu(m):
            x = _load_token_tiles(xbuf, (xb,), m)
            row = lax.broadcasted_iota(jnp.int32, x.shape, 0)
            lo, hi = _unpack_rows(jnp.where(row < n_rows, x, jnp.uint32(0)))
            lo, hi = lo.astype(BF16), hi.astype(BF16)
            gate = _mm(lo, wg_s[:HALF_D, :]) + _mm(hi, wg_s[HALF_D:, :])
            up = _mm(lo, wu_s[:HALF_D, :]) + _mm(hi, wu_s[HALF_D:, :])
            _store_token_tiles(ybuf, (buf,), _pack_rows(_mm((_silu(gate) * up).astype(BF16), wd_s[...])))

        @pl.when(n_rows > MOE_BM // 2)
        def _():
            swiglu(MOE_BM)

        @pl.when(n_rows <= MOE_BM // 2)
        def _():
            swiglu(MOE_BM // 2)
            _store_token_tiles(ybuf, (buf,), jnp.zeros((MOE_BM // 2, HALF_D), jnp.uint32), MOE_BM // 2)

        y_copy(g, buf).start()
        return carry

    lax.fori_loop(0, n_blk, block, 0)

    @pl.when((e == N_EXPERTS - 1) & (total >= 2))
    def _():
        y_copy(total - 2, total % 2).wait()

    @pl.when((e == N_EXPERTS - 1) & (total >= 1))
    def _():
        y_copy(total - 1, (total - 1) % 2).wait()


def _routed_experts(xg, first_blk, n_blk, counts, layer, w_g, w_u, w_d):
    def weight(shape):
        return pl.BlockSpec((None, None) + shape, lambda e, *_: (layer, e, 0, 0))

    any_space = pl.BlockSpec(memory_space=pl.ANY)
    x_bufs = pltpu.VMEM((EXPERT_X_BUFFERS, BLOCK_TILE_ROWS, LANES), jnp.uint32)
    y_bufs = pltpu.VMEM((2, BLOCK_TILE_ROWS, LANES), jnp.uint32)
    grid_spec = pltpu.PrefetchScalarGridSpec(
        num_scalar_prefetch=3,
        grid=(N_EXPERTS,),
        in_specs=[weight((D_MODEL, D_EXPERT)), weight((D_MODEL, D_EXPERT)), weight((D_EXPERT, D_MODEL)), any_space],
        out_specs=any_space,
        scratch_shapes=[pltpu.VMEM((D_MODEL, D_EXPERT), BF16), pltpu.VMEM((D_MODEL, D_EXPERT), BF16),
                        pltpu.VMEM((D_EXPERT, D_MODEL), BF16), x_bufs, y_bufs,
                        pltpu.SemaphoreType.DMA((EXPERT_X_BUFFERS,)), pltpu.SemaphoreType.DMA((2,))])
    return pl.pallas_call(
        _expert_kernel,
        grid_spec=grid_spec,
        out_shape=jax.ShapeDtypeStruct((MOE_ROWS * SUBLANES, LANES), jnp.uint32),
        compiler_params=_params("arbitrary"),
        name="routed_experts",
    )(first_blk, n_blk, counts, w_g, w_u, w_d, xg)


def _ffn_out_kernel(slot_ref, h_ref, wt_ref, x1_ref, wg_ref, wu_ref, wd_ref, g2_ref, lng_ref, lnb_ref, y_ref,
                    oc_ref, ol_ref, ybuf_a, ybuf_b, sems):
    tm = h_ref.shape[0] // SUBLANES
    i = pl.program_id(0)
    n = pl.num_programs(0)

    def gather(tile, dst, sem):
        base = tile * tm
        for r in range(tm):
            for k in range(TOP_K):
                _row_copy(y_ref, slot_ref[base + (k * N_TOK + r)], dst.at[k], r,
                          sem).start(priority=k % ROW_DMA_QUEUES)

    def wait(dst, sem):
        for k in range(TOP_K):
            pltpu.make_async_copy(y_ref.at[pl.ds(0, tm * SUBLANES)], dst.at[k], sem).wait()

    @pl.when(i == 0)
    def _():
        gather(0, ybuf_a, sems.at[0])

    def step(cur, cur_sem, nxt, nxt_sem):
        gather(jnp.minimum(i + 1, n - 1), nxt, nxt_sem)
        lo, hi = _unpack_rows(_load_token_tiles(h_ref, (), tm))
        lo, hi = lo.astype(BF16), hi.astype(BF16)
        gate = _mm(lo, wg_ref[:HALF_D, :]) + _mm(hi, wg_ref[HALF_D:, :])
        up = _mm(lo, wu_ref[:HALF_D, :]) + _mm(hi, wu_ref[HALF_D:, :])
        y = _mm((_silu(gate) * up).astype(BF16), wd_ref[...])
        wait(cur, cur_sem)
        wt = wt_ref[...]
        y_lo, y_hi = y[:, :HALF_D], y[:, HALF_D:]
        for k in range(TOP_K):
            r_lo, r_hi = _unpack_rows(_load_token_tiles(cur, (k,), tm))
            y_lo = y_lo + r_lo * wt[:, k:k + 1]
            y_hi = y_hi + r_hi * wt[:, k:k + 1]
        y = jnp.concatenate([y_lo, y_hi], axis=1)
        out = _layernorm(ALPHA * x1_ref[...] + g2_ref[...] * y, lng_ref[...], lnb_ref[...])
        is_ctx = _is_ctx_tile(tm)

        @pl.when(is_ctx)
        def _():
            oc_ref[...] = out

        @pl.when(jnp.logical_not(is_ctx))
        def _():
            ol_ref[...] = out

        @pl.when(i == n - 1)
        def _():
            wait(nxt, nxt_sem)

    @pl.when(i % 2 == 0)
    def _():
        step(ybuf_a, sems.at[0], ybuf_b, sems.at[1])

    @pl.when(i % 2 == 1)
    def _():
        step(ybuf_b, sems.at[1], ybuf_a, sems.at[0])


def _ffn_out(slot_flat, h2, wt, x1, ws_g, ws_u, ws_d, mod, ln_g, ln_b, y_blk):
    tm = TM_FFN
    row = pl.BlockSpec((tm, D_MODEL), lambda i, s: (i, 0))
    vec = pl.BlockSpec((1, D_MODEL), lambda i, s: (0, 0))
    grid_spec = pltpu.PrefetchScalarGridSpec(
        num_scalar_prefetch=1,
        grid=(N_TOK // tm,),
        in_specs=[pl.BlockSpec((tm * SUBLANES, LANES), lambda i, s: (i, 0)),
                  pl.BlockSpec((tm, 8), lambda i, s: (i, 0)), row,
                  pl.BlockSpec((D_MODEL, D_EXPERT), lambda i, s: (0, 0)),
                  pl.BlockSpec((D_MODEL, D_EXPERT), lambda i, s: (0, 0)),
                  pl.BlockSpec((D_EXPERT, D_MODEL), lambda i, s: (0, 0)),
                  _mod_spec(5, tm), vec, vec,
                  pl.BlockSpec(memory_space=pl.ANY)],
        out_specs=[_ctx_rows(tm), _lat_rows(tm)],
        scratch_shapes=[pltpu.VMEM((TOP_K, tm * SUBLANES, LANES), jnp.uint32),
                        pltpu.VMEM((TOP_K, tm * SUBLANES, LANES), jnp.uint32),
                        pltpu.SemaphoreType.DMA((2,))])
    return pl.pallas_call(
        _ffn_out_kernel,
        grid_spec=grid_spec,
        out_shape=[jax.ShapeDtypeStruct((N_CTX, D_MODEL), F32), jax.ShapeDtypeStruct((N_LAT, D_MODEL), F32)],
        compiler_params=_params("arbitrary"),
        name="shared_expert_out",
    )(slot_flat, h2, wt, x1, ws_g, ws_u, ws_d, mod, ln_g, ln_b, y_blk)


def _dispatch_plan(idx, rank, counts):
    bm = MOE_BM
    counts = counts.reshape(N_EXPERTS).astype(jnp.int32)
    padded = (counts + bm - 1) // bm * bm
    pad_end = jnp.cumsum(padded)
    pad_start = pad_end - padded
    experts = jnp.arange(N_EXPERTS, dtype=jnp.int32)[:, None, None]
    first_row = jnp.sum(jnp.where(idx[None, :TOP_K] == experts, pad_start[:, None, None], 0), axis=0)
    slot = first_row + rank[:TOP_K]
    return slot.reshape(-1).astype(jnp.int32), pad_start // bm, padded // bm, counts


def _rope_tables():
    t = jnp.arange(DEC_SEQ)
    row = (t // GRID_W).astype(F32)
    col = (t % GRID_W).astype(F32)
    nf = HEAD_DIM // 4
    inv = ROPE_BASE ** (-jnp.arange(nf, dtype=F32) / nf)
    ar = row[:, None] * inv[None, :]
    ac = col[:, None] * inv[None, :]
    ang = jnp.concatenate([ar, ar, ac, ac], axis=-1)
    ang = jnp.concatenate([ang, ang], axis=-1)
    cos, sin = jnp.cos(ang), jnp.sin(ang)
    first = (jnp.arange(LANES) % (2 * nf)) < nf
    return cos, jnp.where(first, -sin, 0.0), jnp.where(first, 0.0, sin)


def _lambda_init(l):
    return 0.8 - 0.6 * math.exp(-0.3 * l)


def _gate_pairs(w_r, w_i):
    per = LANES // C_BW
    eye = jnp.eye(per, dtype=w_r.dtype)

    def chunk_diag(w):
        w = w.reshape(C_BLOCKS // per, per, C_BW, C_BW)
        return (eye[None, :, None, :, None] * w[:, :, :, None, :]).reshape(C_BLOCKS // per, LANES, LANES)

    return jnp.concatenate([chunk_diag(w_r), chunk_diag(w_i)], axis=2).reshape(C_WIDTH, 2 * LANES)


def kernel(x_prompt, x_sample, c, cache_a_k, cache_a_v, cache_b_k, cache_b_v, state_c, cache_d_k, cache_d_v, c_ctx, w_mod, b_mod, w_in, a_sink, b_lambda, c_conv_w, c_conv_b, c_w_rgate, c_b_rgate, c_w_igate, c_b_igate, c_lambda, d_rpb, g_mix, w_out, ln_g, ln_b, w_router, b_router, w_e_gate, w_e_up, w_e_down, w_s_gate, w_s_up, w_s_down):
    x_ctx, x_lat = x_prompt.reshape(N_CTX, D_MODEL), x_sample.reshape(N_LAT, D_MODEL)
    cond8 = jnp.zeros((8, D_MODEL), F32).at[0].set(c_ctx).at[1:1 + DEC_BATCH].set(c)
    mod_all = _modulation(cond8, w_mod, b_mod)
    cos, sin_a, sin_b = _rope_tables()
    na_row_idx, na_col_sel, na_col_ok = _na_bias_indices()

    order = ((0, 512), (768, IN_WIDTH), (512, 768))

    ctx_out = [[] for _ in range(7)]
    for l in range(DEPTH):
        lam_init = _lambda_init(l)
        mod = mod_all[l, :N_GROUPS].reshape(N_GROUPS, 6, 1, D_MODEL)
        w_l = w_in[l]
        w_in_b = jnp.concatenate([w_l[:, a:b] for a, b in order], axis=1).astype(BF16)
        slab = _in_projection(x_ctx, x_lat, mod, w_in_b)

        lp = b_lambda[l]
        lam = jnp.exp(jnp.sum(lp[0] * lp[1])) - jnp.exp(jnp.sum(lp[2] * lp[3])) + lam_init
        scal = jnp.zeros((16,), F32).at[:A_HEADS].set(a_sink[l]).at[A_HEADS].set(lam)
        g2d = g_mix[l].reshape(1, -1)

        oa, ob, od = _ctx_attention(slab, scal, g2d, lam_init)

        qa_p, ka_r, va_b, qb_p, kb_r, vb_b, qd_p, kd_b, vd_b = _latent_prep(slab, cos, sin_a, sin_b)
        oa = _window_attention(qa_p, ka_r, va_b, cache_a_k[:, l].reshape(DEC_BATCH, PAST_LEN, LANES),
                               cache_a_v[:, l].reshape(DEC_BATCH, PAST_LEN, LANES), scal, g2d, oa)
        ob = _diff_attention(qb_p, kb_r, vb_b, cache_b_k[:, l].reshape(DEC_BATCH, PAST_LEN, 512),
                             cache_b_v[:, l].reshape(DEC_BATCH, PAST_LEN, 512), scal, g2d, lam_init, ob)
        bias = _na_bias(d_rpb[l], na_row_idx, na_col_sel, na_col_ok)
        od = _neighbourhood_attention(qd_p, kd_b, vd_b, cache_d_k[:, l].reshape(DEC_BATCH, PAST_LEN, 512),
                                      cache_d_v[:, l].reshape(DEC_BATCH, PAST_LEN, 512), bias, g2d, od)

        w_gates = jnp.stack([_gate_pairs(c_w_rgate[l, d], c_w_igate[l, d]) for d in range(2)])
        b_gates = jnp.concatenate([c_b_rgate[l], c_b_igate[l]], axis=-1).reshape(2, 1, 2 * C_WIDTH)
        zeros = jnp.zeros((BATCH, 1, C_WIDTH), F32)
        h0_f = jnp.concatenate([zeros, state_c[:, l, 0][:, None, :]], axis=0)
        h0_b = jnp.concatenate([zeros, state_c[:, l, 1][:, None, :]], axis=0)
        oc, fin_f, fin_b = _rglru(slab, c_conv_w[l], c_conv_b[l].reshape(1, -1), w_gates, b_gates,
                                  c_lambda[l].reshape(2, 1, C_WIDTH), h0_f, h0_b, g2d)

        x1, h2, idx, wt, rank, counts = _mixer_out(oa, ob, oc, od, w_out[l].astype(BF16), x_ctx, x_lat, mod,
                                                   ln_g[l, 0].reshape(1, -1), ln_b[l, 0].reshape(1, -1),
                                                   w_router[l], b_router[l].reshape(1, -1))

        slot, first_blk, n_blk, counts = _dispatch_plan(idx, rank, counts)
        xg = _dispatch_rows(slot, h2)
        y_blk = _routed_experts(xg, first_blk, n_blk, counts, l, w_e_gate, w_e_up, w_e_down)
        x_ctx, x_lat = _ffn_out(slot, h2, wt, x1, w_s_gate[l].astype(BF16), w_s_up[l].astype(BF16),
                                w_s_down[l].astype(BF16), mod, ln_g[l, 1].reshape(1, -1),
                                ln_b[l, 1].reshape(1, -1), y_blk)

        ctx = slab[:N_CTX]
        ctx_out[0].append(ctx[:, COL_KA * LANES:(COL_KA + 1) * LANES].reshape(BATCH, SEQ, A_KV_HEADS, HEAD_DIM))
        ctx_out[1].append(ctx[:, COL_VA * LANES:(COL_VA + 1) * LANES].reshape(BATCH, SEQ, A_KV_HEADS, HEAD_DIM))
        ctx_out[2].append(ctx[:, COL_KB * 512:(COL_KB + 1) * 512].reshape(BATCH, SEQ, B_HEADS, 2, HEAD_DIM))
        ctx_out[3].append(ctx[:, COL_VB * 512:(COL_VB + 1) * 512].reshape(BATCH, SEQ, B_HEADS, 2 * HEAD_DIM))
        ctx_out[4].append(jnp.concatenate([fin_f[:BATCH], fin_b[:BATCH]], axis=1))
        ctx_out[5].append(ctx[:, COL_KD * 512:(COL_KD + 1) * 512].reshape(BATCH, SEQ, D_HEADS, HEAD_DIM))
        ctx_out[6].append(ctx[:, COL_VD * 512:(COL_VD + 1) * 512].reshape(BATCH, SEQ, D_HEADS, HEAD_DIM))

    new = [jnp.stack(t, axis=1) for t in ctx_out]
    return (x_ctx.reshape(BATCH, SEQ, D_MODEL), x_lat.reshape(DEC_BATCH, DEC_SEQ, D_MODEL), *new)
```

```python
import functools
import math

import numpy as np
import jax
import jax.numpy as jnp
from jax import lax
from jax.experimental import pallas as pl
from jax.experimental.pallas import tpu as pltpu

F32 = jnp.float32
BF16 = jnp.bfloat16

D_MODEL = 2048
BATCH = 16
SEQ = 256
DEPTH = 2
DEC_BATCH = 2
DEC_SEQ = 4096
PAST_LEN = 256
GRID_W = 64
HEAD_DIM = 64
ROPE_BASE = 10000.0
A_HEADS = 8
A_KV_HEADS = 2
A_WINDOW = 128
B_HEADS = 4
C_WIDTH = 512
C_BLOCKS = 8
C_BW = C_WIDTH // C_BLOCKS
C_POW = 8.0
D_HEADS = 8
NA_ROWS = 8
NA_COLS = 16
N_EXPERTS = 64
TOP_K = 6
D_EXPERT = 512
ROUTED_SCALE = 2.5
ALPHA = (2.0 * DEPTH) ** 0.25
LN_EPS = 1e-5
NORM_EPS = 1e-6
NEG_INF = -1e30
SCALE = HEAD_DIM ** -0.5
LOG2E = math.log2(math.e)

N_CTX = BATCH * SEQ
N_LAT = DEC_BATCH * DEC_SEQ
N_TOK = N_CTX + N_LAT
GROUP_ROWS = 4096
N_GROUPS = N_TOK // GROUP_ROWS
IN_WIDTH = 4864
LANES = 128
VMEM_LIMIT = 56 * 1024 * 1024

COL_QA, COL_QB, COL_KB, COL_VB, COL_XC, COL_GC, COL_QD, COL_KD, COL_VD = range(9)
COL_KA, COL_VA = 36, 37

TM_PROJ = 512
TN_PROJ = IN_WIDTH // 2
TM_PREP = 256
TQ_A = A_WINDOW
TQ_B = 256
NA_QR = 4
NA_KR = 12
T_SCAN = 256
TM_OUT = 256
MOE_BM = 256
TM_DISPATCH = 256
TM_FFN = 256


def _params(*sem):
    return pltpu.CompilerParams(dimension_semantics=sem, vmem_limit_bytes=VMEM_LIMIT)


def _nt(a, b):
    return lax.dot_general(a, b, (((1,), (1,)), ((), ())), preferred_element_type=F32)


def _mm(a, b):
    return jnp.dot(a, b, preferred_element_type=F32)


def _softmax_unnorm(parts, sink=None):
    m = parts[0].max(axis=-1, keepdims=True)
    for p in parts[1:]:
        m = jnp.maximum(m, p.max(axis=-1, keepdims=True))
    if sink is not None:
        m = jnp.maximum(m, sink)
    es = [jnp.exp(p - m) for p in parts]
    den = es[0].sum(axis=-1, keepdims=True)
    for e in es[1:]:
        den = den + e.sum(axis=-1, keepdims=True)
    if sink is not None:
        den = den + jnp.exp(sink - m)
    return es, den


def _rms_gain(x, g):
    return x * lax.rsqrt(jnp.mean(x * x, axis=-1, keepdims=True) + NORM_EPS) * g


def _layernorm(z, g, b):
    zc = z - jnp.mean(z, axis=-1, keepdims=True)
    var = jnp.mean(zc * zc, axis=-1, keepdims=True)
    return zc * lax.rsqrt(var + LN_EPS) * g + b


def _silu(x):
    return x * jax.nn.sigmoid(x)


def _low_half(shape):
    return lax.broadcasted_iota(jnp.int32, shape, 1) < HEAD_DIM


HALF_D = D_MODEL // 2
HIGH16 = 0xFFFF0000


def _pack_rows(x):
    bits = pltpu.bitcast(x.astype(BF16).astype(F32), jnp.uint32)
    return (bits[:, :HALF_D] >> 16) | (bits[:, HALF_D:] & jnp.uint32(HIGH16))


def _unpack_rows(p):
    return (pltpu.bitcast(p << 16, F32), pltpu.bitcast(p & jnp.uint32(HIGH16), F32))


SUBLANES = 8
assert HALF_D == SUBLANES * LANES


def _store_token_tiles(ref, index, packed, first=0):
    m = packed.shape[0]
    for s in range(SUBLANES):
        rows = pl.ds(first * SUBLANES + s, m, stride=SUBLANES)
        ref[index + (rows, slice(None))] = packed[:, s * LANES:(s + 1) * LANES]


def _load_token_tiles(ref, index, m):
    return jnp.concatenate([ref[index + (pl.ds(s, m, stride=SUBLANES), slice(None))] for s in range(SUBLANES)],
                           axis=1)


def _mod_kernel(c_ref, w_ref, b_ref, o_ref):
    c = c_ref[...]
    o_ref[0] = jnp.dot(_silu(c), w_ref[0], precision=lax.Precision.HIGHEST,
                       preferred_element_type=F32) + b_ref[0]


def _modulation(cond8, w_mod, b_mod):
    tn = 1024
    n = w_mod.shape[-1]
    return pl.pallas_call(
        _mod_kernel,
        grid=(DEPTH, n // tn),
        in_specs=[pl.BlockSpec((8, D_MODEL), lambda l, j: (0, 0)),
                  pl.BlockSpec((1, D_MODEL, tn), lambda l, j: (l, 0, j)),
                  pl.BlockSpec((1, 1, tn), lambda l, j: (l, 0, j))],
        out_specs=pl.BlockSpec((1, 8, tn), lambda l, j: (l, 0, j)),
        out_shape=jax.ShapeDtypeStruct((DEPTH, 8, n), F32),
        compiler_params=_params("parallel", "parallel"),
        name="modulation",
    )(cond8, w_mod, b_mod.reshape(DEPTH, 1, n))


def _mod_spec(which, tm):
    return pl.BlockSpec((None, None, 1, D_MODEL), lambda i, *_: (i * tm // GROUP_ROWS, which, 0, 0))


def _ctx_rows(tm):
    return pl.BlockSpec((tm, D_MODEL), lambda i, *_: (jnp.minimum(i, N_CTX // tm - 1), 0))


def _lat_rows(tm):
    return pl.BlockSpec((tm, D_MODEL), lambda i, *_: (jnp.maximum(i - N_CTX // tm, 0), 0))


def _is_ctx_tile(tm):
    return pl.program_id(0) * tm < N_CTX


def _inproj_kernel(xc_ref, xl_ref, sc_ref, sh_ref, w_ref, o_ref, xb_ref):
    @pl.when(pl.program_id(1) == 0)
    def _():
        x = jnp.where(_is_ctx_tile(xc_ref.shape[0]), xc_ref[...], xl_ref[...])
        xb_ref[...] = (x * (1.0 + sc_ref[...]) + sh_ref[...]).astype(BF16)

    o_ref[...] = _mm(xb_ref[...], w_ref[...])


def _in_projection(x_ctx, x_lat, mod, w_in_b):
    tm, tn = TM_PROJ, TN_PROJ
    return pl.pallas_call(
        _inproj_kernel,
        grid=(N_TOK // tm, IN_WIDTH // tn),
        in_specs=[_ctx_rows(tm), _lat_rows(tm),
                  _mod_spec(1, tm), _mod_spec(0, tm),
                  pl.BlockSpec((D_MODEL, tn), lambda i, j: (0, j))],
        out_specs=pl.BlockSpec((tm, tn), lambda i, j: (i, j)),
        out_shape=jax.ShapeDtypeStruct((N_TOK, IN_WIDTH), F32),
        scratch_shapes=[pltpu.VMEM((tm, D_MODEL), BF16)],
        compiler_params=_params("parallel", "arbitrary"),
        name="in_projection",
    )(x_ctx, x_lat, mod, mod, w_in_b)


def _ctx_attn_kernel(scal_ref, qa_ref, ka_ref, va_ref, qb_ref, kb_ref, vb_ref, qd_ref, kd_ref, vd_ref,
                     ga_ref, gb_ref, gd_ref, oa_ref, ob_ref, od_ref, *, lam_init):
    L = SEQ
    lo = _low_half((L, LANES))

    ka = ka_ref[...].astype(BF16)
    va = va_ref[...].astype(BF16)
    qa = qa_ref[...]
    chunks = []
    for c in range(A_HEADS // 2):
        hk = c // 2
        chunk = qa[:, c * LANES:(c + 1) * LANES]
        keep = lo if hk == 0 else jnp.logical_not(lo)
        halves = []
        for half in range(2):
            x = chunk if half == hk else pltpu.roll(chunk, HEAD_DIM, 1)
            qh = jnp.where(keep, x, 0.0).astype(BF16)
            s = _nt(qh, ka) * SCALE
            (e,), den = _softmax_unnorm([s], scal_ref[2 * c + half])
            o = _mm(e.astype(BF16), va) / den
            halves.append(o if half == hk else pltpu.roll(o, HEAD_DIM, 1))
        chunks.append(jnp.where(lo, halves[0], halves[1]))
    oa = jnp.concatenate(chunks, axis=1)
    oa_ref[...] = _rms_gain(oa, ga_ref[...]).astype(BF16)

    lam = scal_ref[A_HEADS]
    qb = qb_ref[...]
    chunks = []
    for h in range(B_HEADS):
        sl = slice(h * LANES, (h + 1) * LANES)
        qc = qb[:, sl]
        kc = kb_ref[:, sl].astype(BF16)
        vh = vb_ref[:, sl].astype(BF16)
        (e1,), d1 = _softmax_unnorm([_nt(jnp.where(lo, qc, 0.0).astype(BF16), kc) * SCALE])
        (e2,), d2 = _softmax_unnorm([_nt(jnp.where(lo, 0.0, qc).astype(BF16), kc) * SCALE])
        w = e1 * (1.0 / d1) - e2 * (lam / d2)
        o = _mm(w.astype(BF16), vh)
        chunks.append(_rms_gain(o, gb_ref[:, sl]) * (1.0 - lam_init))
    ob_ref[...] = jnp.concatenate(chunks, axis=1).astype(BF16)

    qd = qd_ref[...]
    chunks = []
    for c in range(D_HEADS // 2):
        sl = slice(c * LANES, (c + 1) * LANES)
        qc = qd[:, sl]
        kc = kd_ref[:, sl].astype(BF16)
        vc = vd_ref[:, sl].astype(BF16)
        halves = []
        for half in range(2):
            qh = jnp.where(lo if half == 0 else jnp.logical_not(lo), qc, 0.0).astype(BF16)
            (e,), den = _softmax_unnorm([_nt(qh, kc) * SCALE])
            halves.append(_mm(e.astype(BF16), vc) / den)
        chunks.append(jnp.where(lo, halves[0], halves[1]))
    od = jnp.concatenate(chunks, axis=1)
    od_ref[...] = _rms_gain(od, gd_ref[...]).astype(BF16)


def _ctx_attention(slab, scal, g_mix2d, lam_init):
    L = SEQ

    def wide(col):
        return pl.BlockSpec((L, 512), lambda b: (b, col))

    def narrow(col):
        return pl.BlockSpec((L, LANES), lambda b: (b, col))

    def gain(col):
        return pl.BlockSpec((1, 512), lambda b: (0, col))

    out = jax.ShapeDtypeStruct((N_TOK, 512), BF16)
    ospec = pl.BlockSpec((L, 512), lambda b: (b, 0))
    return pl.pallas_call(
        functools.partial(_ctx_attn_kernel, lam_init=lam_init),
        grid=(BATCH,),
        in_specs=[pl.BlockSpec(memory_space=pltpu.SMEM),
                  wide(COL_QA), narrow(COL_KA), narrow(COL_VA),
                  wide(COL_QB), wide(COL_KB), wide(COL_VB),
                  wide(COL_QD), wide(COL_KD), wide(COL_VD),
                  gain(0), gain(1), gain(3)],
        out_specs=[ospec, ospec, ospec],
        out_shape=[out, out, out],
        compiler_params=_params("parallel"),
        name="ctx_attention",
    )(scal, slab, slab, slab, slab, slab, slab, slab, slab, slab, g_mix2d, g_mix2d, g_mix2d)


def _prep_kernel(qa_ref, ka_ref, va_ref, qb_ref, kb_ref, vb_ref, qd_ref, kd_ref, vd_ref,
                 cos_ref, sa_ref, sb_ref,
                 qa_o, ka_o, va_o, qb_o, kb_o, vb_o, qd_o, kd_o, vd_o):
    cos, sa, sb = cos_ref[...], sa_ref[...], sb_ref[...]
    lo = _low_half(cos.shape)
    hi = jnp.logical_not(lo)

    def rope(x):
        return (x * cos + pltpu.roll(x, LANES - HEAD_DIM // 4, 1) * sa
                + pltpu.roll(x, HEAD_DIM // 4, 1) * sb)

    ka_o[...] = rope(ka_ref[...]).astype(BF16)
    va_o[...] = va_ref[...].astype(BF16)
    vb_o[...] = vb_ref[...].astype(BF16)
    kd_o[...] = kd_ref[...].astype(BF16)
    vd_o[...] = vd_ref[...].astype(BF16)

    for c in range(A_HEADS // 2):
        hk = c // 2
        r = rope(qa_ref[:, c * LANES:(c + 1) * LANES])
        keep = lo if hk == 0 else hi
        for half in range(2):
            h = 2 * c + half
            x = r if half == hk else pltpu.roll(r, HEAD_DIM, 1)
            qa_o[:, h * LANES:(h + 1) * LANES] = jnp.where(keep, x, 0.0).astype(BF16)
    for h in range(B_HEADS):
        sl = slice(h * LANES, (h + 1) * LANES)
        r = rope(qb_ref[:, sl])
        kb_o[:, sl] = rope(kb_ref[:, sl]).astype(BF16)
        qb_o[:, (2 * h) * LANES:(2 * h + 1) * LANES] = jnp.where(lo, r, 0.0).astype(BF16)
        qb_o[:, (2 * h + 1) * LANES:(2 * h + 2) * LANES] = jnp.where(hi, r, 0.0).astype(BF16)
    for c in range(D_HEADS // 2):
        x = qd_ref[:, c * LANES:(c + 1) * LANES]
        qd_o[:, (2 * c) * LANES:(2 * c + 1) * LANES] = jnp.where(lo, x, 0.0).astype(BF16)
        qd_o[:, (2 * c + 1) * LANES:(2 * c + 2) * LANES] = jnp.where(hi, x, 0.0).astype(BF16)


def _latent_prep(slab, cos, sa, sb):
    tm = TM_PREP
    off = N_CTX // tm
    per_seq = DEC_SEQ // tm

    def wide(col):
        return pl.BlockSpec((tm, 512), lambda i: (i + off, col))

    def narrow(col):
        return pl.BlockSpec((tm, LANES), lambda i: (i + off, col))

    tab = pl.BlockSpec((tm, LANES), lambda i: (i % per_seq, 0))

    def out(width):
        return (pl.BlockSpec((tm, width), lambda i: (i, 0)), jax.ShapeDtypeStruct((N_LAT, width), BF16))

    outs = [out(1024), out(LANES), out(LANES), out(1024), out(512), out(512), out(1024), out(512), out(512)]
    return pl.pallas_call(
        _prep_kernel,
        grid=(N_LAT // tm,),
        in_specs=[wide(COL_QA), narrow(COL_KA), narrow(COL_VA), wide(COL_QB), wide(COL_KB), wide(COL_VB),
                  wide(COL_QD), wide(COL_KD), wide(COL_VD), tab, tab, tab],
        out_specs=[o[0] for o in outs],
        out_shape=[o[1] for o in outs],
        compiler_params=_params("parallel"),
        name="latent_prep",
    )(slab, slab, slab, slab, slab, slab, slab, slab, slab, cos, sa, sb)


def _win_attn_kernel(scal_ref, q_ref, kp_ref, kc_ref, kn_ref, vp_ref, vc_ref, vn_ref, kctx_ref, vctx_ref,
                     g_ref, _all_rows_ref, o_ref):
    n = pl.program_id(1)
    nb = pl.num_programs(1)
    tq = TQ_A
    group = A_HEADS // A_KV_HEADS
    k = jnp.concatenate([kp_ref[...], kc_ref[...], kn_ref[...], kctx_ref[...].astype(BF16)], axis=0)
    v = jnp.concatenate([vp_ref[...], vc_ref[...], vn_ref[...], vctx_ref[...].astype(BF16)], axis=0)
    nk = 3 * tq + PAST_LEN
    row = lax.broadcasted_iota(jnp.int32, (group * tq, nk), 0) & (tq - 1)
    col = lax.broadcasted_iota(jnp.int32, (group * tq, nk), 1)
    prev_thr = row + jnp.where(n > 0, 0, tq)
    next_thr = row + 2 * tq - jnp.where(n < nb - 1, 0, tq)
    masked = ((col < tq) & (col < prev_thr)) | ((col >= 2 * tq) & (col < 3 * tq) & (col > next_thr))
    valid = jnp.logical_not(masked)
    lo = _low_half((tq, LANES))
    heads = []
    for hk in range(A_KV_HEADS):
        q4 = jnp.concatenate([q_ref[:, (hk * group + g) * LANES:(hk * group + g + 1) * LANES]
                              for g in range(group)], axis=0)
        s = jnp.where(valid, _nt(q4, k) * SCALE, NEG_INF)
        sink = jnp.concatenate([jnp.full((tq, 1), scal_ref[hk * group + g], F32) for g in range(group)], axis=0)
        (e,), den = _softmax_unnorm([s], sink)
        o = _mm(e.astype(BF16), v) / den
        for g in range(group):
            h = hk * group + g
            og = o[g * tq:(g + 1) * tq]
            heads.append(og if (h % 2) == hk else pltpu.roll(og, HEAD_DIM, 1))
    oa = jnp.concatenate([jnp.where(lo, heads[2 * c], heads[2 * c + 1]) for c in range(A_HEADS // 2)], axis=1)
    o_ref[...] = _rms_gain(oa, g_ref[...]).astype(BF16)


def _window_attention(qa_p, ka_r, va_b, cache_k, cache_v, scal, g_mix2d, o_all):
    tq = TQ_A
    nb = DEC_SEQ // tq

    def band(d):
        return pl.BlockSpec((tq, LANES), lambda b, n: (b * nb + jnp.clip(n + d, 0, nb - 1), 0))

    ctx = pl.BlockSpec((None, PAST_LEN, LANES), lambda b, n: (b, 0, 0))
    return pl.pallas_call(
        _win_attn_kernel,
        grid=(DEC_BATCH, nb),
        in_specs=[pl.BlockSpec(memory_space=pltpu.SMEM),
                  pl.BlockSpec((tq, A_HEADS * LANES), lambda b, n: (b * nb + n, 0)),
                  band(-1), band(0), band(1), band(-1), band(0), band(1), ctx, ctx,
                  pl.BlockSpec((1, 512), lambda b, n: (0, 0)),
                  pl.BlockSpec(memory_space=pl.ANY)],
        out_specs=pl.BlockSpec((tq, 512), lambda b, n: (N_CTX // tq + b * nb + n, 0)),
        out_shape=jax.ShapeDtypeStruct((N_TOK, 512), BF16),
        input_output_aliases={11: 0},
        compiler_params=_params("parallel", "parallel"),
        name="window_attention",
    )(scal, qa_p, ka_r, ka_r, ka_r, va_b, va_b, va_b, cache_k, cache_v, g_mix2d, o_all)


def _diff_attn_kernel(scal_ref, q_ref, k_ref, v_ref, kctx_ref, vctx_ref, g_ref, _all_rows_ref, o_ref, *,
                      lam_init):
    lam = scal_ref[A_HEADS]
    chunks = []
    for h in range(B_HEADS):
        sl = slice(h * LANES, (h + 1) * LANES)
        kh = k_ref[:, sl]
        kc = kctx_ref[:, sl].astype(BF16)
        es, dens = [], []
        for m in range(2):
            q = q_ref[:, (2 * h + m) * LANES:(2 * h + m + 1) * LANES]
            t_lat = _nt(q, kh) * (SCALE * LOG2E)
            t_ctx = _nt(q, kc) * (SCALE * LOG2E)
            mx = jnp.maximum(t_lat.max(axis=-1, keepdims=True), t_ctx.max(axis=-1, keepdims=True))
            e_lat = jnp.exp2(t_lat - mx)
            e_ctx = jnp.exp2(t_ctx - mx)
            es.append((e_lat, e_ctx))
            dens.append(e_lat.sum(axis=-1, keepdims=True) + e_ctx.sum(axis=-1, keepdims=True))
        ratio = lam * dens[0] / dens[1]
        w_lat = (es[0][0] - es[1][0] * ratio).astype(BF16)
        w_ctx = (es[0][1] - es[1][1] * ratio).astype(BF16)
        o = (_mm(w_lat, v_ref[:, sl]) + _mm(w_ctx, vctx_ref[:, sl].astype(BF16))) / dens[0]
        chunks.append(_rms_gain(o, g_ref[:, sl]) * (1.0 - lam_init))
    o_ref[...] = jnp.concatenate(chunks, axis=1).astype(BF16)


def _diff_attention(qb_p, kb_r, vb_b, cache_k, cache_v, scal, g_mix2d, lam_init, o_all):
    tq = TQ_B
    nb = DEC_SEQ // tq
    full = pl.BlockSpec((DEC_SEQ, 512), lambda b, n: (b, 0))
    ctx = pl.BlockSpec((None, PAST_LEN, 512), lambda b, n: (b, 0, 0))
    return pl.pallas_call(
        functools.partial(_diff_attn_kernel, lam_init=lam_init),
        grid=(DEC_BATCH, nb),
        in_specs=[pl.BlockSpec(memory_space=pltpu.SMEM),
                  pl.BlockSpec((tq, 2 * B_HEADS * LANES), lambda b, n: (b * nb + n, 0)),
                  full, full, ctx, ctx,
                  pl.BlockSpec((1, 512), lambda b, n: (0, 1)),
                  pl.BlockSpec(memory_space=pl.ANY)],
        out_specs=pl.BlockSpec((tq, 512), lambda b, n: (N_CTX // tq + b * nb + n, 0)),
        out_shape=jax.ShapeDtypeStruct((N_TOK, 512), BF16),
        input_output_aliases={7: 0},
        compiler_params=_params("parallel", "parallel"),
        name="diff_attention",
    )(scal, qb_p, kb_r, vb_b, cache_k, cache_v, g_mix2d, o_all)


def _na_window_start(step):
    return np.clip(step * NA_QR - NA_ROWS // 2, 0, GRID_W - NA_KR)


def _na_bias_indices():
    rows = DEC_SEQ // GRID_W
    steps = rows // NA_QR
    pats = []
    for step in range(steps):
        w0 = _na_window_start(step)
        r = step * NA_QR + np.arange(NA_QR)[:, None, None, None]
        c = np.arange(GRID_W)[None, :, None, None]
        kr = w0 + np.arange(NA_KR)[None, None, :, None]
        kc = np.arange(GRID_W)[None, None, None, :]
        kr0 = np.clip(r - NA_ROWS // 2, 0, rows - NA_ROWS)
        kc0 = np.clip(c - NA_COLS // 2, 0, GRID_W - NA_COLS)
        valid = (kr >= kr0) & (kr < kr0 + NA_ROWS) & (kc >= kc0) & (kc < kc0 + NA_COLS)
        drow = np.clip(kr - r + NA_ROWS - 1, 0, 2 * NA_ROWS - 2)
        dcol = np.clip(kc - c + NA_COLS - 1, 0, 2 * NA_COLS - 2)
        shape = (NA_QR * GRID_W, NA_KR * GRID_W)
        full = np.broadcast_to
        pats.append((full(drow, valid.shape).reshape(shape), full(dcol, valid.shape).reshape(shape),
                     valid.reshape(shape)))
    for step in range(2, steps - 1):
        for a, b in zip(pats[1], pats[step]):
            assert np.array_equal(a, b)
    kinds = [pats[0], pats[1], pats[-1]]
    drow, dcol, valid = (np.stack([k[i] for k in kinds]) for i in range(3))
    shape6 = (len(kinds), NA_QR, GRID_W, NA_KR, GRID_W)
    valid6 = valid.reshape(shape6)
    row_ok = valid6.any(axis=(2, 4))
    col_ok = valid6.any(axis=(0, 1, 3))
    assert np.array_equal(valid6, row_ok[:, :, None, :, None] & col_ok[None, None, :, None, :])
    outside = 2 * NA_ROWS - 1
    row_idx = np.where(row_ok, drow.reshape(shape6)[:, :, 0, :, 0], outside)
    col_sel = np.eye(2 * NA_COLS - 1, dtype=np.float32)[dcol.reshape(shape6)[0, 0, :, 0, :]]
    return row_idx, col_sel, col_ok


def _na_bias(rpb, row_idx, col_sel, col_ok):
    cols = jnp.einsum('hab,cdb->hacd', rpb, col_sel, precision=lax.Precision.HIGHEST)
    cols = jnp.where(col_ok[None, None], cols, NEG_INF)
    cols = jnp.concatenate([cols, jnp.full((D_HEADS, 1, GRID_W, GRID_W), NEG_INF, F32)], axis=1)
    kinds = []
    for kind in row_idx:
        kinds.append(jnp.concatenate(
            [jnp.concatenate([cols[:, int(a)] for a in per_row], axis=-1) for per_row in kind], axis=1))
    return jnp.stack(kinds)


def _na_kernel(q_ref, k_ref, v_ref, kctx_ref, vctx_ref, bias_ref, g_ref, _all_rows_ref, o_ref):
    step = pl.program_id(1)
    nq = NA_QR * GRID_W
    nk = NA_KR * GRID_W
    w0 = jnp.clip(step * NA_QR - NA_ROWS // 2, 0, GRID_W - NA_KR)
    start = pl.multiple_of(w0 * GRID_W, GRID_W)
    lo = _low_half((nq, LANES))
    chunks = []
    for c in range(D_HEADS // 2):
        sl = slice(c * LANES, (c + 1) * LANES)
        kw = k_ref[pl.ds(start, nk), sl]
        vw = v_ref[pl.ds(start, nk), sl]
        kc = kctx_ref[:, sl].astype(BF16)
        vc = vctx_ref[:, sl].astype(BF16)
        halves = []
        for half in range(2):
            h = 2 * c + half
            q = q_ref[:, h * LANES:(h + 1) * LANES]
            s_loc = _nt(q, kw) * SCALE + bias_ref[h]
            s_ctx = _nt(q, kc) * SCALE
            (e_loc, e_ctx), den = _softmax_unnorm([s_loc, s_ctx])
            halves.append((_mm(e_loc.astype(BF16), vw) + _mm(e_ctx.astype(BF16), vc)) / den)
        chunks.append(jnp.where(lo, halves[0], halves[1]))
    od = jnp.concatenate(chunks, axis=1)
    o_ref[...] = _rms_gain(od, g_ref[...]).astype(BF16)


def _neighbourhood_attention(qd_p, kd_b, vd_b, cache_k, cache_v, bias, g_mix2d, o_all):
    nq = NA_QR * GRID_W
    steps = DEC_SEQ // nq
    full = pl.BlockSpec((DEC_SEQ, 512), lambda b, j: (b, 0))
    ctx = pl.BlockSpec((None, PAST_LEN, 512), lambda b, j: (b, 0, 0))

    def kind(b, j):
        return (jnp.where(j == 0, 0, jnp.where(j == steps - 1, 2, 1)), 0, 0, 0)

    return pl.pallas_call(
        _na_kernel,
        grid=(DEC_BATCH, steps),
        in_specs=[pl.BlockSpec((nq, D_HEADS * LANES), lambda b, j: (b * steps + j, 0)),
                  full, full, ctx, ctx,
                  pl.BlockSpec((None, D_HEADS, nq, NA_KR * GRID_W), kind),
                  pl.BlockSpec((1, 512), lambda b, j: (0, 3)),
                  pl.BlockSpec(memory_space=pl.ANY)],
        out_specs=pl.BlockSpec((nq, 512), lambda b, j: (N_CTX // nq + b * steps + j, 0)),
        out_shape=jax.ShapeDtypeStruct((N_TOK, 512), BF16),
        input_output_aliases={7: 0},
        compiler_params=_params("parallel", "arbitrary"),
        name="neighbourhood_attention",
    )(qd_p, kd_b, vd_b, cache_k, cache_v, bias, g_mix2d, o_all)


N_SCAN_TILES = N_TOK // T_SCAN
CTX_TILES = N_CTX // T_SCAN
TILES_PER_LAT = DEC_SEQ // T_SCAN


def _scan_tile_flags(i):
    is_ctx = i < CTX_TILES
    pos = (i - CTX_TILES) % TILES_PER_LAT
    return is_ctx | (pos == 0), is_ctx | (pos == TILES_PER_LAT - 1)


def _scan_seq(i):
    return jnp.where(i < CTX_TILES, i, CTX_TILES + (i - CTX_TILES) // TILES_PER_LAT)


def _conv_gates(x_ref, prev_ref, next_ref, cw_ref, cb_ref, wgh_ref, wgl_ref, bg_ref, lam_ref, first, last):
    x = x_ref[...]
    t = T_SCAN
    row = lax.broadcasted_iota(jnp.int32, x.shape, 0)
    pm1 = jnp.where(first, 0.0, prev_ref[7:8, :])
    n0 = jnp.where(last, 0.0, next_ref[0:1, :])
    n1 = jnp.where(last, 0.0, next_ref[1:2, :])
    x_m1 = jnp.where(row == 0, pm1, pltpu.roll(x, 1, 0))
    x_p1 = jnp.where(row == t - 1, n0, pltpu.roll(x, t - 1, 0))
    x_p2 = jnp.where(row == t - 2, n0, jnp.where(row == t - 1, n1, pltpu.roll(x, t - 2, 0)))
    y = (cw_ref[0:1, :] * x_m1 + cw_ref[1:2, :] * x + cw_ref[2:3, :] * x_p1 + cw_ref[3:4, :] * x_p2
         + cb_ref[...])
    y_hi = y.astype(BF16)
    y_lo = (y - y_hi.astype(F32)).astype(BF16)
    zr, zi = [], []
    for c in range(C_WIDTH // LANES):
        sl = slice(c * LANES, (c + 1) * LANES)
        w_hi = wgh_ref[0, sl, :]
        z = _mm(y_hi[:, sl], w_hi) + _mm(y_lo[:, sl], w_hi) + _mm(y_hi[:, sl], wgl_ref[0, sl, :])
        zr.append(z[:, :LANES])
        zi.append(z[:, LANES:])
    bias = bg_ref[0]
    r = jax.nn.sigmoid(jnp.concatenate(zr, axis=1) + bias[:, :C_WIDTH])
    gate_i = jax.nn.sigmoid(jnp.concatenate(zi, axis=1) + bias[:, C_WIDTH:])
    nl = -lam_ref[0]
    softplus = jnp.maximum(nl, 0.0) + jnp.log1p(jnp.exp(-jnp.abs(nl)))
    log_a = -C_POW * r * softplus
    a = jnp.exp(log_a)
    u = jnp.sqrt(1.0 - jnp.exp(2.0 * log_a)) * gate_i * y
    return a, u


def _scan_rows(a_s, u_s, h_s, h, reverse):
    groups = T_SCAN // SUBLANES
    row = lax.broadcasted_iota(jnp.int32, (SUBLANES, C_WIDTH), 0)

    def body(i, h):
        g = (groups - 1 - i) if reverse else i
        base = pl.multiple_of(g * SUBLANES, SUBLANES)
        a = a_s[pl.ds(base, SUBLANES), :]
        u = u_s[pl.ds(base, SUBLANES), :]
        for d in (1, 2, 4):
            has_src = (row < SUBLANES - d) if reverse else (row >= d)
            shift = SUBLANES - d if reverse else d
            a_src = jnp.where(has_src, pltpu.roll(a, shift, 0), 1.0)
            u_src = jnp.where(has_src, pltpu.roll(u, shift, 0), 0.0)
            u = u + a * u_src
            a = a * a_src
        hs = u + a * h
        h_s[pl.ds(base, SUBLANES), :] = hs
        return hs[0:1, :] if reverse else hs[SUBLANES - 1:SUBLANES, :]

    return lax.fori_loop(0, groups, body, h, unroll=2)


def _scan_fwd_kernel(x_ref, prev_ref, next_ref, cw_ref, cb_ref, wgh_ref, wgl_ref, bg_ref, lam_ref, h0_ref,
                     hf_ref, fin_ref, a_s, u_s, carry_s):
    i = pl.program_id(0)
    first, last = _scan_tile_flags(i)
    a, u = _conv_gates(x_ref, prev_ref, next_ref, cw_ref, cb_ref, wgh_ref, wgl_ref, bg_ref, lam_ref, first, last)
    a_s[...] = a
    u_s[...] = u

    @pl.when(first)
    def _():
        carry_s[...] = h0_ref[...]

    h = _scan_rows(a_s, u_s, hf_ref, carry_s[...], reverse=False)
    carry_s[...] = h
    fin_ref[...] = h


def _scan_bwd_kernel(x_ref, prev_ref, next_ref, cw_ref, cb_ref, wgh_ref, wgl_ref, bg_ref, lam_ref, h0_ref,
                     hf_ref, gc_ref, g_ref, oc_ref, fin_ref, a_s, u_s, hb_s, carry_s):
    i = N_SCAN_TILES - 1 - pl.program_id(0)
    first, last = _scan_tile_flags(i)
    a, u = _conv_gates(x_ref, prev_ref, next_ref, cw_ref, cb_ref, wgh_ref, wgl_ref, bg_ref, lam_ref, first, last)
    a_s[...] = a
    u_s[...] = u

    @pl.when(last)
    def _():
        carry_s[...] = h0_ref[...]

    h = _scan_rows(a_s, u_s, hb_s, carry_s[...], reverse=True)
    carry_s[...] = h
    fin_ref[...] = h
    g = gc_ref[...]
    gelu = 0.5 * g * (1.0 + jnp.tanh(math.sqrt(2.0 / math.pi) * (g + 0.044715 * (g * g * g))))
    oc_ref[...] = _rms_gain(gelu * (hf_ref[...] + hb_s[...]), g_ref[...]).astype(BF16)


def _scan_common_specs(tile_of, direction):
    halo = T_SCAN // 8
    last_block = N_TOK // 8 - 1
    return [pl.BlockSpec((T_SCAN, 512), lambda j: (tile_of(j), COL_XC)),
            pl.BlockSpec((8, 512), lambda j: (jnp.maximum(tile_of(j) * halo - 1, 0), COL_XC)),
            pl.BlockSpec((8, 512), lambda j: (jnp.minimum((tile_of(j) + 1) * halo, last_block), COL_XC)),
            pl.BlockSpec((4, C_WIDTH), lambda j: (0, 0)),
            pl.BlockSpec((1, C_WIDTH), lambda j: (0, 0)),
            pl.BlockSpec((1, C_WIDTH, 2 * LANES), lambda j: (direction, 0, 0)),
            pl.BlockSpec((1, C_WIDTH, 2 * LANES), lambda j: (direction, 0, 0)),
            pl.BlockSpec((1, 1, 2 * C_WIDTH), lambda j: (direction, 0, 0)),
            pl.BlockSpec((1, 1, C_WIDTH), lambda j: (direction, 0, 0)),
            pl.BlockSpec((None, 1, C_WIDTH), lambda j: (_scan_seq(tile_of(j)), 0, 0))]


def _rglru(slab, conv_w, conv_b, w_gates, b_gates, lam, h0_f, h0_b, g_mix2d):
    wg_hi = w_gates.astype(BF16)
    wg_lo = (w_gates - wg_hi.astype(F32)).astype(BF16)
    fin = jax.ShapeDtypeStruct((N_SCAN_TILES, 1, C_WIDTH), F32)
    fwd_tile = lambda j: j
    h_f, fin_f = pl.pallas_call(
        _scan_fwd_kernel,
        grid=(N_SCAN_TILES,),
        in_specs=_scan_common_specs(fwd_tile, 0),
        out_specs=[pl.BlockSpec((T_SCAN, C_WIDTH), lambda j: (j, 0)),
                   pl.BlockSpec((None, 1, C_WIDTH), lambda j: (j, 0, 0))],
        out_shape=[jax.ShapeDtypeStruct((N_TOK, C_WIDTH), F32), fin],
        scratch_shapes=[pltpu.VMEM((T_SCAN, C_WIDTH), F32), pltpu.VMEM((T_SCAN, C_WIDTH), F32),
                        pltpu.VMEM((1, C_WIDTH), F32)],
        compiler_params=_params("arbitrary"),
        name="rglru_forward",
    )(slab, slab, slab, conv_w, conv_b, wg_hi, wg_lo, b_gates, lam, h0_f)
    bwd_tile = lambda j: N_SCAN_TILES - 1 - j
    oc, fin_b = pl.pallas_call(
        _scan_bwd_kernel,
        grid=(N_SCAN_TILES,),
        in_specs=_scan_common_specs(bwd_tile, 1) + [
            pl.BlockSpec((T_SCAN, C_WIDTH), lambda j: (bwd_tile(j), 0)),
            pl.BlockSpec((T_SCAN, 512), lambda j: (bwd_tile(j), COL_GC)),
            pl.BlockSpec((1, 512), lambda j: (0, 2))],
        out_specs=[pl.BlockSpec((T_SCAN, C_WIDTH), lambda j: (bwd_tile(j), 0)),
                   pl.BlockSpec((None, 1, C_WIDTH), lambda j: (bwd_tile(j), 0, 0))],
        out_shape=[jax.ShapeDtypeStruct((N_TOK, C_WIDTH), BF16), fin],
        scratch_shapes=[pltpu.VMEM((T_SCAN, C_WIDTH), F32), pltpu.VMEM((T_SCAN, C_WIDTH), F32),
                        pltpu.VMEM((T_SCAN, C_WIDTH), F32), pltpu.VMEM((1, C_WIDTH), F32)],
        compiler_params=_params("arbitrary"),
        name="rglru_backward",
    )(slab, slab, slab, conv_w, conv_b, wg_hi, wg_lo, b_gates, lam, h0_b, h_f, slab, g_mix2d)
    return oc, fin_f, fin_b


def _mixout_kernel(oa_ref, ob_ref, oc_ref, od_ref, w_ref, xc_ref, xl_ref, g1_ref, sc2_ref, sh2_ref, lng_ref,
                   lnb_ref, wrh_ref, wrl_ref, br_ref, x1_ref, h2_ref, idx_ref, wt_ref, rank_ref, cnt_ref, cnt_s):
    @pl.when(pl.program_id(0) == 0)
    def _():
        cnt_s[...] = jnp.zeros(cnt_s.shape, F32)

    parts = 2
    pm = xc_ref.shape[0] // parts
    is_ctx = _is_ctx_tile(xc_ref.shape[0])
    for part in range(parts):
        rows = pl.ds(part * pm, pm)
        x = jnp.where(is_ctx, xc_ref[rows, :], xl_ref[rows, :])
        _mixout_rows(rows, part * pm, pm, oa_ref, ob_ref, oc_ref, od_ref, w_ref, x, g1_ref,
                     sc2_ref, sh2_ref, lng_ref, lnb_ref, wrh_ref, wrl_ref, br_ref, x1_ref, h2_ref, idx_ref,
                     wt_ref, rank_ref, cnt_s)
    cnt_ref[...] = cnt_s[...]


def _mixout_rows(rows, first, tm, oa_ref, ob_ref, oc_ref, od_ref, w_ref, x, g1_ref, sc2_ref, sh2_ref,
                 lng_ref, lnb_ref, wrh_ref, wrl_ref, br_ref, x1_ref, h2_ref, idx_ref, wt_ref, rank_ref, cnt_s):
    y = _mm(oa_ref[rows, :], w_ref[0:512, :])
    y += _mm(ob_ref[rows, :], w_ref[512:1024, :])
    y += _mm(oc_ref[rows, :], w_ref[1024:1536, :])
    y += _mm(od_ref[rows, :], w_ref[1536:2048, :])
    x1 = _layernorm(ALPHA * x + g1_ref[...] * y, lng_ref[...], lnb_ref[...])
    x1_ref[rows, :] = x1
    h2 = x1 * (1.0 + sc2_ref[...]) + sh2_ref[...]
    _store_token_tiles(h2_ref, (), _pack_rows(h2), first)

    h_hi = h2.astype(BF16)
    h_lo = (h2 - h_hi.astype(F32)).astype(BF16)
    scores = jax.nn.sigmoid(_mm(h_hi, wrh_ref[...]) + _mm(h_lo, wrh_ref[...]) + _mm(h_hi, wrl_ref[...]))
    sel = scores + br_ref[...]
    lane = lax.broadcasted_iota(jnp.int32, sel.shape, 1).astype(F32)
    slot = lax.broadcasted_iota(jnp.int32, (tm, 8), 1)
    wide = lax.broadcasted_iota(jnp.int32, (tm, LANES), 1)
    idx_out = jnp.zeros((tm, LANES), F32)
    wt_out = jnp.zeros((tm, 8), F32)
    total = jnp.zeros((sel.shape[0], 1), F32)
    hits = []
    for k in range(TOP_K):
        m = sel.max(axis=-1, keepdims=True)
        idx = jnp.where(sel == m, lane, float(N_EXPERTS)).min(axis=-1, keepdims=True)
        hit = lane == idx
        hits.append(hit)
        w = jnp.where(hit, scores, 0.0).sum(axis=-1, keepdims=True)
        total = total + w
        idx_out = jnp.where(wide == k, idx, idx_out)
        wt_out = jnp.where(slot == k, w, wt_out)
        sel = jnp.where(hit, -jnp.inf, sel)
    cols = pl.ds(first, tm)
    idx_ref[:, cols] = idx_out.T[:8, :].astype(jnp.int32)
    wt_ref[rows, :] = wt_out / total * ROUTED_SCALE

    chosen = jnp.zeros(sel.shape, F32)
    for hit in hits:
        chosen = chosen + hit.astype(F32)
    earlier = (lax.broadcasted_iota(jnp.int32, (tm, tm), 1) < lax.broadcasted_iota(jnp.int32, (tm, tm), 0))
    before = _mm(earlier.astype(F32).astype(BF16), chosen.astype(BF16)) + cnt_s[...]
    rank_out = jnp.zeros((tm, LANES), F32)
    for k, hit in enumerate(hits):
        rank_out = jnp.where(wide == k, jnp.where(hit, before, 0.0).sum(axis=-1, keepdims=True), rank_out)
    rank_ref[:, cols] = rank_out.T[:8, :].astype(jnp.int32)
    cnt_s[...] = cnt_s[...] + chosen.sum(axis=0, keepdims=True)


def _mixer_out(oa, ob, oc, od, w_out_b, x_ctx, x_lat, mod, ln_g, ln_b, w_router, b_router):
    tm = TM_OUT
    cat = pl.BlockSpec((tm, 512), lambda i: (i, 0))
    row = pl.BlockSpec((tm, D_MODEL), lambda i: (i, 0))
    vec = pl.BlockSpec((1, D_MODEL), lambda i: (0, 0))
    k8 = pl.BlockSpec((tm, 8), lambda i: (i, 0))
    k_major = pl.BlockSpec((8, tm), lambda i: (0, i))
    per_expert = pl.BlockSpec((1, N_EXPERTS), lambda i: (0, 0))
    router = pl.BlockSpec((D_MODEL, N_EXPERTS), lambda i: (0, 0))
    wr_hi = w_router.astype(BF16)
    wr_lo = (w_router - wr_hi.astype(F32)).astype(BF16)
    return pl.pallas_call(
        _mixout_kernel,
        grid=(N_TOK // tm,),
        in_specs=[cat, cat, cat, cat,
                  pl.BlockSpec((D_MODEL, D_MODEL), lambda i: (0, 0)),
                  _ctx_rows(tm), _lat_rows(tm), _mod_spec(2, tm), _mod_spec(4, tm), _mod_spec(3, tm), vec, vec,
                  router, router, per_expert],
        out_specs=[row, pl.BlockSpec((tm * SUBLANES, LANES), lambda i: (i, 0)), k_major, k8, k_major, per_expert],
        out_shape=[jax.ShapeDtypeStruct((N_TOK, D_MODEL), F32),
                   jax.ShapeDtypeStruct((N_TOK * SUBLANES, LANES), jnp.uint32),
                   jax.ShapeDtypeStruct((8, N_TOK), jnp.int32), jax.ShapeDtypeStruct((N_TOK, 8), F32),
                   jax.ShapeDtypeStruct((8, N_TOK), jnp.int32), jax.ShapeDtypeStruct((1, N_EXPERTS), F32)],
        scratch_shapes=[pltpu.VMEM((1, N_EXPERTS), F32)],
        compiler_params=_params("arbitrary"),
        name="mixer_out_router",
    )(oa, ob, oc, od, w_out_b, x_ctx, x_lat, mod, mod, mod, ln_g, ln_b, wr_hi, wr_lo, b_router)


ROW_DMA_QUEUES = 2
MOE_BLOCKS = N_TOK * TOP_K // MOE_BM + N_EXPERTS
MOE_ROWS = MOE_BLOCKS * MOE_BM


def _row_copy(src, src_row, dst, dst_row, sem):
    def tile(row):
        return pl.ds(row * SUBLANES if isinstance(row, int) else pl.multiple_of(row * SUBLANES, SUBLANES), SUBLANES)

    return pltpu.make_async_copy(src.at[tile(src_row)], dst.at[tile(dst_row)], sem)


def _dispatch_kernel(slot_ref, h_ref, xg_ref, sem):
    tm = h_ref.shape[0] // SUBLANES
    base = pl.program_id(0) * tm

    for r in range(tm):
        for k in range(TOP_K):
            _row_copy(h_ref, r, xg_ref, slot_ref[base + (k * N_TOK + r)], sem).start(priority=k % ROW_DMA_QUEUES)
    for k in range(TOP_K):
        pltpu.make_async_copy(h_ref, xg_ref.at[pl.ds(0, tm * SUBLANES)], sem).wait()


def _dispatch_rows(slot_flat, h2):
    tm = TM_DISPATCH
    grid_spec = pltpu.PrefetchScalarGridSpec(
        num_scalar_prefetch=1,
        grid=(N_TOK // tm,),
        in_specs=[pl.BlockSpec((tm * SUBLANES, LANES), lambda i, s: (i, 0))],
        out_specs=pl.BlockSpec(memory_space=pl.ANY),
        scratch_shapes=[pltpu.SemaphoreType.DMA])
    return pl.pallas_call(
        _dispatch_kernel,
        grid_spec=grid_spec,
        out_shape=jax.ShapeDtypeStruct((MOE_ROWS * SUBLANES, LANES), jnp.uint32),
        compiler_params=_params("arbitrary"),
        name="moe_dispatch",
    )(slot_flat, h2)


BLOCK_TILE_ROWS = MOE_BM * SUBLANES
EXPERT_X_BUFFERS = 3


def _expert_kernel(blk0_ref, nblk_ref, cnt_ref, wg_ref, wu_ref, wd_ref, x_hbm, y_hbm,
                   wg_s, wu_s, wd_s, xbuf, ybuf, xsem, ysem):
    e = pl.program_id(0)
    first_blk = blk0_ref[e]
    n_blk = nblk_ref[e]
    count = cnt_ref[e]
    total = blk0_ref[N_EXPERTS - 1] + nblk_ref[N_EXPERTS - 1]

    def rows_of(g):
        return pl.ds(pl.multiple_of(g * BLOCK_TILE_ROWS, BLOCK_TILE_ROWS), BLOCK_TILE_ROWS)

    def x_copy(g, buf):
        return pltpu.make_async_copy(x_hbm.at[rows_of(g)], xbuf.at[buf], xsem.at[buf])

    def y_copy(g, buf):
        return pltpu.make_async_copy(ybuf.at[buf], y_hbm.at[rows_of(g)], ysem.at[buf])

    for ahead in range(EXPERT_X_BUFFERS - 1):
        @pl.when((e == 0) & (total > ahead))
        def _():
            x_copy(ahead, ahead).start()

    wg_s[...] = wg_ref[...].astype(BF16)
    wu_s[...] = wu_ref[...].astype(BF16)
    wd_s[...] = wd_ref[...].astype(BF16)

    def block(j, carry):
        g = first_blk + j
        buf = g % 2
        xb = lax.rem(g, EXPERT_X_BUFFERS)
        nxt = g + (EXPERT_X_BUFFERS - 1)

        @pl.when(nxt < total)
        def _():
            x_copy(nxt, lax.rem(nxt, EXPERT_X_BUFFERS)).start()

        x_copy(g, xb).wait()

        @pl.when(g >= 2)
        def _():
            y_copy(g - 2, buf).wait()

        n_rows = count - j * MOE_BM

        def swiglu(m):
            x = _load_token_tiles(xbuf, (xb,), m)
            row = lax.broadcasted_iota(jnp.int32, x.shape, 0)
            lo, hi = _unpack_rows(jnp.where(row < n_rows, x, jnp.uint32(0)))
            lo, hi = lo.astype(BF16), hi.astype(BF16)
            gate = _mm(lo, wg_s[:HALF_D, :]) + _mm(hi, wg_s[HALF_D:, :])
            up = _mm(lo, wu_s[:HALF_D, :]) + _mm(hi, wu_s[HALF_D:, :])
            _store_token_tiles(ybuf, (buf,), _pack_rows(_mm((_silu(gate) * up).astype(BF16), wd_s[...])))

        @pl.when(n_rows > MOE_BM // 2)
        def _():
            swiglu(MOE_BM)

        @pl.when(n_rows <= MOE_BM // 2)
        def _():
            swiglu(MOE_BM // 2)
            _store_token_tiles(ybuf, (buf,), jnp.zeros((MOE_BM // 2, HALF_D), jnp.uint32), MOE_BM // 2)

        y_copy(g, buf).start()
        return carry

    lax.fori_loop(0, n_blk, block, 0)

    @pl.when((e == N_EXPERTS - 1) & (total >= 2))
    def _():
        y_copy(total - 2, total % 2).wait()

    @pl.when((e == N_EXPERTS - 1) & (total >= 1))
    def _():
        y_copy(total - 1, (total - 1) % 2).wait()


def _routed_experts(xg, first_blk, n_blk, counts, layer, w_g, w_u, w_d):
    def weight(shape):
        return pl.BlockSpec((None, None) + shape, lambda e, *_: (layer, e, 0, 0))

    any_space = pl.BlockSpec(memory_space=pl.ANY)
    x_bufs = pltpu.VMEM((EXPERT_X_BUFFERS, BLOCK_TILE_ROWS, LANES), jnp.uint32)
    y_bufs = pltpu.VMEM((2, BLOCK_TILE_ROWS, LANES), jnp.uint32)
    grid_spec = pltpu.PrefetchScalarGridSpec(
        num_scalar_prefetch=3,
        grid=(N_EXPERTS,),
        in_specs=[weight((D_MODEL, D_EXPERT)), weight((D_MODEL, D_EXPERT)), weight((D_EXPERT, D_MODEL)), any_space],
        out_specs=any_space,
        scratch_shapes=[pltpu.VMEM((D_MODEL, D_EXPERT), BF16), pltpu.VMEM((D_MODEL, D_EXPERT), BF16),
                        pltpu.VMEM((D_EXPERT, D_MODEL), BF16), x_bufs, y_bufs,
                        pltpu.SemaphoreType.DMA((EXPERT_X_BUFFERS,)), pltpu.SemaphoreType.DMA((2,))])
    return pl.pallas_call(
        _expert_kernel,
        grid_spec=grid_spec,
        out_shape=jax.ShapeDtypeStruct((MOE_ROWS * SUBLANES, LANES), jnp.uint32),
        compiler_params=_params("arbitrary"),
        name="routed_experts",
    )(first_blk, n_blk, counts, w_g, w_u, w_d, xg)


def _ffn_out_kernel(slot_ref, h_ref, wt_ref, x1_ref, wg_ref, wu_ref, wd_ref, g2_ref, lng_ref, lnb_ref, y_ref,
                    oc_ref, ol_ref, ybuf_a, ybuf_b, sems):
    tm = h_ref.shape[0] // SUBLANES
    i = pl.program_id(0)
    n = pl.num_programs(0)

    def gather(tile, dst, sem):
        base = tile * tm
        for r in range(tm):
            for k in range(TOP_K):
                _row_copy(y_ref, slot_ref[base + (k * N_TOK + r)], dst.at[k], r,
                          sem).start(priority=k % ROW_DMA_QUEUES)

    def wait(dst, sem):
        for k in range(TOP_K):
            pltpu.make_async_copy(y_ref.at[pl.ds(0, tm * SUBLANES)], dst.at[k], sem).wait()

    @pl.when(i == 0)
    def _():
        gather(0, ybuf_a, sems.at[0])

    def step(cur, cur_sem, nxt, nxt_sem):
        gather(jnp.minimum(i + 1, n - 1), nxt, nxt_sem)
        lo, hi = _unpack_rows(_load_token_tiles(h_ref, (), tm))
        lo, hi = lo.astype(BF16), hi.astype(BF16)
        gate = _mm(lo, wg_ref[:HALF_D, :]) + _mm(hi, wg_ref[HALF_D:, :])
        up = _mm(lo, wu_ref[:HALF_D, :]) + _mm(hi, wu_ref[HALF_D:, :])
        y = _mm((_silu(gate) * up).astype(BF16), wd_ref[...])
        wait(cur, cur_sem)
        wt = wt_ref[...]
        y_lo, y_hi = y[:, :HALF_D], y[:, HALF_D:]
        for k in range(TOP_K):
            r_lo, r_hi = _unpack_rows(_load_token_tiles(cur, (k,), tm))
            y_lo = y_lo + r_lo * wt[:, k:k + 1]
            y_hi = y_hi + r_hi * wt[:, k:k + 1]
        y = jnp.concatenate([y_lo, y_hi], axis=1)
        out = _layernorm(ALPHA * x1_ref[...] + g2_ref[...] * y, lng_ref[...], lnb_ref[...])
        is_ctx = _is_ctx_tile(tm)

        @pl.when(is_ctx)
        def _():
            oc_ref[...] = out

        @pl.when(jnp.logical_not(is_ctx))
        def _():
            ol_ref[...] = out

        @pl.when(i == n - 1)
        def _():
            wait(nxt, nxt_sem)

    @pl.when(i % 2 == 0)
    def _():
        step(ybuf_a, sems.at[0], ybuf_b, sems.at[1])

    @pl.when(i % 2 == 1)
    def _():
        step(ybuf_b, sems.at[1], ybuf_a, sems.at[0])


def _ffn_out(slot_flat, h2, wt, x1, ws_g, ws_u, ws_d, mod, ln_g, ln_b, y_blk):
    tm = TM_FFN
    row = pl.BlockSpec((tm, D_MODEL), lambda i, s: (i, 0))
    vec = pl.BlockSpec((1, D_MODEL), lambda i, s: (0, 0))
    grid_spec = pltpu.PrefetchScalarGridSpec(
        num_scalar_prefetch=1,
        grid=(N_TOK // tm,),
        in_specs=[pl.BlockSpec((tm * SUBLANES, LANES), lambda i, s: (i, 0)),
                  pl.BlockSpec((tm, 8), lambda i, s: (i, 0)), row,
                  pl.BlockSpec((D_MODEL, D_EXPERT), lambda i, s: (0, 0)),
                  pl.BlockSpec((D_MODEL, D_EXPERT), lambda i, s: (0, 0)),
                  pl.BlockSpec((D_EXPERT, D_MODEL), lambda i, s: (0, 0)),
                  _mod_spec(5, tm), vec, vec,
                  pl.BlockSpec(memory_space=pl.ANY)],
        out_specs=[_ctx_rows(tm), _lat_rows(tm)],
        scratch_shapes=[pltpu.VMEM((TOP_K, tm * SUBLANES, LANES), jnp.uint32),
                        pltpu.VMEM((TOP_K, tm * SUBLANES, LANES), jnp.uint32),
                        pltpu.SemaphoreType.DMA((2,))])
    return pl.pallas_call(
        _ffn_out_kernel,
        grid_spec=grid_spec,
        out_shape=[jax.ShapeDtypeStruct((N_CTX, D_MODEL), F32), jax.ShapeDtypeStruct((N_LAT, D_MODEL), F32)],
        compiler_params=_params("arbitrary"),
        name="shared_expert_out",
    )(slot_flat, h2, wt, x1, ws_g, ws_u, ws_d, mod, ln_g, ln_b, y_blk)


def _dispatch_plan(idx, rank, counts):
    bm = MOE_BM
    counts = counts.reshape(N_EXPERTS).astype(jnp.int32)
    padded = (counts + bm - 1) // bm * bm
    pad_end = jnp.cumsum(padded)
    pad_start = pad_end - padded
    experts = jnp.arange(N_EXPERTS, dtype=jnp.int32)[:, None, None]
    first_row = jnp.sum(jnp.where(idx[None, :TOP_K] == experts, pad_start[:, None, None], 0), axis=0)
    slot = first_row + rank[:TOP_K]
    return slot.reshape(-1).astype(jnp.int32), pad_start // bm, padded // bm, counts


def _rope_tables():
    t = jnp.arange(DEC_SEQ)
    row = (t // GRID_W).astype(F32)
    col = (t % GRID_W).astype(F32)
    nf = HEAD_DIM // 4
    inv = ROPE_BASE ** (-jnp.arange(nf, dtype=F32) / nf)
    ar = row[:, None] * inv[None, :]
    ac = col[:, None] * inv[None, :]
    ang = jnp.concatenate([ar, ar, ac, ac], axis=-1)
    ang = jnp.concatenate([ang, ang], axis=-1)
    cos, sin = jnp.cos(ang), jnp.sin(ang)
    first = (jnp.arange(LANES) % (2 * nf)) < nf
    return cos, jnp.where(first, -sin, 0.0), jnp.where(first, 0.0, sin)


def _lambda_init(l):
    return 0.8 - 0.6 * math.exp(-0.3 * l)


def _gate_pairs(w_r, w_i):
    per = LANES // C_BW
    eye = jnp.eye(per, dtype=w_r.dtype)

    def chunk_diag(w):
        w = w.reshape(C_BLOCKS // per, per, C_BW, C_BW)
        return (eye[None, :, None, :, None] * w[:, :, :, None, :]).reshape(C_BLOCKS // per, LANES, LANES)

    return jnp.concatenate([chunk_diag(w_r), chunk_diag(w_i)], axis=2).reshape(C_WIDTH, 2 * LANES)


def kernel(x_prompt, x_sample, c, cache_a_k, cache_a_v, cache_b_k, cache_b_v, state_c, cache_d_k, cache_d_v, c_ctx, w_mod, b_mod, w_in, a_sink, b_lambda, c_conv_w, c_conv_b, c_w_rgate, c_b_rgate, c_w_igate, c_b_igate, c_lambda, d_rpb, g_mix, w_out, ln_g, ln_b, w_router, b_router, w_e_gate, w_e_up, w_e_down, w_s_gate, w_s_up, w_s_down):
    x_ctx, x_lat = x_prompt.reshape(N_CTX, D_MODEL), x_sample.reshape(N_LAT, D_MODEL)
    cond8 = jnp.zeros((8, D_MODEL), F32).at[0].set(c_ctx).at[1:1 + DEC_BATCH].set(c)
    mod_all = _modulation(cond8, w_mod, b_mod)
    cos, sin_a, sin_b = _rope_tables()
    na_row_idx, na_col_sel, na_col_ok = _na_bias_indices()

    order = ((0, 512), (768, IN_WIDTH), (512, 768))

    ctx_out = [[] for _ in range(7)]
    for l in range(DEPTH):
        lam_init = _lambda_init(l)
        mod = mod_all[l, :N_GROUPS].reshape(N_GROUPS, 6, 1, D_MODEL)
        w_l = w_in[l]
        w_in_b = jnp.concatenate([w_l[:, a:b] for a, b in order], axis=1).astype(BF16)
        slab = _in_projection(x_ctx, x_lat, mod, w_in_b)

        lp = b_lambda[l]
        lam = jnp.exp(jnp.sum(lp[0] * lp[1])) - jnp.exp(jnp.sum(lp[2] * lp[3])) + lam_init
        scal = jnp.zeros((16,), F32).at[:A_HEADS].set(a_sink[l]).at[A_HEADS].set(lam)
        g2d = g_mix[l].reshape(1, -1)

        oa, ob, od = _ctx_attention(slab, scal, g2d, lam_init)

        qa_p, ka_r, va_b, qb_p, kb_r, vb_b, qd_p, kd_b, vd_b = _latent_prep(slab, cos, sin_a, sin_b)
        oa = _window_attention(qa_p, ka_r, va_b, cache_a_k[:, l].reshape(DEC_BATCH, PAST_LEN, LANES),
                               cache_a_v[:, l].reshape(DEC_BATCH, PAST_LEN, LANES), scal, g2d, oa)
        ob = _diff_attention(qb_p, kb_r, vb_b, cache_b_k[:, l].reshape(DEC_BATCH, PAST_LEN, 512),
                             cache_b_v[:, l].reshape(DEC_BATCH, PAST_LEN, 512), scal, g2d, lam_init, ob)
        bias = _na_bias(d_rpb[l], na_row_idx, na_col_sel, na_col_ok)
        od = _neighbourhood_attention(qd_p, kd_b, vd_b, cache_d_k[:, l].reshape(DEC_BATCH, PAST_LEN, 512),
                                      cache_d_v[:, l].reshape(DEC_BATCH, PAST_LEN, 512), bias, g2d, od)

        w_gates = jnp.stack([_gate_pairs(c_w_rgate[l, d], c_w_igate[l, d]) for d in range(2)])
        b_gates = jnp.concatenate([c_b_rgate[l], c_b_igate[l]], axis=-1).reshape(2, 1, 2 * C_WIDTH)
        zeros = jnp.zeros((BATCH, 1, C_WIDTH), F32)
        h0_f = jnp.concatenate([zeros, state_c[:, l, 0][:, None, :]], axis=0)
        h0_b = jnp.concatenate([zeros, state_c[:, l, 1][:, None, :]], axis=0)
        oc, fin_f, fin_b = _rglru(slab, c_conv_w[l], c_conv_b[l].reshape(1, -1), w_gates, b_gates,
                                  c_lambda[l].reshape(2, 1, C_WIDTH), h0_f, h0_b, g2d)

        x1, h2, idx, wt, rank, counts = _mixer_out(oa, ob, oc, od, w_out[l].astype(BF16), x_ctx, x_lat, mod,
                                                   ln_g[l, 0].reshape(1, -1), ln_b[l, 0].reshape(1, -1),
                                                   w_router[l], b_router[l].reshape(1, -1))

        slot, first_blk, n_blk, counts = _dispatch_plan(idx, rank, counts)
        xg = _dispatch_rows(slot, h2)
        y_blk = _routed_experts(xg, first_blk, n_blk, counts, l, w_e_gate, w_e_up, w_e_down)
        x_ctx, x_lat = _ffn_out(slot, h2, wt, x1, w_s_gate[l].astype(BF16), w_s_up[l].astype(BF16),
                                w_s_down[l].astype(BF16), mod, ln_g[l, 1].reshape(1, -1),
                                ln_b[l, 1].reshape(1, -1), y_blk)

        ctx = slab[:N_CTX]
        ctx_out[0].append(ctx[:, COL_KA * LANES:(COL_KA + 1) * LANES].reshape(BATCH, SEQ, A_KV_HEADS, HEAD_DIM))
        ctx_out[1].append(ctx[:, COL_VA * LANES:(COL_VA + 1) * LANES].reshape(BATCH, SEQ, A_KV_HEADS, HEAD_DIM))
        ctx_out[2].append(ctx[:, COL_KB * 512:(COL_KB + 1) * 512].reshape(BATCH, SEQ, B_HEADS, 2, HEAD_DIM))
        ctx_out[3].append(ctx[:, COL_VB * 512:(COL_VB + 1) * 512].reshape(BATCH, SEQ, B_HEADS, 2 * HEAD_DIM))
        ctx_out[4].append(jnp.concatenate([fin_f[:BATCH], fin_b[:BATCH]], axis=1))
        ctx_out[5].append(ctx[:, COL_KD * 512:(COL_KD + 1) * 512].reshape(BATCH, SEQ, D_HEADS, HEAD_DIM))
        ctx_out[6].append(ctx[:, COL_VD * 512:(COL_VD + 1) * 512].reshape(BATCH, SEQ, D_HEADS, HEAD_DIM))

    new = [jnp.stack(t, axis=1) for t in ctx_out]
    return (x_ctx.reshape(BATCH, SEQ, D_MODEL), x_lat.reshape(DEC_BATCH, DEC_SEQ, D_MODEL), *new)
```

```python
import functools
import math

import numpy as np
import jax
import jax.numpy as jnp
from jax import lax
from jax.experimental import pallas as pl
from jax.experimental.pallas import tpu as pltpu

F32 = jnp.float32
BF16 = jnp.bfloat16

D_MODEL = 2048
BATCH = 16
SEQ = 256
DEPTH = 2
DEC_BATCH = 2
DEC_SEQ = 4096
PAST_LEN = 256
GRID_W = 64
HEAD_DIM = 64
ROPE_BASE = 10000.0
A_HEADS = 8
A_KV_HEADS = 2
A_WINDOW = 128
B_HEADS = 4
C_WIDTH = 512
C_BLOCKS = 8
C_BW = C_WIDTH // C_BLOCKS
C_POW = 8.0
D_HEADS = 8
NA_ROWS = 8
NA_COLS = 16
N_EXPERTS = 64
TOP_K = 6
D_EXPERT = 512
ROUTED_SCALE = 2.5
ALPHA = (2.0 * DEPTH) ** 0.25
LN_EPS = 1e-5
NORM_EPS = 1e-6
NEG_INF = -1e30
SCALE = HEAD_DIM ** -0.5
LOG2E = math.log2(math.e)

N_CTX = BATCH * SEQ
N_LAT = DEC_BATCH * DEC_SEQ
N_TOK = N_CTX + N_LAT
GROUP_ROWS = 4096
N_GROUPS = N_TOK // GROUP_ROWS
IN_WIDTH = 4864
LANES = 128
VMEM_LIMIT = 56 * 1024 * 1024

COL_QA, COL_QB, COL_KB, COL_VB, COL_XC, COL_GC, COL_QD, COL_KD, COL_VD = range(9)
COL_KA, COL_VA = 36, 37

TM_PROJ = 512
TN_PROJ = IN_WIDTH // 2
TM_PREP = 256
TQ_A = A_WINDOW
TQ_B = 256
NA_QR = 4
NA_KR = 12
T_SCAN = 256
TM_OUT = 256
MOE_BM = 256
TM_DISPATCH = 256
TM_FFN = 256


def _params(*sem):
    return pltpu.CompilerParams(dimension_semantics=sem, vmem_limit_bytes=VMEM_LIMIT)


def _nt(a, b):
    return lax.dot_general(a, b, (((1,), (1,)), ((), ())), preferred_element_type=F32)


def _mm(a, b):
    return jnp.dot(a, b, preferred_element_type=F32)


def _softmax_unnorm(parts, sink=None):
    m = parts[0].max(axis=-1, keepdims=True)
    for p in parts[1:]:
        m = jnp.maximum(m, p.max(axis=-1, keepdims=True))
    if sink is not None:
        m = jnp.maximum(m, sink)
    es = [jnp.exp(p - m) for p in parts]
    den = es[0].sum(axis=-1, keepdims=True)
    for e in es[1:]:
        den = den + e.sum(axis=-1, keepdims=True)
    if sink is not None:
        den = den + jnp.exp(sink - m)
    return es, den


def _rms_gain(x, g):
    return x * lax.rsqrt(jnp.mean(x * x, axis=-1, keepdims=True) + NORM_EPS) * g


def _layernorm(z, g, b):
    zc = z - jnp.mean(z, axis=-1, keepdims=True)
    var = jnp.mean(zc * zc, axis=-1, keepdims=True)
    return zc * lax.rsqrt(var + LN_EPS) * g + b


def _silu(x):
    return x * jax.nn.sigmoid(x)


def _low_half(shape):
    return lax.broadcasted_iota(jnp.int32, shape, 1) < HEAD_DIM


HALF_D = D_MODEL // 2
HIGH16 = 0xFFFF0000


def _pack_rows(x):
    bits = pltpu.bitcast(x.astype(BF16).astype(F32), jnp.uint32)
    return (bits[:, :HALF_D] >> 16) | (bits[:, HALF_D:] & jnp.uint32(HIGH16))


def _unpack_rows(p):
    return (pltpu.bitcast(p << 16, F32), pltpu.bitcast(p & jnp.uint32(HIGH16), F32))


SUBLANES = 8
assert HALF_D == SUBLANES * LANES


def _store_token_tiles(ref, index, packed, first=0):
    m = packed.shape[0]
    for s in range(SUBLANES):
        rows = pl.ds(first * SUBLANES + s, m, stride=SUBLANES)
        ref[index + (rows, slice(None))] = packed[:, s * LANES:(s + 1) * LANES]


def _load_token_tiles(ref, index, m):
    return jnp.concatenate([ref[index + (pl.ds(s, m, stride=SUBLANES), slice(None))] for s in range(SUBLANES)],
                           axis=1)


def _mod_kernel(c_ref, w_ref, b_ref, o_ref):
    c = c_ref[...]
    o_ref[0] = jnp.dot(_silu(c), w_ref[0], precision=lax.Precision.HIGHEST,
                       preferred_element_type=F32) + b_ref[0]


def _modulation(cond8, w_mod, b_mod):
    tn = 1024
    n = w_mod.shape[-1]
    return pl.pallas_call(
        _mod_kernel,
        grid=(DEPTH, n // tn),
        in_specs=[pl.BlockSpec((8, D_MODEL), lambda l, j: (0, 0)),
                  pl.BlockSpec((1, D_MODEL, tn), lambda l, j: (l, 0, j)),
                  pl.BlockSpec((1, 1, tn), lambda l, j: (l, 0, j))],
        out_specs=pl.BlockSpec((1, 8, tn), lambda l, j: (l, 0, j)),
        out_shape=jax.ShapeDtypeStruct((DEPTH, 8, n), F32),
        compiler_params=_params("parallel", "parallel"),
        name="modulation",
    )(cond8, w_mod, b_mod.reshape(DEPTH, 1, n))


def _mod_spec(which, tm):
    return pl.BlockSpec((None, None, 1, D_MODEL), lambda i, *_: (i * tm // GROUP_ROWS, which, 0, 0))


def _ctx_rows(tm):
    return pl.BlockSpec((tm, D_MODEL), lambda i, *_: (jnp.minimum(i, N_CTX // tm - 1), 0))


def _lat_rows(tm):
    return pl.BlockSpec((tm, D_MODEL), lambda i, *_: (jnp.maximum(i - N_CTX // tm, 0), 0))


def _is_ctx_tile(tm):
    return pl.program_id(0) * tm < N_CTX


def _inproj_kernel(xc_ref, xl_ref, sc_ref, sh_ref, w_ref, o_ref, xb_ref):
    @pl.when(pl.program_id(1) == 0)
    def _():
        x = jnp.where(_is_ctx_tile(xc_ref.shape[0]), xc_ref[...], xl_ref[...])
        xb_ref[...] = (x * (1.0 + sc_ref[...]) + sh_ref[...]).astype(BF16)

    o_ref[...] = _mm(xb_ref[...], w_ref[...])


def _in_projection(x_ctx, x_lat, mod, w_in_b):
    tm, tn = TM_PROJ, TN_PROJ
    return pl.pallas_call(
        _inproj_kernel,
        grid=(N_TOK // tm, IN_WIDTH // tn),
        in_specs=[_ctx_rows(tm), _lat_rows(tm),
                  _mod_spec(1, tm), _mod_spec(0, tm),
                  pl.BlockSpec((D_MODEL, tn), lambda i, j: (0, j))],
        out_specs=pl.BlockSpec((tm, tn), lambda i, j: (i, j)),
        out_shape=jax.ShapeDtypeStruct((N_TOK, IN_WIDTH), F32),
        scratch_shapes=[pltpu.VMEM((tm, D_MODEL), BF16)],
        compiler_params=_params("parallel", "arbitrary"),
        name="in_projection",
    )(x_ctx, x_lat, mod, mod, w_in_b)


def _ctx_attn_kernel(scal_ref, qa_ref, ka_ref, va_ref, qb_ref, kb_ref, vb_ref, qd_ref, kd_ref, vd_ref,
                     ga_ref, gb_ref, gd_ref, oa_ref, ob_ref, od_ref, *, lam_init):
    L = SEQ
    lo = _low_half((L, LANES))

    ka = ka_ref[...].astype(BF16)
    va = va_ref[...].astype(BF16)
    qa = qa_ref[...]
    chunks = []
    for c in range(A_HEADS // 2):
        hk = c // 2
        chunk = qa[:, c * LANES:(c + 1) * LANES]
        keep = lo if hk == 0 else jnp.logical_not(lo)
        halves = []
        for half in range(2):
            x = chunk if half == hk else pltpu.roll(chunk, HEAD_DIM, 1)
            qh = jnp.where(keep, x, 0.0).astype(BF16)
            s = _nt(qh, ka) * SCALE
            (e,), den = _softmax_unnorm([s], scal_ref[2 * c + half])
            o = _mm(e.astype(BF16), va) / den
            halves.append(o if half == hk else pltpu.roll(o, HEAD_DIM, 1))
        chunks.append(jnp.where(lo, halves[0], halves[1]))
    oa = jnp.concatenate(chunks, axis=1)
    oa_ref[...] = _rms_gain(oa, ga_ref[...]).astype(BF16)

    lam = scal_ref[A_HEADS]
    qb = qb_ref[...]
    chunks = []
    for h in range(B_HEADS):
        sl = slice(h * LANES, (h + 1) * LANES)
        qc = qb[:, sl]
        kc = kb_ref[:, sl].astype(BF16)
        vh = vb_ref[:, sl].astype(BF16)
        (e1,), d1 = _softmax_unnorm([_nt(jnp.where(lo, qc, 0.0).astype(BF16), kc) * SCALE])
        (e2,), d2 = _softmax_unnorm([_nt(jnp.where(lo, 0.0, qc).astype(BF16), kc) * SCALE])
        w = e1 * (1.0 / d1) - e2 * (lam / d2)
        o = _mm(w.astype(BF16), vh)
        chunks.append(_rms_gain(o, gb_ref[:, sl]) * (1.0 - lam_init))
    ob_ref[...] = jnp.concatenate(chunks, axis=1).astype(BF16)

    qd = qd_ref[...]
    chunks = []
    for c in range(D_HEADS // 2):
        sl = slice(c * LANES, (c + 1) * LANES)
        qc = qd[:, sl]
        kc = kd_ref[:, sl].astype(BF16)
        vc = vd_ref[:, sl].astype(BF16)
        halves = []
        for half in range(2):
            qh = jnp.where(lo if half == 0 else jnp.logical_not(lo), qc, 0.0).astype(BF16)
            (e,), den = _softmax_unnorm([_nt(qh, kc) * SCALE])
            halves.append(_mm(e.astype(BF16), vc) / den)
        chunks.append(jnp.where(lo, halves[0], halves[1]))
    od = jnp.concatenate(chunks, axis=1)
    od_ref[...] = _rms_gain(od, gd_ref[...]).astype(BF16)


def _ctx_attention(slab, scal, g_mix2d, lam_init):
    L = SEQ

    def wide(col):
        return pl.BlockSpec((L, 512), lambda b: (b, col))

    def narrow(col):
        return pl.BlockSpec((L, LANES), lambda b: (b, col))

    def gain(col):
        return pl.BlockSpec((1, 512), lambda b: (0, col))

    out = jax.ShapeDtypeStruct((N_TOK, 512), BF16)
    ospec = pl.BlockSpec((L, 512), lambda b: (b, 0))
    return pl.pallas_call(
        functools.partial(_ctx_attn_kernel, lam_init=lam_init),
        grid=(BATCH,),
        in_specs=[pl.BlockSpec(memory_space=pltpu.SMEM),
                  wide(COL_QA), narrow(COL_KA), narrow(COL_VA),
                  wide(COL_QB), wide(COL_KB), wide(COL_VB),
                  wide(COL_QD), wide(COL_KD), wide(COL_VD),
                  gain(0), gain(1), gain(3)],
        out_specs=[ospec, ospec, ospec],
        out_shape=[out, out, out],
        compiler_params=_params("parallel"),
        name="ctx_attention",
    )(scal, slab, slab, slab, slab, slab, slab, slab, slab, slab, g_mix2d, g_mix2d, g_mix2d)


def _prep_kernel(qa_ref, ka_ref, va_ref, qb_ref, kb_ref, vb_ref, qd_ref, kd_ref, vd_ref,
                 cos_ref, sa_ref, sb_ref,
                 qa_o, ka_o, va_o, qb_o, kb_o, vb_o, qd_o, kd_o, vd_o):
    cos, sa, sb = cos_ref[...], sa_ref[...], sb_ref[...]
    lo = _low_half(cos.shape)
    hi = jnp.logical_not(lo)

    def rope(x):
        return (x * cos + pltpu.roll(x, LANES - HEAD_DIM // 4, 1) * sa
                + pltpu.roll(x, HEAD_DIM // 4, 1) * sb)

    ka_o[...] = rope(ka_ref[...]).astype(BF16)
    va_o[...] = va_ref[...].astype(BF16)
    vb_o[...] = vb_ref[...].astype(BF16)
    kd_o[...] = kd_ref[...].astype(BF16)
    vd_o[...] = vd_ref[...].astype(BF16)

    for c in range(A_HEADS // 2):
        hk = c // 2
        r = rope(qa_ref[:, c * LANES:(c + 1) * LANES])
        keep = lo if hk == 0 else hi
        for half in range(2):
            h = 2 * c + half
            x = r if half == hk else pltpu.roll(r, HEAD_DIM, 1)
            qa_o[:, h * LANES:(h + 1) * LANES] = jnp.where(keep, x, 0.0).astype(BF16)
    for h in range(B_HEADS):
        sl = slice(h * LANES, (h + 1) * LANES)
        r = rope(qb_ref[:, sl])
        kb_o[:, sl] = rope(kb_ref[:, sl]).astype(BF16)
        qb_o[:, (2 * h) * LANES:(2 * h + 1) * LANES] = jnp.where(lo, r, 0.0).astype(BF16)
        qb_o[:, (2 * h + 1) * LANES:(2 * h + 2) * LANES] = jnp.where(hi, r, 0.0).astype(BF16)
    for c in range(D_HEADS // 2):
        x = qd_ref[:, c * LANES:(c + 1) * LANES]
        qd_o[:, (2 * c) * LANES:(2 * c + 1) * LANES] = jnp.where(lo, x, 0.0).astype(BF16)
        qd_o[:, (2 * c + 1) * LANES:(2 * c + 2) * LANES] = jnp.where(hi, x, 0.0).astype(BF16)


def _latent_prep(slab, cos, sa, sb):
    tm = TM_PREP
    off = N_CTX // tm
    per_seq = DEC_SEQ // tm

    def wide(col):
        return pl.BlockSpec((tm, 512), lambda i: (i + off, col))

    def narrow(col):
        return pl.BlockSpec((tm, LANES), lambda i: (i + off, col))

    tab = pl.BlockSpec((tm, LANES), lambda i: (i % per_seq, 0))

    def out(width):
        return (pl.BlockSpec((tm, width), lambda i: (i, 0)), jax.ShapeDtypeStruct((N_LAT, width), BF16))

    outs = [out(1024), out(LANES), out(LANES), out(1024), out(512), out(512), out(1024), out(512), out(512)]
    return pl.pallas_call(
        _prep_kernel,
        grid=(N_LAT // tm,),
        in_specs=[wide(COL_QA), narrow(COL_KA), narrow(COL_VA), wide(COL_QB), wide(COL_KB), wide(COL_VB),
                  wide(COL_QD), wide(COL_KD), wide(COL_VD), tab, tab, tab],
        out_specs=[o[0] for o in outs],
        out_shape=[o[1] for o in outs],
        compiler_params=_params("parallel"),
        name="latent_prep",
    )(slab, slab, slab, slab, slab, slab, slab, slab, slab, cos, sa, sb)


def _win_attn_kernel(scal_ref, q_ref, kp_ref, kc_ref, kn_ref, vp_ref, vc_ref, vn_ref, kctx_ref, vctx_ref,
                     g_ref, _all_rows_ref, o_ref):
    n = pl.program_id(1)
    nb = pl.num_programs(1)
    tq = TQ_A
    group = A_HEADS // A_KV_HEADS
    k = jnp.concatenate([kp_ref[...], kc_ref[...], kn_ref[...], kctx_ref[...].astype(BF16)], axis=0)
    v = jnp.concatenate([vp_ref[...], vc_ref[...], vn_ref[...], vctx_ref[...].astype(BF16)], axis=0)
    nk = 3 * tq + PAST_LEN
    row = lax.broadcasted_iota(jnp.int32, (group * tq, nk), 0) & (tq - 1)
    col = lax.broadcasted_iota(jnp.int32, (group * tq, nk), 1)
    prev_thr = row + jnp.where(n > 0, 0, tq)
    next_thr = row + 2 * tq - jnp.where(n < nb - 1, 0, tq)
    masked = ((col < tq) & (col < prev_thr)) | ((col >= 2 * tq) & (col < 3 * tq) & (col > next_thr))
    valid = jnp.logical_not(masked)
    lo = _low_half((tq, LANES))
    heads = []
    for hk in range(A_KV_HEADS):
        q4 = jnp.concatenate([q_ref[:, (hk * group + g) * LANES:(hk * group + g + 1) * LANES]
                              for g in range(group)], axis=0)
        s = jnp.where(valid, _nt(q4, k) * SCALE, NEG_INF)
        sink = jnp.concatenate([jnp.full((tq, 1), scal_ref[hk * group + g], F32) for g in range(group)], axis=0)
        (e,), den = _softmax_unnorm([s], sink)
        o = _mm(e.astype(BF16), v) / den
        for g in range(group):
            h = hk * group + g
            og = o[g * tq:(g + 1) * tq]
            heads.append(og if (h % 2) == hk else pltpu.roll(og, HEAD_DIM, 1))
    oa = jnp.concatenate([jnp.where(lo, heads[2 * c], heads[2 * c + 1]) for c in range(A_HEADS // 2)], axis=1)
    o_ref[...] = _rms_gain(oa, g_ref[...]).astype(BF16)


def _window_attention(qa_p, ka_r, va_b, cache_k, cache_v, scal, g_mix2d, o_all):
    tq = TQ_A
    nb = DEC_SEQ // tq

    def band(d):
        return pl.BlockSpec((tq, LANES), lambda b, n: (b * nb + jnp.clip(n + d, 0, nb - 1), 0))

    ctx = pl.BlockSpec((None, PAST_LEN, LANES), lambda b, n: (b, 0, 0))
    return pl.pallas_call(
        _win_attn_kernel,
        grid=(DEC_BATCH, nb),
        in_specs=[pl.BlockSpec(memory_space=pltpu.SMEM),
                  pl.BlockSpec((tq, A_HEADS * LANES), lambda b, n: (b * nb + n, 0)),
                  band(-1), band(0), band(1), band(-1), band(0), band(1), ctx, ctx,
                  pl.BlockSpec((1, 512), lambda b, n: (0, 0)),
                  pl.BlockSpec(memory_space=pl.ANY)],
        out_specs=pl.BlockSpec((tq, 512), lambda b, n: (N_CTX // tq + b * nb + n, 0)),
        out_shape=jax.ShapeDtypeStruct((N_TOK, 512), BF16),
        input_output_aliases={11: 0},
        compiler_params=_params("parallel", "parallel"),
        name="window_attention",
    )(scal, qa_p, ka_r, ka_r, ka_r, va_b, va_b, va_b, cache_k, cache_v, g_mix2d, o_all)


def _diff_attn_kernel(scal_ref, q_ref, k_ref, v_ref, kctx_ref, vctx_ref, g_ref, _all_rows_ref, o_ref, *,
                      lam_init):
    lam = scal_ref[A_HEADS]
    chunks = []
    for h in range(B_HEADS):
        sl = slice(h * LANES, (h + 1) * LANES)
        kh = k_ref[:, sl]
        kc = kctx_ref[:, sl].astype(BF16)
        es, dens = [], []
        for m in range(2):
            q = q_ref[:, (2 * h + m) * LANES:(2 * h + m + 1) * LANES]
            t_lat = _nt(q, kh) * (SCALE * LOG2E)
            t_ctx = _nt(q, kc) * (SCALE * LOG2E)
            mx = jnp.maximum(t_lat.max(axis=-1, keepdims=True), t_ctx.max(axis=-1, keepdims=True))
            e_lat = jnp.exp2(t_lat - mx)
            e_ctx = jnp.exp2(t_ctx - mx)
            es.append((e_lat, e_ctx))
            dens.append(e_lat.sum(axis=-1, keepdims=True) + e_ctx.sum(axis=-1, keepdims=True))
        ratio = lam * dens[0] / dens[1]
        w_lat = (es[0][0] - es[1][0] * ratio).astype(BF16)
        w_ctx = (es[0][1] - es[1][1] * ratio).astype(BF16)
        o = (_mm(w_lat, v_ref[:, sl]) + _mm(w_ctx, vctx_ref[:, sl].astype(BF16))) / dens[0]
        chunks.append(_rms_gain(o, g_ref[:, sl]) * (1.0 - lam_init))
    o_ref[...] = jnp.concatenate(chunks, axis=1).astype(BF16)


def _diff_attention(qb_p, kb_r, vb_b, cache_k, cache_v, scal, g_mix2d, lam_init, o_all):
    tq = TQ_B
    nb = DEC_SEQ // tq
    full = pl.BlockSpec((DEC_SEQ, 512), lambda b, n: (b, 0))
    ctx = pl.BlockSpec((None, PAST_LEN, 512), lambda b, n: (b, 0, 0))
    return pl.pallas_call(
        functools.partial(_diff_attn_kernel, lam_init=lam_init),
        grid=(DEC_BATCH, nb),
        in_specs=[pl.BlockSpec(memory_space=pltpu.SMEM),
                  pl.BlockSpec((tq, 2 * B_HEADS * LANES), lambda b, n: (b * nb + n, 0)),
                  full, full, ctx, ctx,
                  pl.BlockSpec((1, 512), lambda b, n: (0, 1)),
                  pl.BlockSpec(memory_space=pl.ANY)],
        out_specs=pl.BlockSpec((tq, 512), lambda b, n: (N_CTX // tq + b * nb + n, 0)),
        out_shape=jax.ShapeDtypeStruct((N_TOK, 512), BF16),
        input_output_aliases={7: 0},
        compiler_params=_params("parallel", "parallel"),
        name="diff_attention",
    )(scal, qb_p, kb_r, vb_b, cache_k, cache_v, g_mix2d, o_all)


def _na_window_start(step):
    return np.clip(step * NA_QR - NA_ROWS // 2, 0, GRID_W - NA_KR)


def _na_bias_indices():
    rows = DEC_SEQ // GRID_W
    steps = rows // NA_QR
    pats = []
    for step in range(steps):
        w0 = _na_window_start(step)
        r = step * NA_QR + np.arange(NA_QR)[:, None, None, None]
        c = np.arange(GRID_W)[None, :, None, None]
        kr = w0 + np.arange(NA_KR)[None, None, :, None]
        kc = np.arange(GRID_W)[None, None, None, :]
        kr0 = np.clip(r - NA_ROWS // 2, 0, rows - NA_ROWS)
        kc0 = np.clip(c - NA_COLS // 2, 0, GRID_W - NA_COLS)
        valid = (kr >= kr0) & (kr < kr0 + NA_ROWS) & (kc >= kc0) & (kc < kc0 + NA_COLS)
        drow = np.clip(kr - r + NA_ROWS - 1, 0, 2 * NA_ROWS - 2)
        dcol = np.clip(kc - c + NA_COLS - 1, 0, 2 * NA_COLS - 2)
        shape = (NA_QR * GRID_W, NA_KR * GRID_W)
        full = np.broadcast_to
        pats.append((full(drow, valid.shape).reshape(shape), full(dcol, valid.shape).reshape(shape),
                     valid.reshape(shape)))
    for step in range(2, steps - 1):
        for a, b in zip(pats[1], pats[step]):
            assert np.array_equal(a, b)
    kinds = [pats[0], pats[1], pats[-1]]
    drow, dcol, valid = (np.stack([k[i] for k in kinds]) for i in range(3))
    shape6 = (len(kinds), NA_QR, GRID_W, NA_KR, GRID_W)
    valid6 = valid.reshape(shape6)
    row_ok = valid6.any(axis=(2, 4))
    col_ok = valid6.any(axis=(0, 1, 3))
    assert np.array_equal(valid6, row_ok[:, :, None, :, None] & col_ok[None, None, :, None, :])
    outside = 2 * NA_ROWS - 1
    row_idx = np.where(row_ok, drow.reshape(shape6)[:, :, 0, :, 0], outside)
    col_sel = np.eye(2 * NA_COLS - 1, dtype=np.float32)[dcol.reshape(shape6)[0, 0, :, 0, :]]
    return row_idx, col_sel, col_ok


def _na_bias(rpb, row_idx, col_sel, col_ok):
    cols = jnp.einsum('hab,cdb->hacd', rpb, col_sel, precision=lax.Precision.HIGHEST)
    cols = jnp.where(col_ok[None, None], cols, NEG_INF)
    cols = jnp.concatenate([cols, jnp.full((D_HEADS, 1, GRID_W, GRID_W), NEG_INF, F32)], axis=1)
    kinds = []
    for kind in row_idx:
        kinds.append(jnp.concatenate(
            [jnp.concatenate([cols[:, int(a)] for a in per_row], axis=-1) for per_row in kind], axis=1))
    return jnp.stack(kinds)


def _na_kernel(q_ref, k_ref, v_ref, kctx_ref, vctx_ref, bias_ref, g_ref, _all_rows_ref, o_ref):
    step = pl.program_id(1)
    nq = NA_QR * GRID_W
    nk = NA_KR * GRID_W
    w0 = jnp.clip(step * NA_QR - NA_ROWS // 2, 0, GRID_W - NA_KR)
    start = pl.multiple_of(w0 * GRID_W, GRID_W)
    lo = _low_half((nq, LANES))
    chunks = []
    for c in range(D_HEADS // 2):
        sl = slice(c * LANES, (c + 1) * LANES)
        kw = k_ref[pl.ds(start, nk), sl]
        vw = v_ref[pl.ds(start, nk), sl]
        kc = kctx_ref[:, sl].astype(BF16)
        vc = vctx_ref[:, sl].astype(BF16)
        halves = []
        for half in range(2):
            h = 2 * c + half
            q = q_ref[:, h * LANES:(h + 1) * LANES]
            s_loc = _nt(q, kw) * SCALE + bias_ref[h]
            s_ctx = _nt(q, kc) * SCALE
            (e_loc, e_ctx), den = _softmax_unnorm([s_loc, s_ctx])
            halves.append((_mm(e_loc.astype(BF16), vw) + _mm(e_ctx.astype(BF16), vc)) / den)
        chunks.append(jnp.where(lo, halves[0], halves[1]))
    od = jnp.concatenate(chunks, axis=1)
    o_ref[...] = _rms_gain(od, g_ref[...]).astype(BF16)


def _neighbourhood_attention(qd_p, kd_b, vd_b, cache_k, cache_v, bias, g_mix2d, o_all):
    nq = NA_QR * GRID_W
    steps = DEC_SEQ // nq
    full = pl.BlockSpec((DEC_SEQ, 512), lambda b, j: (b, 0))
    ctx = pl.BlockSpec((None, PAST_LEN, 512), lambda b, j: (b, 0, 0))

    def kind(b, j):
        return (jnp.where(j == 0, 0, jnp.where(j == steps - 1, 2, 1)), 0, 0, 0)

    return pl.pallas_call(
        _na_kernel,
        grid=(DEC_BATCH, steps),
        in_specs=[pl.BlockSpec((nq, D_HEADS * LANES), lambda b, j: (b * steps + j, 0)),
                  full, full, ctx, ctx,
                  pl.BlockSpec((None, D_HEADS, nq, NA_KR * GRID_W), kind),
                  pl.BlockSpec((1, 512), lambda b, j: (0, 3)),
                  pl.BlockSpec(memory_space=pl.ANY)],
        out_specs=pl.BlockSpec((nq, 512), lambda b, j: (N_CTX // nq + b * steps + j, 0)),
        out_shape=jax.ShapeDtypeStruct((N_TOK, 512), BF16),
        input_output_aliases={7: 0},
        compiler_params=_params("parallel", "arbitrary"),
        name="neighbourhood_attention",
    )(qd_p, kd_b, vd_b, cache_k, cache_v, bias, g_mix2d, o_all)


N_SCAN_TILES = N_TOK // T_SCAN
CTX_TILES = N_CTX // T_SCAN
TILES_PER_LAT = DEC_SEQ // T_SCAN


def _scan_tile_flags(i):
    is_ctx = i < CTX_TILES
    pos = (i - CTX_TILES) % TILES_PER_LAT
    return is_ctx | (pos == 0), is_ctx | (pos == TILES_PER_LAT - 1)


def _scan_seq(i):
    return jnp.where(i < CTX_TILES, i, CTX_TILES + (i - CTX_TILES) // TILES_PER_LAT)


def _conv_gates(x_ref, prev_ref, next_ref, cw_ref, cb_ref, wgh_ref, wgl_ref, bg_ref, lam_ref, first, last):
    x = x_ref[...]
    t = T_SCAN
    row = lax.broadcasted_iota(jnp.int32, x.shape, 0)
    pm1 = jnp.where(first, 0.0, prev_ref[7:8, :])
    n0 = jnp.where(last, 0.0, next_ref[0:1, :])
    n1 = jnp.where(last, 0.0, next_ref[1:2, :])
    x_m1 = jnp.where(row == 0, pm1, pltpu.roll(x, 1, 0))
    x_p1 = jnp.where(row == t - 1, n0, pltpu.roll(x, t - 1, 0))
    x_p2 = jnp.where(row == t - 2, n0, jnp.where(row == t - 1, n1, pltpu.roll(x, t - 2, 0)))
    y = (cw_ref[0:1, :] * x_m1 + cw_ref[1:2, :] * x + cw_ref[2:3, :] * x_p1 + cw_ref[3:4, :] * x_p2
         + cb_ref[...])
    y_hi = y.astype(BF16)
    y_lo = (y - y_hi.astype(F32)).astype(BF16)
    zr, zi = [], []
    for c in range(C_WIDTH // LANES):
        sl = slice(c * LANES, (c + 1) * LANES)
        w_hi = wgh_ref[0, sl, :]
        z = _mm(y_hi[:, sl], w_hi) + _mm(y_lo[:, sl], w_hi) + _mm(y_hi[:, sl], wgl_ref[0, sl, :])
        zr.append(z[:, :LANES])
        zi.append(z[:, LANES:])
    bias = bg_ref[0]
    r = jax.nn.sigmoid(jnp.concatenate(zr, axis=1) + bias[:, :C_WIDTH])
    gate_i = jax.nn.sigmoid(jnp.concatenate(zi, axis=1) + bias[:, C_WIDTH:])
    nl = -lam_ref[0]
    softplus = jnp.maximum(nl, 0.0) + jnp.log1p(jnp.exp(-jnp.abs(nl)))
    log_a = -C_POW * r * softplus
    a = jnp.exp(log_a)
    u = jnp.sqrt(1.0 - jnp.exp(2.0 * log_a)) * gate_i * y
    return a, u


def _scan_rows(a_s, u_s, h_s, h, reverse):
    groups = T_SCAN // SUBLANES
    row = lax.broadcasted_iota(jnp.int32, (SUBLANES, C_WIDTH), 0)

    def body(i, h):
        g = (groups - 1 - i) if reverse else i
        base = pl.multiple_of(g * SUBLANES, SUBLANES)
        a = a_s[pl.ds(base, SUBLANES), :]
        u = u_s[pl.ds(base, SUBLANES), :]
        for d in (1, 2, 4):
            has_src = (row < SUBLANES - d) if reverse else (row >= d)
            shift = SUBLANES - d if reverse else d
            a_src = jnp.where(has_src, pltpu.roll(a, shift, 0), 1.0)
            u_src = jnp.where(has_src, pltpu.roll(u, shift, 0), 0.0)
            u = u + a * u_src
            a = a * a_src
        hs = u + a * h
        h_s[pl.ds(base, SUBLANES), :] = hs
        return hs[0:1, :] if reverse else hs[SUBLANES - 1:SUBLANES, :]

    return lax.fori_loop(0, groups, body, h, unroll=2)


def _scan_fwd_kernel(x_ref, prev_ref, next_ref, cw_ref, cb_ref, wgh_ref, wgl_ref, bg_ref, lam_ref, h0_ref,
                     hf_ref, fin_ref, a_s, u_s, carry_s):
    i = pl.program_id(0)
    first, last = _scan_tile_flags(i)
    a, u = _conv_gates(x_ref, prev_ref, next_ref, cw_ref, cb_ref, wgh_ref, wgl_ref, bg_ref, lam_ref, first, last)
    a_s[...] = a
    u_s[...] = u

    @pl.when(first)
    def _():
        carry_s[...] = h0_ref[...]

    h = _scan_rows(a_s, u_s, hf_ref, carry_s[...], reverse=False)
    carry_s[...] = h
    fin_ref[...] = h


def _scan_bwd_kernel(x_ref, prev_ref, next_ref, cw_ref, cb_ref, wgh_ref, wgl_ref, bg_ref, lam_ref, h0_ref,
                     hf_ref, gc_ref, g_ref, oc_ref, fin_ref, a_s, u_s, hb_s, carry_s):
    i = N_SCAN_TILES - 1 - pl.program_id(0)
    first, last = _scan_tile_flags(i)
    a, u = _conv_gates(x_ref, prev_ref, next_ref, cw_ref, cb_ref, wgh_ref, wgl_ref, bg_ref, lam_ref, first, last)
    a_s[...] = a
    u_s[...] = u

    @pl.when(last)
    def _():
        carry_s[...] = h0_ref[...]

    h = _scan_rows(a_s, u_s, hb_s, carry_s[...], reverse=True)
    carry_s[...] = h
    fin_ref[...] = h
    g = gc_ref[...]
    gelu = 0.5 * g * (1.0 + jnp.tanh(math.sqrt(2.0 / math.pi) * (g + 0.044715 * (g * g * g))))
    oc_ref[...] = _rms_gain(gelu * (hf_ref[...] + hb_s[...]), g_ref[...]).astype(BF16)


def _scan_common_specs(tile_of, direction):
    halo = T_SCAN // 8
    last_block = N_TOK // 8 - 1
    return [pl.BlockSpec((T_SCAN, 512), lambda j: (tile_of(j), COL_XC)),
            pl.BlockSpec((8, 512), lambda j: (jnp.maximum(tile_of(j) * halo - 1, 0), COL_XC)),
            pl.BlockSpec((8, 512), lambda j: (jnp.minimum((tile_of(j) + 1) * halo, last_block), COL_XC)),
            pl.BlockSpec((4, C_WIDTH), lambda j: (0, 0)),
            pl.BlockSpec((1, C_WIDTH), lambda j: (0, 0)),
            pl.BlockSpec((1, C_WIDTH, 2 * LANES), lambda j: (direction, 0, 0)),
            pl.BlockSpec((1, C_WIDTH, 2 * LANES), lambda j: (direction, 0, 0)),
            pl.BlockSpec((1, 1, 2 * C_WIDTH), lambda j: (direction, 0, 0)),
            pl.BlockSpec((1, 1, C_WIDTH), lambda j: (direction, 0, 0)),
            pl.BlockSpec((None, 1, C_WIDTH), lambda j: (_scan_seq(tile_of(j)), 0, 0))]


def _rglru(slab, conv_w, conv_b, w_gates, b_gates, lam, h0_f, h0_b, g_mix2d):
    wg_hi = w_gates.astype(BF16)
    wg_lo = (w_gates - wg_hi.astype(F32)).astype(BF16)
    fin = jax.ShapeDtypeStruct((N_SCAN_TILES, 1, C_WIDTH), F32)
    fwd_tile = lambda j: j
    h_f, fin_f = pl.pallas_call(
        _scan_fwd_kernel,
        grid=(N_SCAN_TILES,),
        in_specs=_scan_common_specs(fwd_tile, 0),
        out_specs=[pl.BlockSpec((T_SCAN, C_WIDTH), lambda j: (j, 0)),
                   pl.BlockSpec((None, 1, C_WIDTH), lambda j: (j, 0, 0))],
        out_shape=[jax.ShapeDtypeStruct((N_TOK, C_WIDTH), F32), fin],
        scratch_shapes=[pltpu.VMEM((T_SCAN, C_WIDTH), F32), pltpu.VMEM((T_SCAN, C_WIDTH), F32),
                        pltpu.VMEM((1, C_WIDTH), F32)],
        compiler_params=_params("arbitrary"),
        name="rglru_forward",
    )(slab, slab, slab, conv_w, conv_b, wg_hi, wg_lo, b_gates, lam, h0_f)
    bwd_tile = lambda j: N_SCAN_TILES - 1 - j
    oc, fin_b = pl.pallas_call(
        _scan_bwd_kernel,
        grid=(N_SCAN_TILES,),
        in_specs=_scan_common_specs(bwd_tile, 1) + [
            pl.BlockSpec((T_SCAN, C_WIDTH), lambda j: (bwd_tile(j), 0)),
            pl.BlockSpec((T_SCAN, 512), lambda j: (bwd_tile(j), COL_GC)),
            pl.BlockSpec((1, 512), lambda j: (0, 2))],
        out_specs=[pl.BlockSpec((T_SCAN, C_WIDTH), lambda j: (bwd_tile(j), 0)),
                   pl.BlockSpec((None, 1, C_WIDTH), lambda j: (bwd_tile(j), 0, 0))],
        out_shape=[jax.ShapeDtypeStruct((N_TOK, C_WIDTH), BF16), fin],
        scratch_shapes=[pltpu.VMEM((T_SCAN, C_WIDTH), F32), pltpu.VMEM((T_SCAN, C_WIDTH), F32),
                        pltpu.VMEM((T_SCAN, C_WIDTH), F32), pltpu.VMEM((1, C_WIDTH), F32)],
        compiler_params=_params("arbitrary"),
        name="rglru_backward",
    )(slab, slab, slab, conv_w, conv_b, wg_hi, wg_lo, b_gates, lam, h0_b, h_f, slab, g_mix2d)
    return oc, fin_f, fin_b


def _mixout_kernel(oa_ref, ob_ref, oc_ref, od_ref, w_ref, xc_ref, xl_ref, g1_ref, sc2_ref, sh2_ref, lng_ref,
                   lnb_ref, wrh_ref, wrl_ref, br_ref, x1_ref, h2_ref, idx_ref, wt_ref, rank_ref, cnt_ref, cnt_s):
    @pl.when(pl.program_id(0) == 0)
    def _():
        cnt_s[...] = jnp.zeros(cnt_s.shape, F32)

    parts = 2
    pm = xc_ref.shape[0] // parts
    is_ctx = _is_ctx_tile(xc_ref.shape[0])
    for part in range(parts):
        rows = pl.ds(part * pm, pm)
        x = jnp.where(is_ctx, xc_ref[rows, :], xl_ref[rows, :])
        _mixout_rows(rows, part * pm, pm, oa_ref, ob_ref, oc_ref, od_ref, w_ref, x, g1_ref,
                     sc2_ref, sh2_ref, lng_ref, lnb_ref, wrh_ref, wrl_ref, br_ref, x1_ref, h2_ref, idx_ref,
                     wt_ref, rank_ref, cnt_s)
    cnt_ref[...] = cnt_s[...]


def _mixout_rows(rows, first, tm, oa_ref, ob_ref, oc_ref, od_ref, w_ref, x, g1_ref, sc2_ref, sh2_ref,
                 lng_ref, lnb_ref, wrh_ref, wrl_ref, br_ref, x1_ref, h2_ref, idx_ref, wt_ref, rank_ref, cnt_s):
    y = _mm(oa_ref[rows, :], w_ref[0:512, :])
    y += _mm(ob_ref[rows, :], w_ref[512:1024, :])
    y += _mm(oc_ref[rows, :], w_ref[1024:1536, :])
    y += _mm(od_ref[rows, :], w_ref[1536:2048, :])
    x1 = _layernorm(ALPHA * x + g1_ref[...] * y, lng_ref[...], lnb_ref[...])
    x1_ref[rows, :] = x1
    h2 = x1 * (1.0 + sc2_ref[...]) + sh2_ref[...]
    _store_token_tiles(h2_ref, (), _pack_rows(h2), first)

    h_hi = h2.astype(BF16)
    h_lo = (h2 - h_hi.astype(F32)).astype(BF16)
    scores = jax.nn.sigmoid(_mm(h_hi, wrh_ref[...]) + _mm(h_lo, wrh_ref[...]) + _mm(h_hi, wrl_ref[...]))
    sel = scores + br_ref[...]
    lane = lax.broadcasted_iota(jnp.int32, sel.shape, 1).astype(F32)
    slot = lax.broadcasted_iota(jnp.int32, (tm, 8), 1)
    wide = lax.broadcasted_iota(jnp.int32, (tm, LANES), 1)
    idx_out = jnp.zeros((tm, LANES), F32)
    wt_out = jnp.zeros((tm, 8), F32)
    total = jnp.zeros((sel.shape[0], 1), F32)
    hits = []
    for k in range(TOP_K):
        m = sel.max(axis=-1, keepdims=True)
        idx = jnp.where(sel == m, lane, float(N_EXPERTS)).min(axis=-1, keepdims=True)
        hit = lane == idx
        hits.append(hit)
        w = jnp.where(hit, scores, 0.0).sum(axis=-1, keepdims=True)
        total = total + w
        idx_out = jnp.where(wide == k, idx, idx_out)
        wt_out = jnp.where(slot == k, w, wt_out)
        sel = jnp.where(hit, -jnp.inf, sel)
    cols = pl.ds(first, tm)
    idx_ref[:, cols] = idx_out.T[:8, :].astype(jnp.int32)
    wt_ref[rows, :] = wt_out / total * ROUTED_SCALE

    chosen = jnp.zeros(sel.shape, F32)
    for hit in hits:
        chosen = chosen + hit.astype(F32)
    earlier = (lax.broadcasted_iota(jnp.int32, (tm, tm), 1) < lax.broadcasted_iota(jnp.int32, (tm, tm), 0))
    before = _mm(earlier.astype(F32).astype(BF16), chosen.astype(BF16)) + cnt_s[...]
    rank_out = jnp.zeros((tm, LANES), F32)
    for k, hit in enumerate(hits):
        rank_out = jnp.where(wide == k, jnp.where(hit, before, 0.0).sum(axis=-1, keepdims=True), rank_out)
    rank_ref[:, cols] = rank_out.T[:8, :].astype(jnp.int32)
    cnt_s[...] = cnt_s[...] + chosen.sum(axis=0, keepdims=True)


def _mixer_out(oa, ob, oc, od, w_out_b, x_ctx, x_lat, mod, ln_g, ln_b, w_router, b_router):
    tm = TM_OUT
    cat = pl.BlockSpec((tm, 512), lambda i: (i, 0))
    row = pl.BlockSpec((tm, D_MODEL), lambda i: (i, 0))
    vec = pl.BlockSpec((1, D_MODEL), lambda i: (0, 0))
    k8 = pl.BlockSpec((tm, 8), lambda i: (i, 0))
    k_major = pl.BlockSpec((8, tm), lambda i: (0, i))
    per_expert = pl.BlockSpec((1, N_EXPERTS), lambda i: (0, 0))
    router = pl.BlockSpec((D_MODEL, N_EXPERTS), lambda i: (0, 0))
    wr_hi = w_router.astype(BF16)
    wr_lo = (w_router - wr_hi.astype(F32)).astype(BF16)
    return pl.pallas_call(
        _mixout_kernel,
        grid=(N_TOK // tm,),
        in_specs=[cat, cat, cat, cat,
                  pl.BlockSpec((D_MODEL, D_MODEL), lambda i: (0, 0)),
                  _ctx_rows(tm), _lat_rows(tm), _mod_spec(2, tm), _mod_spec(4, tm), _mod_spec(3, tm), vec, vec,
                  router, router, per_expert],
        out_specs=[row, pl.BlockSpec((tm * SUBLANES, LANES), lambda i: (i, 0)), k_major, k8, k_major, per_expert],
        out_shape=[jax.ShapeDtypeStruct((N_TOK, D_MODEL), F32),
                   jax.ShapeDtypeStruct((N_TOK * SUBLANES, LANES), jnp.uint32),
                   jax.ShapeDtypeStruct((8, N_TOK), jnp.int32), jax.ShapeDtypeStruct((N_TOK, 8), F32),
                   jax.ShapeDtypeStruct((8, N_TOK), jnp.int32), jax.ShapeDtypeStruct((1, N_EXPERTS), F32)],
        scratch_shapes=[pltpu.VMEM((1, N_EXPERTS), F32)],
        compiler_params=_params("arbitrary"),
        name="mixer_out_router",
    )(oa, ob, oc, od, w_out_b, x_ctx, x_lat, mod, mod, mod, ln_g, ln_b, wr_hi, wr_lo, b_router)


ROW_DMA_QUEUES = 2
MOE_BLOCKS = N_TOK * TOP_K // MOE_BM + N_EXPERTS
MOE_ROWS = MOE_BLOCKS * MOE_BM


def _row_copy(src, src_row, dst, dst_row, sem):
    def tile(row):
        return pl.ds(row * SUBLANES if isinstance(row, int) else pl.multiple_of(row * SUBLANES, SUBLANES), SUBLANES)

    return pltpu.make_async_copy(src.at[tile(src_row)], dst.at[tile(dst_row)], sem)


def _dispatch_kernel(slot_ref, h_ref, wg_ref, wu_ref, wd_ref, xg_ref, shared_ref, sem):
    tm = h_ref.shape[0] // SUBLANES
    base = pl.program_id(0) * tm

    for r in range(tm):
        for k in range(TOP_K):
            _row_copy(h_ref, r, xg_ref, slot_ref[base + (k * N_TOK + r)], sem).start(priority=k % ROW_DMA_QUEUES)
    lo, hi = _unpack_rows(_load_token_tiles(h_ref, (), tm))
    lo, hi = lo.astype(BF16), hi.astype(BF16)
    gate = _mm(lo, wg_ref[:HALF_D, :]) + _mm(hi, wg_ref[HALF_D:, :])
    up = _mm(lo, wu_ref[:HALF_D, :]) + _mm(hi, wu_ref[HALF_D:, :])
    shared_ref[...] = _mm((_silu(gate) * up).astype(BF16), wd_ref[...])
    for k in range(TOP_K):
        pltpu.make_async_copy(h_ref, xg_ref.at[pl.ds(0, tm * SUBLANES)], sem).wait()


def _dispatch_rows(slot_flat, h2, ws_g, ws_u, ws_d):
    tm = TM_DISPATCH
    grid_spec = pltpu.PrefetchScalarGridSpec(
        num_scalar_prefetch=1,
        grid=(N_TOK // tm,),
        in_specs=[pl.BlockSpec((tm * SUBLANES, LANES), lambda i, s: (i, 0)),
                  pl.BlockSpec((D_MODEL, D_EXPERT), lambda i, s: (0, 0)),
                  pl.BlockSpec((D_MODEL, D_EXPERT), lambda i, s: (0, 0)),
                  pl.BlockSpec((D_EXPERT, D_MODEL), lambda i, s: (0, 0))],
        out_specs=[pl.BlockSpec(memory_space=pl.ANY), pl.BlockSpec((tm, D_MODEL), lambda i, s: (i, 0))],
        scratch_shapes=[pltpu.SemaphoreType.DMA])
    return pl.pallas_call(
        _dispatch_kernel,
        grid_spec=grid_spec,
        out_shape=[jax.ShapeDtypeStruct((MOE_ROWS * SUBLANES, LANES), jnp.uint32),
                   jax.ShapeDtypeStruct((N_TOK, D_MODEL), F32)],
        compiler_params=_params("arbitrary"),
        name="moe_dispatch",
    )(slot_flat, h2, ws_g, ws_u, ws_d)


BLOCK_TILE_ROWS = MOE_BM * SUBLANES
EXPERT_X_BUFFERS = 3


def _expert_kernel(blk0_ref, nblk_ref, cnt_ref, wg_ref, wu_ref, wd_ref, x_hbm, y_hbm,
                   wg_s, wu_s, wd_s, xbuf, ybuf, xsem, ysem):
    e = pl.program_id(0)
    first_blk = blk0_ref[e]
    n_blk = nblk_ref[e]
    count = cnt_ref[e]
    total = blk0_ref[N_EXPERTS - 1] + nblk_ref[N_EXPERTS - 1]

    def rows_of(g):
        return pl.ds(pl.multiple_of(g * BLOCK_TILE_ROWS, BLOCK_TILE_ROWS), BLOCK_TILE_ROWS)

    def x_copy(g, buf):
        return pltpu.make_async_copy(x_hbm.at[rows_of(g)], xbuf.at[buf], xsem.at[buf])

    def y_copy(g, buf):
        return pltpu.make_async_copy(ybuf.at[buf], y_hbm.at[rows_of(g)], ysem.at[buf])

    for ahead in range(EXPERT_X_BUFFERS - 1):
        @pl.when((e == 0) & (total > ahead))
        def _():
            x_copy(ahead, ahead).start()

    wg_s[...] = wg_ref[...].astype(BF16)
    wu_s[...] = wu_ref[...].astype(BF16)
    wd_s[...] = wd_ref[...].astype(BF16)

    def block(j, carry):
        g = first_blk + j
        buf = g % 2
        xb = lax.rem(g, EXPERT_X_BUFFERS)
        nxt = g + (EXPERT_X_BUFFERS - 1)

        @pl.when(nxt < total)
        def _():
            x_copy(nxt, lax.rem(nxt, EXPERT_X_BUFFERS)).start()

        x_copy(g, xb).wait()

        @pl.when(g >= 2)
        def _():
            y_copy(g - 2, buf).wait()

        n_rows = count - j * MOE_BM

        def swiglu(m):
            x = _load_token_tiles(xbuf, (xb,), m)
            row = lax.broadcasted_iota(jnp.int32, x.shape, 0)
            lo, hi = _unpack_rows(jnp.where(row < n_rows, x, jnp.uint32(0)))
            lo, hi = lo.astype(BF16), hi.astype(BF16)
            gate = _mm(lo, wg_s[:HALF_D, :]) + _mm(hi, wg_s[HALF_D:, :])
            up = _mm(lo, wu_s[:HALF_D, :]) + _mm(hi, wu_s[HALF_D:, :])
            _store_token_tiles(ybuf, (buf,), _pack_rows(_mm((_silu(gate) * up).astype(BF16), wd_s[...])))

        @pl.when(n_rows > MOE_BM // 2)
        def _():
            swiglu(MOE_BM)

        @pl.when(n_rows <= MOE_BM // 2)
        def _():
            swiglu(MOE_BM // 2)
            _store_token_tiles(ybuf, (buf,), jnp.zeros((MOE_BM // 2, HALF_D), jnp.uint32), MOE_BM // 2)

        y_copy(g, buf).start()
        return carry

    lax.fori_loop(0, n_blk, block, 0)

    @pl.when((e == N_EXPERTS - 1) & (total >= 2))
    def _():
        y_copy(total - 2, total % 2).wait()

    @pl.when((e == N_EXPERTS - 1) & (total >= 1))
    def _():
        y_copy(total - 1, (total - 1) % 2).wait()


def _routed_experts(xg, first_blk, n_blk, counts, layer, w_g, w_u, w_d):
    def weight(shape):
        return pl.BlockSpec((None, None) + shape, lambda e, *_: (layer, e, 0, 0))

    any_space = pl.BlockSpec(memory_space=pl.ANY)
    x_bufs = pltpu.VMEM((EXPERT_X_BUFFERS, BLOCK_TILE_ROWS, LANES), jnp.uint32)
    y_bufs = pltpu.VMEM((2, BLOCK_TILE_ROWS, LANES), jnp.uint32)
    grid_spec = pltpu.PrefetchScalarGridSpec(
        num_scalar_prefetch=3,
        grid=(N_EXPERTS,),
        in_specs=[weight((D_MODEL, D_EXPERT)), weight((D_MODEL, D_EXPERT)), weight((D_EXPERT, D_MODEL)), any_space],
        out_specs=any_space,
        scratch_shapes=[pltpu.VMEM((D_MODEL, D_EXPERT), BF16), pltpu.VMEM((D_MODEL, D_EXPERT), BF16),
                        pltpu.VMEM((D_EXPERT, D_MODEL), BF16), x_bufs, y_bufs,
                        pltpu.SemaphoreType.DMA((EXPERT_X_BUFFERS,)), pltpu.SemaphoreType.DMA((2,))])
    return pl.pallas_call(
        _expert_kernel,
        grid_spec=grid_spec,
        out_shape=jax.ShapeDtypeStruct((MOE_ROWS * SUBLANES, LANES), jnp.uint32),
        compiler_params=_params("arbitrary"),
        name="routed_experts",
    )(first_blk, n_blk, counts, w_g, w_u, w_d, xg)


def _ffn_out_kernel(slot_ref, shared_ref, wt_ref, x1_ref, g2_ref, lng_ref, lnb_ref, y_ref,
                    oc_ref, ol_ref, ybuf_a, ybuf_b, sems):
    tm = shared_ref.shape[0]
    i = pl.program_id(0)
    n = pl.num_programs(0)

    def gather(tile, dst, sem):
        base = tile * tm
        for r in range(tm):
            for k in range(TOP_K):
                _row_copy(y_ref, slot_ref[base + (k * N_TOK + r)], dst.at[k], r,
                          sem).start(priority=k % ROW_DMA_QUEUES)

    def wait(dst, sem):
        for k in range(TOP_K):
            pltpu.make_async_copy(y_ref.at[pl.ds(0, tm * SUBLANES)], dst.at[k], sem).wait()

    @pl.when(i == 0)
    def _():
        gather(0, ybuf_a, sems.at[0])

    def step(cur, cur_sem, nxt, nxt_sem):
        gather(jnp.minimum(i + 1, n - 1), nxt, nxt_sem)
        y = shared_ref[...]
        wait(cur, cur_sem)
        wt = wt_ref[...]
        y_lo, y_hi = y[:, :HALF_D], y[:, HALF_D:]
        for k in range(TOP_K):
            r_lo, r_hi = _unpack_rows(_load_token_tiles(cur, (k,), tm))
            y_lo = y_lo + r_lo * wt[:, k:k + 1]
            y_hi = y_hi + r_hi * wt[:, k:k + 1]
        y = jnp.concatenate([y_lo, y_hi], axis=1)
        out = _layernorm(ALPHA * x1_ref[...] + g2_ref[...] * y, lng_ref[...], lnb_ref[...])
        is_ctx = _is_ctx_tile(tm)

        @pl.when(is_ctx)
        def _():
            oc_ref[...] = out

        @pl.when(jnp.logical_not(is_ctx))
        def _():
            ol_ref[...] = out

        @pl.when(i == n - 1)
        def _():
            wait(nxt, nxt_sem)

    @pl.when(i % 2 == 0)
    def _():
        step(ybuf_a, sems.at[0], ybuf_b, sems.at[1])

    @pl.when(i % 2 == 1)
    def _():
        step(ybuf_b, sems.at[1], ybuf_a, sems.at[0])


def _ffn_out(slot_flat, shared, wt, x1, mod, ln_g, ln_b, y_blk):
    tm = TM_FFN
    row = pl.BlockSpec((tm, D_MODEL), lambda i, s: (i, 0))
    vec = pl.BlockSpec((1, D_MODEL), lambda i, s: (0, 0))
    grid_spec = pltpu.PrefetchScalarGridSpec(
        num_scalar_prefetch=1,
        grid=(N_TOK // tm,),
        in_specs=[row, pl.BlockSpec((tm, 8), lambda i, s: (i, 0)), row,
                  _mod_spec(5, tm), vec, vec,
                  pl.BlockSpec(memory_space=pl.ANY)],
        out_specs=[_ctx_rows(tm), _lat_rows(tm)],
        scratch_shapes=[pltpu.VMEM((TOP_K, tm * SUBLANES, LANES), jnp.uint32),
                        pltpu.VMEM((TOP_K, tm * SUBLANES, LANES), jnp.uint32),
                        pltpu.SemaphoreType.DMA((2,))])
    return pl.pallas_call(
        _ffn_out_kernel,
        grid_spec=grid_spec,
        out_shape=[jax.ShapeDtypeStruct((N_CTX, D_MODEL), F32), jax.ShapeDtypeStruct((N_LAT, D_MODEL), F32)],
        compiler_params=_params("arbitrary"),
        name="shared_expert_out",
    )(slot_flat, shared, wt, x1, mod, ln_g, ln_b, y_blk)


def _dispatch_plan(idx, rank, counts):
    bm = MOE_BM
    counts = counts.reshape(N_EXPERTS).astype(jnp.int32)
    padded = (counts + bm - 1) // bm * bm
    pad_end = jnp.cumsum(padded)
    pad_start = pad_end - padded
    experts = jnp.arange(N_EXPERTS, dtype=jnp.int32)[:, None, None]
    first_row = jnp.sum(jnp.where(idx[None, :TOP_K] == experts, pad_start[:, None, None], 0), axis=0)
    slot = first_row + rank[:TOP_K]
    return slot.reshape(-1).astype(jnp.int32), pad_start // bm, padded // bm, counts


def _rope_tables():
    t = jnp.arange(DEC_SEQ)
    row = (t // GRID_W).astype(F32)
    col = (t % GRID_W).astype(F32)
    nf = HEAD_DIM // 4
    inv = ROPE_BASE ** (-jnp.arange(nf, dtype=F32) / nf)
    ar = row[:, None] * inv[None, :]
    ac = col[:, None] * inv[None, :]
    ang = jnp.concatenate([ar, ar, ac, ac], axis=-1)
    ang = jnp.concatenate([ang, ang], axis=-1)
    cos, sin = jnp.cos(ang), jnp.sin(ang)
    first = (jnp.arange(LANES) % (2 * nf)) < nf
    return cos, jnp.where(first, -sin, 0.0), jnp.where(first, 0.0, sin)


def _lambda_init(l):
    return 0.8 - 0.6 * math.exp(-0.3 * l)


def _gate_pairs(w_r, w_i):
    per = LANES // C_BW
    eye = jnp.eye(per, dtype=w_r.dtype)

    def chunk_diag(w):
        w = w.reshape(C_BLOCKS // per, per, C_BW, C_BW)
        return (eye[None, :, None, :, None] * w[:, :, :, None, :]).reshape(C_BLOCKS // per, LANES, LANES)

    return jnp.concatenate([chunk_diag(w_r), chunk_diag(w_i)], axis=2).reshape(C_WIDTH, 2 * LANES)


def kernel(x_prompt, x_sample, c, cache_a_k, cache_a_v, cache_b_k, cache_b_v, state_c, cache_d_k, cache_d_v, c_ctx, w_mod, b_mod, w_in, a_sink, b_lambda, c_conv_w, c_conv_b, c_w_rgate, c_b_rgate, c_w_igate, c_b_igate, c_lambda, d_rpb, g_mix, w_out, ln_g, ln_b, w_router, b_router, w_e_gate, w_e_up, w_e_down, w_s_gate, w_s_up, w_s_down):
    x_ctx, x_lat = x_prompt.reshape(N_CTX, D_MODEL), x_sample.reshape(N_LAT, D_MODEL)
    cond8 = jnp.zeros((8, D_MODEL), F32).at[0].set(c_ctx).at[1:1 + DEC_BATCH].set(c)
    mod_all = _modulation(cond8, w_mod, b_mod)
    cos, sin_a, sin_b = _rope_tables()
    na_row_idx, na_col_sel, na_col_ok = _na_bias_indices()

    order = ((0, 512), (768, IN_WIDTH), (512, 768))

    ctx_out = [[] for _ in range(7)]
    for l in range(DEPTH):
        lam_init = _lambda_init(l)
        mod = mod_all[l, :N_GROUPS].reshape(N_GROUPS, 6, 1, D_MODEL)
        w_l = w_in[l]
        w_in_b = jnp.concatenate([w_l[:, a:b] for a, b in order], axis=1).astype(BF16)
        slab = _in_projection(x_ctx, x_lat, mod, w_in_b)

        lp = b_lambda[l]
        lam = jnp.exp(jnp.sum(lp[0] * lp[1])) - jnp.exp(jnp.sum(lp[2] * lp[3])) + lam_init
        scal = jnp.zeros((16,), F32).at[:A_HEADS].set(a_sink[l]).at[A_HEADS].set(lam)
        g2d = g_mix[l].reshape(1, -1)

        oa, ob, od = _ctx_attention(slab, scal, g2d, lam_init)

        qa_p, ka_r, va_b, qb_p, kb_r, vb_b, qd_p, kd_b, vd_b = _latent_prep(slab, cos, sin_a, sin_b)
        oa = _window_attention(qa_p, ka_r, va_b, cache_a_k[:, l].reshape(DEC_BATCH, PAST_LEN, LANES),
                               cache_a_v[:, l].reshape(DEC_BATCH, PAST_LEN, LANES), scal, g2d, oa)
        ob = _diff_attention(qb_p, kb_r, vb_b, cache_b_k[:, l].reshape(DEC_BATCH, PAST_LEN, 512),
                             cache_b_v[:, l].reshape(DEC_BATCH, PAST_LEN, 512), scal, g2d, lam_init, ob)
        bias = _na_bias(d_rpb[l], na_row_idx, na_col_sel, na_col_ok)
        od = _neighbourhood_attention(qd_p, kd_b, vd_b, cache_d_k[:, l].reshape(DEC_BATCH, PAST_LEN, 512),
                                      cache_d_v[:, l].reshape(DEC_BATCH, PAST_LEN, 512), bias, g2d, od)

        w_gates = jnp.stack([_gate_pairs(c_w_rgate[l, d], c_w_igate[l, d]) for d in range(2)])
        b_gates = jnp.concatenate([c_b_rgate[l], c_b_igate[l]], axis=-1).reshape(2, 1, 2 * C_WIDTH)
        zeros = jnp.zeros((BATCH, 1, C_WIDTH), F32)
        h0_f = jnp.concatenate([zeros, state_c[:, l, 0][:, None, :]], axis=0)
        h0_b = jnp.concatenate([zeros, state_c[:, l, 1][:, None, :]], axis=0)
        oc, fin_f, fin_b = _rglru(slab, c_conv_w[l], c_conv_b[l].reshape(1, -1), w_gates, b_gates,
                                  c_lambda[l].reshape(2, 1, C_WIDTH), h0_f, h0_b, g2d)

        x1, h2, idx, wt, rank, counts = _mixer_out(oa, ob, oc, od, w_out[l].astype(BF16), x_ctx, x_lat, mod,
                                                   ln_g[l, 0].reshape(1, -1), ln_b[l, 0].reshape(1, -1),
                                                   w_router[l], b_router[l].reshape(1, -1))

        slot, first_blk, n_blk, counts = _dispatch_plan(idx, rank, counts)
        xg, shared = _dispatch_rows(slot, h2, w_s_gate[l].astype(BF16), w_s_up[l].astype(BF16),
                                    w_s_down[l].astype(BF16))
        y_blk = _routed_experts(xg, first_blk, n_blk, counts, l, w_e_gate, w_e_up, w_e_down)
        x_ctx, x_lat = _ffn_out(slot, shared, wt, x1, mod, ln_g[l, 1].reshape(1, -1),
                                ln_b[l, 1].reshape(1, -1), y_blk)

        ctx = slab[:N_CTX]
        ctx_out[0].append(ctx[:, COL_KA * LANES:(COL_KA + 1) * LANES].reshape(BATCH, SEQ, A_KV_HEADS, HEAD_DIM))
        ctx_out[1].append(ctx[:, COL_VA * LANES:(COL_VA + 1) * LANES].reshape(BATCH, SEQ, A_KV_HEADS, HEAD_DIM))
        ctx_out[2].append(ctx[:, COL_KB * 512:(COL_KB + 1) * 512].reshape(BATCH, SEQ, B_HEADS, 2, HEAD_DIM))
        ctx_out[3].append(ctx[:, COL_VB * 512:(COL_VB + 1) * 512].reshape(BATCH, SEQ, B_HEADS, 2 * HEAD_DIM))
        ctx_out[4].append(jnp.concatenate([fin_f[:BATCH], fin_b[:BATCH]], axis=1))
        ctx_out[5].append(ctx[:, COL_KD * 512:(COL_KD + 1) * 512].reshape(BATCH, SEQ, D_HEADS, HEAD_DIM))
        ctx_out[6].append(ctx[:, COL_VD * 512:(COL_VD + 1) * 512].reshape(BATCH, SEQ, D_HEADS, HEAD_DIM))

    new = [jnp.stack(t, axis=1) for t in ctx_out]
    return (x_ctx.reshape(BATCH, SEQ, D_MODEL), x_lat.reshape(DEC_BATCH, DEC_SEQ, D_MODEL), *new)
```
